```python
import math
import jax
import jax.numpy as jnp
from jax import lax
import numpy as np

D_MODEL = 2048
BATCH = 1
SEQ = 8192
DEPTH = 2

MEM_LEN = 256
HEAD_DIM = 128
N_MIX_HEADS = D_MODEL // HEAD_DIM
MEM_HEADS = 4
MEM_W = MEM_HEADS * HEAD_DIM
ATTN_GROUPS = ((128, 1), (512, 4), (2048, 16))
ATTN_HEADS = N_MIX_HEADS - MEM_HEADS
HEADS_PER_GROUP = ATTN_HEADS // len(ATTN_GROUPS)
ATTN_W = ATTN_HEADS * HEAD_DIM
ATTN_OUT_W = HEADS_PER_GROUP * HEAD_DIM
BLK = 128
SGU_GROUPS = ATTN_HEADS
SGU_GROUP_DIM = HEAD_DIM
SGU_W = SGU_GROUPS * SGU_GROUP_DIM
SGU_CHUNK = 128
ROT_DIM = HEAD_DIM // 4
ROPE_THETA = 500000.0
FFN_HIDDEN = ((8 * D_MODEL // 3 + 255) // 256) * 256
NORM_EPS = 1e-6
LN_EPS = 1e-5
NEG_INF = -1e30

kernel_name = 'hybrid_dilated_attn_gmlp_memory_trunk'


def _rms_norm(x, g):
    xf = x.astype(jnp.float32)
    y = xf * lax.rsqrt(jnp.mean(xf * xf, axis=-1, keepdims=True) + NORM_EPS)
    return (y * g.astype(jnp.float32)).astype(x.dtype)


def _layer_norm(x, g, b):
    xf = x.astype(jnp.float32)
    mu = jnp.mean(xf, axis=-1, keepdims=True)
    var = jnp.mean(jnp.square(xf - mu), axis=-1, keepdims=True)
    y = (xf - mu) * lax.rsqrt(var + LN_EPS)
    return (y * g.astype(jnp.float32) + b.astype(jnp.float32)).astype(x.dtype)


def _partial_rotary(t, positions):
    half = ROT_DIM // 2
    inv_freq = ROPE_THETA ** (-jnp.arange(half, dtype=jnp.float32) / half)
    ang = positions.astype(jnp.float32)[:, :, None] * inv_freq
    cos = jnp.cos(ang)[:, :, None, :]
    sin = jnp.sin(ang)[:, :, None, :]
    tf = t.astype(jnp.float32)
    x1 = tf[..., :half]
    x2 = tf[..., half:ROT_DIM]
    rot = jnp.concatenate([x1 * cos - x2 * sin, x2 * cos + x1 * sin, tf[..., ROT_DIM:]], axis=-1)
    return rot.astype(t.dtype)


def _dilated_group(q, k, v, window, dilation):
    b, s, h, dh = q.shape
    n_back = window // dilation
    span = dilation * BLK
    sp = -(-s // span) * span
    length = sp // dilation
    nb = length // BLK

    def to_blocks(t):
        t = jnp.pad(t, ((0, 0), (0, sp - s), (0, 0), (0, 0)))
        t = t.reshape(b, length, dilation, h, dh).transpose(0, 2, 3, 1, 4)
        return t.reshape(b, dilation, h, nb, BLK, dh)

    def with_prev(t):
        prev = jnp.pad(t[:, :, :, :-1], ((0, 0), (0, 0), (0, 0), (1, 0), (0, 0), (0, 0)))
        return jnp.concatenate([prev, t], axis=4)

    qb = to_blocks(q)
    kb = with_prev(to_blocks(k))
    vb = with_prev(to_blocks(v))
    logits = jnp.einsum('brhnqd,brhnkd->brhnqk', qb, kb,
                        preferred_element_type=jnp.float32) * (dh ** -0.5)
    qi = jnp.arange(BLK)[:, None]
    ki = jnp.arange(2 * BLK)[None, :]
    dist = BLK + qi - ki
    band = (dist >= 0) & (dist <= n_back)
    first = (jnp.arange(nb) == 0)[:, None, None]
    mask = band[None] & (jnp.logical_not(first) | (ki >= BLK)[None])
    logits = jnp.where(mask, logits, NEG_INF)
    lse = jax.nn.logsumexp(logits, axis=-1)
    p = jnp.exp(logits - lse[..., None])
    out = jnp.einsum('brhnqk,brhnkd->brhnqd', p.astype(v.dtype), vb)
    out = out.reshape(b, dilation, h, length, dh).transpose(0, 3, 1, 2, 4).reshape(b, sp, h, dh)[:, :s]
    lse = lse.reshape(b, dilation, h, length).transpose(0, 3, 1, 2).reshape(b, sp, h)[:, :s]
    return out, lse


def _dilated_attention_mixer(h, positions, w_in):
    b, s, _ = h.shape
    proj = h @ w_in
    q, k, v, q_mem = jnp.split(proj, [ATTN_W, 2 * ATTN_W, 3 * ATTN_W], axis=-1)
    q = _partial_rotary(q.reshape(b, s, ATTN_HEADS, HEAD_DIM), positions)
    k = _partial_rotary(k.reshape(b, s, ATTN_HEADS, HEAD_DIM), positions)
    v = v.reshape(b, s, ATTN_HEADS, HEAD_DIM)
    outs, lses = [], []
    for g, (window, dilation) in enumerate(ATTN_GROUPS):
        sl = slice(g * HEADS_PER_GROUP, (g + 1) * HEADS_PER_GROUP)
        o, l = _dilated_group(q[:, :, sl], k[:, :, sl], v[:, :, sl], window, dilation)
        outs.append(o)
        lses.append(l)
    w = jax.nn.softmax(jnp.stack(lses, axis=0), axis=0)
    merged = jnp.einsum('gbsh,gbshd->bshd', w.astype(v.dtype), jnp.stack(outs, axis=0))
    return merged.reshape(b, s, ATTN_OUT_W), q_mem


def _spatial_gating_mixer(h, w_in, ln_g, ln_b, w_spatial, b_spatial):
    b, s, _ = h.shape
    proj = h @ w_in
    u, v, q_mem = jnp.split(proj, [SGU_W, 2 * SGU_W], axis=-1)
    u = jax.nn.gelu(u)
    v = _layer_norm(jax.nn.gelu(v), ln_g, ln_b)
    v = v.reshape(b, s // SGU_CHUNK, SGU_CHUNK, SGU_GROUPS, SGU_GROUP_DIM)
    causal = jnp.tril(jnp.ones((SGU_CHUNK, SGU_CHUNK), dtype=bool))
    w_s = jnp.where(causal[None], w_spatial, 0.0).astype(v.dtype)
    mixed = jnp.einsum('gts,bnsgc->bntgc', w_s, v) + b_spatial.T[None, None, :, :, None]
    return u * mixed.reshape(b, s, SGU_W), q_mem


def _memory_attention(q_mem, mem_n, w_mem_kv):
    b, s, _ = q_mem.shape
    kv = mem_n @ w_mem_kv
    k, v = jnp.split(kv, 2, axis=-1)
    q = q_mem.reshape(b, s, MEM_HEADS, HEAD_DIM)
    k = k.reshape(b, -1, MEM_HEADS, HEAD_DIM)
    v = v.reshape(b, -1, MEM_HEADS, HEAD_DIM)
    logits = jnp.einsum('bshd,bmhd->bhsm', q, k,
                        preferred_element_type=jnp.float32) * (HEAD_DIM ** -0.5)
    p = jax.nn.softmax(logits, axis=-1)
    out = jnp.einsum('bhsm,bmhd->bshd', p.astype(v.dtype), v)
    return out.reshape(b, s, MEM_W)


def _swiglu(h, w_gate, w_up, w_down):
    return (jax.nn.silu(h @ w_gate) * (h @ w_up)) @ w_down


def setup_inputs(seed: int = 0) -> dict:
    key = jax.random.key(seed)
    ks = jax.random.split(key, 24)
    n_a = (DEPTH + 1) // 2
    n_b = DEPTH // 2

    def dense(k, shape, fan_in):
        return jax.random.normal(k, shape, jnp.float32) * (fan_in ** -0.5)

    def gain(k, shape):
        return 1.0 + 0.02 * jax.random.normal(k, shape, jnp.float32)

    def small(k, shape):
        return 0.02 * jax.random.normal(k, shape, jnp.float32)

    x = jax.random.normal(ks[0], (BATCH, SEQ, D_MODEL), jnp.float32)
    mem = jax.random.normal(ks[1], (BATCH, MEM_LEN, D_MODEL), jnp.float32)
    offset = jax.random.randint(ks[2], (BATCH, 1), 0, 4096, dtype=jnp.int32)
    positions = offset + jnp.arange(SEQ, dtype=jnp.int32)[None, :]
    return {
        'x': x,
        'mem': mem,
        'positions': positions,
        'mix_norm': gain(ks[3], (DEPTH, D_MODEL)),
        'mem_norm': gain(ks[4], (DEPTH, D_MODEL)),
        'w_mem_kv': dense(ks[5], (DEPTH, D_MODEL, 2 * MEM_W), D_MODEL),
        'ffn_norm': gain(ks[6], (DEPTH, D_MODEL)),
        'w_gate': dense(ks[7], (DEPTH, D_MODEL, FFN_HIDDEN), D_MODEL),
        'w_up': dense(ks[8], (DEPTH, D_MODEL, FFN_HIDDEN), D_MODEL),
        'w_down': dense(ks[9], (DEPTH, FFN_HIDDEN, D_MODEL), FFN_HIDDEN),
        'attn_w_in': dense(ks[10], (n_a, D_MODEL, 3 * ATTN_W + MEM_W), D_MODEL),
        'attn_w_out': dense(ks[11], (n_a, ATTN_OUT_W + MEM_W, D_MODEL), ATTN_OUT_W + MEM_W),
        'sgu_w_in': dense(ks[12], (n_b, D_MODEL, 2 * SGU_W + MEM_W), D_MODEL),
        'sgu_ln_g': gain(ks[13], (n_b, SGU_W)),
        'sgu_ln_b': small(ks[14], (n_b, SGU_W)),
        'sgu_w_spatial': dense(ks[15], (n_b, SGU_GROUPS, SGU_CHUNK, SGU_CHUNK), SGU_CHUNK),
        'sgu_b_spatial': gain(ks[16], (n_b, SGU_GROUPS, SGU_CHUNK)),
        'sgu_w_out': dense(ks[17], (n_b, SGU_W + MEM_W, D_MODEL), SGU_W + MEM_W),
        'final_norm': gain(ks[18], (D_MODEL,)),
    }


def reference(x, mem, positions, mix_norm, mem_norm, w_mem_kv, ffn_norm, w_gate, w_up, w_down,
              attn_w_in, attn_w_out, sgu_w_in, sgu_ln_g, sgu_ln_b, sgu_w_spatial, sgu_b_spatial,
              sgu_w_out, final_norm):
    for i in range(DEPTH):
        j = i // 2
        h = _rms_norm(x, mix_norm[i])
        if i % 2 == 0:
            mix_out, q_mem = _dilated_attention_mixer(h, positions, attn_w_in[j])
            w_out = attn_w_out[j]
        else:
            mix_out, q_mem = _spatial_gating_mixer(h, sgu_w_in[j], sgu_ln_g[j], sgu_ln_b[j],
                                                   sgu_w_spatial[j], sgu_b_spatial[j])
            w_out = sgu_w_out[j]
        mem_out = _memory_attention(q_mem, _rms_norm(mem, mem_norm[i]), w_mem_kv[i])
        x = x + jnp.concatenate([mix_out, mem_out], axis=-1) @ w_out
        x = x + _swiglu(_rms_norm(x, ffn_norm[i]), w_gate[i], w_up[i], w_down[i])
    return _rms_norm(x, final_norm)
```

```python
import functools
import math

import jax
import jax.numpy as jnp
from jax import lax
from jax.experimental import pallas as pl
from jax.experimental.pallas import tpu as pltpu

D_MODEL = 2048
SEQ = 8192
DEPTH = 2
MEM_LEN = 256
HEAD_DIM = 128
MEM_HEADS = 4
MEM_W = MEM_HEADS * HEAD_DIM
ATTN_GROUPS = ((128, 1), (512, 4), (2048, 16))
ATTN_HEADS = 12
HEADS_PER_GROUP = 4
ATTN_W = ATTN_HEADS * HEAD_DIM
ATTN_OUT_W = HEADS_PER_GROUP * HEAD_DIM
BLK = 128
SGU_GROUPS = 12
SGU_W = SGU_GROUPS * HEAD_DIM
SGU_CHUNK = 128
ROT_DIM = HEAD_DIM // 4
ROT_HALF = ROT_DIM // 2
ROPE_THETA = 500000.0
FFN_HIDDEN = 5632
NORM_EPS = 1e-6
LN_EPS = 1e-5
NEG_INF = -1e30
ATTN_SCALE = HEAD_DIM ** -0.5

N_BACK = BLK
assert all(w // d == N_BACK for w, d in ATTN_GROUPS)
ATTN_SPAN = max(d for _, d in ATTN_GROUPS) * BLK

V7X_VMEM_LIMIT_BYTES = 56 * 1024 * 1024

F32 = jnp.float32
BF16 = jnp.bfloat16


def _params(*semantics):
    return pltpu.CompilerParams(dimension_semantics=semantics,
                                vmem_limit_bytes=V7X_VMEM_LIMIT_BYTES)


def _rms_rows(x, g):
    ms = jnp.mean(x * x, axis=-1, keepdims=True)
    return x * lax.rsqrt(ms + NORM_EPS) * g


def _rope_table_kernel(pos_ref, invf_ref, c_ref, s1_ref, s2_ref):
    ang = pos_ref[...].astype(F32) * invf_ref[...]
    lane = lax.broadcasted_iota(jnp.int32, ang.shape, 1)
    cos = jnp.cos(ang)
    sin = jnp.sin(ang)
    c_ref[...] = jnp.where(lane < ROT_DIM, cos, 1.0)
    s1_ref[...] = jnp.where(lane < ROT_HALF, 0.0, jnp.where(lane < ROT_DIM, sin, 0.0))
    s2_ref[...] = jnp.where(lane < ROT_HALF, -sin, 0.0)


def _rope_tables(pos_col, invf):
    s = pos_col.shape[0]
    tm = 1024
    tab = jax.ShapeDtypeStruct((s, HEAD_DIM), F32)
    row = pl.BlockSpec((tm, HEAD_DIM), lambda i: (i, 0))
    return pl.pallas_call(
        _rope_table_kernel,
        grid=(s // tm,),
        in_specs=[pl.BlockSpec((tm, 1), lambda i: (i, 0)),
                  pl.BlockSpec((1, HEAD_DIM), lambda i: (0, 0))],
        out_specs=[row, row, row],
        out_shape=[tab, tab, tab],
        compiler_params=_params("parallel"),
        name="rope_tables",
    )(pos_col, invf)


def _norm_matmul_kernel(x_ref, g_ref, w_ref, *rest, rope_blocks):
    if rope_blocks:
        c_ref, s1_ref, s2_ref, o_ref, h_ref = rest
    else:
        o_ref, h_ref = rest
    n = pl.program_id(1)

    @pl.when(n == 0)
    def _():
        h_ref[...] = _rms_rows(x_ref[...], g_ref[...]).astype(BF16)

    acc = jnp.dot(h_ref[...], w_ref[...].astype(BF16), preferred_element_type=F32)
    if not rope_blocks:
        o_ref[...] = acc
        return

    @pl.when(n < rope_blocks)
    def _():
        c, s1, s2 = c_ref[...], s1_ref[...], s2_ref[...]
        for j in range(acc.shape[1] // HEAD_DIM):
            t = acc[:, j * HEAD_DIM:(j + 1) * HEAD_DIM]
            o_ref[:, j * HEAD_DIM:(j + 1) * HEAD_DIM] = (
                t * c + pltpu.roll(t, ROT_HALF, 1) * s1
                + pltpu.roll(t, HEAD_DIM - ROT_HALF, 1) * s2)

    @pl.when(n >= rope_blocks)
    def _():
        o_ref[...] = acc


def _norm_matmul(x, g, w, *, tm, tn, rope=None, rope_cols=0):
    m, k = x.shape
    n = w.shape[1]
    assert m % tm == 0 and n % tn == 0 and rope_cols % tn == 0
    in_specs = [pl.BlockSpec((tm, k), lambda i, j: (i, 0)),
                pl.BlockSpec((1, k), lambda i, j: (0, 0)),
                pl.BlockSpec((k, tn), lambda i, j: (0, j))]
    args = [x, g.reshape(1, k), w]
    if rope is not None:
        in_specs += [pl.BlockSpec((tm, HEAD_DIM), lambda i, j: (i, 0))] * 3
        args += list(rope)
    return pl.pallas_call(
        functools.partial(_norm_matmul_kernel, rope_blocks=rope_cols // tn),
        grid=(m // tm, n // tn),
        in_specs=in_specs,
        out_specs=pl.BlockSpec((tm, tn), lambda i, j: (i, j)),
        out_shape=jax.ShapeDtypeStruct((m, n), F32),
        scratch_shapes=[pltpu.VMEM((tm, k), BF16)],
        compiler_params=_params("parallel", "arbitrary"),
        name="norm_matmul",
    )(*args)


def _rows(ref, start, size, stride):
    if stride == 1:
        return ref[pl.ds(start, size), :]
    return ref[pl.ds(start, size, stride=stride), :]


def _attn_unit(q, kc, vc, bias):
    s = lax.dot_general(q.astype(BF16), kc.astype(BF16), (((1,), (1,)), ((), ())),
                        preferred_element_type=F32)
    s = s * ATTN_SCALE + bias
    m = jnp.max(s, axis=-1, keepdims=True)
    p = jnp.exp(s - m)
    l = jnp.sum(p, axis=-1, keepdims=True)
    o = jnp.dot(p.astype(BF16), vc.astype(BF16), preferred_element_type=F32)
    return o * (1.0 / l), m + jnp.log(l)


def _attn_kernel(*refs):
    ngrp = len(ATTN_GROUPS)
    in_refs = refs[:5 * ngrp]
    o_ref = refs[5 * ngrp]
    o_scr = refs[5 * ngrp + 1:5 * ngrp + 1 + ngrp]
    l_scr = refs[5 * ngrp + 1 + ngrp:5 * ngrp + 1 + 2 * ngrp]
    bias_ref = refs[-1]

    qi = lax.broadcasted_iota(jnp.int32, (BLK, 2 * BLK), 0)
    ki = lax.broadcasted_iota(jnp.int32, (BLK, 2 * BLK), 1)
    lo = jnp.where(pl.program_id(0) == 0, BLK, 0)
    bias_ref[0] = jnp.where((ki >= qi) & (ki <= qi + N_BACK), 0.0, NEG_INF)
    bias_ref[1] = jnp.where((ki >= jnp.maximum(qi, lo)) & (ki <= qi + N_BACK), 0.0, NEG_INF)

    for g, (_, d) in enumerate(ATTN_GROUPS):
        q_ref, k_ref, v_ref, kp_ref, vp_ref = in_refs[5 * g:5 * g + 5]
        og, lg = o_scr[g], l_scr[g]
        sub = d * BLK
        nsub = ATTN_SPAN // sub

        def put(start, o, lse, og=og, lg=lg, d=d):
            if d == 1:
                idx = pl.ds(start, BLK)
            else:
                idx = pl.ds(start, BLK, stride=d)
            og[idx, :] = o
            lg[idx, :] = jnp.broadcast_to(lse, (BLK, HEAD_DIM))

        for r in range(d):
            kc = jnp.concatenate([_rows(kp_ref, r, BLK, d), _rows(k_ref, r, BLK, d)], axis=0)
            vc = jnp.concatenate([_rows(vp_ref, r, BLK, d), _rows(v_ref, r, BLK, d)], axis=0)
            o, lse = _attn_unit(_rows(q_ref, r, BLK, d), kc, vc, bias_ref[1])
            put(r, o, lse)

        if nsub > 1:
            per_iter = d if d > 1 else 3
            assert ((nsub - 1) * d) % per_iter == 0

            def body(t, carry, d=d, sub=sub, per_iter=per_iter,
                     q_ref=q_ref, k_ref=k_ref, v_ref=v_ref, put=put):
                for u in range(per_iter):
                    idx = t * per_iter + u
                    c = idx // d + 1
                    r = idx % d
                    q0 = c * sub + r
                    k0 = (c - 1) * sub + r
                    o, lse = _attn_unit(_rows(q_ref, q0, BLK, d), _rows(k_ref, k0, 2 * BLK, d),
                                        _rows(v_ref, k0, 2 * BLK, d), bias_ref[0])
                    put(q0, o, lse)
                return carry

            lax.fori_loop(0, (nsub - 1) * d // per_iter, body, 0)

    lses = [l[...] for l in l_scr]
    mx = functools.reduce(jnp.maximum, lses)
    ws = [jnp.exp(l - mx) for l in lses]
    num = functools.reduce(lambda a, b: a + b, [w * o[...] for w, o in zip(ws, o_scr)])
    den = functools.reduce(lambda a, b: a + b, ws)
    o_ref[...] = (num * (1.0 / den)).astype(o_ref.dtype)


def _dilated_attention(proj):
    s = proj.shape[0]
    assert s % ATTN_SPAN == 0
    hpg = HEADS_PER_GROUP
    in_specs = []
    for g, (_, d) in enumerate(ATTN_GROUPS):
        sub = d * BLK
        ratio = ATTN_SPAN // sub
        qc, kc, vc = g * hpg, ATTN_HEADS + g * hpg, 2 * ATTN_HEADS + g * hpg
        cur = lambda col: pl.BlockSpec((ATTN_SPAN, HEAD_DIM), lambda i, h, col=col: (i, col + h))
        prev = lambda col, ratio=ratio, sub=sub: pl.BlockSpec(
            (sub, HEAD_DIM), lambda i, h, col=col, ratio=ratio: (jnp.maximum(i * ratio - 1, 0), col + h))
        in_specs += [cur(qc), cur(kc), cur(vc), prev(kc), prev(vc)]
    ngrp = len(ATTN_GROUPS)
    return pl.pallas_call(
        _attn_kernel,
        grid=(s // ATTN_SPAN, hpg),
        in_specs=in_specs,
        out_specs=pl.BlockSpec((ATTN_SPAN, HEAD_DIM), lambda i, h: (i, h)),
        out_shape=jax.ShapeDtypeStruct((s, ATTN_OUT_W), BF16),
        scratch_shapes=([pltpu.VMEM((ATTN_SPAN, HEAD_DIM), F32)] * (2 * ngrp)
                        + [pltpu.VMEM((2, BLK, 2 * BLK), F32)]),
        compiler_params=_params("parallel", "arbitrary"),
        name="dilated_attention",
    )(*([proj] * (5 * ngrp)))


def _mem_attn_kernel(q_ref, kv_ref, o_ref):
    for h in range(MEM_HEADS):
        sl = slice(h * HEAD_DIM, (h + 1) * HEAD_DIM)
        k = kv_ref[:, sl].astype(BF16)
        v = kv_ref[:, MEM_W + h * HEAD_DIM:MEM_W + (h + 1) * HEAD_DIM].astype(BF16)
        s = lax.dot_general(q_ref[:, sl].astype(BF16), k, (((1,), (1,)), ((), ())),
                            preferred_element_type=F32) * ATTN_SCALE
        m = jnp.max(s, axis=-1, keepdims=True)
        p = jnp.exp(s - m)
        l = jnp.sum(p, axis=-1, keepdims=True)
        o = jnp.dot(p.astype(BF16), v, preferred_element_type=F32)
        o_ref[:, sl] = (o * (1.0 / l)).astype(o_ref.dtype)


def _memory_attention(proj, q_col_block, kv):
    s = proj.shape[0]
    tm = 1024
    return pl.pallas_call(
        _mem_attn_kernel,
        grid=(s // tm,),
        in_specs=[pl.BlockSpec((tm, MEM_W), lambda i: (i, q_col_block)),
                  pl.BlockSpec((MEM_LEN, 2 * MEM_W), lambda i: (0, 0))],
        out_specs=pl.BlockSpec((tm, MEM_W), lambda i: (i, 0)),
        out_shape=jax.ShapeDtypeStruct((s, MEM_W), BF16),
        compiler_params=_params("parallel"),
        name="memory_attention",
    )(proj, kv)


def _gelu_tanh(x):
    return x * (0.5 * (1.0 + jnp.tanh(math.sqrt(2.0 / math.pi) * (x + 0.044715 * (x * x * x)))))


def _sgu_kernel(u_ref, v_ref, g_ref, b_ref, ws_ref, bt_ref, o_ref, wsb_ref, vn_ref):
    @pl.when(pl.program_id(0) == 0)
    def _():
        t = lax.broadcasted_iota(jnp.int32, (SGU_CHUNK, SGU_CHUNK), 0)
        s = lax.broadcasted_iota(jnp.int32, (SGU_CHUNK, SGU_CHUNK), 1)
        for g in range(SGU_GROUPS):
            wsb_ref[g] = jnp.where(t >= s, ws_ref[g], 0.0).astype(BF16)

    v = _gelu_tanh(v_ref[...])
    mu = jnp.mean(v, axis=-1, keepdims=True)
    vc = v - mu
    var = jnp.mean(vc * vc, axis=-1, keepdims=True)
    vn_ref[...] = (vc * lax.rsqrt(var + LN_EPS) * g_ref[...] + b_ref[...]).astype(BF16)

    bt = bt_ref[...]
    for c in range(u_ref.shape[0] // SGU_CHUNK):
        rows = slice(c * SGU_CHUNK, (c + 1) * SGU_CHUNK)
        for g in range(SGU_GROUPS):
            cols = slice(g * HEAD_DIM, (g + 1) * HEAD_DIM)
            mixed = jnp.dot(wsb_ref[g], vn_ref[rows, cols], preferred_element_type=F32)
            mixed = mixed + bt[:, g:g + 1]
            o_ref[rows, cols] = (_gelu_tanh(u_ref[rows, cols]) * mixed).astype(o_ref.dtype)


def _spatial_gating(proj, ln_g, ln_b, w_spatial, b_spatial_t):
    s = proj.shape[0]
    tm = 512
    return pl.pallas_call(
        _sgu_kernel,
        grid=(s // tm,),
        in_specs=[pl.BlockSpec((tm, SGU_W), lambda i: (i, 0)),
                  pl.BlockSpec((tm, SGU_W), lambda i: (i, 1)),
                  pl.BlockSpec((1, SGU_W), lambda i: (0, 0)),
                  pl.BlockSpec((1, SGU_W), lambda i: (0, 0)),
                  pl.BlockSpec((SGU_GROUPS, SGU_CHUNK, SGU_CHUNK), lambda i: (0, 0, 0)),
                  pl.BlockSpec((SGU_CHUNK, SGU_GROUPS), lambda i: (0, 0))],
        out_specs=pl.BlockSpec((tm, SGU_W), lambda i: (i, 0)),
        out_shape=jax.ShapeDtypeStruct((s, SGU_W), BF16),
        scratch_shapes=[pltpu.VMEM((SGU_GROUPS, SGU_CHUNK, SGU_CHUNK), BF16),
                        pltpu.VMEM((tm, SGU_W), BF16)],
        compiler_params=_params("arbitrary"),
        name="spatial_gating",
    )(proj, proj, ln_g.reshape(1, SGU_W), ln_b.reshape(1, SGU_W), w_spatial, b_spatial_t)


def _out_proj_kernel(a_ref, b_ref, wa_ref, wb_ref, x_ref, o_ref, wa_bf, wb_bf):
    @pl.when(pl.program_id(1) == 0)
    def _():
        wa_bf[...] = wa_ref[...].astype(BF16)
        wb_bf[...] = wb_ref[...].astype(BF16)

    o_ref[...] = (x_ref[...]
                  + jnp.dot(a_ref[...], wa_bf[...], preferred_element_type=F32)
                  + jnp.dot(b_ref[...], wb_bf[...], preferred_element_type=F32))


def _out_proj(a, b, w, x):
    s, ka = a.shape
    kb = b.shape[1]
    n = w.shape[1]
    assert ka % kb == 0 and w.shape[0] == ka + kb
    tm, tn = 1024, 512
    return pl.pallas_call(
        _out_proj_kernel,
        grid=(n // tn, s // tm),
        in_specs=[pl.BlockSpec((tm, ka), lambda j, i: (i, 0)),
                  pl.BlockSpec((tm, kb), lambda j, i: (i, 0)),
                  pl.BlockSpec((ka, tn), lambda j, i: (0, j)),
                  pl.BlockSpec((kb, tn), lambda j, i: (ka // kb, j)),
                  pl.BlockSpec((tm, tn), lambda j, i: (i, j))],
        out_specs=pl.BlockSpec((tm, tn), lambda j, i: (i, j)),
        out_shape=jax.ShapeDtypeStruct((s, n), F32),
        scratch_shapes=[pltpu.VMEM((ka, tn), BF16), pltpu.VMEM((kb, tn), BF16)],
        compiler_params=_params("parallel", "arbitrary"),
        name="out_proj",
    )(a, b, w, w, x)


def _ffn_gate_up_kernel(x_ref, g_ref, wg_ref, wu_ref, o_ref, h_ref):
    @pl.when(pl.program_id(1) == 0)
    def _():
        h_ref[...] = _rms_rows(x_ref[...], g_ref[...]).astype(BF16)

    h = h_ref[...]
    gate = jnp.dot(h, wg_ref[...].astype(BF16), preferred_element_type=F32)
    up = jnp.dot(h, wu_ref[...].astype(BF16), preferred_element_type=F32)
    o_ref[...] = (gate * (1.0 / (1.0 + jnp.exp(-gate))) * up).astype(o_ref.dtype)


def _ffn_gate_up(x, g, w_gate, w_up):
    s, k = x.shape
    f = w_gate.shape[1]
    tm, tf = 1024, 512
    wspec = pl.BlockSpec((k, tf), lambda i, j: (0, j))
    return pl.pallas_call(
        _ffn_gate_up_kernel,
        grid=(s // tm, f // tf),
        in_specs=[pl.BlockSpec((tm, k), lambda i, j: (i, 0)),
                  pl.BlockSpec((1, k), lambda i, j: (0, 0)),
                  wspec, wspec],
        out_specs=pl.BlockSpec((tm, tf), lambda i, j: (i, j)),
        out_shape=jax.ShapeDtypeStruct((s, f), BF16),
        scratch_shapes=[pltpu.VMEM((tm, k), BF16)],
        compiler_params=_params("parallel", "arbitrary"),
        name="ffn_gate_up",
    )(x, g.reshape(1, k), w_gate, w_up)


def _ffn_down_kernel(a_ref, w_ref, x_ref, o_ref):
    o_ref[...] = x_ref[...] + jnp.dot(a_ref[...], w_ref[...].astype(BF16),
                                      preferred_element_type=F32)


def _ffn_down(a, w, x):
    s, f = a.shape
    n = w.shape[1]
    tm, tn = 1024, 256
    return pl.pallas_call(
        _ffn_down_kernel,
        grid=(s // tm, n // tn),
        in_specs=[pl.BlockSpec((tm, f), lambda i, j: (i, 0)),
                  pl.BlockSpec((f, tn), lambda i, j: (0, j)),
                  pl.BlockSpec((tm, tn), lambda i, j: (i, j))],
        out_specs=pl.BlockSpec((tm, tn), lambda i, j: (i, j)),
        out_shape=jax.ShapeDtypeStruct((s, n), F32),
        compiler_params=_params("parallel", "arbitrary"),
        name="ffn_down",
    )(a, w, x)


def _final_norm_kernel(x_ref, g_ref, o_ref):
    o_ref[...] = _rms_rows(x_ref[...], g_ref[...])


def _final_norm(x, g):
    s, k = x.shape
    tm = 1024
    return pl.pallas_call(
        _final_norm_kernel,
        grid=(s // tm,),
        in_specs=[pl.BlockSpec((tm, k), lambda i: (i, 0)), pl.BlockSpec((1, k), lambda i: (0, 0))],
        out_specs=pl.BlockSpec((tm, k), lambda i: (i, 0)),
        out_shape=jax.ShapeDtypeStruct((s, k), F32),
        compiler_params=_params("parallel"),
        name="final_norm",
    )(x, g.reshape(1, k))


def kernel(x, mem, positions, mix_norm, mem_norm, w_mem_kv, ffn_norm, w_gate, w_up, w_down,
           attn_w_in, attn_w_out, sgu_w_in, sgu_ln_g, sgu_ln_b, sgu_w_spatial, sgu_b_spatial,
           sgu_w_out, final_norm):
    b, s, d = x.shape
    assert (b, s, d) == (1, SEQ, D_MODEL) and mem.shape == (1, MEM_LEN, D_MODEL)
    xs = x.reshape(s, d)
    mems = mem.reshape(MEM_LEN, d)

    inv_freq = ROPE_THETA ** (-jnp.arange(ROT_HALF, dtype=F32) / ROT_HALF)
    invf = jnp.concatenate([inv_freq, inv_freq, jnp.zeros((HEAD_DIM - ROT_DIM,), F32)])
    rope = _rope_tables(positions.reshape(s, 1), invf.reshape(1, HEAD_DIM))

    for i in range(DEPTH):
        j = i // 2
        kv = _norm_matmul(mems, mem_norm[i], w_mem_kv[i], tm=MEM_LEN, tn=512)
        if i % 2 == 0:
            proj = _norm_matmul(xs, mix_norm[i], attn_w_in[j], tm=1024, tn=512,
                                rope=rope, rope_cols=2 * ATTN_W)
            mix = _dilated_attention(proj)
            q_mem_block = 3 * ATTN_W // MEM_W
            w_out = attn_w_out[j]
        else:
            proj = _norm_matmul(xs, mix_norm[i], sgu_w_in[j], tm=1024, tn=512)
            mix = _spatial_gating(proj, sgu_ln_g[j], sgu_ln_b[j], sgu_w_spatial[j],
                                  sgu_b_spatial[j].T)
            q_mem_block = 2 * SGU_W // MEM_W
            w_out = sgu_w_out[j]
        mem_out = _memory_attention(proj, q_mem_block, kv)
        xs = _out_proj(mix, mem_out, w_out, xs)
        act = _ffn_gate_up(xs, ffn_norm[i], w_gate[i], w_up[i])
        xs = _ffn_down(act, w_down[i], xs)
    return _final_norm(xs, final_norm).reshape(b, s, d)
```

```python
import functools
import math

import jax
import jax.numpy as jnp
from jax import lax
from jax.experimental import pallas as pl
from jax.experimental.pallas import tpu as pltpu

D_MODEL = 2048
SEQ = 8192
DEPTH = 2
MEM_LEN = 256
HEAD_DIM = 128
MEM_HEADS = 4
MEM_W = MEM_HEADS * HEAD_DIM
ATTN_GROUPS = ((128, 1), (512, 4), (2048, 16))
ATTN_HEADS = 12
HEADS_PER_GROUP = 4
ATTN_W = ATTN_HEADS * HEAD_DIM
ATTN_OUT_W = HEADS_PER_GROUP * HEAD_DIM
BLK = 128
SGU_GROUPS = 12
SGU_W = SGU_GROUPS * HEAD_DIM
SGU_CHUNK = 128
ROT_DIM = HEAD_DIM // 4
ROT_HALF = ROT_DIM // 2
ROPE_THETA = 500000.0
FFN_HIDDEN = 5632
NORM_EPS = 1e-6
LN_EPS = 1e-5
NEG_INF = -1e30
ATTN_SCALE = HEAD_DIM ** -0.5

N_BACK = BLK
assert all(w // d == N_BACK for w, d in ATTN_GROUPS)
ATTN_SPAN = max(d for _, d in ATTN_GROUPS) * BLK

V7X_VMEM_LIMIT_BYTES = 56 * 1024 * 1024

F32 = jnp.float32
BF16 = jnp.bfloat16


def _params(*semantics):
    return pltpu.CompilerParams(dimension_semantics=semantics,
                                vmem_limit_bytes=V7X_VMEM_LIMIT_BYTES)


def _rms_rows(x, g):
    ms = jnp.mean(x * x, axis=-1, keepdims=True)
    return x * lax.rsqrt(ms + NORM_EPS) * g


def _rope_table_kernel(pos_ref, invf_ref, c_ref, s1_ref, s2_ref):
    ang = pos_ref[...].astype(F32) * invf_ref[...]
    lane = lax.broadcasted_iota(jnp.int32, ang.shape, 1)
    cos = jnp.cos(ang)
    sin = jnp.sin(ang)
    c_ref[...] = jnp.where(lane < ROT_DIM, cos, 1.0)
    s1_ref[...] = jnp.where(lane < ROT_HALF, 0.0, jnp.where(lane < ROT_DIM, sin, 0.0))
    s2_ref[...] = jnp.where(lane < ROT_HALF, -sin, 0.0)


def _rope_tables(pos_col, invf):
    s = pos_col.shape[0]
    tm = 1024
    tab = jax.ShapeDtypeStruct((s, HEAD_DIM), F32)
    row = pl.BlockSpec((tm, HEAD_DIM), lambda i: (i, 0))
    return pl.pallas_call(
        _rope_table_kernel,
        grid=(s // tm,),
        in_specs=[pl.BlockSpec((tm, 1), lambda i: (i, 0)),
                  pl.BlockSpec((1, HEAD_DIM), lambda i: (0, 0))],
        out_specs=[row, row, row],
        out_shape=[tab, tab, tab],
        compiler_params=_params("parallel"),
        name="rope_tables",
    )(pos_col, invf)


def _norm_matmul_kernel(x_ref, g_ref, w_ref, *rest, rope_blocks):
    if rope_blocks:
        c_ref, s1_ref, s2_ref, o_ref, h_ref = rest
    else:
        o_ref, h_ref = rest
    n = pl.program_id(1)

    @pl.when(n == 0)
    def _():
        h_ref[...] = _rms_rows(x_ref[...], g_ref[...]).astype(BF16)

    acc = jnp.dot(h_ref[...], w_ref[...].astype(BF16), preferred_element_type=F32)
    if not rope_blocks:
        o_ref[...] = acc
        return

    @pl.when(n < rope_blocks)
    def _():
        c, s1, s2 = c_ref[...], s1_ref[...], s2_ref[...]
        for j in range(acc.shape[1] // HEAD_DIM):
            t = acc[:, j * HEAD_DIM:(j + 1) * HEAD_DIM]
            o_ref[:, j * HEAD_DIM:(j + 1) * HEAD_DIM] = (
                t * c + pltpu.roll(t, ROT_HALF, 1) * s1
                + pltpu.roll(t, HEAD_DIM - ROT_HALF, 1) * s2)

    @pl.when(n >= rope_blocks)
    def _():
        o_ref[...] = acc


def _gain_spec(layer, k, grid_rank):
    if grid_rank == 1:
        return pl.BlockSpec((None, 1, k), lambda i: (layer, 0, 0))
    return pl.BlockSpec((None, 1, k), lambda i, j: (layer, 0, 0))


def _gain_arg(g):
    return g.reshape(g.shape[0], 1, g.shape[1])


def _norm_matmul(x, g, g_layer, w, w_layer, *, tm, tn, rope=None, rope_cols=0):
    m, k = x.shape
    n = w.shape[2]
    assert m % tm == 0 and n % tn == 0 and rope_cols % tn == 0
    in_specs = [pl.BlockSpec((tm, k), lambda i, j: (i, 0)),
                _gain_spec(g_layer, k, 2),
                pl.BlockSpec((None, k, tn), lambda i, j: (w_layer, 0, j))]
    args = [x, _gain_arg(g), w]
    if rope is not None:
        in_specs += [pl.BlockSpec((tm, HEAD_DIM), lambda i, j: (i, 0))] * 3
        args += list(rope)
    return pl.pallas_call(
        functools.partial(_norm_matmul_kernel, rope_blocks=rope_cols // tn),
        grid=(m // tm, n // tn),
        in_specs=in_specs,
        out_specs=pl.BlockSpec((tm, tn), lambda i, j: (i, j)),
        out_shape=jax.ShapeDtypeStruct((m, n), F32),
        scratch_shapes=[pltpu.VMEM((tm, k), BF16)],
        compiler_params=_params("parallel", "arbitrary"),
        name="norm_matmul",
    )(*args)


def _rows(ref, start, size, stride):
    if stride == 1:
        return ref[pl.ds(start, size), :]
    return ref[pl.ds(start, size, stride=stride), :]


def _attn_unit(q, kc, vc, bias):
    s = lax.dot_general(q.astype(BF16), kc.astype(BF16), (((1,), (1,)), ((), ())),
                        preferred_element_type=F32)
    s = s * ATTN_SCALE + bias
    m = jnp.max(s, axis=-1, keepdims=True)
    p = jnp.exp(s - m)
    l = jnp.sum(p, axis=-1, keepdims=True)
    o = jnp.dot(p.astype(BF16), vc.astype(BF16), preferred_element_type=F32)
    return o * (1.0 / l), m + jnp.log(l)


def _attn_kernel(*refs):
    ngrp = len(ATTN_GROUPS)
    in_refs = refs[:5 * ngrp]
    o_ref = refs[5 * ngrp]
    o_scr = refs[5 * ngrp + 1:5 * ngrp + 1 + ngrp]
    l_scr = refs[5 * ngrp + 1 + ngrp:5 * ngrp + 1 + 2 * ngrp]
    bias_ref = refs[-1]

    qi = lax.broadcasted_iota(jnp.int32, (BLK, 2 * BLK), 0)
    ki = lax.broadcasted_iota(jnp.int32, (BLK, 2 * BLK), 1)
    lo = jnp.where(pl.program_id(0) == 0, BLK, 0)
    bias_ref[0] = jnp.where((ki >= qi) & (ki <= qi + N_BACK), 0.0, NEG_INF)
    bias_ref[1] = jnp.where((ki >= jnp.maximum(qi, lo)) & (ki <= qi + N_BACK), 0.0, NEG_INF)

    for g, (_, d) in enumerate(ATTN_GROUPS):
        q_ref, k_ref, v_ref, kp_ref, vp_ref = in_refs[5 * g:5 * g + 5]
        og, lg = o_scr[g], l_scr[g]
        sub = d * BLK
        nsub = ATTN_SPAN // sub

        def put(start, o, lse, og=og, lg=lg, d=d):
            if d == 1:
                idx = pl.ds(start, BLK)
            else:
                idx = pl.ds(start, BLK, stride=d)
            og[idx, :] = o
            lg[idx, :] = jnp.broadcast_to(lse, (BLK, HEAD_DIM))

        for r in range(d):
            kc = jnp.concatenate([_rows(kp_ref, r, BLK, d), _rows(k_ref, r, BLK, d)], axis=0)
            vc = jnp.concatenate([_rows(vp_ref, r, BLK, d), _rows(v_ref, r, BLK, d)], axis=0)
            o, lse = _attn_unit(_rows(q_ref, r, BLK, d), kc, vc, bias_ref[1])
            put(r, o, lse)

        if nsub > 1:
            per_iter = d if d > 1 else 3
            assert ((nsub - 1) * d) % per_iter == 0

            def body(t, carry, d=d, sub=sub, per_iter=per_iter,
                     q_ref=q_ref, k_ref=k_ref, v_ref=v_ref, put=put):
                for u in range(per_iter):
                    idx = t * per_iter + u
                    c = idx // d + 1
                    r = idx % d
                    q0 = c * sub + r
                    k0 = (c - 1) * sub + r
                    o, lse = _attn_unit(_rows(q_ref, q0, BLK, d), _rows(k_ref, k0, 2 * BLK, d),
                                        _rows(v_ref, k0, 2 * BLK, d), bias_ref[0])
                    put(q0, o, lse)
                return carry

            lax.fori_loop(0, (nsub - 1) * d // per_iter, body, 0)

    lses = [l[...] for l in l_scr]
    mx = functools.reduce(jnp.maximum, lses)
    ws = [jnp.exp(l - mx) for l in lses]
    num = functools.reduce(lambda a, b: a + b, [w * o[...] for w, o in zip(ws, o_scr)])
    den = functools.reduce(lambda a, b: a + b, ws)
    o_ref[...] = (num * (1.0 / den)).astype(o_ref.dtype)


def _dilated_attention(proj):
    s = proj.shape[0]
    assert s % ATTN_SPAN == 0
    hpg = HEADS_PER_GROUP
    in_specs = []
    for g, (_, d) in enumerate(ATTN_GROUPS):
        sub = d * BLK
        ratio = ATTN_SPAN // sub
        qc, kc, vc = g * hpg, ATTN_HEADS + g * hpg, 2 * ATTN_HEADS + g * hpg
        cur = lambda col: pl.BlockSpec((ATTN_SPAN, HEAD_DIM), lambda i, h, col=col: (i, col + h))
        prev = lambda col, ratio=ratio, sub=sub: pl.BlockSpec(
            (sub, HEAD_DIM), lambda i, h, col=col, ratio=ratio: (jnp.maximum(i * ratio - 1, 0), col + h))
        in_specs += [cur(qc), cur(kc), cur(vc), prev(kc), prev(vc)]
    ngrp = len(ATTN_GROUPS)
    return pl.pallas_call(
        _attn_kernel,
        grid=(s // ATTN_SPAN, hpg),
        in_specs=in_specs,
        out_specs=pl.BlockSpec((ATTN_SPAN, HEAD_DIM), lambda i, h: (i, h)),
        out_shape=jax.ShapeDtypeStruct((s, ATTN_OUT_W), BF16),
        scratch_shapes=([pltpu.VMEM((ATTN_SPAN, HEAD_DIM), F32)] * (2 * ngrp)
                        + [pltpu.VMEM((2, BLK, 2 * BLK), F32)]),
        compiler_params=_params("parallel", "arbitrary"),
        name="dilated_attention",
    )(*([proj] * (5 * ngrp)))


def _mem_attn_kernel(q_ref, kv_ref, o_ref):
    for h in range(MEM_HEADS):
        sl = slice(h * HEAD_DIM, (h + 1) * HEAD_DIM)
        k = kv_ref[:, sl].astype(BF16)
        v = kv_ref[:, MEM_W + h * HEAD_DIM:MEM_W + (h + 1) * HEAD_DIM].astype(BF16)
        s = lax.dot_general(q_ref[:, sl].astype(BF16), k, (((1,), (1,)), ((), ())),
                            preferred_element_type=F32) * ATTN_SCALE
        m = jnp.max(s, axis=-1, keepdims=True)
        p = jnp.exp(s - m)
        l = jnp.sum(p, axis=-1, keepdims=True)
        o = jnp.dot(p.astype(BF16), v, preferred_element_type=F32)
        o_ref[:, sl] = (o * (1.0 / l)).astype(o_ref.dtype)


def _memory_attention(proj, q_col_block, kv):
    s = proj.shape[0]
    tm = 1024
    return pl.pallas_call(
        _mem_attn_kernel,
        grid=(s // tm,),
        in_specs=[pl.BlockSpec((tm, MEM_W), lambda i: (i, q_col_block)),
                  pl.BlockSpec((MEM_LEN, 2 * MEM_W), lambda i: (0, 0))],
        out_specs=pl.BlockSpec((tm, MEM_W), lambda i: (i, 0)),
        out_shape=jax.ShapeDtypeStruct((s, MEM_W), BF16),
        compiler_params=_params("parallel"),
        name="memory_attention",
    )(proj, kv)


def _gelu_tanh(x):
    return x * (0.5 * (1.0 + jnp.tanh(math.sqrt(2.0 / math.pi) * (x + 0.044715 * (x * x * x)))))


def _sgu_kernel(u_ref, v_ref, g_ref, b_ref, ws_ref, bt_ref, o_ref, wsb_ref, vn_ref):
    @pl.when(pl.program_id(0) == 0)
    def _():
        t = lax.broadcasted_iota(jnp.int32, (SGU_CHUNK, SGU_CHUNK), 0)
        s = lax.broadcasted_iota(jnp.int32, (SGU_CHUNK, SGU_CHUNK), 1)
        for g in range(SGU_GROUPS):
            wsb_ref[g] = jnp.where(t >= s, ws_ref[g], 0.0).astype(BF16)

    v = _gelu_tanh(v_ref[...])
    mu = jnp.mean(v, axis=-1, keepdims=True)
    vc = v - mu
    var = jnp.mean(vc * vc, axis=-1, keepdims=True)
    vn_ref[...] = (vc * lax.rsqrt(var + LN_EPS) * g_ref[...] + b_ref[...]).astype(BF16)

    bt = bt_ref[...]
    for c in range(u_ref.shape[0] // SGU_CHUNK):
        rows = slice(c * SGU_CHUNK, (c + 1) * SGU_CHUNK)
        for g in range(SGU_GROUPS):
            cols = slice(g * HEAD_DIM, (g + 1) * HEAD_DIM)
            mixed = jnp.dot(wsb_ref[g], vn_ref[rows, cols], preferred_element_type=F32)
            mixed = mixed + bt[:, g:g + 1]
            o_ref[rows, cols] = (_gelu_tanh(u_ref[rows, cols]) * mixed).astype(o_ref.dtype)


def _spatial_gating(proj, ln_g, ln_b, w_spatial, b_spatial_t, layer):
    s = proj.shape[0]
    tm = 512
    return pl.pallas_call(
        _sgu_kernel,
        grid=(s // tm,),
        in_specs=[pl.BlockSpec((tm, SGU_W), lambda i: (i, 0)),
                  pl.BlockSpec((tm, SGU_W), lambda i: (i, 1)),
                  _gain_spec(layer, SGU_W, 1),
                  _gain_spec(layer, SGU_W, 1),
                  pl.BlockSpec((None, SGU_GROUPS, SGU_CHUNK, SGU_CHUNK), lambda i: (layer, 0, 0, 0)),
                  pl.BlockSpec((None, SGU_CHUNK, SGU_GROUPS), lambda i: (layer, 0, 0))],
        out_specs=pl.BlockSpec((tm, SGU_W), lambda i: (i, 0)),
        out_shape=jax.ShapeDtypeStruct((s, SGU_W), BF16),
        scratch_shapes=[pltpu.VMEM((SGU_GROUPS, SGU_CHUNK, SGU_CHUNK), BF16),
                        pltpu.VMEM((tm, SGU_W), BF16)],
        compiler_params=_params("arbitrary"),
        name="spatial_gating",
    )(proj, proj, _gain_arg(ln_g), _gain_arg(ln_b), w_spatial, b_spatial_t)


def _out_proj_kernel(a_ref, b_ref, wa_ref, wb_ref, x_ref, o_ref, wa_bf, wb_bf):
    @pl.when(pl.program_id(1) == 0)
    def _():
        wa_bf[...] = wa_ref[...].astype(BF16)
        wb_bf[...] = wb_ref[...].astype(BF16)

    o_ref[...] = (x_ref[...]
                  + jnp.dot(a_ref[...], wa_bf[...], preferred_element_type=F32)
                  + jnp.dot(b_ref[...], wb_bf[...], preferred_element_type=F32))


def _out_proj(a, b, w, layer, x):
    s, ka = a.shape
    kb = b.shape[1]
    n = w.shape[2]
    assert ka % kb == 0 and w.shape[1] == ka + kb
    tm, tn = 1024, 512
    return pl.pallas_call(
        _out_proj_kernel,
        grid=(n // tn, s // tm),
        in_specs=[pl.BlockSpec((tm, ka), lambda j, i: (i, 0)),
                  pl.BlockSpec((tm, kb), lambda j, i: (i, 0)),
                  pl.BlockSpec((None, ka, tn), lambda j, i: (layer, 0, j)),
                  pl.BlockSpec((None, kb, tn), lambda j, i: (layer, ka // kb, j)),
                  pl.BlockSpec((tm, tn), lambda j, i: (i, j))],
        out_specs=pl.BlockSpec((tm, tn), lambda j, i: (i, j)),
        out_shape=jax.ShapeDtypeStruct((s, n), F32),
        scratch_shapes=[pltpu.VMEM((ka, tn), BF16), pltpu.VMEM((kb, tn), BF16)],
        compiler_params=_params("parallel", "arbitrary"),
        name="out_proj",
    )(a, b, w, w, x)


def _ffn_gate_up_kernel(x_ref, g_ref, wg_ref, wu_ref, o_ref, h_ref):
    @pl.when(pl.program_id(1) == 0)
    def _():
        h_ref[...] = _rms_rows(x_ref[...], g_ref[...]).astype(BF16)

    h = h_ref[...]
    gate = jnp.dot(h, wg_ref[...].astype(BF16), preferred_element_type=F32)
    up = jnp.dot(h, wu_ref[...].astype(BF16), preferred_element_type=F32)
    o_ref[...] = (gate * (1.0 / (1.0 + jnp.exp(-gate))) * up).astype(o_ref.dtype)


def _ffn_gate_up(x, g, w_gate, w_up, layer):
    s, k = x.shape
    f = w_gate.shape[2]
    tm, tf = 1024, 512
    wspec = pl.BlockSpec((None, k, tf), lambda i, j: (layer, 0, j))
    return pl.pallas_call(
        _ffn_gate_up_kernel,
        grid=(s // tm, f // tf),
        in_specs=[pl.BlockSpec((tm, k), lambda i, j: (i, 0)),
                  _gain_spec(layer, k, 2),
                  wspec, wspec],
        out_specs=pl.BlockSpec((tm, tf), lambda i, j: (i, j)),
        out_shape=jax.ShapeDtypeStruct((s, f), BF16),
        scratch_shapes=[pltpu.VMEM((tm, k), BF16)],
        compiler_params=_params("parallel", "arbitrary"),
        name="ffn_gate_up",
    )(x, _gain_arg(g), w_gate, w_up)


def _ffn_down_kernel(a_ref, w_ref, x_ref, o_ref):
    o_ref[...] = x_ref[...] + jnp.dot(a_ref[...], w_ref[...].astype(BF16),
                                      preferred_element_type=F32)


def _ffn_down(a, w, layer, x):
    s, f = a.shape
    n = w.shape[2]
    tm, tn = 1024, 256
    return pl.pallas_call(
        _ffn_down_kernel,
        grid=(s // tm, n // tn),
        in_specs=[pl.BlockSpec((tm, f), lambda i, j: (i, 0)),
                  pl.BlockSpec((None, f, tn), lambda i, j: (layer, 0, j)),
                  pl.BlockSpec((tm, tn), lambda i, j: (i, j))],
        out_specs=pl.BlockSpec((tm, tn), lambda i, j: (i, j)),
        out_shape=jax.ShapeDtypeStruct((s, n), F32),
        compiler_params=_params("parallel", "arbitrary"),
        name="ffn_down",
    )(a, w, x)


def _final_norm_kernel(x_ref, g_ref, o_ref):
    o_ref[...] = _rms_rows(x_ref[...], g_ref[...])


def _final_norm(x, g, layer):
    s, k = x.shape
    tm = 1024
    return pl.pallas_call(
        _final_norm_kernel,
        grid=(s // tm,),
        in_specs=[pl.BlockSpec((tm, k), lambda i: (i, 0)), _gain_spec(layer, k, 1)],
        out_specs=pl.BlockSpec((tm, k), lambda i: (i, 0)),
        out_shape=jax.ShapeDtypeStruct((s, k), F32),
        compiler_params=_params("parallel"),
        name="final_norm",
    )(x, _gain_arg(g))


def kernel(x, mem, positions, mix_norm, mem_norm, w_mem_kv, ffn_norm, w_gate, w_up, w_down,
           attn_w_in, attn_w_out, sgu_w_in, sgu_ln_g, sgu_ln_b, sgu_w_spatial, sgu_b_spatial,
           sgu_w_out, final_norm):
    b, s, d = x.shape
    assert (b, s, d) == (1, SEQ, D_MODEL) and mem.shape == (1, MEM_LEN, D_MODEL)
    xs = x.reshape(s, d)
    mems = mem.reshape(MEM_LEN, d)

    inv_freq = ROPE_THETA ** (-jnp.arange(ROT_HALF, dtype=F32) / ROT_HALF)
    invf = jnp.concatenate([inv_freq, inv_freq, jnp.zeros((HEAD_DIM - ROT_DIM,), F32)])
    rope = _rope_tables(positions.reshape(s, 1), invf.reshape(1, HEAD_DIM))

    for i in range(DEPTH):
        j = i // 2
        kv = _norm_matmul(mems, mem_norm, i, w_mem_kv, i, tm=MEM_LEN, tn=512)
        if i % 2 == 0:
            proj = _norm_matmul(xs, mix_norm, i, attn_w_in, j, tm=1024, tn=512,
                                rope=rope, rope_cols=2 * ATTN_W)
            mix = _dilated_attention(proj)
            q_mem_block = 3 * ATTN_W // MEM_W
            w_out = attn_w_out
        else:
            proj = _norm_matmul(xs, mix_norm, i, sgu_w_in, j, tm=1024, tn=512)
            mix = _spatial_gating(proj, sgu_ln_g, sgu_ln_b, sgu_w_spatial,
                                  jnp.swapaxes(sgu_b_spatial, 1, 2), j)
            q_mem_block = 2 * SGU_W // MEM_W
            w_out = sgu_w_out
        mem_out = _memory_attention(proj, q_mem_block, kv)
        xs = _out_proj(mix, mem_out, w_out, j, xs)
        act = _ffn_gate_up(xs, ffn_norm, w_gate, w_up, i)
        xs = _ffn_down(act, w_down, i, xs)
    return _final_norm(xs, final_norm.reshape(1, d), 0).reshape(b, s, d)
```

```python
import functools
import math

import jax
import jax.numpy as jnp
from jax import lax
from jax.experimental import pallas as pl
from jax.experimental.pallas import tpu as pltpu

D_MODEL = 2048
SEQ = 8192
DEPTH = 2
MEM_LEN = 256
HEAD_DIM = 128
MEM_HEADS = 4
MEM_W = MEM_HEADS * HEAD_DIM
ATTN_GROUPS = ((128, 1), (512, 4), (2048, 16))
ATTN_HEADS = 12
HEADS_PER_GROUP = 4
ATTN_W = ATTN_HEADS * HEAD_DIM
ATTN_OUT_W = HEADS_PER_GROUP * HEAD_DIM
BLK = 128
SGU_GROUPS = 12
SGU_W = SGU_GROUPS * HEAD_DIM
SGU_CHUNK = 128
ROT_DIM = HEAD_DIM // 4
ROT_HALF = ROT_DIM // 2
ROPE_THETA = 500000.0
FFN_HIDDEN = 5632
NORM_EPS = 1e-6
LN_EPS = 1e-5
NEG_INF = -1e30
ATTN_SCALE = HEAD_DIM ** -0.5

N_BACK = BLK
assert all(w // d == N_BACK for w, d in ATTN_GROUPS)
ATTN_SPAN = max(d for _, d in ATTN_GROUPS) * BLK

V7X_VMEM_LIMIT_BYTES = 56 * 1024 * 1024
V7X_MXU_COLS = 256

F32 = jnp.float32
BF16 = jnp.bfloat16


def _params(*semantics):
    return pltpu.CompilerParams(dimension_semantics=semantics,
                                vmem_limit_bytes=V7X_VMEM_LIMIT_BYTES)


def _rms_rows(x, g):
    ms = jnp.mean(x * x, axis=-1, keepdims=True)
    return x * lax.rsqrt(ms + NORM_EPS) * g


def _gain_spec(layer, k):
    return pl.BlockSpec((None, 1, k), lambda *_: (layer, 0, 0))


def _gain_arg(g):
    return g.reshape(g.shape[0], 1, g.shape[1])


def _rope_table_kernel(pos_ref, invf_ref, c_ref, s1_ref, s2_ref):
    ang = pos_ref[...].astype(F32) * invf_ref[...]
    lane = lax.broadcasted_iota(jnp.int32, ang.shape, 1)
    cos = jnp.cos(ang)
    sin = jnp.sin(ang)
    c_ref[...] = jnp.where(lane < ROT_DIM, cos, 1.0)
    s1_ref[...] = jnp.where(lane < ROT_HALF, 0.0, jnp.where(lane < ROT_DIM, sin, 0.0))
    s2_ref[...] = jnp.where(lane < ROT_HALF, -sin, 0.0)


def _rope_tables(pos_col, invf):
    s = pos_col.shape[0]
    tm = 1024
    tab = jax.ShapeDtypeStruct((s, HEAD_DIM), F32)
    row = pl.BlockSpec((tm, HEAD_DIM), lambda i: (i, 0))
    return pl.pallas_call(
        _rope_table_kernel,
        grid=(s // tm,),
        in_specs=[pl.BlockSpec((tm, 1), lambda i: (i, 0)),
                  pl.BlockSpec((1, HEAD_DIM), lambda i: (0, 0))],
        out_specs=[row, row, row],
        out_shape=[tab, tab, tab],
        compiler_params=_params("parallel"),
        name="rope_tables",
    )(pos_col, invf)


def _norm_matmul_kernel(x_ref, g_ref, w_ref, *rest, rope):
    if rope:
        c_ref, s1_ref, s2_ref, o_ref, h_ref, wres_ref = rest
    else:
        o_ref, h_ref, wres_ref = rest
    m, n = pl.program_id(0), pl.program_id(1)

    @pl.when(n == 0)
    def _():
        h_ref[...] = _rms_rows(x_ref[...], g_ref[...]).astype(BF16)

    @pl.when(m == 0)
    def _():
        wres_ref[n] = w_ref[...].astype(BF16)

    for c in range(o_ref.shape[1] // V7X_MXU_COLS):
        cols = slice(c * V7X_MXU_COLS, (c + 1) * V7X_MXU_COLS)
        acc = jnp.dot(h_ref[...], wres_ref[n, :, cols], preferred_element_type=F32)
        if not rope:
            o_ref[:, cols] = acc
            continue
        cs, s1, s2 = c_ref[...], s1_ref[...], s2_ref[...]
        for j in range(V7X_MXU_COLS // HEAD_DIM):
            t = acc[:, j * HEAD_DIM:(j + 1) * HEAD_DIM]
            lo = c * V7X_MXU_COLS + j * HEAD_DIM
            o_ref[:, lo:lo + HEAD_DIM] = (t * cs + pltpu.roll(t, ROT_HALF, 1) * s1
                                          + pltpu.roll(t, HEAD_DIM - ROT_HALF, 1) * s2)


def _norm_matmul(x, g, g_layer, w, w_layer, col0, n, *, tm, tn, rope=None):
    m, k = x.shape
    assert m % tm == 0 and n % tn == 0 and col0 % tn == 0 and tn % V7X_MXU_COLS == 0
    nn = n // tn
    tile0 = col0 // tn
    in_specs = [pl.BlockSpec((tm, k), lambda i, j: (i, 0)),
                _gain_spec(g_layer, k),
                pl.BlockSpec((None, k, tn),
                             lambda i, j: (w_layer, 0, tile0 + jnp.where(i == 0, j, nn - 1)))]
    args = [x, _gain_arg(g), w]
    if rope is not None:
        in_specs += [pl.BlockSpec((tm, HEAD_DIM), lambda i, j: (i, 0))] * 3
        args += list(rope)
    return pl.pallas_call(
        functools.partial(_norm_matmul_kernel, rope=rope is not None),
        grid=(m // tm, nn),
        in_specs=in_specs,
        out_specs=pl.BlockSpec((tm, tn), lambda i, j: (i, j)),
        out_shape=jax.ShapeDtypeStruct((m, n), F32),
        scratch_shapes=[pltpu.VMEM((tm, k), BF16), pltpu.VMEM((nn, k, tn), BF16)],
        compiler_params=_params("arbitrary", "arbitrary"),
        name="norm_matmul",
    )(*args)


def _rows(ref, start, size, stride):
    if stride == 1:
        return ref[pl.ds(start, size), :]
    return ref[pl.ds(start, size, stride=stride), :]


def _attn_unit(q, kc, vc, bias):
    s = lax.dot_general(q.astype(BF16), kc.astype(BF16), (((1,), (1,)), ((), ())),
                        preferred_element_type=F32)
    s = s * ATTN_SCALE + bias
    m = jnp.max(s, axis=-1, keepdims=True)
    p = jnp.exp(s - m)
    l = jnp.sum(p, axis=-1, keepdims=True)
    o = jnp.dot(p.astype(BF16), vc.astype(BF16), preferred_element_type=F32)
    return o * (1.0 / l), m + jnp.log(l)


def _attn_kernel(*refs):
    ngrp = len(ATTN_GROUPS)
    in_refs = refs[:5 * ngrp]
    o_ref = refs[5 * ngrp]
    o_scr = refs[5 * ngrp + 1:5 * ngrp + 1 + ngrp]
    l_scr = refs[5 * ngrp + 1 + ngrp:5 * ngrp + 1 + 2 * ngrp]
    bias_ref = refs[-1]

    qi = lax.broadcasted_iota(jnp.int32, (BLK, 2 * BLK), 0)
    ki = lax.broadcasted_iota(jnp.int32, (BLK, 2 * BLK), 1)
    lo = jnp.where(pl.program_id(0) == 0, BLK, 0)
    bias_ref[0] = jnp.where((ki >= qi) & (ki <= qi + N_BACK), 0.0, NEG_INF)
    bias_ref[1] = jnp.where((ki >= jnp.maximum(qi, lo)) & (ki <= qi + N_BACK), 0.0, NEG_INF)

    for g, (_, d) in enumerate(ATTN_GROUPS):
        q_ref, k_ref, v_ref, kp_ref, vp_ref = in_refs[5 * g:5 * g + 5]
        og, lg = o_scr[g], l_scr[g]
        sub = d * BLK
        nsub = ATTN_SPAN // sub

        def put(start, o, lse, og=og, lg=lg, d=d):
            if d == 1:
                idx = pl.ds(start, BLK)
            else:
                idx = pl.ds(start, BLK, stride=d)
            og[idx, :] = o
            lg[idx, :] = jnp.broadcast_to(lse, (BLK, HEAD_DIM))

        for r in range(d):
            kc = jnp.concatenate([_rows(kp_ref, r, BLK, d), _rows(k_ref, r, BLK, d)], axis=0)
            vc = jnp.concatenate([_rows(vp_ref, r, BLK, d), _rows(v_ref, r, BLK, d)], axis=0)
            o, lse = _attn_unit(_rows(q_ref, r, BLK, d), kc, vc, bias_ref[1])
            put(r, o, lse)

        if nsub > 1:
            per_iter = d if d > 1 else 3
            assert ((nsub - 1) * d) % per_iter == 0

            def body(t, carry, d=d, sub=sub, per_iter=per_iter,
                     q_ref=q_ref, k_ref=k_ref, v_ref=v_ref, put=put):
                for u in range(per_iter):
                    idx = t * per_iter + u
                    c = idx // d + 1
                    r = idx % d
                    q0 = c * sub + r
                    k0 = (c - 1) * sub + r
                    o, lse = _attn_unit(_rows(q_ref, q0, BLK, d), _rows(k_ref, k0, 2 * BLK, d),
                                        _rows(v_ref, k0, 2 * BLK, d), bias_ref[0])
                    put(q0, o, lse)
                return carry

            lax.fori_loop(0, (nsub - 1) * d // per_iter, body, 0)

    lses = [l[...] for l in l_scr]
    mx = functools.reduce(jnp.maximum, lses)
    ws = [jnp.exp(l - mx) for l in lses]
    num = functools.reduce(lambda a, b: a + b, [w * o[...] for w, o in zip(ws, o_scr)])
    den = functools.reduce(lambda a, b: a + b, ws)
    o_ref[...] = (num * (1.0 / den)).astype(o_ref.dtype)


def _dilated_attention(qk, vm):
    s = qk.shape[0]
    assert s % ATTN_SPAN == 0
    hpg = HEADS_PER_GROUP
    in_specs, args = [], []
    for g, (_, d) in enumerate(ATTN_GROUPS):
        sub = d * BLK
        ratio = ATTN_SPAN // sub
        qc, kc, vc = g * hpg, ATTN_HEADS + g * hpg, g * hpg
        cur = lambda col: pl.BlockSpec((ATTN_SPAN, HEAD_DIM), lambda i, h, col=col: (i, col + h))
        prev = lambda col, ratio=ratio, sub=sub: pl.BlockSpec(
            (sub, HEAD_DIM), lambda i, h, col=col, ratio=ratio: (jnp.maximum(i * ratio - 1, 0), col + h))
        in_specs += [cur(qc), cur(kc), cur(vc), prev(kc), prev(vc)]
        args += [qk, qk, vm, qk, vm]
    ngrp = len(ATTN_GROUPS)
    return pl.pallas_call(
        _attn_kernel,
        grid=(s // ATTN_SPAN, hpg),
        in_specs=in_specs,
        out_specs=pl.BlockSpec((ATTN_SPAN, HEAD_DIM), lambda i, h: (i, h)),
        out_shape=jax.ShapeDtypeStruct((s, ATTN_OUT_W), BF16),
        scratch_shapes=([pltpu.VMEM((ATTN_SPAN, HEAD_DIM), F32)] * (2 * ngrp)
                        + [pltpu.VMEM((2, BLK, 2 * BLK), F32)]),
        compiler_params=_params("parallel", "arbitrary"),
        name="dilated_attention",
    )(*args)


def _mem_attn_kernel(q_ref, kv_ref, o_ref):
    for h in range(MEM_HEADS):
        sl = slice(h * HEAD_DIM, (h + 1) * HEAD_DIM)
        k = kv_ref[:, sl].astype(BF16)
        v = kv_ref[:, MEM_W + h * HEAD_DIM:MEM_W + (h + 1) * HEAD_DIM].astype(BF16)
        s = lax.dot_general(q_ref[:, sl].astype(BF16), k, (((1,), (1,)), ((), ())),
                            preferred_element_type=F32) * ATTN_SCALE
        m = jnp.max(s, axis=-1, keepdims=True)
        p = jnp.exp(s - m)
        l = jnp.sum(p, axis=-1, keepdims=True)
        o = jnp.dot(p.astype(BF16), v, preferred_element_type=F32)
        o_ref[:, sl] = (o * (1.0 / l)).astype(o_ref.dtype)


def _memory_attention(proj, q_col_block, kv):
    s = proj.shape[0]
    tm = 1024
    return pl.pallas_call(
        _mem_attn_kernel,
        grid=(s // tm,),
        in_specs=[pl.BlockSpec((tm, MEM_W), lambda i: (i, q_col_block)),
                  pl.BlockSpec((MEM_LEN, 2 * MEM_W), lambda i: (0, 0))],
        out_specs=pl.BlockSpec((tm, MEM_W), lambda i: (i, 0)),
        out_shape=jax.ShapeDtypeStruct((s, MEM_W), BF16),
        compiler_params=_params("parallel"),
        name="memory_attention",
    )(proj, kv)


def _gelu_tanh(x):
    return x * (0.5 * (1.0 + jnp.tanh(math.sqrt(2.0 / math.pi) * (x + 0.044715 * (x * x * x)))))


def _sgu_kernel(u_ref, v_ref, g_ref, b_ref, ws_ref, bt_ref, o_ref, wsb_ref, vn_ref):
    @pl.when(pl.program_id(0) == 0)
    def _():
        t = lax.broadcasted_iota(jnp.int32, (SGU_CHUNK, SGU_CHUNK), 0)
        s = lax.broadcasted_iota(jnp.int32, (SGU_CHUNK, SGU_CHUNK), 1)
        for g in range(SGU_GROUPS):
            wsb_ref[g] = jnp.where(t >= s, ws_ref[g], 0.0).astype(BF16)

    v = _gelu_tanh(v_ref[...])
    mu = jnp.mean(v, axis=-1, keepdims=True)
    vc = v - mu
    var = jnp.mean(vc * vc, axis=-1, keepdims=True)
    vn_ref[...] = (vc * lax.rsqrt(var + LN_EPS) * g_ref[...] + b_ref[...]).astype(BF16)

    bt = bt_ref[...]
    for c in range(u_ref.shape[0] // SGU_CHUNK):
        rows = slice(c * SGU_CHUNK, (c + 1) * SGU_CHUNK)
        for g in range(SGU_GROUPS):
            cols = slice(g * HEAD_DIM, (g + 1) * HEAD_DIM)
            mixed = jnp.dot(wsb_ref[g], vn_ref[rows, cols], preferred_element_type=F32)
            mixed = mixed + bt[:, g:g + 1]
            o_ref[rows, cols] = (_gelu_tanh(u_ref[rows, cols]) * mixed).astype(o_ref.dtype)


def _spatial_gating(proj, ln_g, ln_b, w_spatial, b_spatial_t, layer):
    s = proj.shape[0]
    tm = 512
    return pl.pallas_call(
        _sgu_kernel,
        grid=(s // tm,),
        in_specs=[pl.BlockSpec((tm, SGU_W), lambda i: (i, 0)),
                  pl.BlockSpec((tm, SGU_W), lambda i: (i, 1)),
                  _gain_spec(layer, SGU_W),
                  _gain_spec(layer, SGU_W),
                  pl.BlockSpec((None, SGU_GROUPS, SGU_CHUNK, SGU_CHUNK), lambda i: (layer, 0, 0, 0)),
                  pl.BlockSpec((None, SGU_CHUNK, SGU_GROUPS), lambda i: (layer, 0, 0))],
        out_specs=pl.BlockSpec((tm, SGU_W), lambda i: (i, 0)),
        out_shape=jax.ShapeDtypeStruct((s, SGU_W), BF16),
        scratch_shapes=[pltpu.VMEM((SGU_GROUPS, SGU_CHUNK, SGU_CHUNK), BF16),
                        pltpu.VMEM((tm, SGU_W), BF16)],
        compiler_params=_params("arbitrary"),
        name="spatial_gating",
    )(proj, proj, _gain_arg(ln_g), _gain_arg(ln_b), w_spatial, b_spatial_t)


def _resid_matmul_kernel(*refs, part_tiles, nk, n_chunk, res_cw, resident, norm):
    nparts = len(part_tiles)
    a_refs = refs[:nparts]
    w_ref, x_ref = refs[nparts], refs[nparts + 1]
    pos = nparts + 2
    g_ref = None
    if norm:
        g_ref = refs[pos]
        pos += 1
    o_ref = refs[pos]
    pos += 1
    hn_ref = None
    if norm == "emit":
        hn_ref = refs[pos]
        pos += 1
    wres_ref = refs[pos] if resident else None
    m, k = pl.program_id(0), pl.program_id(1)
    width = o_ref.shape[1]

    if resident:
        @pl.when(m == 0)
        def _():
            wres_ref[k] = w_ref[...].astype(BF16)

    def lhs():
        a = a_refs[0][...]
        bound = part_tiles[0]
        for p in range(1, nparts):
            a = jnp.where(k < bound, a, a_refs[p][...])
            bound += part_tiles[p]
        return a

    def w_cols(cols):
        if resident:
            return wres_ref[k, :, cols]
        return w_ref[:, cols].astype(BF16)

    def accumulate(first):
        for c in range(width // n_chunk):
            cols = slice(c * n_chunk, (c + 1) * n_chunk)
            part = jnp.dot(lhs(), w_cols(cols), preferred_element_type=F32)
            if first:
                o_ref[:, cols] = part
            else:
                o_ref[:, cols] += part

    pl.when(k == 0)(lambda: accumulate(True))
    pl.when(k > 0)(lambda: accumulate(False))

    for c in range(width // res_cw):
        @pl.when(k == c)
        def _(c=c):
            o_ref[:, c * res_cw:(c + 1) * res_cw] += x_ref[...]

    if norm:
        @pl.when(k == nk - 1)
        def _():
            y = _rms_rows(o_ref[...], g_ref[...])
            if norm == "emit":
                hn_ref[...] = y.astype(BF16)
            else:
                o_ref[...] = y


def _resid_matmul(parts, w, w_layer, x, *, tm, tk, res_cw, resident, norm=None, g=None, g_layer=0,
                  name):
    s, width = x.shape
    part_tiles = tuple(p.shape[1] // tk for p in parts)
    assert all(p.shape[1] % tk == 0 for p in parts)
    nk = sum(part_tiles)
    assert w.shape[1] == nk * tk and w.shape[2] == width
    n_res = width // res_cw
    assert s % tm == 0 and width % res_cw == 0 and n_res <= nk
    n_chunk = 512
    in_specs, off = [], 0
    for p, tiles in zip(parts, part_tiles):
        in_specs.append(pl.BlockSpec(
            (tm, tk), lambda i, k, off=off, tiles=tiles: (i, jnp.clip(k - off, 0, tiles - 1))))
        off += tiles
    if resident:
        in_specs.append(pl.BlockSpec((None, tk, width),
                                     lambda i, k: (w_layer, jnp.where(i == 0, k, nk - 1), 0)))
    else:
        in_specs.append(pl.BlockSpec((None, tk, width), lambda i, k: (w_layer, k, 0)))
    in_specs.append(pl.BlockSpec((tm, res_cw), lambda i, k: (i, jnp.minimum(k, n_res - 1))))
    args = list(parts) + [w, x]
    if norm:
        in_specs.append(_gain_spec(g_layer, width))
        args.append(_gain_arg(g))
    row_spec = pl.BlockSpec((tm, width), lambda i, k: (i, 0))
    out_specs = [row_spec]
    out_shape = [jax.ShapeDtypeStruct((s, width), F32)]
    if norm == "emit":
        out_specs.append(row_spec)
        out_shape.append(jax.ShapeDtypeStruct((s, width), BF16))
    scratch = [pltpu.VMEM((nk, tk, width), BF16)] if resident else []
    return pl.pallas_call(
        functools.partial(_resid_matmul_kernel, part_tiles=part_tiles, nk=nk, n_chunk=n_chunk,
                          res_cw=res_cw, resident=resident, norm=norm),
        grid=(s // tm, nk),
        in_specs=in_specs,
        out_specs=out_specs,
        out_shape=out_shape,
        scratch_shapes=scratch,
        compiler_params=_params("arbitrary", "arbitrary"),
        name=name,
    )(*args)


def _ffn_gate_up_kernel(h_ref, wg_ref, wu_ref, o_ref):
    for c in range(o_ref.shape[1] // V7X_MXU_COLS):
        cols = slice(c * V7X_MXU_COLS, (c + 1) * V7X_MXU_COLS)
        gate = jnp.dot(h_ref[...], wg_ref[:, cols].astype(BF16), preferred_element_type=F32)
        up = jnp.dot(h_ref[...], wu_ref[:, cols].astype(BF16), preferred_element_type=F32)
        o_ref[:, cols] = (gate * (1.0 / (1.0 + jnp.exp(-gate))) * up).astype(o_ref.dtype)


def _ffn_gate_up(h, w_gate, w_up, layer):
    s, k = h.shape
    f = w_gate.shape[2]
    tm, tf = 2048, 512
    wspec = pl.BlockSpec((None, k, tf), lambda i, j: (layer, 0, j))
    return pl.pallas_call(
        _ffn_gate_up_kernel,
        grid=(s // tm, f // tf),
        in_specs=[pl.BlockSpec((tm, k), lambda i, j: (i, 0)), wspec, wspec],
        out_specs=pl.BlockSpec((tm, tf), lambda i, j: (i, j)),
        out_shape=jax.ShapeDtypeStruct((s, f), BF16),
        compiler_params=_params("parallel", "arbitrary"),
        name="ffn_gate_up",
    )(h, w_gate, w_up)


def kernel(x, mem, positions, mix_norm, mem_norm, w_mem_kv, ffn_norm, w_gate, w_up, w_down,
           attn_w_in, attn_w_out, sgu_w_in, sgu_ln_g, sgu_ln_b, sgu_w_spatial, sgu_b_spatial,
           sgu_w_out, final_norm):
    b, s, d = x.shape
    assert (b, s, d) == (1, SEQ, D_MODEL) and mem.shape == (1, MEM_LEN, D_MODEL)
    xs = x.reshape(s, d)
    mems = mem.reshape(MEM_LEN, d)

    inv_freq = ROPE_THETA ** (-jnp.arange(ROT_HALF, dtype=F32) / ROT_HALF)
    invf = jnp.concatenate([inv_freq, inv_freq, jnp.zeros((HEAD_DIM - ROT_DIM,), F32)])
    rope = _rope_tables(positions.reshape(s, 1), invf.reshape(1, HEAD_DIM))

    for i in range(DEPTH):
        j = i // 2
        last = i == DEPTH - 1
        kv = _norm_matmul(mems, mem_norm, i, w_mem_kv, i, 0, 2 * MEM_W, tm=MEM_LEN, tn=512)
        if i % 2 == 0:
            qk = _norm_matmul(xs, mix_norm, i, attn_w_in, j, 0, 2 * ATTN_W, tm=1024, tn=512, rope=rope)
            vm = _norm_matmul(xs, mix_norm, i, attn_w_in, j, 2 * ATTN_W, ATTN_W + MEM_W,
                              tm=1024, tn=512)
            mix = _dilated_attention(qk, vm)
            mem_out = _memory_attention(vm, ATTN_W // MEM_W, kv)
            w_out, tk = attn_w_out, 256
        else:
            proj = _norm_matmul(xs, mix_norm, i, sgu_w_in, j, 0, 2 * SGU_W + MEM_W, tm=1024, tn=512)
            mix = _spatial_gating(proj, sgu_ln_g, sgu_ln_b, sgu_w_spatial,
                                  jnp.swapaxes(sgu_b_spatial, 1, 2), j)
            mem_out = _memory_attention(proj, 2 * SGU_W // MEM_W, kv)
            w_out, tk = sgu_w_out, 512
        xs, hn = _resid_matmul([mix, mem_out], w_out, j, xs, tm=1024, tk=tk, res_cw=512,
                               resident=True, norm="emit", g=ffn_norm, g_layer=i, name="out_proj")
        act = _ffn_gate_up(hn, w_gate, w_up, i)
        if last:
            g_fin = final_norm.reshape(1, d)
            xs, = _resid_matmul([act], w_down, i, xs, tm=2048, tk=512, res_cw=256, resident=False,
                                norm="inplace", g=g_fin, g_layer=0, name="ffn_down")
        else:
            xs, = _resid_matmul([act], w_down, i, xs, tm=2048, tk=512, res_cw=256, resident=False,
                                name="ffn_down")
    return xs.reshape(b, s, d)
```

```python
import functools
import math

import jax
import jax.numpy as jnp
from jax import lax
from jax.experimental import pallas as pl
from jax.experimental.pallas import tpu as pltpu

D_MODEL = 2048
SEQ = 8192
DEPTH = 2
MEM_LEN = 256
HEAD_DIM = 128
MEM_HEADS = 4
MEM_W = MEM_HEADS * HEAD_DIM
ATTN_GROUPS = ((128, 1), (512, 4), (2048, 16))
ATTN_HEADS = 12
HEADS_PER_GROUP = 4
ATTN_W = ATTN_HEADS * HEAD_DIM
ATTN_OUT_W = HEADS_PER_GROUP * HEAD_DIM
BLK = 128
SGU_GROUPS = 12
SGU_W = SGU_GROUPS * HEAD_DIM
SGU_CHUNK = 128
ROT_DIM = HEAD_DIM // 4
ROT_HALF = ROT_DIM // 2
ROPE_THETA = 500000.0
FFN_HIDDEN = 5632
NORM_EPS = 1e-6
LN_EPS = 1e-5
NEG_INF = -1e30
ATTN_SCALE = HEAD_DIM ** -0.5

N_BACK = BLK
assert all(w // d == N_BACK for w, d in ATTN_GROUPS)
ATTN_SPAN = max(d for _, d in ATTN_GROUPS) * BLK

V7X_VMEM_LIMIT_BYTES = 56 * 1024 * 1024
V7X_MXU_COLS = 256

F32 = jnp.float32
BF16 = jnp.bfloat16


def _params(*semantics):
    return pltpu.CompilerParams(dimension_semantics=semantics,
                                vmem_limit_bytes=V7X_VMEM_LIMIT_BYTES)


def _rms_rows(x, g):
    ms = jnp.mean(x * x, axis=-1, keepdims=True)
    return x * lax.rsqrt(ms + NORM_EPS) * g


def _gain_spec(layer, k):
    return pl.BlockSpec((None, 1, k), lambda *_: (layer, 0, 0))


def _gain_arg(g):
    return g.reshape(g.shape[0], 1, g.shape[1])


def _rope_table_kernel(pos_ref, invf_ref, c_ref, s1_ref, s2_ref):
    ang = pos_ref[...].astype(F32) * invf_ref[...]
    lane = lax.broadcasted_iota(jnp.int32, ang.shape, 1)
    cos = jnp.cos(ang)
    sin = jnp.sin(ang)
    c_ref[...] = jnp.where(lane < ROT_DIM, cos, 1.0)
    s1_ref[...] = jnp.where(lane < ROT_HALF, 0.0, jnp.where(lane < ROT_DIM, sin, 0.0))
    s2_ref[...] = jnp.where(lane < ROT_HALF, -sin, 0.0)


def _rope_tables(pos_col, invf):
    s = pos_col.shape[0]
    tm = 1024
    tab = jax.ShapeDtypeStruct((s, HEAD_DIM), F32)
    row = pl.BlockSpec((tm, HEAD_DIM), lambda i: (i, 0))
    return pl.pallas_call(
        _rope_table_kernel,
        grid=(s // tm,),
        in_specs=[pl.BlockSpec((tm, 1), lambda i: (i, 0)),
                  pl.BlockSpec((1, HEAD_DIM), lambda i: (0, 0))],
        out_specs=[row, row, row],
        out_shape=[tab, tab, tab],
        compiler_params=_params("parallel"),
        name="rope_tables",
    )(pos_col, invf)


def _norm_matmul_kernel(x_ref, g_ref, w_ref, *rest, rope):
    if rope:
        c_ref, s1_ref, s2_ref, o_ref, h_ref, wres_ref = rest
    else:
        o_ref, h_ref, wres_ref = rest
    m, n = pl.program_id(0), pl.program_id(1)

    @pl.when(n == 0)
    def _():
        h_ref[...] = _rms_rows(x_ref[...], g_ref[...]).astype(BF16)

    @pl.when(m == 0)
    def _():
        wres_ref[n] = w_ref[...].astype(BF16)

    for c in range(o_ref.shape[1] // V7X_MXU_COLS):
        cols = slice(c * V7X_MXU_COLS, (c + 1) * V7X_MXU_COLS)
        acc = jnp.dot(h_ref[...], wres_ref[n, :, cols], preferred_element_type=F32)
        if not rope:
            o_ref[:, cols] = acc
            continue
        cs, s1, s2 = c_ref[...], s1_ref[...], s2_ref[...]
        for j in range(V7X_MXU_COLS // HEAD_DIM):
            t = acc[:, j * HEAD_DIM:(j + 1) * HEAD_DIM]
            lo = c * V7X_MXU_COLS + j * HEAD_DIM
            o_ref[:, lo:lo + HEAD_DIM] = (t * cs + pltpu.roll(t, ROT_HALF, 1) * s1
                                          + pltpu.roll(t, HEAD_DIM - ROT_HALF, 1) * s2)


def _norm_matmul(x, g, g_layer, w, w_layer, col0, n, *, tm, tn, rope=None):
    m, k = x.shape
    assert m % tm == 0 and n % tn == 0 and col0 % tn == 0 and tn % V7X_MXU_COLS == 0
    nn = n // tn
    tile0 = col0 // tn
    in_specs = [pl.BlockSpec((tm, k), lambda i, j: (i, 0)),
                _gain_spec(g_layer, k),
                pl.BlockSpec((None, k, tn),
                             lambda i, j: (w_layer, 0, tile0 + jnp.where(i == 0, j, nn - 1)))]
    args = [x, _gain_arg(g), w]
    if rope is not None:
        in_specs += [pl.BlockSpec((tm, HEAD_DIM), lambda i, j: (i, 0))] * 3
        args += list(rope)
    return pl.pallas_call(
        functools.partial(_norm_matmul_kernel, rope=rope is not None),
        grid=(m // tm, nn),
        in_specs=in_specs,
        out_specs=pl.BlockSpec((tm, tn), lambda i, j: (i, j)),
        out_shape=jax.ShapeDtypeStruct((m, n), F32),
        scratch_shapes=[pltpu.VMEM((tm, k), BF16), pltpu.VMEM((nn, k, tn), BF16)],
        compiler_params=_params("arbitrary", "arbitrary"),
        name="norm_matmul",
    )(*args)


def _cast_kernel(w_ref, o_ref):
    o_ref[...] = w_ref[...].astype(BF16)


def _cast_bf16(w, layer):
    _, k, n = w.shape
    tk = 256
    assert k % tk == 0
    return pl.pallas_call(
        _cast_kernel,
        grid=(k // tk,),
        in_specs=[pl.BlockSpec((None, tk, n), lambda i: (layer, i, 0))],
        out_specs=pl.BlockSpec((tk, n), lambda i: (i, 0)),
        out_shape=jax.ShapeDtypeStruct((k, n), BF16),
        compiler_params=_params("parallel"),
        name="cast_bf16",
    )(w)


def _resident_spec(shape):
    return pl.BlockSpec(shape, lambda *_: (0,) * len(shape), pipeline_mode=pl.Buffered(1))


def _in_proj_kernel(x_ref, g_ref, w_ref, *rest, widths, rope_cols):
    if rope_cols:
        c_ref, s1_ref, s2_ref = rest[:3]
        rest = rest[3:]
    o_refs, h_ref = rest[:len(widths)], rest[len(widths)]
    h_ref[...] = _rms_rows(x_ref[...], g_ref[...]).astype(BF16)
    col = 0
    for o_ref, width in zip(o_refs, widths):
        for c in range(width // V7X_MXU_COLS):
            acc = jnp.dot(h_ref[...], w_ref[:, col:col + V7X_MXU_COLS], preferred_element_type=F32)
            for j in range(V7X_MXU_COLS // HEAD_DIM):
                t = acc[:, j * HEAD_DIM:(j + 1) * HEAD_DIM]
                if col < rope_cols:
                    t = (t * c_ref[...] + pltpu.roll(t, ROT_HALF, 1) * s1_ref[...]
                         + pltpu.roll(t, HEAD_DIM - ROT_HALF, 1) * s2_ref[...])
                lo = c * V7X_MXU_COLS + j * HEAD_DIM
                o_ref[:, lo:lo + HEAD_DIM] = t
            col += V7X_MXU_COLS


def _in_proj(x, g, g_layer, w_bf, widths, *, rope=None, rope_cols=0):
    s, k = x.shape
    n = w_bf.shape[1]
    tm = 512
    assert sum(widths) == n and all(w % V7X_MXU_COLS == 0 for w in widths)
    assert rope_cols % V7X_MXU_COLS == 0 and (rope is not None) == bool(rope_cols)
    in_specs = [pl.BlockSpec((tm, k), lambda i: (i, 0)), _gain_spec(g_layer, k),
                _resident_spec((k, n))]
    args = [x, _gain_arg(g), w_bf]
    if rope is not None:
        in_specs += [pl.BlockSpec((tm, HEAD_DIM), lambda i: (i, 0))] * 3
        args += list(rope)
    return pl.pallas_call(
        functools.partial(_in_proj_kernel, widths=tuple(widths), rope_cols=rope_cols),
        grid=(s // tm,),
        in_specs=in_specs,
        out_specs=[pl.BlockSpec((tm, w), lambda i: (i, 0)) for w in widths],
        out_shape=[jax.ShapeDtypeStruct((s, w), F32) for w in widths],
        scratch_shapes=[pltpu.VMEM((tm, k), BF16)],
        compiler_params=_params("parallel"),
        name="in_proj",
    )(*args)


def _out_proj_kernel(a_ref, b_ref, w_ref, x_ref, g_ref, o_ref, hn_ref):
    ka = a_ref.shape[1]
    width = o_ref.shape[1]
    n_chunk = 512
    ssq = jnp.zeros((o_ref.shape[0], 1), F32)
    for c in range(width // n_chunk):
        cols = slice(c * n_chunk, (c + 1) * n_chunk)
        y = (x_ref[:, cols]
             + jnp.dot(a_ref[...], w_ref[:ka, cols], preferred_element_type=F32)
             + jnp.dot(b_ref[...], w_ref[ka:, cols], preferred_element_type=F32))
        o_ref[:, cols] = y
        ssq = ssq + jnp.sum(y * y, axis=-1, keepdims=True)
    scale = lax.rsqrt(ssq * (1.0 / width) + NORM_EPS)
    for c in range(width // n_chunk):
        cols = slice(c * n_chunk, (c + 1) * n_chunk)
        hn_ref[:, cols] = (o_ref[:, cols] * scale * g_ref[:, cols]).astype(BF16)


def _out_proj(a, b, w_bf, x, g, g_layer):
    s, ka = a.shape
    kb = b.shape[1]
    width = x.shape[1]
    tm = 512
    assert w_bf.shape == (ka + kb, width)
    row = lambda cols: pl.BlockSpec((tm, cols), lambda i: (i, 0))
    return pl.pallas_call(
        _out_proj_kernel,
        grid=(s // tm,),
        in_specs=[row(ka), row(kb), _resident_spec((ka + kb, width)), row(width),
                  _gain_spec(g_layer, width)],
        out_specs=[row(width), row(width)],
        out_shape=[jax.ShapeDtypeStruct((s, width), F32), jax.ShapeDtypeStruct((s, width), BF16)],
        compiler_params=_params("parallel"),
        name="out_proj",
    )(a, b, w_bf, x, _gain_arg(g))


def _rows(ref, start, size, stride):
    if stride == 1:
        return ref[pl.ds(start, size), :]
    return ref[pl.ds(start, size, stride=stride), :]


def _attn_unit(q, kc, vc, bias):
    s = lax.dot_general(q.astype(BF16), kc.astype(BF16), (((1,), (1,)), ((), ())),
                        preferred_element_type=F32)
    s = s * ATTN_SCALE + bias
    m = jnp.max(s, axis=-1, keepdims=True)
    p = jnp.exp(s - m)
    l = jnp.sum(p, axis=-1, keepdims=True)
    o = jnp.dot(p.astype(BF16), vc.astype(BF16), preferred_element_type=F32)
    return o * (1.0 / l), m + jnp.log(l)


def _attn_kernel(*refs):
    ngrp = len(ATTN_GROUPS)
    in_refs = refs[:5 * ngrp]
    o_ref = refs[5 * ngrp]
    o_scr = refs[5 * ngrp + 1:5 * ngrp + 1 + ngrp]
    l_scr = refs[5 * ngrp + 1 + ngrp:5 * ngrp + 1 + 2 * ngrp]
    bias_ref = refs[-1]

    qi = lax.broadcasted_iota(jnp.int32, (BLK, 2 * BLK), 0)
    ki = lax.broadcasted_iota(jnp.int32, (BLK, 2 * BLK), 1)
    lo = jnp.where(pl.program_id(0) == 0, BLK, 0)
    bias_ref[0] = jnp.where((ki >= qi) & (ki <= qi + N_BACK), 0.0, NEG_INF)
    bias_ref[1] = jnp.where((ki >= jnp.maximum(qi, lo)) & (ki <= qi + N_BACK), 0.0, NEG_INF)

    for g, (_, d) in enumerate(ATTN_GROUPS):
        q_ref, k_ref, v_ref, kp_ref, vp_ref = in_refs[5 * g:5 * g + 5]
        og, lg = o_scr[g], l_scr[g]
        sub = d * BLK
        nsub = ATTN_SPAN // sub

        def put(start, o, lse, og=og, lg=lg, d=d):
            if d == 1:
                idx = pl.ds(start, BLK)
            else:
                idx = pl.ds(start, BLK, stride=d)
            og[idx, :] = o
            lg[idx, :] = jnp.broadcast_to(lse, (BLK, HEAD_DIM))

        for r in range(d):
            kc = jnp.concatenate([_rows(kp_ref, r, BLK, d), _rows(k_ref, r, BLK, d)], axis=0)
            vc = jnp.concatenate([_rows(vp_ref, r, BLK, d), _rows(v_ref, r, BLK, d)], axis=0)
            o, lse = _attn_unit(_rows(q_ref, r, BLK, d), kc, vc, bias_ref[1])
            put(r, o, lse)

        if nsub > 1:
            per_iter = d if d > 1 else 3
            assert ((nsub - 1) * d) % per_iter == 0

            def body(t, carry, d=d, sub=sub, per_iter=per_iter,
                     q_ref=q_ref, k_ref=k_ref, v_ref=v_ref, put=put):
                for u in range(per_iter):
                    idx = t * per_iter + u
                    c = idx // d + 1
                    r = idx % d
                    q0 = c * sub + r
                    k0 = (c - 1) * sub + r
                    o, lse = _attn_unit(_rows(q_ref, q0, BLK, d), _rows(k_ref, k0, 2 * BLK, d),
                                        _rows(v_ref, k0, 2 * BLK, d), bias_ref[0])
                    put(q0, o, lse)
                return carry

            lax.fori_loop(0, (nsub - 1) * d // per_iter, body, 0)

    lses = [l[...] for l in l_scr]
    mx = functools.reduce(jnp.maximum, lses)
    ws = [jnp.exp(l - mx) for l in lses]
    num = functools.reduce(lambda a, b: a + b, [w * o[...] for w, o in zip(ws, o_scr)])
    den = functools.reduce(lambda a, b: a + b, ws)
    o_ref[...] = (num * (1.0 / den)).astype(o_ref.dtype)


def _dilated_attention(qk, vm):
    s = qk.shape[0]
    assert s % ATTN_SPAN == 0
    hpg = HEADS_PER_GROUP
    in_specs, args = [], []
    for g, (_, d) in enumerate(ATTN_GROUPS):
        sub = d * BLK
        ratio = ATTN_SPAN // sub
        qc, kc, vc = g * hpg, ATTN_HEADS + g * hpg, g * hpg
        cur = lambda col: pl.BlockSpec((ATTN_SPAN, HEAD_DIM), lambda i, h, col=col: (i, col + h))
        prev = lambda col, ratio=ratio, sub=sub: pl.BlockSpec(
            (sub, HEAD_DIM), lambda i, h, col=col, ratio=ratio: (jnp.maximum(i * ratio - 1, 0), col + h))
        in_specs += [cur(qc), cur(kc), cur(vc), prev(kc), prev(vc)]
        args += [qk, qk, vm, qk, vm]
    ngrp = len(ATTN_GROUPS)
    return pl.pallas_call(
        _attn_kernel,
        grid=(s // ATTN_SPAN, hpg),
        in_specs=in_specs,
        out_specs=pl.BlockSpec((ATTN_SPAN, HEAD_DIM), lambda i, h: (i, h)),
        out_shape=jax.ShapeDtypeStruct((s, ATTN_OUT_W), BF16),
        scratch_shapes=([pltpu.VMEM((ATTN_SPAN, HEAD_DIM), F32)] * (2 * ngrp)
                        + [pltpu.VMEM((2, BLK, 2 * BLK), F32)]),
        compiler_params=_params("parallel", "arbitrary"),
        name="dilated_attention",
    )(*args)


def _mem_attn_kernel(q_ref, kv_ref, o_ref):
    for h in range(MEM_HEADS):
        sl = slice(h * HEAD_DIM, (h + 1) * HEAD_DIM)
        k = kv_ref[:, sl].astype(BF16)
        v = kv_ref[:, MEM_W + h * HEAD_DIM:MEM_W + (h + 1) * HEAD_DIM].astype(BF16)
        s = lax.dot_general(q_ref[:, sl].astype(BF16), k, (((1,), (1,)), ((), ())),
                            preferred_element_type=F32) * ATTN_SCALE
        m = jnp.max(s, axis=-1, keepdims=True)
        p = jnp.exp(s - m)
        l = jnp.sum(p, axis=-1, keepdims=True)
        o = jnp.dot(p.astype(BF16), v, preferred_element_type=F32)
        o_ref[:, sl] = (o * (1.0 / l)).astype(o_ref.dtype)


def _memory_attention(proj, q_col_block, kv):
    s = proj.shape[0]
    tm = 1024
    return pl.pallas_call(
        _mem_attn_kernel,
        grid=(s // tm,),
        in_specs=[pl.BlockSpec((tm, MEM_W), lambda i: (i, q_col_block)),
                  pl.BlockSpec((MEM_LEN, 2 * MEM_W), lambda i: (0, 0))],
        out_specs=pl.BlockSpec((tm, MEM_W), lambda i: (i, 0)),
        out_shape=jax.ShapeDtypeStruct((s, MEM_W), BF16),
        compiler_params=_params("parallel"),
        name="memory_attention",
    )(proj, kv)


def _gelu_tanh(x):
    return x * (0.5 * (1.0 + jnp.tanh(math.sqrt(2.0 / math.pi) * (x + 0.044715 * (x * x * x)))))


def _sgu_kernel(u_ref, v_ref, g_ref, b_ref, ws_ref, bt_ref, o_ref, wsb_ref, vn_ref):
    @pl.when(pl.program_id(0) == 0)
    def _():
        t = lax.broadcasted_iota(jnp.int32, (SGU_CHUNK, SGU_CHUNK), 0)
        s = lax.broadcasted_iota(jnp.int32, (SGU_CHUNK, SGU_CHUNK), 1)
        for g in range(SGU_GROUPS):
            wsb_ref[g] = jnp.where(t >= s, ws_ref[g], 0.0).astype(BF16)

    v = _gelu_tanh(v_ref[...])
    mu = jnp.mean(v, axis=-1, keepdims=True)
    vc = v - mu
    var = jnp.mean(vc * vc, axis=-1, keepdims=True)
    vn_ref[...] = (vc * lax.rsqrt(var + LN_EPS) * g_ref[...] + b_ref[...]).astype(BF16)

    bt = bt_ref[...]
    for c in range(u_ref.shape[0] // SGU_CHUNK):
        rows = slice(c * SGU_CHUNK, (c + 1) * SGU_CHUNK)
        for g in range(SGU_GROUPS):
            cols = slice(g * HEAD_DIM, (g + 1) * HEAD_DIM)
            mixed = jnp.dot(wsb_ref[g], vn_ref[rows, cols], preferred_element_type=F32)
            mixed = mixed + bt[:, g:g + 1]
            o_ref[rows, cols] = (_gelu_tanh(u_ref[rows, cols]) * mixed).astype(o_ref.dtype)


def _spatial_gating(proj, ln_g, ln_b, w_spatial, b_spatial_t, layer):
    s = proj.shape[0]
    tm = 512
    return pl.pallas_call(
        _sgu_kernel,
        grid=(s // tm,),
        in_specs=[pl.BlockSpec((tm, SGU_W), lambda i: (i, 0)),
                  pl.BlockSpec((tm, SGU_W), lambda i: (i, 1)),
                  _gain_spec(layer, SGU_W),
                  _gain_spec(layer, SGU_W),
                  pl.BlockSpec((None, SGU_GROUPS, SGU_CHUNK, SGU_CHUNK), lambda i: (layer, 0, 0, 0)),
                  pl.BlockSpec((None, SGU_CHUNK, SGU_GROUPS), lambda i: (layer, 0, 0))],
        out_specs=pl.BlockSpec((tm, SGU_W), lambda i: (i, 0)),
        out_shape=jax.ShapeDtypeStruct((s, SGU_W), BF16),
        scratch_shapes=[pltpu.VMEM((SGU_GROUPS, SGU_CHUNK, SGU_CHUNK), BF16),
                        pltpu.VMEM((tm, SGU_W), BF16)],
        compiler_params=_params("arbitrary"),
        name="spatial_gating",
    )(proj, proj, _gain_arg(ln_g), _gain_arg(ln_b), w_spatial, b_spatial_t)


def _resid_matmul_kernel(*refs, part_tiles, nk, n_chunk, res_cw, resident, norm):
    nparts = len(part_tiles)
    a_refs = refs[:nparts]
    w_ref, x_ref = refs[nparts], refs[nparts + 1]
    pos = nparts + 2
    g_ref = None
    if norm:
        g_ref = refs[pos]
        pos += 1
    o_ref = refs[pos]
    pos += 1
    hn_ref = None
    if norm == "emit":
        hn_ref = refs[pos]
        pos += 1
    wres_ref = refs[pos] if resident else None
    m, k = pl.program_id(0), pl.program_id(1)
    width = o_ref.shape[1]

    if resident:
        @pl.when(m == 0)
        def _():
            wres_ref[k] = w_ref[...].astype(BF16)

    def lhs():
        a = a_refs[0][...]
        bound = part_tiles[0]
        for p in range(1, nparts):
            a = jnp.where(k < bound, a, a_refs[p][...])
            bound += part_tiles[p]
        return a

    def w_cols(cols):
        if resident:
            return wres_ref[k, :, cols]
        return w_ref[:, cols].astype(BF16)

    def accumulate(first):
        for c in range(width // n_chunk):
            cols = slice(c * n_chunk, (c + 1) * n_chunk)
            part = jnp.dot(lhs(), w_cols(cols), preferred_element_type=F32)
            if first:
                o_ref[:, cols] = part
            else:
                o_ref[:, cols] += part

    pl.when(k == 0)(lambda: accumulate(True))
    pl.when(k > 0)(lambda: accumulate(False))

    for c in range(width // res_cw):
        @pl.when(k == c)
        def _(c=c):
            o_ref[:, c * res_cw:(c + 1) * res_cw] += x_ref[...]

    if norm:
        @pl.when(k == nk - 1)
        def _():
            y = _rms_rows(o_ref[...], g_ref[...])
            if norm == "emit":
                hn_ref[...] = y.astype(BF16)
            else:
                o_ref[...] = y


def _resid_matmul(parts, w, w_layer, x, *, tm, tk, res_cw, resident, norm=None, g=None, g_layer=0,
                  name):
    s, width = x.shape
    part_tiles = tuple(p.shape[1] // tk for p in parts)
    assert all(p.shape[1] % tk == 0 for p in parts)
    nk = sum(part_tiles)
    assert w.shape[1] == nk * tk and w.shape[2] == width
    n_res = width // res_cw
    assert s % tm == 0 and width % res_cw == 0 and n_res <= nk
    n_chunk = 512
    in_specs, off = [], 0
    for p, tiles in zip(parts, part_tiles):
        in_specs.append(pl.BlockSpec(
            (tm, tk), lambda i, k, off=off, tiles=tiles: (i, jnp.clip(k - off, 0, tiles - 1))))
        off += tiles
    if resident:
        in_specs.append(pl.BlockSpec((None, tk, width),
                                     lambda i, k: (w_layer, jnp.where(i == 0, k, nk - 1), 0)))
    else:
        in_specs.append(pl.BlockSpec((None, tk, width), lambda i, k: (w_layer, k, 0)))
    in_specs.append(pl.BlockSpec((tm, res_cw), lambda i, k: (i, jnp.minimum(k, n_res - 1))))
    args = list(parts) + [w, x]
    if norm:
        in_specs.append(_gain_spec(g_layer, width))
        args.append(_gain_arg(g))
    row_spec = pl.BlockSpec((tm, width), lambda i, k: (i, 0))
    out_specs = [row_spec]
    out_shape = [jax.ShapeDtypeStruct((s, width), F32)]
    if norm == "emit":
        out_specs.append(row_spec)
        out_shape.append(jax.ShapeDtypeStruct((s, width), BF16))
    scratch = [pltpu.VMEM((nk, tk, width), BF16)] if resident else []
    return pl.pallas_call(
        functools.partial(_resid_matmul_kernel, part_tiles=part_tiles, nk=nk, n_chunk=n_chunk,
                          res_cw=res_cw, resident=resident, norm=norm),
        grid=(s // tm, nk),
        in_specs=in_specs,
        out_specs=out_specs,
        out_shape=out_shape,
        scratch_shapes=scratch,
        compiler_params=_params("arbitrary", "arbitrary"),
        name=name,
    )(*args)


def _ffn_gate_up_kernel(h_ref, wg_ref, wu_ref, o_ref):
    for c in range(o_ref.shape[1] // V7X_MXU_COLS):
        cols = slice(c * V7X_MXU_COLS, (c + 1) * V7X_MXU_COLS)
        gate = jnp.dot(h_ref[...], wg_ref[:, cols].astype(BF16), preferred_element_type=F32)
        up = jnp.dot(h_ref[...], wu_ref[:, cols].astype(BF16), preferred_element_type=F32)
        o_ref[:, cols] = (gate * (1.0 / (1.0 + jnp.exp(-gate))) * up).astype(o_ref.dtype)


def _ffn_gate_up(h, w_gate, w_up, layer):
    s, k = h.shape
    f = w_gate.shape[2]
    tm, tf = 2048, 512
    wspec = pl.BlockSpec((None, k, tf), lambda i, j: (layer, 0, j))
    return pl.pallas_call(
        _ffn_gate_up_kernel,
        grid=(s // tm, f // tf),
        in_specs=[pl.BlockSpec((tm, k), lambda i, j: (i, 0)), wspec, wspec],
        out_specs=pl.BlockSpec((tm, tf), lambda i, j: (i, j)),
        out_shape=jax.ShapeDtypeStruct((s, f), BF16),
        compiler_params=_params("parallel", "arbitrary"),
        name="ffn_gate_up",
    )(h, w_gate, w_up)


def kernel(x, mem, positions, mix_norm, mem_norm, w_mem_kv, ffn_norm, w_gate, w_up, w_down,
           attn_w_in, attn_w_out, sgu_w_in, sgu_ln_g, sgu_ln_b, sgu_w_spatial, sgu_b_spatial,
           sgu_w_out, final_norm):
    b, s, d = x.shape
    assert (b, s, d) == (1, SEQ, D_MODEL) and mem.shape == (1, MEM_LEN, D_MODEL)
    xs = x.reshape(s, d)
    mems = mem.reshape(MEM_LEN, d)

    inv_freq = ROPE_THETA ** (-jnp.arange(ROT_HALF, dtype=F32) / ROT_HALF)
    invf = jnp.concatenate([inv_freq, inv_freq, jnp.zeros((HEAD_DIM - ROT_DIM,), F32)])
    rope = _rope_tables(positions.reshape(s, 1), invf.reshape(1, HEAD_DIM))

    for i in range(DEPTH):
        j = i // 2
        last = i == DEPTH - 1
        kv = _norm_matmul(mems, mem_norm, i, w_mem_kv, i, 0, 2 * MEM_W, tm=MEM_LEN, tn=512)
        if i % 2 == 0:
            qk, vm = _in_proj(xs, mix_norm, i, _cast_bf16(attn_w_in, j), (2 * ATTN_W, ATTN_W + MEM_W),
                              rope=rope, rope_cols=2 * ATTN_W)
            mix = _dilated_attention(qk, vm)
            mem_out = _memory_attention(vm, ATTN_W // MEM_W, kv)
            w_out = attn_w_out
        else:
            proj, = _in_proj(xs, mix_norm, i, _cast_bf16(sgu_w_in, j), (2 * SGU_W + MEM_W,))
            mix = _spatial_gating(proj, sgu_ln_g, sgu_ln_b, sgu_w_spatial,
                                  jnp.swapaxes(sgu_b_spatial, 1, 2), j)
            mem_out = _memory_attention(proj, 2 * SGU_W // MEM_W, kv)
            w_out = sgu_w_out
        xs, hn = _out_proj(mix, mem_out, _cast_bf16(w_out, j), xs, ffn_norm, i)
        act = _ffn_gate_up(hn, w_gate, w_up, i)
        if last:
            g_fin = final_norm.reshape(1, d)
            xs, = _resid_matmul([act], w_down, i, xs, tm=2048, tk=512, res_cw=256, resident=False,
                                norm="inplace", g=g_fin, g_layer=0, name="ffn_down")
        else:
            xs, = _resid_matmul([act], w_down, i, xs, tm=2048, tk=512, res_cw=256, resident=False,
                                name="ffn_down")
    return xs.reshape(b, s, d)
```

```python
import functools
import math

import jax
import jax.numpy as jnp
from jax import lax
from jax.experimental import pallas as pl
from jax.experimental.pallas import tpu as pltpu

D_MODEL = 2048
SEQ = 8192
DEPTH = 2
MEM_LEN = 256
HEAD_DIM = 128
MEM_HEADS = 4
MEM_W = MEM_HEADS * HEAD_DIM
ATTN_GROUPS = ((128, 1), (512, 4), (2048, 16))
ATTN_HEADS = 12
HEADS_PER_GROUP = 4
ATTN_W = ATTN_HEADS * HEAD_DIM
ATTN_OUT_W = HEADS_PER_GROUP * HEAD_DIM
BLK = 128
SGU_GROUPS = 12
SGU_W = SGU_GROUPS * HEAD_DIM
SGU_CHUNK = 128
ROT_DIM = HEAD_DIM // 4
ROT_HALF = ROT_DIM // 2
ROPE_THETA = 500000.0
FFN_HIDDEN = 5632
NORM_EPS = 1e-6
LN_EPS = 1e-5
NEG_INF = -1e30
ATTN_SCALE = HEAD_DIM ** -0.5

N_BACK = BLK
assert all(w // d == N_BACK for w, d in ATTN_GROUPS)
ATTN_SPAN = max(d for _, d in ATTN_GROUPS) * BLK

V7X_VMEM_LIMIT_BYTES = 56 * 1024 * 1024
V7X_MXU_COLS = 256

F32 = jnp.float32
BF16 = jnp.bfloat16


def _params(*semantics):
    return pltpu.CompilerParams(dimension_semantics=semantics,
                                vmem_limit_bytes=V7X_VMEM_LIMIT_BYTES)


def _rms_rows(x, g):
    ms = jnp.mean(x * x, axis=-1, keepdims=True)
    return x * lax.rsqrt(ms + NORM_EPS) * g


def _gain_spec(layer, k):
    return pl.BlockSpec((None, 1, k), lambda *_: (layer, 0, 0))


def _gain_arg(g):
    return g.reshape(g.shape[0], 1, g.shape[1])


def _rope_table_kernel(pos_ref, invf_ref, c_ref, s1_ref, s2_ref):
    ang = pos_ref[...].astype(F32) * invf_ref[...]
    lane = lax.broadcasted_iota(jnp.int32, ang.shape, 1)
    cos = jnp.cos(ang)
    sin = jnp.sin(ang)
    c_ref[...] = jnp.where(lane < ROT_DIM, cos, 1.0)
    s1_ref[...] = jnp.where(lane < ROT_HALF, 0.0, jnp.where(lane < ROT_DIM, sin, 0.0))
    s2_ref[...] = jnp.where(lane < ROT_HALF, -sin, 0.0)


def _rope_tables(pos_col, invf):
    s = pos_col.shape[0]
    tm = 1024
    tab = jax.ShapeDtypeStruct((s, HEAD_DIM), F32)
    row = pl.BlockSpec((tm, HEAD_DIM), lambda i: (i, 0))
    return pl.pallas_call(
        _rope_table_kernel,
        grid=(s // tm,),
        in_specs=[pl.BlockSpec((tm, 1), lambda i: (i, 0)),
                  pl.BlockSpec((1, HEAD_DIM), lambda i: (0, 0))],
        out_specs=[row, row, row],
        out_shape=[tab, tab, tab],
        compiler_params=_params("parallel"),
        name="rope_tables",
    )(pos_col, invf)


def _norm_matmul_kernel(x_ref, g_ref, w_ref, *rest, rope):
    if rope:
        c_ref, s1_ref, s2_ref, o_ref, h_ref, wres_ref = rest
    else:
        o_ref, h_ref, wres_ref = rest
    m, n = pl.program_id(0), pl.program_id(1)

    @pl.when(n == 0)
    def _():
        h_ref[...] = _rms_rows(x_ref[...], g_ref[...]).astype(BF16)

    @pl.when(m == 0)
    def _():
        wres_ref[n] = w_ref[...].astype(BF16)

    for c in range(o_ref.shape[1] // V7X_MXU_COLS):
        cols = slice(c * V7X_MXU_COLS, (c + 1) * V7X_MXU_COLS)
        acc = jnp.dot(h_ref[...], wres_ref[n, :, cols], preferred_element_type=F32)
        if not rope:
            o_ref[:, cols] = acc
            continue
        cs, s1, s2 = c_ref[...], s1_ref[...], s2_ref[...]
        for j in range(V7X_MXU_COLS // HEAD_DIM):
            t = acc[:, j * HEAD_DIM:(j + 1) * HEAD_DIM]
            lo = c * V7X_MXU_COLS + j * HEAD_DIM
            o_ref[:, lo:lo + HEAD_DIM] = (t * cs + pltpu.roll(t, ROT_HALF, 1) * s1
                                          + pltpu.roll(t, HEAD_DIM - ROT_HALF, 1) * s2)


def _norm_matmul(x, g, g_layer, w, w_layer, col0, n, *, tm, tn, rope=None):
    m, k = x.shape
    assert m % tm == 0 and n % tn == 0 and col0 % tn == 0 and tn % V7X_MXU_COLS == 0
    nn = n // tn
    tile0 = col0 // tn
    in_specs = [pl.BlockSpec((tm, k), lambda i, j: (i, 0)),
                _gain_spec(g_layer, k),
                pl.BlockSpec((None, k, tn),
                             lambda i, j: (w_layer, 0, tile0 + jnp.where(i == 0, j, nn - 1)))]
    args = [x, _gain_arg(g), w]
    if rope is not None:
        in_specs += [pl.BlockSpec((tm, HEAD_DIM), lambda i, j: (i, 0))] * 3
        args += list(rope)
    return pl.pallas_call(
        functools.partial(_norm_matmul_kernel, rope=rope is not None),
        grid=(m // tm, nn),
        in_specs=in_specs,
        out_specs=pl.BlockSpec((tm, tn), lambda i, j: (i, j)),
        out_shape=jax.ShapeDtypeStruct((m, n), F32),
        scratch_shapes=[pltpu.VMEM((tm, k), BF16), pltpu.VMEM((nn, k, tn), BF16)],
        compiler_params=_params("arbitrary", "arbitrary"),
        name="norm_matmul",
    )(*args)


def _cast_kernel(w_ref, o_ref):
    o_ref[...] = w_ref[...].astype(BF16)


def _cast_bf16(w, layer):
    _, k, n = w.shape
    tk = 256
    assert k % tk == 0
    return pl.pallas_call(
        _cast_kernel,
        grid=(k // tk,),
        in_specs=[pl.BlockSpec((None, tk, n), lambda i: (layer, i, 0))],
        out_specs=pl.BlockSpec((tk, n), lambda i: (i, 0)),
        out_shape=jax.ShapeDtypeStruct((k, n), BF16),
        compiler_params=_params("parallel"),
        name="cast_bf16",
    )(w)


def _resident_spec(shape):
    return pl.BlockSpec(shape, lambda *_: (0,) * len(shape), pipeline_mode=pl.Buffered(1))


def _in_proj_kernel(x_ref, g_ref, w_ref, *rest, widths, rope_cols):
    if rope_cols:
        c_ref, s1_ref, s2_ref = rest[:3]
        rest = rest[3:]
    o_refs, h_ref = rest[:len(widths)], rest[len(widths)]
    h_ref[...] = _rms_rows(x_ref[...], g_ref[...]).astype(BF16)
    col = 0
    for o_ref, width in zip(o_refs, widths):
        for c in range(width // V7X_MXU_COLS):
            acc = jnp.dot(h_ref[...], w_ref[:, col:col + V7X_MXU_COLS], preferred_element_type=F32)
            for j in range(V7X_MXU_COLS // HEAD_DIM):
                t = acc[:, j * HEAD_DIM:(j + 1) * HEAD_DIM]
                if col < rope_cols:
                    t = (t * c_ref[...] + pltpu.roll(t, ROT_HALF, 1) * s1_ref[...]
                         + pltpu.roll(t, HEAD_DIM - ROT_HALF, 1) * s2_ref[...])
                lo = c * V7X_MXU_COLS + j * HEAD_DIM
                o_ref[:, lo:lo + HEAD_DIM] = t
            col += V7X_MXU_COLS


def _in_proj(x, g, g_layer, w_bf, widths, *, rope=None, rope_cols=0):
    s, k = x.shape
    n = w_bf.shape[1]
    tm = 512
    assert sum(widths) == n and all(w % V7X_MXU_COLS == 0 for w in widths)
    assert rope_cols % V7X_MXU_COLS == 0 and (rope is not None) == bool(rope_cols)
    in_specs = [pl.BlockSpec((tm, k), lambda i: (i, 0)), _gain_spec(g_layer, k),
                _resident_spec((k, n))]
    args = [x, _gain_arg(g), w_bf]
    if rope is not None:
        in_specs += [pl.BlockSpec((tm, HEAD_DIM), lambda i: (i, 0))] * 3
        args += list(rope)
    return pl.pallas_call(
        functools.partial(_in_proj_kernel, widths=tuple(widths), rope_cols=rope_cols),
        grid=(s // tm,),
        in_specs=in_specs,
        out_specs=[pl.BlockSpec((tm, w), lambda i: (i, 0)) for w in widths],
        out_shape=[jax.ShapeDtypeStruct((s, w), F32) for w in widths],
        scratch_shapes=[pltpu.VMEM((tm, k), BF16)],
        compiler_params=_params("parallel"),
        name="in_proj",
    )(*args)


def _out_proj_kernel(a_ref, b_ref, w_ref, x_ref, g_ref, o_ref, hn_ref):
    ka = a_ref.shape[1]
    width = o_ref.shape[1]
    n_chunk = 512
    ssq = jnp.zeros((o_ref.shape[0], 1), F32)
    for c in range(width // n_chunk):
        cols = slice(c * n_chunk, (c + 1) * n_chunk)
        y = (x_ref[:, cols]
             + jnp.dot(a_ref[...], w_ref[:ka, cols], preferred_element_type=F32)
             + jnp.dot(b_ref[...], w_ref[ka:, cols], preferred_element_type=F32))
        o_ref[:, cols] = y
        ssq = ssq + jnp.sum(y * y, axis=-1, keepdims=True)
    scale = lax.rsqrt(ssq * (1.0 / width) + NORM_EPS)
    for c in range(width // n_chunk):
        cols = slice(c * n_chunk, (c + 1) * n_chunk)
        hn_ref[:, cols] = (o_ref[:, cols] * scale * g_ref[:, cols]).astype(BF16)


def _out_proj(a, b, w_bf, x, g, g_layer):
    s, ka = a.shape
    kb = b.shape[1]
    width = x.shape[1]
    tm = 512
    assert w_bf.shape == (ka + kb, width)
    row = lambda cols: pl.BlockSpec((tm, cols), lambda i: (i, 0))
    return pl.pallas_call(
        _out_proj_kernel,
        grid=(s // tm,),
        in_specs=[row(ka), row(kb), _resident_spec((ka + kb, width)), row(width),
                  _gain_spec(g_layer, width)],
        out_specs=[row(width), row(width)],
        out_shape=[jax.ShapeDtypeStruct((s, width), F32), jax.ShapeDtypeStruct((s, width), BF16)],
        compiler_params=_params("parallel"),
        name="out_proj",
    )(a, b, w_bf, x, _gain_arg(g))


def _rows(ref, start, size, stride):
    if stride == 1:
        return ref[pl.ds(start, size), :]
    return ref[pl.ds(start, size, stride=stride), :]


ATTN_UNITS_PER_BATCH = 16
V7X_FREE_SUBLANE_STRIDE = 4


def _presplit(d):
    return d // V7X_FREE_SUBLANE_STRIDE if d > V7X_FREE_SUBLANE_STRIDE else 1


def _attn_batch(units):
    scores = [lax.dot_general(q.astype(BF16), kc.astype(BF16), (((1,), (1,)), ((), ())),
                              preferred_element_type=F32) * ATTN_SCALE + bias
              for q, kc, _, bias in units]
    probs = []
    for s in scores:
        m = jnp.max(s, axis=-1, keepdims=True)
        p = jnp.exp(s - m)
        probs.append((p, m, jnp.sum(p, axis=-1, keepdims=True)))
    outs = []
    for (p, m, l), (_, _, vc, _) in zip(probs, units):
        o = jnp.dot(p.astype(BF16), vc.astype(BF16), preferred_element_type=F32)
        outs.append((o * (1.0 / l), m + jnp.log(l)))
    return outs


def _attn_kernel(*refs):
    ngrp = len(ATTN_GROUPS)
    in_refs = refs[:5 * ngrp]
    o_ref = refs[5 * ngrp]
    pos = 5 * ngrp + 1
    o_scr, l_scr = refs[pos:pos + ngrp], refs[pos + ngrp:pos + 2 * ngrp]
    k_scr, v_scr = refs[pos + 2 * ngrp:pos + 3 * ngrp], refs[pos + 3 * ngrp:pos + 4 * ngrp]
    split_refs = refs[pos + 4 * ngrp:pos + 4 * ngrp + 3]
    bias_ref = refs[-1]

    qi = lax.broadcasted_iota(jnp.int32, (BLK, 2 * BLK), 0)
    ki = lax.broadcasted_iota(jnp.int32, (BLK, 2 * BLK), 1)
    lo = jnp.where(pl.program_id(0) == 0, BLK, 0)
    bias_ref[0] = jnp.where((ki >= qi) & (ki <= qi + N_BACK), 0.0, NEG_INF)
    bias_ref[1] = jnp.where((ki >= jnp.maximum(qi, lo)) & (ki <= qi + N_BACK), 0.0, NEG_INF)

    for g, (_, d) in enumerate(ATTN_GROUPS):
        q_ref, k_ref, v_ref, kp_ref, vp_ref = in_refs[5 * g:5 * g + 5]
        og, lg, kcat, vcat = o_scr[g], l_scr[g], k_scr[g], v_scr[g]
        sub = d * BLK
        n_units = ATTN_SPAN // BLK
        assert n_units % ATTN_UNITS_PER_BATCH == 0
        f = _presplit(d)
        d2 = d // f
        klen, qlen, sublen = (sub + ATTN_SPAN) // f, ATTN_SPAN // f, sub // f

        for rf in range(f):
            for src_p, src, cat in ((kp_ref, k_ref, kcat), (vp_ref, v_ref, vcat)):
                cat[pl.ds(rf * klen, sublen), :] = _rows(src_p, rf, sublen, f)
                cat[pl.ds(rf * klen + sublen, qlen), :] = _rows(src, rf, qlen, f)
            if f > 1:
                split_refs[0][pl.ds(rf * qlen, qlen), :] = _rows(q_ref, rf, qlen, f)
        q_src, o_dst, l_dst = (split_refs[0], split_refs[1], split_refs[2]) if f > 1 else (q_ref, og, lg)

        def body(t, carry, d=d, f=f, d2=d2, klen=klen, qlen=qlen, sublen=sublen,
                 q_src=q_src, kcat=kcat, vcat=vcat, o_dst=o_dst, l_dst=l_dst):
            units, starts = [], []
            for u in range(ATTN_UNITS_PER_BATCH):
                idx = t * ATTN_UNITS_PER_BATCH + u
                r = idx % d
                base = (idx // d) * sublen + r // f
                qstart = (r % f) * qlen + base
                kstart = (r % f) * klen + base
                bias = bias_ref[jnp.where(idx < d, 1, 0)]
                units.append((_rows(q_src, qstart, BLK, d2), _rows(kcat, kstart, 2 * BLK, d2),
                              _rows(vcat, kstart, 2 * BLK, d2), bias))
                starts.append(qstart)
            for qstart, (o, lse) in zip(starts, _attn_batch(units)):
                idx = pl.ds(qstart, BLK) if d2 == 1 else pl.ds(qstart, BLK, stride=d2)
                o_dst[idx, :] = o
                l_dst[idx, :] = jnp.broadcast_to(lse, (BLK, HEAD_DIM))
            return carry

        lax.fori_loop(0, n_units // ATTN_UNITS_PER_BATCH, body, 0)
        if f > 1:
            for rf in range(f):
                og[pl.ds(rf, qlen, stride=f), :] = o_dst[pl.ds(rf * qlen, qlen), :]
                lg[pl.ds(rf, qlen, stride=f), :] = l_dst[pl.ds(rf * qlen, qlen), :]

    lses = [l[...] for l in l_scr]
    mx = functools.reduce(jnp.maximum, lses)
    ws = [jnp.exp(l - mx) for l in lses]
    num = functools.reduce(lambda a, b: a + b, [w * o[...] for w, o in zip(ws, o_scr)])
    den = functools.reduce(lambda a, b: a + b, ws)
    o_ref[...] = (num * (1.0 / den)).astype(o_ref.dtype)


def _dilated_attention(qk, vm):
    s = qk.shape[0]
    assert s % ATTN_SPAN == 0
    hpg = HEADS_PER_GROUP
    in_specs, args = [], []
    for g, (_, d) in enumerate(ATTN_GROUPS):
        sub = d * BLK
        ratio = ATTN_SPAN // sub
        qc, kc, vc = g * hpg, ATTN_HEADS + g * hpg, g * hpg
        cur = lambda col: pl.BlockSpec((ATTN_SPAN, HEAD_DIM), lambda i, h, col=col: (i, col + h))
        prev = lambda col, ratio=ratio, sub=sub: pl.BlockSpec(
            (sub, HEAD_DIM), lambda i, h, col=col, ratio=ratio: (jnp.maximum(i * ratio - 1, 0), col + h))
        in_specs += [cur(qc), cur(kc), cur(vc), prev(kc), prev(vc)]
        args += [qk, qk, vm, qk, vm]
    ngrp = len(ATTN_GROUPS)
    return pl.pallas_call(
        _attn_kernel,
        grid=(s // ATTN_SPAN, hpg),
        in_specs=in_specs,
        out_specs=pl.BlockSpec((ATTN_SPAN, HEAD_DIM), lambda i, h: (i, h)),
        out_shape=jax.ShapeDtypeStruct((s, ATTN_OUT_W), BF16),
        scratch_shapes=([pltpu.VMEM((ATTN_SPAN, HEAD_DIM), F32)] * (2 * ngrp)
                        + [pltpu.VMEM((d * BLK + ATTN_SPAN, HEAD_DIM), F32) for _, d in ATTN_GROUPS] * 2
                        + [pltpu.VMEM((ATTN_SPAN, HEAD_DIM), F32)] * 3
                        + [pltpu.VMEM((2, BLK, 2 * BLK), F32)]),
        compiler_params=_params("parallel", "arbitrary"),
        name="dilated_attention",
    )(*args)


def _mem_attn_kernel(q_ref, kv_ref, o_ref):
    for h in range(MEM_HEADS):
        sl = slice(h * HEAD_DIM, (h + 1) * HEAD_DIM)
        k = kv_ref[:, sl].astype(BF16)
        v = kv_ref[:, MEM_W + h * HEAD_DIM:MEM_W + (h + 1) * HEAD_DIM].astype(BF16)
        s = lax.dot_general(q_ref[:, sl].astype(BF16), k, (((1,), (1,)), ((), ())),
                            preferred_element_type=F32) * ATTN_SCALE
        m = jnp.max(s, axis=-1, keepdims=True)
        p = jnp.exp(s - m)
        l = jnp.sum(p, axis=-1, keepdims=True)
        o = jnp.dot(p.astype(BF16), v, preferred_element_type=F32)
        o_ref[:, sl] = (o * (1.0 / l)).astype(o_ref.dtype)


def _memory_attention(proj, q_col_block, kv):
    s = proj.shape[0]
    tm = 1024
    return pl.pallas_call(
        _mem_attn_kernel,
        grid=(s // tm,),
        in_specs=[pl.BlockSpec((tm, MEM_W), lambda i: (i, q_col_block)),
                  pl.BlockSpec((MEM_LEN, 2 * MEM_W), lambda i: (0, 0))],
        out_specs=pl.BlockSpec((tm, MEM_W), lambda i: (i, 0)),
        out_shape=jax.ShapeDtypeStruct((s, MEM_W), BF16),
        compiler_params=_params("parallel"),
        name="memory_attention",
    )(proj, kv)


def _gelu_tanh(x):
    return x * (0.5 * (1.0 + jnp.tanh(math.sqrt(2.0 / math.pi) * (x + 0.044715 * (x * x * x)))))


def _sgu_kernel(u_ref, v_ref, g_ref, b_ref, ws_ref, bt_ref, o_ref, wsb_ref, vn_ref):
    @pl.when(pl.program_id(0) == 0)
    def _():
        t = lax.broadcasted_iota(jnp.int32, (SGU_CHUNK, SGU_CHUNK), 0)
        s = lax.broadcasted_iota(jnp.int32, (SGU_CHUNK, SGU_CHUNK), 1)
        for g in range(SGU_GROUPS):
            wsb_ref[g] = jnp.where(t >= s, ws_ref[g], 0.0).astype(BF16)

    v = _gelu_tanh(v_ref[...])
    mu = jnp.mean(v, axis=-1, keepdims=True)
    vc = v - mu
    var = jnp.mean(vc * vc, axis=-1, keepdims=True)
    vn_ref[...] = (vc * lax.rsqrt(var + LN_EPS) * g_ref[...] + b_ref[...]).astype(BF16)

    bt = bt_ref[...]
    for c in range(u_ref.shape[0] // SGU_CHUNK):
        rows = slice(c * SGU_CHUNK, (c + 1) * SGU_CHUNK)
        for g in range(SGU_GROUPS):
            cols = slice(g * HEAD_DIM, (g + 1) * HEAD_DIM)
            mixed = jnp.dot(wsb_ref[g], vn_ref[rows, cols], preferred_element_type=F32)
            mixed = mixed + bt[:, g:g + 1]
            o_ref[rows, cols] = (_gelu_tanh(u_ref[rows, cols]) * mixed).astype(o_ref.dtype)


def _spatial_gating(proj, ln_g, ln_b, w_spatial, b_spatial_t, layer):
    s = proj.shape[0]
    tm = 512
    return pl.pallas_call(
        _sgu_kernel,
        grid=(s // tm,),
        in_specs=[pl.BlockSpec((tm, SGU_W), lambda i: (i, 0)),
                  pl.BlockSpec((tm, SGU_W), lambda i: (i, 1)),
                  _gain_spec(layer, SGU_W),
                  _gain_spec(layer, SGU_W),
                  pl.BlockSpec((None, SGU_GROUPS, SGU_CHUNK, SGU_CHUNK), lambda i: (layer, 0, 0, 0)),
                  pl.BlockSpec((None, SGU_CHUNK, SGU_GROUPS), lambda i: (layer, 0, 0))],
        out_specs=pl.BlockSpec((tm, SGU_W), lambda i: (i, 0)),
        out_shape=jax.ShapeDtypeStruct((s, SGU_W), BF16),
        scratch_shapes=[pltpu.VMEM((SGU_GROUPS, SGU_CHUNK, SGU_CHUNK), BF16),
                        pltpu.VMEM((tm, SGU_W), BF16)],
        compiler_params=_params("arbitrary"),
        name="spatial_gating",
    )(proj, proj, _gain_arg(ln_g), _gain_arg(ln_b), w_spatial, b_spatial_t)


def _resid_matmul_kernel(*refs, part_tiles, nk, n_chunk, res_cw, resident, norm):
    nparts = len(part_tiles)
    a_refs = refs[:nparts]
    w_ref, x_ref = refs[nparts], refs[nparts + 1]
    pos = nparts + 2
    g_ref = None
    if norm:
        g_ref = refs[pos]
        pos += 1
    o_ref = refs[pos]
    pos += 1
    hn_ref = None
    if norm == "emit":
        hn_ref = refs[pos]
        pos += 1
    wres_ref = refs[pos] if resident else None
    m, k = pl.program_id(0), pl.program_id(1)
    width = o_ref.shape[1]

    if resident:
        @pl.when(m == 0)
        def _():
            wres_ref[k] = w_ref[...].astype(BF16)

    def lhs():
        a = a_refs[0][...]
        bound = part_tiles[0]
        for p in range(1, nparts):
            a = jnp.where(k < bound, a, a_refs[p][...])
            bound += part_tiles[p]
        return a

    def w_cols(cols):
        if resident:
            return wres_ref[k, :, cols]
        return w_ref[:, cols].astype(BF16)

    def accumulate(first):
        for c in range(width // n_chunk):
            cols = slice(c * n_chunk, (c + 1) * n_chunk)
            part = jnp.dot(lhs(), w_cols(cols), preferred_element_type=F32)
            if first:
                o_ref[:, cols] = part
            else:
                o_ref[:, cols] += part

    pl.when(k == 0)(lambda: accumulate(True))
    pl.when(k > 0)(lambda: accumulate(False))

    for c in range(width // res_cw):
        @pl.when(k == c)
        def _(c=c):
            o_ref[:, c * res_cw:(c + 1) * res_cw] += x_ref[...]

    if norm:
        @pl.when(k == nk - 1)
        def _():
            y = _rms_rows(o_ref[...], g_ref[...])
            if norm == "emit":
                hn_ref[...] = y.astype(BF16)
            else:
                o_ref[...] = y


def _resid_matmul(parts, w, w_layer, x, *, tm, tk, res_cw, resident, norm=None, g=None, g_layer=0,
                  name):
    s, width = x.shape
    part_tiles = tuple(p.shape[1] // tk for p in parts)
    assert all(p.shape[1] % tk == 0 for p in parts)
    nk = sum(part_tiles)
    assert w.shape[1] == nk * tk and w.shape[2] == width
    n_res = width // res_cw
    assert s % tm == 0 and width % res_cw == 0 and n_res <= nk
    n_chunk = 512
    in_specs, off = [], 0
    for p, tiles in zip(parts, part_tiles):
        in_specs.append(pl.BlockSpec(
            (tm, tk), lambda i, k, off=off, tiles=tiles: (i, jnp.clip(k - off, 0, tiles - 1))))
        off += tiles
    if resident:
        in_specs.append(pl.BlockSpec((None, tk, width),
                                     lambda i, k: (w_layer, jnp.where(i == 0, k, nk - 1), 0)))
    else:
        in_specs.append(pl.BlockSpec((None, tk, width), lambda i, k: (w_layer, k, 0)))
    in_specs.append(pl.BlockSpec((tm, res_cw), lambda i, k: (i, jnp.minimum(k, n_res - 1))))
    args = list(parts) + [w, x]
    if norm:
        in_specs.append(_gain_spec(g_layer, width))
        args.append(_gain_arg(g))
    row_spec = pl.BlockSpec((tm, width), lambda i, k: (i, 0))
    out_specs = [row_spec]
    out_shape = [jax.ShapeDtypeStruct((s, width), F32)]
    if norm == "emit":
        out_specs.append(row_spec)
        out_shape.append(jax.ShapeDtypeStruct((s, width), BF16))
    scratch = [pltpu.VMEM((nk, tk, width), BF16)] if resident else []
    return pl.pallas_call(
        functools.partial(_resid_matmul_kernel, part_tiles=part_tiles, nk=nk, n_chunk=n_chunk,
                          res_cw=res_cw, resident=resident, norm=norm),
        grid=(s // tm, nk),
        in_specs=in_specs,
        out_specs=out_specs,
        out_shape=out_shape,
        scratch_shapes=scratch,
        compiler_params=_params("arbitrary", "arbitrary"),
        name=name,
    )(*args)


def _ffn_gate_up_kernel(h_ref, wg_ref, wu_ref, o_ref):
    for c in range(o_ref.shape[1] // V7X_MXU_COLS):
        cols = slice(c * V7X_MXU_COLS, (c + 1) * V7X_MXU_COLS)
        gate = jnp.dot(h_ref[...], wg_ref[:, cols].astype(BF16), preferred_element_type=F32)
        up = jnp.dot(h_ref[...], wu_ref[:, cols].astype(BF16), preferred_element_type=F32)
        o_ref[:, cols] = (gate * (1.0 / (1.0 + jnp.exp(-gate))) * up).astype(o_ref.dtype)


def _ffn_gate_up(h, w_gate, w_up, layer):
    s, k = h.shape
    f = w_gate.shape[2]
    tm, tf = 2048, 512
    wspec = pl.BlockSpec((None, k, tf), lambda i, j: (layer, 0, j))
    return pl.pallas_call(
        _ffn_gate_up_kernel,
        grid=(s // tm, f // tf),
        in_specs=[pl.BlockSpec((tm, k), lambda i, j: (i, 0)), wspec, wspec],
        out_specs=pl.BlockSpec((tm, tf), lambda i, j: (i, j)),
        out_shape=jax.ShapeDtypeStruct((s, f), BF16),
        compiler_params=_params("parallel", "arbitrary"),
        name="ffn_gate_up",
    )(h, w_gate, w_up)


def kernel(x, mem, positions, mix_norm, mem_norm, w_mem_kv, ffn_norm, w_gate, w_up, w_down,
           attn_w_in, attn_w_out, sgu_w_in, sgu_ln_g, sgu_ln_b, sgu_w_spatial, sgu_b_spatial,
           sgu_w_out, final_norm):
    b, s, d = x.shape
    assert (b, s, d) == (1, SEQ, D_MODEL) and mem.shape == (1, MEM_LEN, D_MODEL)
    xs = x.reshape(s, d)
    mems = mem.reshape(MEM_LEN, d)

    inv_freq = ROPE_THETA ** (-jnp.arange(ROT_HALF, dtype=F32) / ROT_HALF)
    invf = jnp.concatenate([inv_freq, inv_freq, jnp.zeros((HEAD_DIM - ROT_DIM,), F32)])
    rope = _rope_tables(positions.reshape(s, 1), invf.reshape(1, HEAD_DIM))

    for i in range(DEPTH):
        j = i // 2
        last = i == DEPTH - 1
        kv = _norm_matmul(mems, mem_norm, i, w_mem_kv, i, 0, 2 * MEM_W, tm=MEM_LEN, tn=512)
        if i % 2 == 0:
            qk, vm = _in_proj(xs, mix_norm, i, _cast_bf16(attn_w_in, j), (2 * ATTN_W, ATTN_W + MEM_W),
                              rope=rope, rope_cols=2 * ATTN_W)
            mix = _dilated_attention(qk, vm)
            mem_out = _memory_attention(vm, ATTN_W // MEM_W, kv)
            w_out = attn_w_out
        else:
            proj, = _in_proj(xs, mix_norm, i, _cast_bf16(sgu_w_in, j), (2 * SGU_W + MEM_W,))
            mix = _spatial_gating(proj, sgu_ln_g, sgu_ln_b, sgu_w_spatial,
                                  jnp.swapaxes(sgu_b_spatial, 1, 2), j)
            mem_out = _memory_attention(proj, 2 * SGU_W // MEM_W, kv)
            w_out = sgu_w_out
        xs, hn = _out_proj(mix, mem_out, _cast_bf16(w_out, j), xs, ffn_norm, i)
        act = _ffn_gate_up(hn, w_gate, w_up, i)
        if last:
            g_fin = final_norm.reshape(1, d)
            xs, = _resid_matmul([act], w_down, i, xs, tm=2048, tk=512, res_cw=256, resident=False,
                                norm="inplace", g=g_fin, g_layer=0, name="ffn_down")
        else:
            xs, = _resid_matmul([act], w_down, i, xs, tm=2048, tk=512, res_cw=256, resident=False,
                                name="ffn_down")
    return xs.reshape(b, s, d)
```

```python
import functools
import math

import jax
import jax.numpy as jnp
from jax import lax
from jax.experimental import pallas as pl
from jax.experimental.pallas import tpu as pltpu

D_MODEL = 2048
SEQ = 8192
DEPTH = 2
MEM_LEN = 256
HEAD_DIM = 128
MEM_HEADS = 4
MEM_W = MEM_HEADS * HEAD_DIM
ATTN_GROUPS = ((128, 1), (512, 4), (2048, 16))
ATTN_HEADS = 12
HEADS_PER_GROUP = 4
ATTN_W = ATTN_HEADS * HEAD_DIM
ATTN_OUT_W = HEADS_PER_GROUP * HEAD_DIM
BLK = 128
SGU_GROUPS = 12
SGU_W = SGU_GROUPS * HEAD_DIM
SGU_CHUNK = 128
ROT_DIM = HEAD_DIM // 4
ROT_HALF = ROT_DIM // 2
ROPE_THETA = 500000.0
FFN_HIDDEN = 5632
NORM_EPS = 1e-6
LN_EPS = 1e-5
NEG_INF = -1e30
ATTN_SCALE = HEAD_DIM ** -0.5

N_BACK = BLK
assert all(w // d == N_BACK for w, d in ATTN_GROUPS)
ATTN_SPAN = max(d for _, d in ATTN_GROUPS) * BLK

V7X_VMEM_LIMIT_BYTES = 56 * 1024 * 1024
V7X_MXU_COLS = 256

F32 = jnp.float32
BF16 = jnp.bfloat16


def _params(*semantics):
    return pltpu.CompilerParams(dimension_semantics=semantics,
                                vmem_limit_bytes=V7X_VMEM_LIMIT_BYTES)


def _rms_rows(x, g):
    ms = jnp.mean(x * x, axis=-1, keepdims=True)
    return x * lax.rsqrt(ms + NORM_EPS) * g


def _gain_spec(layer, k):
    return pl.BlockSpec((None, 1, k), lambda *_: (layer, 0, 0))


def _gain_arg(g):
    return g.reshape(g.shape[0], 1, g.shape[1])


def _rope_table_kernel(pos_ref, invf_ref, c_ref, s1_ref, s2_ref):
    ang = pos_ref[...].astype(F32) * invf_ref[...]
    lane = lax.broadcasted_iota(jnp.int32, ang.shape, 1)
    cos = jnp.cos(ang)
    sin = jnp.sin(ang)
    c_ref[...] = jnp.where(lane < ROT_DIM, cos, 1.0)
    s1_ref[...] = jnp.where(lane < ROT_HALF, 0.0, jnp.where(lane < ROT_DIM, sin, 0.0))
    s2_ref[...] = jnp.where(lane < ROT_HALF, -sin, 0.0)


def _rope_tables(pos_col, invf):
    s = pos_col.shape[0]
    tm = 1024
    tab = jax.ShapeDtypeStruct((s, HEAD_DIM), F32)
    row = pl.BlockSpec((tm, HEAD_DIM), lambda i: (i, 0))
    return pl.pallas_call(
        _rope_table_kernel,
        grid=(s // tm,),
        in_specs=[pl.BlockSpec((tm, 1), lambda i: (i, 0)),
                  pl.BlockSpec((1, HEAD_DIM), lambda i: (0, 0))],
        out_specs=[row, row, row],
        out_shape=[tab, tab, tab],
        compiler_params=_params("parallel"),
        name="rope_tables",
    )(pos_col, invf)


def _mem_kv_kernel(x_ref, g_ref, w_ref, o_ref):
    h = _rms_rows(x_ref[...], g_ref[...]).astype(BF16)
    o_ref[...] = jnp.dot(h, w_ref[...].astype(BF16), preferred_element_type=F32)


def _mem_kv(x, g, w, layer):
    m, k = x.shape
    n = w.shape[2]
    tn = 512
    assert n % tn == 0
    return pl.pallas_call(
        _mem_kv_kernel,
        grid=(n // tn,),
        in_specs=[pl.BlockSpec((m, k), lambda j: (0, 0)), _gain_spec(layer, k),
                  pl.BlockSpec((None, k, tn), lambda j: (layer, 0, j))],
        out_specs=pl.BlockSpec((m, tn), lambda j: (0, j)),
        out_shape=jax.ShapeDtypeStruct((m, n), F32),
        compiler_params=_params("parallel"),
        name="mem_kv",
    )(x, _gain_arg(g), w)


def _cast_kernel(w_ref, o_ref):
    o_ref[...] = w_ref[...].astype(BF16)


def _cast_bf16(w, layer):
    _, k, n = w.shape
    tk = 256
    assert k % tk == 0
    return pl.pallas_call(
        _cast_kernel,
        grid=(k // tk,),
        in_specs=[pl.BlockSpec((None, tk, n), lambda i: (layer, i, 0))],
        out_specs=pl.BlockSpec((tk, n), lambda i: (i, 0)),
        out_shape=jax.ShapeDtypeStruct((k, n), BF16),
        compiler_params=_params("parallel"),
        name="cast_bf16",
    )(w)


def _resident_spec(shape):
    return pl.BlockSpec(shape, lambda *_: (0,) * len(shape), pipeline_mode=pl.Buffered(1))


def _gelu_tanh(x):
    return x * (0.5 * (1.0 + jnp.tanh(math.sqrt(2.0 / math.pi) * (x + 0.044715 * (x * x * x)))))


def _in_proj_kernel(x_ref, g_ref, w_ref, *rest, widths, rope_cols, gelu_cols):
    if rope_cols:
        c_ref, s1_ref, s2_ref = rest[:3]
        rest = rest[3:]
    o_refs, h_ref = rest[:len(widths)], rest[len(widths)]
    h_ref[...] = _rms_rows(x_ref[...], g_ref[...]).astype(BF16)
    col = 0
    for o_ref, width in zip(o_refs, widths):
        for c in range(width // V7X_MXU_COLS):
            acc = jnp.dot(h_ref[...], w_ref[:, col:col + V7X_MXU_COLS], preferred_element_type=F32)
            for j in range(V7X_MXU_COLS // HEAD_DIM):
                t = acc[:, j * HEAD_DIM:(j + 1) * HEAD_DIM]
                if col < rope_cols:
                    t = (t * c_ref[...] + pltpu.roll(t, ROT_HALF, 1) * s1_ref[...]
                         + pltpu.roll(t, HEAD_DIM - ROT_HALF, 1) * s2_ref[...])
                if col < gelu_cols:
                    t = _gelu_tanh(t)
                lo = c * V7X_MXU_COLS + j * HEAD_DIM
                o_ref[:, lo:lo + HEAD_DIM] = t
            col += V7X_MXU_COLS


def _in_proj(x, g, g_layer, w_bf, widths, *, rope=None, rope_cols=0, gelu_cols=0):
    s, k = x.shape
    n = w_bf.shape[1]
    tm = 512
    assert sum(widths) == n and all(w % V7X_MXU_COLS == 0 for w in widths)
    assert rope_cols % V7X_MXU_COLS == 0 and (rope is not None) == bool(rope_cols)
    assert gelu_cols % V7X_MXU_COLS == 0
    in_specs = [pl.BlockSpec((tm, k), lambda i: (i, 0)), _gain_spec(g_layer, k),
                _resident_spec((k, n))]
    args = [x, _gain_arg(g), w_bf]
    if rope is not None:
        in_specs += [pl.BlockSpec((tm, HEAD_DIM), lambda i: (i, 0))] * 3
        args += list(rope)
    return pl.pallas_call(
        functools.partial(_in_proj_kernel, widths=tuple(widths), rope_cols=rope_cols,
                          gelu_cols=gelu_cols),
        grid=(s // tm,),
        in_specs=in_specs,
        out_specs=[pl.BlockSpec((tm, w), lambda i: (i, 0)) for w in widths],
        out_shape=[jax.ShapeDtypeStruct((s, w), F32) for w in widths],
        scratch_shapes=[pltpu.VMEM((tm, k), BF16)],
        compiler_params=_params("parallel"),
        name="in_proj",
    )(*args)


def _out_proj_kernel(a_ref, b_ref, w_ref, x_ref, g_ref, o_ref, hn_ref):
    ka = a_ref.shape[1]
    width = o_ref.shape[1]
    n_chunk = 512
    ssq = jnp.zeros((o_ref.shape[0], 1), F32)
    for c in range(width // n_chunk):
        cols = slice(c * n_chunk, (c + 1) * n_chunk)
        y = (x_ref[:, cols]
             + jnp.dot(a_ref[...], w_ref[:ka, cols], preferred_element_type=F32)
             + jnp.dot(b_ref[...], w_ref[ka:, cols], preferred_element_type=F32))
        o_ref[:, cols] = y
        ssq = ssq + jnp.sum(y * y, axis=-1, keepdims=True)
    scale = lax.rsqrt(ssq * (1.0 / width) + NORM_EPS)
    for c in range(width // n_chunk):
        cols = slice(c * n_chunk, (c + 1) * n_chunk)
        hn_ref[:, cols] = (o_ref[:, cols] * scale * g_ref[:, cols]).astype(BF16)


def _out_proj(a, b, w_bf, x, g, g_layer):
    s, ka = a.shape
    kb = b.shape[1]
    width = x.shape[1]
    tm = 512
    assert w_bf.shape == (ka + kb, width)
    row = lambda cols: pl.BlockSpec((tm, cols), lambda i: (i, 0))
    return pl.pallas_call(
        _out_proj_kernel,
        grid=(s // tm,),
        in_specs=[row(ka), row(kb), _resident_spec((ka + kb, width)), row(width),
                  _gain_spec(g_layer, width)],
        out_specs=[row(width), row(width)],
        out_shape=[jax.ShapeDtypeStruct((s, width), F32), jax.ShapeDtypeStruct((s, width), BF16)],
        compiler_params=_params("parallel"),
        name="out_proj",
    )(a, b, w_bf, x, _gain_arg(g))


def _rows(ref, start, size, stride):
    if stride == 1:
        return ref[pl.ds(start, size), :]
    return ref[pl.ds(start, size, stride=stride), :]


ATTN_UNITS_PER_BATCH = 16
V7X_FREE_SUBLANE_STRIDE = 4


def _presplit(d):
    return d // V7X_FREE_SUBLANE_STRIDE if d > V7X_FREE_SUBLANE_STRIDE else 1


def _attn_batch(units):
    scores = [lax.dot_general(q.astype(BF16), kc.astype(BF16), (((1,), (1,)), ((), ())),
                              preferred_element_type=F32) * ATTN_SCALE + bias
              for q, kc, _, bias in units]
    probs = []
    for s in scores:
        m = jnp.max(s, axis=-1, keepdims=True)
        p = jnp.exp(s - m)
        probs.append((p, m, jnp.sum(p, axis=-1, keepdims=True)))
    outs = []
    for (p, m, l), (_, _, vc, _) in zip(probs, units):
        o = jnp.dot(p.astype(BF16), vc.astype(BF16), preferred_element_type=F32)
        outs.append((o * (1.0 / l), m + jnp.log(l)))
    return outs


def _attn_kernel(*refs):
    ngrp = len(ATTN_GROUPS)
    in_refs = refs[:5 * ngrp]
    o_ref = refs[5 * ngrp]
    pos = 5 * ngrp + 1
    o_scr, l_scr = refs[pos:pos + ngrp], refs[pos + ngrp:pos + 2 * ngrp]
    k_scr, v_scr = refs[pos + 2 * ngrp:pos + 3 * ngrp], refs[pos + 3 * ngrp:pos + 4 * ngrp]
    split_refs = refs[pos + 4 * ngrp:pos + 4 * ngrp + 3]
    bias_ref = refs[-1]

    qi = lax.broadcasted_iota(jnp.int32, (BLK, 2 * BLK), 0)
    ki = lax.broadcasted_iota(jnp.int32, (BLK, 2 * BLK), 1)
    lo = jnp.where(pl.program_id(0) == 0, BLK, 0)
    bias_ref[0] = jnp.where((ki >= qi) & (ki <= qi + N_BACK), 0.0, NEG_INF)
    bias_ref[1] = jnp.where((ki >= jnp.maximum(qi, lo)) & (ki <= qi + N_BACK), 0.0, NEG_INF)

    for g, (_, d) in enumerate(ATTN_GROUPS):
        q_ref, k_ref, v_ref, kp_ref, vp_ref = in_refs[5 * g:5 * g + 5]
        og, lg, kcat, vcat = o_scr[g], l_scr[g], k_scr[g], v_scr[g]
        sub = d * BLK
        n_units = ATTN_SPAN // BLK
        assert n_units % ATTN_UNITS_PER_BATCH == 0
        f = _presplit(d)
        d2 = d // f
        klen, qlen, sublen = (sub + ATTN_SPAN) // f, ATTN_SPAN // f, sub // f

        for rf in range(f):
            for src_p, src, cat in ((kp_ref, k_ref, kcat), (vp_ref, v_ref, vcat)):
                cat[pl.ds(rf * klen, sublen), :] = _rows(src_p, rf, sublen, f)
                cat[pl.ds(rf * klen + sublen, qlen), :] = _rows(src, rf, qlen, f)
            if f > 1:
                split_refs[0][pl.ds(rf * qlen, qlen), :] = _rows(q_ref, rf, qlen, f)
        q_src, o_dst, l_dst = (split_refs[0], split_refs[1], split_refs[2]) if f > 1 else (q_ref, og, lg)

        def body(t, carry, d=d, f=f, d2=d2, klen=klen, qlen=qlen, sublen=sublen,
                 q_src=q_src, kcat=kcat, vcat=vcat, o_dst=o_dst, l_dst=l_dst):
            units, starts = [], []
            for u in range(ATTN_UNITS_PER_BATCH):
                idx = t * ATTN_UNITS_PER_BATCH + u
                r = idx % d
                base = (idx // d) * sublen + r // f
                qstart = (r % f) * qlen + base
                kstart = (r % f) * klen + base
                bias = bias_ref[jnp.where(idx < d, 1, 0)]
                units.append((_rows(q_src, qstart, BLK, d2), _rows(kcat, kstart, 2 * BLK, d2),
                              _rows(vcat, kstart, 2 * BLK, d2), bias))
                starts.append(qstart)
            for qstart, (o, lse) in zip(starts, _attn_batch(units)):
                idx = pl.ds(qstart, BLK) if d2 == 1 else pl.ds(qstart, BLK, stride=d2)
                o_dst[idx, :] = o
                l_dst[idx, :] = jnp.broadcast_to(lse, (BLK, HEAD_DIM))
            return carry

        lax.fori_loop(0, n_units // ATTN_UNITS_PER_BATCH, body, 0)
        if f > 1:
            for rf in range(f):
                og[pl.ds(rf, qlen, stride=f), :] = o_dst[pl.ds(rf * qlen, qlen), :]
                lg[pl.ds(rf, qlen, stride=f), :] = l_dst[pl.ds(rf * qlen, qlen), :]

    lses = [l[...] for l in l_scr]
    mx = functools.reduce(jnp.maximum, lses)
    ws = [jnp.exp(l - mx) for l in lses]
    num = functools.reduce(lambda a, b: a + b, [w * o[...] for w, o in zip(ws, o_scr)])
    den = functools.reduce(lambda a, b: a + b, ws)
    o_ref[...] = (num * (1.0 / den)).astype(o_ref.dtype)


def _dilated_attention(qk, vm):
    s = qk.shape[0]
    assert s % ATTN_SPAN == 0
    hpg = HEADS_PER_GROUP
    in_specs, args = [], []
    for g, (_, d) in enumerate(ATTN_GROUPS):
        sub = d * BLK
        ratio = ATTN_SPAN // sub
        qc, kc, vc = g * hpg, ATTN_HEADS + g * hpg, g * hpg
        cur = lambda col: pl.BlockSpec((ATTN_SPAN, HEAD_DIM), lambda i, h, col=col: (i, col + h))
        prev = lambda col, ratio=ratio, sub=sub: pl.BlockSpec(
            (sub, HEAD_DIM), lambda i, h, col=col, ratio=ratio: (jnp.maximum(i * ratio - 1, 0), col + h))
        in_specs += [cur(qc), cur(kc), cur(vc), prev(kc), prev(vc)]
        args += [qk, qk, vm, qk, vm]
    ngrp = len(ATTN_GROUPS)
    return pl.pallas_call(
        _attn_kernel,
        grid=(s // ATTN_SPAN, hpg),
        in_specs=in_specs,
        out_specs=pl.BlockSpec((ATTN_SPAN, HEAD_DIM), lambda i, h: (i, h)),
        out_shape=jax.ShapeDtypeStruct((s, ATTN_OUT_W), BF16),
        scratch_shapes=([pltpu.VMEM((ATTN_SPAN, HEAD_DIM), F32)] * (2 * ngrp)
                        + [pltpu.VMEM((d * BLK + ATTN_SPAN, HEAD_DIM), F32) for _, d in ATTN_GROUPS] * 2
                        + [pltpu.VMEM((ATTN_SPAN, HEAD_DIM), F32)] * 3
                        + [pltpu.VMEM((2, BLK, 2 * BLK), F32)]),
        compiler_params=_params("parallel", "arbitrary"),
        name="dilated_attention",
    )(*args)


def _mem_attn_kernel(q_ref, kv_ref, o_ref):
    for h in range(MEM_HEADS):
        sl = slice(h * HEAD_DIM, (h + 1) * HEAD_DIM)
        k = kv_ref[:, sl].astype(BF16)
        v = kv_ref[:, MEM_W + h * HEAD_DIM:MEM_W + (h + 1) * HEAD_DIM].astype(BF16)
        s = lax.dot_general(q_ref[:, sl].astype(BF16), k, (((1,), (1,)), ((), ())),
                            preferred_element_type=F32) * ATTN_SCALE
        m = jnp.max(s, axis=-1, keepdims=True)
        p = jnp.exp(s - m)
        l = jnp.sum(p, axis=-1, keepdims=True)
        o = jnp.dot(p.astype(BF16), v, preferred_element_type=F32)
        o_ref[:, sl] = (o * (1.0 / l)).astype(o_ref.dtype)


def _memory_attention(proj, q_col_block, kv):
    s = proj.shape[0]
    tm = 1024
    return pl.pallas_call(
        _mem_attn_kernel,
        grid=(s // tm,),
        in_specs=[pl.BlockSpec((tm, MEM_W), lambda i: (i, q_col_block)),
                  pl.BlockSpec((MEM_LEN, 2 * MEM_W), lambda i: (0, 0))],
        out_specs=pl.BlockSpec((tm, MEM_W), lambda i: (i, 0)),
        out_shape=jax.ShapeDtypeStruct((s, MEM_W), BF16),
        compiler_params=_params("parallel"),
        name="memory_attention",
    )(proj, kv)


def _sgu_kernel(u_ref, v_ref, g_ref, b_ref, ws_ref, bt_ref, o_ref, wsb_ref, vn_ref):
    @pl.when(pl.program_id(0) == 0)
    def _():
        t = lax.broadcasted_iota(jnp.int32, (SGU_CHUNK, SGU_CHUNK), 0)
        s = lax.broadcasted_iota(jnp.int32, (SGU_CHUNK, SGU_CHUNK), 1)
        for g in range(SGU_GROUPS):
            wsb_ref[g] = jnp.where(t >= s, ws_ref[g], 0.0).astype(BF16)

    v = v_ref[...]
    mu = jnp.mean(v, axis=-1, keepdims=True)
    vc = v - mu
    var = jnp.mean(vc * vc, axis=-1, keepdims=True)
    vn_ref[...] = (vc * lax.rsqrt(var + LN_EPS) * g_ref[...] + b_ref[...]).astype(BF16)

    bt = bt_ref[...]
    for c in range(u_ref.shape[0] // SGU_CHUNK):
        rows = slice(c * SGU_CHUNK, (c + 1) * SGU_CHUNK)
        for g in range(SGU_GROUPS):
            cols = slice(g * HEAD_DIM, (g + 1) * HEAD_DIM)
            mixed = jnp.dot(wsb_ref[g], vn_ref[rows, cols], preferred_element_type=F32)
            mixed = mixed + bt[:, g:g + 1]
            o_ref[rows, cols] = (u_ref[rows, cols] * mixed).astype(o_ref.dtype)


def _spatial_gating(proj, ln_g, ln_b, w_spatial, b_spatial_t, layer):
    s = proj.shape[0]
    tm = 512
    return pl.pallas_call(
        _sgu_kernel,
        grid=(s // tm,),
        in_specs=[pl.BlockSpec((tm, SGU_W), lambda i: (i, 0)),
                  pl.BlockSpec((tm, SGU_W), lambda i: (i, 1)),
                  _gain_spec(layer, SGU_W),
                  _gain_spec(layer, SGU_W),
                  pl.BlockSpec((None, SGU_GROUPS, SGU_CHUNK, SGU_CHUNK), lambda i: (layer, 0, 0, 0)),
                  pl.BlockSpec((None, SGU_CHUNK, SGU_GROUPS), lambda i: (layer, 0, 0))],
        out_specs=pl.BlockSpec((tm, SGU_W), lambda i: (i, 0)),
        out_shape=jax.ShapeDtypeStruct((s, SGU_W), BF16),
        scratch_shapes=[pltpu.VMEM((SGU_GROUPS, SGU_CHUNK, SGU_CHUNK), BF16),
                        pltpu.VMEM((tm, SGU_W), BF16)],
        compiler_params=_params("arbitrary"),
        name="spatial_gating",
    )(proj, proj, _gain_arg(ln_g), _gain_arg(ln_b), w_spatial, b_spatial_t)


def _ffn_down_kernel(a_ref, w_ref, x_ref, *rest, nk, final_norm):
    if final_norm:
        g_ref, o_ref = rest
    else:
        (o_ref,) = rest
    k = pl.program_id(1)
    cw = x_ref.shape[1]

    def accumulate(first):
        for c in range(o_ref.shape[1] // cw):
            cols = slice(c * cw, (c + 1) * cw)
            part = jnp.dot(a_ref[...], w_ref[:, cols].astype(BF16), preferred_element_type=F32)
            part = part + jnp.where(k == c, x_ref[...], 0.0)
            if first:
                o_ref[:, cols] = part
            else:
                o_ref[:, cols] += part

    pl.when(k == 0)(lambda: accumulate(True))
    pl.when(k > 0)(lambda: accumulate(False))

    if final_norm:
        @pl.when(k == nk - 1)
        def _():
            o_ref[...] = _rms_rows(o_ref[...], g_ref[...])


def _ffn_down(a, w, layer, x, final_g=None):
    s, width = x.shape
    tm, tk, cw = 2048, 512, 256
    nk = a.shape[1] // tk
    n_res = width // cw
    assert a.shape[1] % tk == 0 and w.shape[1:] == (nk * tk, width)
    assert s % tm == 0 and width % cw == 0 and n_res <= nk
    in_specs = [pl.BlockSpec((tm, tk), lambda i, k: (i, k)),
                pl.BlockSpec((None, tk, width), lambda i, k: (layer, k, 0)),
                pl.BlockSpec((tm, cw), lambda i, k: (i, jnp.minimum(k, n_res - 1)))]
    args = [a, w, x]
    if final_g is not None:
        in_specs.append(_gain_spec(0, width))
        args.append(final_g.reshape(1, 1, width))
    return pl.pallas_call(
        functools.partial(_ffn_down_kernel, nk=nk, final_norm=final_g is not None),
        grid=(s // tm, nk),
        in_specs=in_specs,
        out_specs=pl.BlockSpec((tm, width), lambda i, k: (i, 0)),
        out_shape=jax.ShapeDtypeStruct((s, width), F32),
        compiler_params=_params("parallel", "arbitrary"),
        name="ffn_down",
    )(*args)


def _ffn_gate_up_kernel(h_ref, wg_ref, wu_ref, o_ref):
    for c in range(o_ref.shape[1] // V7X_MXU_COLS):
        cols = slice(c * V7X_MXU_COLS, (c + 1) * V7X_MXU_COLS)
        gate = jnp.dot(h_ref[...], wg_ref[:, cols].astype(BF16), preferred_element_type=F32)
        up = jnp.dot(h_ref[...], wu_ref[:, cols].astype(BF16), preferred_element_type=F32)
        o_ref[:, cols] = (gate * (1.0 / (1.0 + jnp.exp(-gate))) * up).astype(o_ref.dtype)


def _ffn_gate_up(h, w_gate, w_up, layer):
    s, k = h.shape
    f = w_gate.shape[2]
    tm, tf = 2048, 512
    wspec = pl.BlockSpec((None, k, tf), lambda i, j: (layer, 0, j))
    return pl.pallas_call(
        _ffn_gate_up_kernel,
        grid=(s // tm, f // tf),
        in_specs=[pl.BlockSpec((tm, k), lambda i, j: (i, 0)), wspec, wspec],
        out_specs=pl.BlockSpec((tm, tf), lambda i, j: (i, j)),
        out_shape=jax.ShapeDtypeStruct((s, f), BF16),
        compiler_params=_params("parallel", "arbitrary"),
        name="ffn_gate_up",
    )(h, w_gate, w_up)


def kernel(x, mem, positions, mix_norm, mem_norm, w_mem_kv, ffn_norm, w_gate, w_up, w_down,
           attn_w_in, attn_w_out, sgu_w_in, sgu_ln_g, sgu_ln_b, sgu_w_spatial, sgu_b_spatial,
           sgu_w_out, final_norm):
    b, s, d = x.shape
    assert (b, s, d) == (1, SEQ, D_MODEL) and mem.shape == (1, MEM_LEN, D_MODEL)
    xs = x.reshape(s, d)
    mems = mem.reshape(MEM_LEN, d)

    inv_freq = ROPE_THETA ** (-jnp.arange(ROT_HALF, dtype=F32) / ROT_HALF)
    invf = jnp.concatenate([inv_freq, inv_freq, jnp.zeros((HEAD_DIM - ROT_DIM,), F32)])
    rope = _rope_tables(positions.reshape(s, 1), invf.reshape(1, HEAD_DIM))

    for i in range(DEPTH):
        j = i // 2
        kv = _mem_kv(mems, mem_norm, w_mem_kv, i)
        if i % 2 == 0:
            qk, vm = _in_proj(xs, mix_norm, i, _cast_bf16(attn_w_in, j), (2 * ATTN_W, ATTN_W + MEM_W),
                              rope=rope, rope_cols=2 * ATTN_W)
            mix = _dilated_attention(qk, vm)
            mem_out = _memory_attention(vm, ATTN_W // MEM_W, kv)
            w_out = attn_w_out
        else:
            proj, = _in_proj(xs, mix_norm, i, _cast_bf16(sgu_w_in, j), (2 * SGU_W + MEM_W,),
                             gelu_cols=2 * SGU_W)
            mix = _spatial_gating(proj, sgu_ln_g, sgu_ln_b, sgu_w_spatial,
                                  jnp.swapaxes(sgu_b_spatial, 1, 2), j)
            mem_out = _memory_attention(proj, 2 * SGU_W // MEM_W, kv)
            w_out = sgu_w_out
        xs, hn = _out_proj(mix, mem_out, _cast_bf16(w_out, j), xs, ffn_norm, i)
        act = _ffn_gate_up(hn, w_gate, w_up, i)
        xs = _ffn_down(act, w_down, i, xs, final_g=final_norm if i == DEPTH - 1 else None)
    return xs.reshape(b, s, d)
```

```python
import functools
import math

import jax
import jax.numpy as jnp
from jax import lax
from jax.experimental import pallas as pl
from jax.experimental.pallas import tpu as pltpu

D_MODEL = 2048
SEQ = 8192
DEPTH = 2
MEM_LEN = 256
HEAD_DIM = 128
MEM_HEADS = 4
MEM_W = MEM_HEADS * HEAD_DIM
ATTN_GROUPS = ((128, 1), (512, 4), (2048, 16))
ATTN_HEADS = 12
HEADS_PER_GROUP = 4
ATTN_W = ATTN_HEADS * HEAD_DIM
ATTN_OUT_W = HEADS_PER_GROUP * HEAD_DIM
BLK = 128
SGU_GROUPS = 12
SGU_W = SGU_GROUPS * HEAD_DIM
SGU_CHUNK = 128
ROT_DIM = HEAD_DIM // 4
ROT_HALF = ROT_DIM // 2
ROPE_THETA = 500000.0
FFN_HIDDEN = 5632
NORM_EPS = 1e-6
LN_EPS = 1e-5
NEG_INF = -1e30
ATTN_SCALE = HEAD_DIM ** -0.5

N_BACK = BLK
assert all(w // d == N_BACK for w, d in ATTN_GROUPS)
ATTN_SPAN = max(d for _, d in ATTN_GROUPS) * BLK

V7X_VMEM_LIMIT_BYTES = 56 * 1024 * 1024
V7X_MXU_COLS = 256

F32 = jnp.float32
BF16 = jnp.bfloat16


def _params(*semantics):
    return pltpu.CompilerParams(dimension_semantics=semantics,
                                vmem_limit_bytes=V7X_VMEM_LIMIT_BYTES)


def _rms_rows(x, g):
    ms = jnp.mean(x * x, axis=-1, keepdims=True)
    return x * lax.rsqrt(ms + NORM_EPS) * g


def _gain_spec(layer, k):
    return pl.BlockSpec((None, 1, k), lambda *_: (layer, 0, 0))


def _gain_arg(g):
    return g.reshape(g.shape[0], 1, g.shape[1])


def _rope_table_kernel(pos_ref, invf_ref, c_ref, s1_ref, s2_ref):
    ang = pos_ref[...].astype(F32) * invf_ref[...]
    lane = lax.broadcasted_iota(jnp.int32, ang.shape, 1)
    cos = jnp.cos(ang)
    sin = jnp.sin(ang)
    c_ref[...] = jnp.where(lane < ROT_DIM, cos, 1.0)
    s1_ref[...] = jnp.where(lane < ROT_HALF, 0.0, jnp.where(lane < ROT_DIM, sin, 0.0))
    s2_ref[...] = jnp.where(lane < ROT_HALF, -sin, 0.0)


def _rope_tables(pos_col, invf):
    s = pos_col.shape[0]
    tm = 1024
    tab = jax.ShapeDtypeStruct((s, HEAD_DIM), F32)
    row = pl.BlockSpec((tm, HEAD_DIM), lambda i: (i, 0))
    return pl.pallas_call(
        _rope_table_kernel,
        grid=(s // tm,),
        in_specs=[pl.BlockSpec((tm, 1), lambda i: (i, 0)),
                  pl.BlockSpec((1, HEAD_DIM), lambda i: (0, 0))],
        out_specs=[row, row, row],
        out_shape=[tab, tab, tab],
        compiler_params=_params("parallel"),
        name="rope_tables",
    )(pos_col, invf)


def _mem_kv_kernel(x_ref, g_ref, w_ref, o_ref):
    h = _rms_rows(x_ref[...], g_ref[...]).astype(BF16)
    o_ref[...] = jnp.dot(h, w_ref[...].astype(BF16), preferred_element_type=F32)


def _mem_kv(x, g, w, layer):
    m, k = x.shape
    n = w.shape[2]
    tn = 512
    assert n % tn == 0
    return pl.pallas_call(
        _mem_kv_kernel,
        grid=(n // tn,),
        in_specs=[pl.BlockSpec((m, k), lambda j: (0, 0)), _gain_spec(layer, k),
                  pl.BlockSpec((None, k, tn), lambda j: (layer, 0, j))],
        out_specs=pl.BlockSpec((m, tn), lambda j: (0, j)),
        out_shape=jax.ShapeDtypeStruct((m, n), F32),
        compiler_params=_params("parallel"),
        name="mem_kv",
    )(x, _gain_arg(g), w)


def _cast_kernel(w_ref, o_ref):
    o_ref[...] = w_ref[...].astype(BF16)


def _cast_bf16(w, layer):
    _, k, n = w.shape
    tk = 256
    assert k % tk == 0
    return pl.pallas_call(
        _cast_kernel,
        grid=(k // tk,),
        in_specs=[pl.BlockSpec((None, tk, n), lambda i: (layer, i, 0))],
        out_specs=pl.BlockSpec((tk, n), lambda i: (i, 0)),
        out_shape=jax.ShapeDtypeStruct((k, n), BF16),
        compiler_params=_params("parallel"),
        name="cast_bf16",
    )(w)


def _resident_spec(shape):
    return pl.BlockSpec(shape, lambda *_: (0,) * len(shape), pipeline_mode=pl.Buffered(1))


def _gelu_tanh(x):
    return x * (0.5 * (1.0 + jnp.tanh(math.sqrt(2.0 / math.pi) * (x + 0.044715 * (x * x * x)))))


IN_PROJ_ROWS = 512


def _pass(h_ref, w_ref, col):
    return jnp.dot(h_ref[...], w_ref[:, col:col + V7X_MXU_COLS], preferred_element_type=F32)


def _heads(acc):
    return [acc[:, j * HEAD_DIM:(j + 1) * HEAD_DIM] for j in range(acc.shape[1] // HEAD_DIM)]


def _attn_in_proj_kernel(x_ref, g_ref, w_ref, c_ref, s1_ref, s2_ref, qk_ref, vm_ref, h_ref):
    h_ref[...] = _rms_rows(x_ref[...], g_ref[...]).astype(BF16)
    for c in range(2 * ATTN_W // V7X_MXU_COLS):
        for j, t in enumerate(_heads(_pass(h_ref, w_ref, c * V7X_MXU_COLS))):
            lo = c * V7X_MXU_COLS + j * HEAD_DIM
            qk_ref[:, lo:lo + HEAD_DIM] = (t * c_ref[...] + pltpu.roll(t, ROT_HALF, 1) * s1_ref[...]
                                           + pltpu.roll(t, HEAD_DIM - ROT_HALF, 1) * s2_ref[...])
    for c in range((ATTN_W + MEM_W) // V7X_MXU_COLS):
        cols = slice(c * V7X_MXU_COLS, (c + 1) * V7X_MXU_COLS)
        vm_ref[:, cols] = _pass(h_ref, w_ref, 2 * ATTN_W + c * V7X_MXU_COLS)


def _attn_in_proj(x, g, g_layer, w_bf, rope):
    s, k = x.shape
    tm = IN_PROJ_ROWS
    assert w_bf.shape == (k, 3 * ATTN_W + MEM_W)
    row = lambda cols: pl.BlockSpec((tm, cols), lambda i: (i, 0))
    return pl.pallas_call(
        _attn_in_proj_kernel,
        grid=(s // tm,),
        in_specs=[row(k), _gain_spec(g_layer, k), _resident_spec(w_bf.shape)] + [row(HEAD_DIM)] * 3,
        out_specs=[row(2 * ATTN_W), row(ATTN_W + MEM_W)],
        out_shape=[jax.ShapeDtypeStruct((s, 2 * ATTN_W), F32),
                   jax.ShapeDtypeStruct((s, ATTN_W + MEM_W), F32)],
        scratch_shapes=[pltpu.VMEM((tm, k), BF16)],
        compiler_params=_params("parallel"),
        name="attn_in_proj",
    )(x, _gain_arg(g), w_bf, *rope)


def _sgu_in_proj_kernel(x_ref, g_ref, w_ref, lng_ref, lnb_ref, ws_ref, bt_ref,
                        mix_ref, qm_ref, h_ref, v_scr, vn_scr, wsb_scr):
    @pl.when(pl.program_id(0) == 0)
    def _():
        t = lax.broadcasted_iota(jnp.int32, (SGU_CHUNK, SGU_CHUNK), 0)
        s = lax.broadcasted_iota(jnp.int32, (SGU_CHUNK, SGU_CHUNK), 1)
        for g in range(SGU_GROUPS):
            wsb_scr[g] = jnp.where(t >= s, ws_ref[g], 0.0).astype(BF16)

    h_ref[...] = _rms_rows(x_ref[...], g_ref[...]).astype(BF16)
    for c in range(SGU_W // V7X_MXU_COLS):
        cols = slice(c * V7X_MXU_COLS, (c + 1) * V7X_MXU_COLS)
        v_scr[:, cols] = _gelu_tanh(_pass(h_ref, w_ref, SGU_W + c * V7X_MXU_COLS))
    for c in range(MEM_W // V7X_MXU_COLS):
        cols = slice(c * V7X_MXU_COLS, (c + 1) * V7X_MXU_COLS)
        qm_ref[:, cols] = _pass(h_ref, w_ref, 2 * SGU_W + c * V7X_MXU_COLS)
    v = v_scr[...]
    mu = jnp.mean(v, axis=-1, keepdims=True)
    vc = v - mu
    var = jnp.mean(vc * vc, axis=-1, keepdims=True)
    vn_scr[...] = (vc * lax.rsqrt(var + LN_EPS) * lng_ref[...] + lnb_ref[...]).astype(BF16)

    bt = bt_ref[...]
    n_chunks = mix_ref.shape[0] // SGU_CHUNK
    for c in range(SGU_W // V7X_MXU_COLS):
        for j, u in enumerate(_heads(_gelu_tanh(_pass(h_ref, w_ref, c * V7X_MXU_COLS)))):
            g = c * (V7X_MXU_COLS // HEAD_DIM) + j
            cols = slice(g * HEAD_DIM, (g + 1) * HEAD_DIM)
            vg = jnp.concatenate([vn_scr[r * SGU_CHUNK:(r + 1) * SGU_CHUNK, cols]
                                  for r in range(n_chunks)], axis=1)
            mixed = jnp.dot(wsb_scr[g], vg, preferred_element_type=F32) + bt[:, g:g + 1]
            for r in range(n_chunks):
                rows = slice(r * SGU_CHUNK, (r + 1) * SGU_CHUNK)
                mix_ref[rows, cols] = (u[rows, :] * mixed[:, r * HEAD_DIM:(r + 1) * HEAD_DIM]
                                       ).astype(mix_ref.dtype)


def _sgu_in_proj(x, g, g_layer, w_bf, ln_g, ln_b, w_spatial, b_spatial_t, layer):
    s, k = x.shape
    tm = IN_PROJ_ROWS
    assert w_bf.shape == (k, 2 * SGU_W + MEM_W) and tm % SGU_CHUNK == 0
    row = lambda cols: pl.BlockSpec((tm, cols), lambda i: (i, 0))
    return pl.pallas_call(
        _sgu_in_proj_kernel,
        grid=(s // tm,),
        in_specs=[row(k), _gain_spec(g_layer, k), _resident_spec(w_bf.shape),
                  _gain_spec(layer, SGU_W), _gain_spec(layer, SGU_W),
                  pl.BlockSpec((None, SGU_GROUPS, SGU_CHUNK, SGU_CHUNK), lambda i: (layer, 0, 0, 0)),
                  pl.BlockSpec((None, SGU_CHUNK, SGU_GROUPS), lambda i: (layer, 0, 0))],
        out_specs=[row(SGU_W), row(MEM_W)],
        out_shape=[jax.ShapeDtypeStruct((s, SGU_W), BF16), jax.ShapeDtypeStruct((s, MEM_W), F32)],
        scratch_shapes=[pltpu.VMEM((tm, k), BF16), pltpu.VMEM((tm, SGU_W), F32),
                        pltpu.VMEM((tm, SGU_W), BF16),
                        pltpu.VMEM((SGU_GROUPS, SGU_CHUNK, SGU_CHUNK), BF16)],
        compiler_params=_params("arbitrary"),
        name="sgu_in_proj",
    )(x, _gain_arg(g), w_bf, _gain_arg(ln_g), _gain_arg(ln_b), w_spatial, b_spatial_t)


def _mem_attn_kernel(q_ref, kv_ref, o_ref):
    for h in range(MEM_HEADS):
        sl = slice(h * HEAD_DIM, (h + 1) * HEAD_DIM)
        k = kv_ref[:, sl].astype(BF16)
        v = kv_ref[:, MEM_W + h * HEAD_DIM:MEM_W + (h + 1) * HEAD_DIM].astype(BF16)
        s = lax.dot_general(q_ref[:, sl].astype(BF16), k, (((1,), (1,)), ((), ())),
                            preferred_element_type=F32) * ATTN_SCALE
        m = jnp.max(s, axis=-1, keepdims=True)
        p = jnp.exp(s - m)
        l = jnp.sum(p, axis=-1, keepdims=True)
        o = jnp.dot(p.astype(BF16), v, preferred_element_type=F32)
        o_ref[:, sl] = (o * (1.0 / l)).astype(o_ref.dtype)


def _memory_attention(proj, q_col_block, kv):
    s = proj.shape[0]
    tm = 1024
    return pl.pallas_call(
        _mem_attn_kernel,
        grid=(s // tm,),
        in_specs=[pl.BlockSpec((tm, MEM_W), lambda i: (i, q_col_block)),
                  pl.BlockSpec((MEM_LEN, 2 * MEM_W), lambda i: (0, 0))],
        out_specs=pl.BlockSpec((tm, MEM_W), lambda i: (i, 0)),
        out_shape=jax.ShapeDtypeStruct((s, MEM_W), BF16),
        compiler_params=_params("parallel"),
        name="memory_attention",
    )(proj, kv)


def _out_proj_kernel(a_ref, b_ref, w_ref, x_ref, g_ref, o_ref, hn_ref):
    ka = a_ref.shape[1]
    width = o_ref.shape[1]
    n_chunk = 512
    ssq = jnp.zeros((o_ref.shape[0], 1), F32)
    for c in range(width // n_chunk):
        cols = slice(c * n_chunk, (c + 1) * n_chunk)
        y = (x_ref[:, cols]
             + jnp.dot(a_ref[...], w_ref[:ka, cols], preferred_element_type=F32)
             + jnp.dot(b_ref[...], w_ref[ka:, cols], preferred_element_type=F32))
        o_ref[:, cols] = y
        ssq = ssq + jnp.sum(y * y, axis=-1, keepdims=True)
    scale = lax.rsqrt(ssq * (1.0 / width) + NORM_EPS)
    for c in range(width // n_chunk):
        cols = slice(c * n_chunk, (c + 1) * n_chunk)
        hn_ref[:, cols] = (o_ref[:, cols] * scale * g_ref[:, cols]).astype(BF16)


def _out_proj(a, b, w_bf, x, g, g_layer):
    s, ka = a.shape
    kb = b.shape[1]
    width = x.shape[1]
    tm = 512
    assert w_bf.shape == (ka + kb, width)
    row = lambda cols: pl.BlockSpec((tm, cols), lambda i: (i, 0))
    return pl.pallas_call(
        _out_proj_kernel,
        grid=(s // tm,),
        in_specs=[row(ka), row(kb), _resident_spec((ka + kb, width)), row(width),
                  _gain_spec(g_layer, width)],
        out_specs=[row(width), row(width)],
        out_shape=[jax.ShapeDtypeStruct((s, width), F32), jax.ShapeDtypeStruct((s, width), BF16)],
        compiler_params=_params("parallel"),
        name="out_proj",
    )(a, b, w_bf, x, _gain_arg(g))


def _rows(ref, start, size, stride):
    if stride == 1:
        return ref[pl.ds(start, size), :]
    return ref[pl.ds(start, size, stride=stride), :]


ATTN_UNITS_PER_BATCH = 16
V7X_FREE_SUBLANE_STRIDE = 4


def _presplit(d):
    return d // V7X_FREE_SUBLANE_STRIDE if d > V7X_FREE_SUBLANE_STRIDE else 1


def _attn_batch(units):
    scores = [lax.dot_general(q.astype(BF16), kc.astype(BF16), (((1,), (1,)), ((), ())),
                              preferred_element_type=F32) * ATTN_SCALE + bias
              for q, kc, _, bias in units]
    probs = []
    for s in scores:
        m = jnp.max(s, axis=-1, keepdims=True)
        p = jnp.exp(s - m)
        probs.append((p, m, jnp.sum(p, axis=-1, keepdims=True)))
    outs = []
    for (p, m, l), (_, _, vc, _) in zip(probs, units):
        o = jnp.dot(p.astype(BF16), vc.astype(BF16), preferred_element_type=F32)
        outs.append((o * (1.0 / l), m + jnp.log(l)))
    return outs


def _attn_kernel(*refs):
    ngrp = len(ATTN_GROUPS)
    in_refs = refs[:5 * ngrp]
    o_ref = refs[5 * ngrp]
    pos = 5 * ngrp + 1
    o_scr, l_scr = refs[pos:pos + ngrp], refs[pos + ngrp:pos + 2 * ngrp]
    k_scr, v_scr = refs[pos + 2 * ngrp:pos + 3 * ngrp], refs[pos + 3 * ngrp:pos + 4 * ngrp]
    split_refs = refs[pos + 4 * ngrp:pos + 4 * ngrp + 3]
    bias_ref = refs[-1]

    qi = lax.broadcasted_iota(jnp.int32, (BLK, 2 * BLK), 0)
    ki = lax.broadcasted_iota(jnp.int32, (BLK, 2 * BLK), 1)
    lo = jnp.where(pl.program_id(0) == 0, BLK, 0)
    bias_ref[0] = jnp.where((ki >= qi) & (ki <= qi + N_BACK), 0.0, NEG_INF)
    bias_ref[1] = jnp.where((ki >= jnp.maximum(qi, lo)) & (ki <= qi + N_BACK), 0.0, NEG_INF)

    for g, (_, d) in enumerate(ATTN_GROUPS):
        q_ref, k_ref, v_ref, kp_ref, vp_ref = in_refs[5 * g:5 * g + 5]
        og, lg, kcat, vcat = o_scr[g], l_scr[g], k_scr[g], v_scr[g]
        sub = d * BLK
        n_units = ATTN_SPAN // BLK
        assert n_units % ATTN_UNITS_PER_BATCH == 0
        f = _presplit(d)
        d2 = d // f
        klen, qlen, sublen = (sub + ATTN_SPAN) // f, ATTN_SPAN // f, sub // f

        for rf in range(f):
            for src_p, src, cat in ((kp_ref, k_ref, kcat), (vp_ref, v_ref, vcat)):
                cat[pl.ds(rf * klen, sublen), :] = _rows(src_p, rf, sublen, f)
                cat[pl.ds(rf * klen + sublen, qlen), :] = _rows(src, rf, qlen, f)
            if f > 1:
                split_refs[0][pl.ds(rf * qlen, qlen), :] = _rows(q_ref, rf, qlen, f)
        q_src, o_dst, l_dst = (split_refs[0], split_refs[1], split_refs[2]) if f > 1 else (q_ref, og, lg)

        def body(t, carry, d=d, f=f, d2=d2, klen=klen, qlen=qlen, sublen=sublen,
                 q_src=q_src, kcat=kcat, vcat=vcat, o_dst=o_dst, l_dst=l_dst):
            units, starts = [], []
            for u in range(ATTN_UNITS_PER_BATCH):
                idx = t * ATTN_UNITS_PER_BATCH + u
                r = idx % d
                base = (idx // d) * sublen + r // f
                qstart = (r % f) * qlen + base
                kstart = (r % f) * klen + base
                bias = bias_ref[jnp.where(idx < d, 1, 0)]
                units.append((_rows(q_src, qstart, BLK, d2), _rows(kcat, kstart, 2 * BLK, d2),
                              _rows(vcat, kstart, 2 * BLK, d2), bias))
                starts.append(qstart)
            for qstart, (o, lse) in zip(starts, _attn_batch(units)):
                idx = pl.ds(qstart, BLK) if d2 == 1 else pl.ds(qstart, BLK, stride=d2)
                o_dst[idx, :] = o
                l_dst[idx, :] = jnp.broadcast_to(lse, (BLK, HEAD_DIM))
            return carry

        lax.fori_loop(0, n_units // ATTN_UNITS_PER_BATCH, body, 0)
        if f > 1:
            for rf in range(f):
                og[pl.ds(rf, qlen, stride=f), :] = o_dst[pl.ds(rf * qlen, qlen), :]
                lg[pl.ds(rf, qlen, stride=f), :] = l_dst[pl.ds(rf * qlen, qlen), :]

    lses = [l[...] for l in l_scr]
    mx = functools.reduce(jnp.maximum, lses)
    ws = [jnp.exp(l - mx) for l in lses]
    num = functools.reduce(lambda a, b: a + b, [w * o[...] for w, o in zip(ws, o_scr)])
    den = functools.reduce(lambda a, b: a + b, ws)
    o_ref[...] = (num * (1.0 / den)).astype(o_ref.dtype)


def _dilated_attention(qk, vm):
    s = qk.shape[0]
    assert s % ATTN_SPAN == 0
    hpg = HEADS_PER_GROUP
    in_specs, args = [], []
    for g, (_, d) in enumerate(ATTN_GROUPS):
        sub = d * BLK
        ratio = ATTN_SPAN // sub
        qc, kc, vc = g * hpg, ATTN_HEADS + g * hpg, g * hpg
        cur = lambda col: pl.BlockSpec((ATTN_SPAN, HEAD_DIM), lambda i, h, col=col: (i, col + h))
        prev = lambda col, ratio=ratio, sub=sub: pl.BlockSpec(
            (sub, HEAD_DIM), lambda i, h, col=col, ratio=ratio: (jnp.maximum(i * ratio - 1, 0), col + h))
        in_specs += [cur(qc), cur(kc), cur(vc), prev(kc), prev(vc)]
        args += [qk, qk, vm, qk, vm]
    ngrp = len(ATTN_GROUPS)
    return pl.pallas_call(
        _attn_kernel,
        grid=(s // ATTN_SPAN, hpg),
        in_specs=in_specs,
        out_specs=pl.BlockSpec((ATTN_SPAN, HEAD_DIM), lambda i, h: (i, h)),
        out_shape=jax.ShapeDtypeStruct((s, ATTN_OUT_W), BF16),
        scratch_shapes=([pltpu.VMEM((ATTN_SPAN, HEAD_DIM), F32)] * (2 * ngrp)
                        + [pltpu.VMEM((d * BLK + ATTN_SPAN, HEAD_DIM), F32) for _, d in ATTN_GROUPS] * 2
                        + [pltpu.VMEM((ATTN_SPAN, HEAD_DIM), F32)] * 3
                        + [pltpu.VMEM((2, BLK, 2 * BLK), F32)]),
        compiler_params=_params("parallel", "arbitrary"),
        name="dilated_attention",
    )(*args)


def _ffn_down_kernel(a_ref, w_ref, x_ref, *rest, nk, final_norm):
    if final_norm:
        g_ref, o_ref = rest
    else:
        (o_ref,) = rest
    k = pl.program_id(1)
    cw = x_ref.shape[1]

    def accumulate(first):
        for c in range(o_ref.shape[1] // cw):
            cols = slice(c * cw, (c + 1) * cw)
            part = jnp.dot(a_ref[...], w_ref[:, cols].astype(BF16), preferred_element_type=F32)
            part = part + jnp.where(k == c, x_ref[...], 0.0)
            if first:
                o_ref[:, cols] = part
            else:
                o_ref[:, cols] += part

    pl.when(k == 0)(lambda: accumulate(True))
    pl.when(k > 0)(lambda: accumulate(False))

    if final_norm:
        @pl.when(k == nk - 1)
        def _():
            o_ref[...] = _rms_rows(o_ref[...], g_ref[...])


def _ffn_down(a, w, layer, x, final_g=None):
    s, width = x.shape
    tm, tk, cw = 2048, 512, 256
    nk = a.shape[1] // tk
    n_res = width // cw
    assert a.shape[1] % tk == 0 and w.shape[1:] == (nk * tk, width)
    assert s % tm == 0 and width % cw == 0 and n_res <= nk
    in_specs = [pl.BlockSpec((tm, tk), lambda i, k: (i, k)),
                pl.BlockSpec((None, tk, width), lambda i, k: (layer, k, 0)),
                pl.BlockSpec((tm, cw), lambda i, k: (i, jnp.minimum(k, n_res - 1)))]
    args = [a, w, x]
    if final_g is not None:
        in_specs.append(_gain_spec(0, width))
        args.append(final_g.reshape(1, 1, width))
    return pl.pallas_call(
        functools.partial(_ffn_down_kernel, nk=nk, final_norm=final_g is not None),
        grid=(s // tm, nk),
        in_specs=in_specs,
        out_specs=pl.BlockSpec((tm, width), lambda i, k: (i, 0)),
        out_shape=jax.ShapeDtypeStruct((s, width), F32),
        compiler_params=_params("parallel", "arbitrary"),
        name="ffn_down",
    )(*args)


def _ffn_gate_up_kernel(h_ref, wg_ref, wu_ref, o_ref):
    for c in range(o_ref.shape[1] // V7X_MXU_COLS):
        cols = slice(c * V7X_MXU_COLS, (c + 1) * V7X_MXU_COLS)
        gate = jnp.dot(h_ref[...], wg_ref[:, cols].astype(BF16), preferred_element_type=F32)
        up = jnp.dot(h_ref[...], wu_ref[:, cols].astype(BF16), preferred_element_type=F32)
        o_ref[:, cols] = (gate * (1.0 / (1.0 + jnp.exp(-gate))) * up).astype(o_ref.dtype)


def _ffn_gate_up(h, w_gate, w_up, layer):
    s, k = h.shape
    f = w_gate.shape[2]
    tm, tf = 2048, 512
    wspec = pl.BlockSpec((None, k, tf), lambda i, j: (layer, 0, j))
    return pl.pallas_call(
        _ffn_gate_up_kernel,
        grid=(s // tm, f // tf),
        in_specs=[pl.BlockSpec((tm, k), lambda i, j: (i, 0)), wspec, wspec],
        out_specs=pl.BlockSpec((tm, tf), lambda i, j: (i, j)),
        out_shape=jax.ShapeDtypeStruct((s, f), BF16),
        compiler_params=_params("parallel", "arbitrary"),
        name="ffn_gate_up",
    )(h, w_gate, w_up)


def kernel(x, mem, positions, mix_norm, mem_norm, w_mem_kv, ffn_norm, w_gate, w_up, w_down,
           attn_w_in, attn_w_out, sgu_w_in, sgu_ln_g, sgu_ln_b, sgu_w_spatial, sgu_b_spatial,
           sgu_w_out, final_norm):
    b, s, d = x.shape
    assert (b, s, d) == (1, SEQ, D_MODEL) and mem.shape == (1, MEM_LEN, D_MODEL)
    xs = x.reshape(s, d)
    mems = mem.reshape(MEM_LEN, d)

    inv_freq = ROPE_THETA ** (-jnp.arange(ROT_HALF, dtype=F32) / ROT_HALF)
    invf = jnp.concatenate([inv_freq, inv_freq, jnp.zeros((HEAD_DIM - ROT_DIM,), F32)])
    rope = _rope_tables(positions.reshape(s, 1), invf.reshape(1, HEAD_DIM))

    for i in range(DEPTH):
        j = i // 2
        kv = _mem_kv(mems, mem_norm, w_mem_kv, i)
        if i % 2 == 0:
            qk, vm = _attn_in_proj(xs, mix_norm, i, _cast_bf16(attn_w_in, j), rope)
            mix = _dilated_attention(qk, vm)
            mem_out = _memory_attention(vm, ATTN_W // MEM_W, kv)
            w_out = attn_w_out
        else:
            mix, q_mem = _sgu_in_proj(xs, mix_norm, i, _cast_bf16(sgu_w_in, j), sgu_ln_g, sgu_ln_b,
                                      sgu_w_spatial, jnp.swapaxes(sgu_b_spatial, 1, 2), j)
            mem_out = _memory_attention(q_mem, 0, kv)
            w_out = sgu_w_out
        xs, hn = _out_proj(mix, mem_out, _cast_bf16(w_out, j), xs, ffn_norm, i)
        act = _ffn_gate_up(hn, w_gate, w_up, i)
        xs = _ffn_down(act, w_down, i, xs, final_g=final_norm if i == DEPTH - 1 else None)
    return xs.reshape(b, s, d)
```

```python
import functools
import math

import jax
import jax.numpy as jnp
from jax import lax
from jax.experimental import pallas as pl
from jax.experimental.pallas import tpu as pltpu

D_MODEL = 2048
SEQ = 8192
DEPTH = 2
MEM_LEN = 256
HEAD_DIM = 128
MEM_HEADS = 4
MEM_W = MEM_HEADS * HEAD_DIM
ATTN_GROUPS = ((128, 1), (512, 4), (2048, 16))
ATTN_HEADS = 12
HEADS_PER_GROUP = 4
ATTN_W = ATTN_HEADS * HEAD_DIM
ATTN_OUT_W = HEADS_PER_GROUP * HEAD_DIM
BLK = 128
SGU_GROUPS = 12
SGU_W = SGU_GROUPS * HEAD_DIM
SGU_CHUNK = 128
ROT_DIM = HEAD_DIM // 4
ROT_HALF = ROT_DIM // 2
ROPE_THETA = 500000.0
FFN_HIDDEN = 5632
NORM_EPS = 1e-6
LN_EPS = 1e-5
NEG_INF = -1e30
ATTN_SCALE = HEAD_DIM ** -0.5

N_BACK = BLK
assert all(w // d == N_BACK for w, d in ATTN_GROUPS)
ATTN_SPAN = max(d for _, d in ATTN_GROUPS) * BLK

V7X_VMEM_LIMIT_BYTES = 56 * 1024 * 1024
V7X_MXU_COLS = 256

F32 = jnp.float32
BF16 = jnp.bfloat16


def _params(*semantics):
    return pltpu.CompilerParams(dimension_semantics=semantics,
                                vmem_limit_bytes=V7X_VMEM_LIMIT_BYTES)


def _rms_rows(x, g):
    ms = jnp.mean(x * x, axis=-1, keepdims=True)
    return x * lax.rsqrt(ms + NORM_EPS) * g


def _gain_spec(layer, k):
    return pl.BlockSpec((None, 1, k), lambda *_: (layer, 0, 0))


def _gain_arg(g):
    return g.reshape(g.shape[0], 1, g.shape[1])


def _rope_tables(pos_ref, invf_ref, c_ref, s1_ref, s2_ref):
    ang = pos_ref[...].astype(F32) * invf_ref[...]
    lane = lax.broadcasted_iota(jnp.int32, ang.shape, 1)
    cos = jnp.cos(ang)
    sin = jnp.sin(ang)
    c_ref[...] = jnp.where(lane < ROT_DIM, cos, 1.0)
    s1_ref[...] = jnp.where(lane < ROT_HALF, 0.0, jnp.where(lane < ROT_DIM, sin, 0.0))
    s2_ref[...] = jnp.where(lane < ROT_HALF, -sin, 0.0)


def _mem_kv_kernel(x_ref, g_ref, w_ref, o_ref):
    h = _rms_rows(x_ref[...], g_ref[...]).astype(BF16)
    o_ref[...] = jnp.dot(h, w_ref[...].astype(BF16), preferred_element_type=F32)


def _mem_kv(x, g, w, layer):
    m, k = x.shape
    n = w.shape[2]
    tn = 512
    assert n % tn == 0
    return pl.pallas_call(
        _mem_kv_kernel,
        grid=(n // tn,),
        in_specs=[pl.BlockSpec((m, k), lambda j: (0, 0)), _gain_spec(layer, k),
                  pl.BlockSpec((None, k, tn), lambda j: (layer, 0, j))],
        out_specs=pl.BlockSpec((m, tn), lambda j: (0, j)),
        out_shape=jax.ShapeDtypeStruct((m, n), F32),
        compiler_params=_params("parallel"),
        name="mem_kv",
    )(x, _gain_arg(g), w)


def _cast_kernel(w_ref, o_ref):
    o_ref[...] = w_ref[...].astype(BF16)


def _cast_bf16(w, layer):
    _, k, n = w.shape
    tk = 256
    assert k % tk == 0
    return pl.pallas_call(
        _cast_kernel,
        grid=(k // tk,),
        in_specs=[pl.BlockSpec((None, tk, n), lambda i: (layer, i, 0))],
        out_specs=pl.BlockSpec((tk, n), lambda i: (i, 0)),
        out_shape=jax.ShapeDtypeStruct((k, n), BF16),
        compiler_params=_params("parallel"),
        name="cast_bf16",
    )(w)


BF16_SUBLANES = 16


def _cast_jobs(weights, n_steps, step_of):
    in_specs, args, out_specs, out_shape = [], [], [], []
    for w, layer in weights:
        _, k, n = w.shape
        slab = -(-(-(-k // n_steps)) // BF16_SUBLANES) * BF16_SUBLANES
        while k % slab:
            slab += BF16_SUBLANES
        n_slabs = k // slab
        in_specs.append(pl.BlockSpec(
            (None, slab, n), lambda *g, layer=layer, n_slabs=n_slabs:
            (layer, jnp.minimum(step_of(*g), n_slabs - 1), 0)))
        out_specs.append(pl.BlockSpec(
            (slab, n), lambda *g, n_slabs=n_slabs: (jnp.minimum(step_of(*g), n_slabs - 1), 0)))
        args.append(w)
        out_shape.append(jax.ShapeDtypeStruct((k, n), BF16))
    return in_specs, args, out_specs, out_shape


def _run_cast_jobs(in_refs, out_refs):
    for src, dst in zip(in_refs, out_refs):
        dst[...] = src[...].astype(BF16)


def _resident_spec(shape):
    return pl.BlockSpec(shape, lambda *_: (0,) * len(shape), pipeline_mode=pl.Buffered(1))


def _gelu_tanh(x):
    return x * (0.5 * (1.0 + jnp.tanh(math.sqrt(2.0 / math.pi) * (x + 0.044715 * (x * x * x)))))


IN_PROJ_ROWS = 512


def _pass(h_ref, w_ref, col):
    return jnp.dot(h_ref[...], w_ref[:, col:col + V7X_MXU_COLS], preferred_element_type=F32)


def _heads(acc):
    return [acc[:, j * HEAD_DIM:(j + 1) * HEAD_DIM] for j in range(acc.shape[1] // HEAD_DIM)]


def _attn_in_proj_kernel(x_ref, g_ref, w_ref, pos_ref, invf_ref, *rest, n_jobs):
    qk_ref, vm_ref = rest[n_jobs:n_jobs + 2]
    h_ref, c_ref, s1_ref, s2_ref = rest[2 * n_jobs + 2:]
    _run_cast_jobs(rest[:n_jobs], rest[n_jobs + 2:2 * n_jobs + 2])
    h_ref[...] = _rms_rows(x_ref[...], g_ref[...]).astype(BF16)
    _rope_tables(pos_ref, invf_ref, c_ref, s1_ref, s2_ref)
    for c in range(2 * ATTN_W // V7X_MXU_COLS):
        for j, t in enumerate(_heads(_pass(h_ref, w_ref, c * V7X_MXU_COLS))):
            lo = c * V7X_MXU_COLS + j * HEAD_DIM
            qk_ref[:, lo:lo + HEAD_DIM] = (t * c_ref[...] + pltpu.roll(t, ROT_HALF, 1) * s1_ref[...]
                                           + pltpu.roll(t, HEAD_DIM - ROT_HALF, 1) * s2_ref[...])
    for c in range((ATTN_W + MEM_W) // V7X_MXU_COLS):
        cols = slice(c * V7X_MXU_COLS, (c + 1) * V7X_MXU_COLS)
        vm_ref[:, cols] = _pass(h_ref, w_ref, 2 * ATTN_W + c * V7X_MXU_COLS)


def _attn_in_proj(x, g, g_layer, w_bf, pos_col, invf, cast_weights=()):
    s, k = x.shape
    tm = IN_PROJ_ROWS
    assert w_bf.shape == (k, 3 * ATTN_W + MEM_W)
    row = lambda cols: pl.BlockSpec((tm, cols), lambda i: (i, 0))
    job_in, job_args, job_out, job_shape = _cast_jobs(cast_weights, s // tm, lambda i: i)
    return pl.pallas_call(
        functools.partial(_attn_in_proj_kernel, n_jobs=len(job_args)),
        grid=(s // tm,),
        in_specs=[row(k), _gain_spec(g_layer, k), _resident_spec(w_bf.shape), row(1),
                  pl.BlockSpec((1, HEAD_DIM), lambda i: (0, 0))] + job_in,
        out_specs=[row(2 * ATTN_W), row(ATTN_W + MEM_W)] + job_out,
        out_shape=[jax.ShapeDtypeStruct((s, 2 * ATTN_W), F32),
                   jax.ShapeDtypeStruct((s, ATTN_W + MEM_W), F32)] + job_shape,
        scratch_shapes=[pltpu.VMEM((tm, k), BF16)] + [pltpu.VMEM((tm, HEAD_DIM), F32)] * 3,
        compiler_params=_params("arbitrary"),
        name="attn_in_proj",
    )(x, _gain_arg(g), w_bf, pos_col, invf, *job_args)


def _sgu_in_proj_kernel(x_ref, g_ref, w_ref, lng_ref, lnb_ref, ws_ref, bt_ref,
                        mix_ref, qm_ref, h_ref, v_scr, vn_scr, wsb_scr):
    @pl.when(pl.program_id(0) == 0)
    def _():
        t = lax.broadcasted_iota(jnp.int32, (SGU_CHUNK, SGU_CHUNK), 0)
        s = lax.broadcasted_iota(jnp.int32, (SGU_CHUNK, SGU_CHUNK), 1)
        for g in range(SGU_GROUPS):
            wsb_scr[g] = jnp.where(t >= s, ws_ref[g], 0.0).astype(BF16)

    h_ref[...] = _rms_rows(x_ref[...], g_ref[...]).astype(BF16)
    for c in range(SGU_W // V7X_MXU_COLS):
        cols = slice(c * V7X_MXU_COLS, (c + 1) * V7X_MXU_COLS)
        v_scr[:, cols] = _gelu_tanh(_pass(h_ref, w_ref, SGU_W + c * V7X_MXU_COLS))
    for c in range(MEM_W // V7X_MXU_COLS):
        cols = slice(c * V7X_MXU_COLS, (c + 1) * V7X_MXU_COLS)
        qm_ref[:, cols] = _pass(h_ref, w_ref, 2 * SGU_W + c * V7X_MXU_COLS)
    v = v_scr[...]
    mu = jnp.mean(v, axis=-1, keepdims=True)
    vc = v - mu
    var = jnp.mean(vc * vc, axis=-1, keepdims=True)
    vn_scr[...] = (vc * lax.rsqrt(var + LN_EPS) * lng_ref[...] + lnb_ref[...]).astype(BF16)

    bt = bt_ref[...]
    n_chunks = mix_ref.shape[0] // SGU_CHUNK
    for c in range(SGU_W // V7X_MXU_COLS):
        for j, u in enumerate(_heads(_gelu_tanh(_pass(h_ref, w_ref, c * V7X_MXU_COLS)))):
            g = c * (V7X_MXU_COLS // HEAD_DIM) + j
            cols = slice(g * HEAD_DIM, (g + 1) * HEAD_DIM)
            vg = jnp.concatenate([vn_scr[r * SGU_CHUNK:(r + 1) * SGU_CHUNK, cols]
                                  for r in range(n_chunks)], axis=1)
            mixed = jnp.dot(wsb_scr[g], vg, preferred_element_type=F32) + bt[:, g:g + 1]
            for r in range(n_chunks):
                rows = slice(r * SGU_CHUNK, (r + 1) * SGU_CHUNK)
                mix_ref[rows, cols] = (u[rows, :] * mixed[:, r * HEAD_DIM:(r + 1) * HEAD_DIM]
                                       ).astype(mix_ref.dtype)


def _sgu_in_proj(x, g, g_layer, w_bf, ln_g, ln_b, w_spatial, b_spatial_t, layer):
    s, k = x.shape
    tm = IN_PROJ_ROWS
    assert w_bf.shape == (k, 2 * SGU_W + MEM_W) and tm % SGU_CHUNK == 0
    row = lambda cols: pl.BlockSpec((tm, cols), lambda i: (i, 0))
    return pl.pallas_call(
        _sgu_in_proj_kernel,
        grid=(s // tm,),
        in_specs=[row(k), _gain_spec(g_layer, k), _resident_spec(w_bf.shape),
                  _gain_spec(layer, SGU_W), _gain_spec(layer, SGU_W),
                  pl.BlockSpec((None, SGU_GROUPS, SGU_CHUNK, SGU_CHUNK), lambda i: (layer, 0, 0, 0)),
                  pl.BlockSpec((None, SGU_CHUNK, SGU_GROUPS), lambda i: (layer, 0, 0))],
        out_specs=[row(SGU_W), row(MEM_W)],
        out_shape=[jax.ShapeDtypeStruct((s, SGU_W), BF16), jax.ShapeDtypeStruct((s, MEM_W), F32)],
        scratch_shapes=[pltpu.VMEM((tm, k), BF16), pltpu.VMEM((tm, SGU_W), F32),
                        pltpu.VMEM((tm, SGU_W), BF16),
                        pltpu.VMEM((SGU_GROUPS, SGU_CHUNK, SGU_CHUNK), BF16)],
        compiler_params=_params("arbitrary"),
        name="sgu_in_proj",
    )(x, _gain_arg(g), w_bf, _gain_arg(ln_g), _gain_arg(ln_b), w_spatial, b_spatial_t)


def _mem_attn_kernel(q_ref, kv_ref, o_ref):
    for h in range(MEM_HEADS):
        sl = slice(h * HEAD_DIM, (h + 1) * HEAD_DIM)
        k = kv_ref[:, sl].astype(BF16)
        v = kv_ref[:, MEM_W + h * HEAD_DIM:MEM_W + (h + 1) * HEAD_DIM].astype(BF16)
        s = lax.dot_general(q_ref[:, sl].astype(BF16), k, (((1,), (1,)), ((), ())),
                            preferred_element_type=F32) * ATTN_SCALE
        m = jnp.max(s, axis=-1, keepdims=True)
        p = jnp.exp(s - m)
        l = jnp.sum(p, axis=-1, keepdims=True)
        o = jnp.dot(p.astype(BF16), v, preferred_element_type=F32)
        o_ref[:, sl] = (o * (1.0 / l)).astype(o_ref.dtype)


def _memory_attention(proj, q_col_block, kv):
    s = proj.shape[0]
    tm = 1024
    return pl.pallas_call(
        _mem_attn_kernel,
        grid=(s // tm,),
        in_specs=[pl.BlockSpec((tm, MEM_W), lambda i: (i, q_col_block)),
                  pl.BlockSpec((MEM_LEN, 2 * MEM_W), lambda i: (0, 0))],
        out_specs=pl.BlockSpec((tm, MEM_W), lambda i: (i, 0)),
        out_shape=jax.ShapeDtypeStruct((s, MEM_W), BF16),
        compiler_params=_params("parallel"),
        name="memory_attention",
    )(proj, kv)


def _out_proj_kernel(a_ref, b_ref, w_ref, x_ref, g_ref, o_ref, hn_ref):
    ka = a_ref.shape[1]
    width = o_ref.shape[1]
    n_chunk = 512
    ssq = jnp.zeros((o_ref.shape[0], 1), F32)
    for c in range(width // n_chunk):
        cols = slice(c * n_chunk, (c + 1) * n_chunk)
        y = (x_ref[:, cols]
             + jnp.dot(a_ref[...], w_ref[:ka, cols], preferred_element_type=F32)
             + jnp.dot(b_ref[...], w_ref[ka:, cols], preferred_element_type=F32))
        o_ref[:, cols] = y
        ssq = ssq + jnp.sum(y * y, axis=-1, keepdims=True)
    scale = lax.rsqrt(ssq * (1.0 / width) + NORM_EPS)
    for c in range(width // n_chunk):
        cols = slice(c * n_chunk, (c + 1) * n_chunk)
        hn_ref[:, cols] = (o_ref[:, cols] * scale * g_ref[:, cols]).astype(BF16)


def _out_proj(a, b, w_bf, x, g, g_layer):
    s, ka = a.shape
    kb = b.shape[1]
    width = x.shape[1]
    tm = 512
    assert w_bf.shape == (ka + kb, width)
    row = lambda cols: pl.BlockSpec((tm, cols), lambda i: (i, 0))
    return pl.pallas_call(
        _out_proj_kernel,
        grid=(s // tm,),
        in_specs=[row(ka), row(kb), _resident_spec((ka + kb, width)), row(width),
                  _gain_spec(g_layer, width)],
        out_specs=[row(width), row(width)],
        out_shape=[jax.ShapeDtypeStruct((s, width), F32), jax.ShapeDtypeStruct((s, width), BF16)],
        compiler_params=_params("parallel"),
        name="out_proj",
    )(a, b, w_bf, x, _gain_arg(g))


def _rows(ref, start, size, stride):
    if stride == 1:
        return ref[pl.ds(start, size), :]
    return ref[pl.ds(start, size, stride=stride), :]


ATTN_UNITS_PER_BATCH = 16
V7X_FREE_SUBLANE_STRIDE = 4


def _presplit(d):
    return d // V7X_FREE_SUBLANE_STRIDE if d > V7X_FREE_SUBLANE_STRIDE else 1


def _attn_batch(units):
    scores = [lax.dot_general(q.astype(BF16), kc.astype(BF16), (((1,), (1,)), ((), ())),
                              preferred_element_type=F32) * ATTN_SCALE + bias
              for q, kc, _, bias in units]
    probs = []
    for s in scores:
        m = jnp.max(s, axis=-1, keepdims=True)
        p = jnp.exp(s - m)
        probs.append((p, m, jnp.sum(p, axis=-1, keepdims=True)))
    outs = []
    for (p, m, l), (_, _, vc, _) in zip(probs, units):
        o = jnp.dot(p.astype(BF16), vc.astype(BF16), preferred_element_type=F32)
        outs.append((o * (1.0 / l), m + jnp.log(l)))
    return outs


def _attn_kernel(*refs):
    ngrp = len(ATTN_GROUPS)
    in_refs = refs[:5 * ngrp]
    o_ref = refs[5 * ngrp]
    pos = 5 * ngrp + 1
    o_scr, l_scr = refs[pos:pos + ngrp], refs[pos + ngrp:pos + 2 * ngrp]
    k_scr, v_scr = refs[pos + 2 * ngrp:pos + 3 * ngrp], refs[pos + 3 * ngrp:pos + 4 * ngrp]
    split_refs = refs[pos + 4 * ngrp:pos + 4 * ngrp + 3]
    bias_ref = refs[-1]

    qi = lax.broadcasted_iota(jnp.int32, (BLK, 2 * BLK), 0)
    ki = lax.broadcasted_iota(jnp.int32, (BLK, 2 * BLK), 1)
    lo = jnp.where(pl.program_id(0) == 0, BLK, 0)
    bias_ref[0] = jnp.where((ki >= qi) & (ki <= qi + N_BACK), 0.0, NEG_INF)
    bias_ref[1] = jnp.where((ki >= jnp.maximum(qi, lo)) & (ki <= qi + N_BACK), 0.0, NEG_INF)

    for g, (_, d) in enumerate(ATTN_GROUPS):
        q_ref, k_ref, v_ref, kp_ref, vp_ref = in_refs[5 * g:5 * g + 5]
        og, lg, kcat, vcat = o_scr[g], l_scr[g], k_scr[g], v_scr[g]
        sub = d * BLK
        n_units = ATTN_SPAN // BLK
        assert n_units % ATTN_UNITS_PER_BATCH == 0
        f = _presplit(d)
        d2 = d // f
        klen, qlen, sublen = (sub + ATTN_SPAN) // f, ATTN_SPAN // f, sub // f

        for rf in range(f):
            for src_p, src, cat in ((kp_ref, k_ref, kcat), (vp_ref, v_ref, vcat)):
                cat[pl.ds(rf * klen, sublen), :] = _rows(src_p, rf, sublen, f)
                cat[pl.ds(rf * klen + sublen, qlen), :] = _rows(src, rf, qlen, f)
            if f > 1:
                split_refs[0][pl.ds(rf * qlen, qlen), :] = _rows(q_ref, rf, qlen, f)
        q_src, o_dst, l_dst = (split_refs[0], split_refs[1], split_refs[2]) if f > 1 else (q_ref, og, lg)

        def body(t, carry, d=d, f=f, d2=d2, klen=klen, qlen=qlen, sublen=sublen,
                 q_src=q_src, kcat=kcat, vcat=vcat, o_dst=o_dst, l_dst=l_dst):
            units, starts = [], []
            for u in range(ATTN_UNITS_PER_BATCH):
                idx = t * ATTN_UNITS_PER_BATCH + u
                r = idx % d
                base = (idx // d) * sublen + r // f
                qstart = (r % f) * qlen + base
                kstart = (r % f) * klen + base
                bias = bias_ref[jnp.where(idx < d, 1, 0)]
                units.append((_rows(q_src, qstart, BLK, d2), _rows(kcat, kstart, 2 * BLK, d2),
                              _rows(vcat, kstart, 2 * BLK, d2), bias))
                starts.append(qstart)
            for qstart, (o, lse) in zip(starts, _attn_batch(units)):
                idx = pl.ds(qstart, BLK) if d2 == 1 else pl.ds(qstart, BLK, stride=d2)
                o_dst[idx, :] = o
                l_dst[idx, :] = jnp.broadcast_to(lse, (BLK, HEAD_DIM))
            return carry

        lax.fori_loop(0, n_units // ATTN_UNITS_PER_BATCH, body, 0)
        if f > 1:
            for rf in range(f):
                og[pl.ds(rf, qlen, stride=f), :] = o_dst[pl.ds(rf * qlen, qlen), :]
                lg[pl.ds(rf, qlen, stride=f), :] = l_dst[pl.ds(rf * qlen, qlen), :]

    lses = [l[...] for l in l_scr]
    mx = functools.reduce(jnp.maximum, lses)
    ws = [jnp.exp(l - mx) for l in lses]
    num = functools.reduce(lambda a, b: a + b, [w * o[...] for w, o in zip(ws, o_scr)])
    den = functools.reduce(lambda a, b: a + b, ws)
    o_ref[...] = (num * (1.0 / den)).astype(o_ref.dtype)


def _dilated_attention(qk, vm):
    s = qk.shape[0]
    assert s % ATTN_SPAN == 0
    hpg = HEADS_PER_GROUP
    in_specs, args = [], []
    for g, (_, d) in enumerate(ATTN_GROUPS):
        sub = d * BLK
        ratio = ATTN_SPAN // sub
        qc, kc, vc = g * hpg, ATTN_HEADS + g * hpg, g * hpg
        cur = lambda col: pl.BlockSpec((ATTN_SPAN, HEAD_DIM), lambda i, h, col=col: (i, col + h))
        prev = lambda col, ratio=ratio, sub=sub: pl.BlockSpec(
            (sub, HEAD_DIM), lambda i, h, col=col, ratio=ratio: (jnp.maximum(i * ratio - 1, 0), col + h))
        in_specs += [cur(qc), cur(kc), cur(vc), prev(kc), prev(vc)]
        args += [qk, qk, vm, qk, vm]
    ngrp = len(ATTN_GROUPS)
    return pl.pallas_call(
        _attn_kernel,
        grid=(s // ATTN_SPAN, hpg),
        in_specs=in_specs,
        out_specs=pl.BlockSpec((ATTN_SPAN, HEAD_DIM), lambda i, h: (i, h)),
        out_shape=jax.ShapeDtypeStruct((s, ATTN_OUT_W), BF16),
        scratch_shapes=([pltpu.VMEM((ATTN_SPAN, HEAD_DIM), F32)] * (2 * ngrp)
                        + [pltpu.VMEM((d * BLK + ATTN_SPAN, HEAD_DIM), F32) for _, d in ATTN_GROUPS] * 2
                        + [pltpu.VMEM((ATTN_SPAN, HEAD_DIM), F32)] * 3
                        + [pltpu.VMEM((2, BLK, 2 * BLK), F32)]),
        compiler_params=_params("parallel", "arbitrary"),
        name="dilated_attention",
    )(*args)


def _ffn_down_kernel(a_ref, w_ref, x_ref, *rest, nk, final_norm):
    if final_norm:
        g_ref, o_ref = rest
    else:
        (o_ref,) = rest
    k = pl.program_id(1)
    cw = x_ref.shape[1]

    def accumulate(first):
        for c in range(o_ref.shape[1] // cw):
            cols = slice(c * cw, (c + 1) * cw)
            part = jnp.dot(a_ref[...], w_ref[:, cols].astype(BF16), preferred_element_type=F32)
            part = part + jnp.where(k == c, x_ref[...], 0.0)
            if first:
                o_ref[:, cols] = part
            else:
                o_ref[:, cols] += part

    pl.when(k == 0)(lambda: accumulate(True))
    pl.when(k > 0)(lambda: accumulate(False))

    if final_norm:
        @pl.when(k == nk - 1)
        def _():
            o_ref[...] = _rms_rows(o_ref[...], g_ref[...])


def _ffn_down(a, w, layer, x, final_g=None):
    s, width = x.shape
    tm, tk, cw = 2048, 512, 256
    nk = a.shape[1] // tk
    n_res = width // cw
    assert a.shape[1] % tk == 0 and w.shape[1:] == (nk * tk, width)
    assert s % tm == 0 and width % cw == 0 and n_res <= nk
    in_specs = [pl.BlockSpec((tm, tk), lambda i, k: (i, k)),
                pl.BlockSpec((None, tk, width), lambda i, k: (layer, k, 0)),
                pl.BlockSpec((tm, cw), lambda i, k: (i, jnp.minimum(k, n_res - 1)))]
    args = [a, w, x]
    if final_g is not None:
        in_specs.append(_gain_spec(0, width))
        args.append(final_g.reshape(1, 1, width))
    return pl.pallas_call(
        functools.partial(_ffn_down_kernel, nk=nk, final_norm=final_g is not None),
        grid=(s // tm, nk),
        in_specs=in_specs,
        out_specs=pl.BlockSpec((tm, width), lambda i, k: (i, 0)),
        out_shape=jax.ShapeDtypeStruct((s, width), F32),
        compiler_params=_params("parallel", "arbitrary"),
        name="ffn_down",
    )(*args)


def _ffn_gate_up_kernel(h_ref, wg_ref, wu_ref, *rest, n_jobs):
    o_ref = rest[n_jobs]
    _run_cast_jobs(rest[:n_jobs], rest[n_jobs + 1:])
    for c in range(o_ref.shape[1] // V7X_MXU_COLS):
        cols = slice(c * V7X_MXU_COLS, (c + 1) * V7X_MXU_COLS)
        gate = jnp.dot(h_ref[...], wg_ref[:, cols].astype(BF16), preferred_element_type=F32)
        up = jnp.dot(h_ref[...], wu_ref[:, cols].astype(BF16), preferred_element_type=F32)
        o_ref[:, cols] = (gate * (1.0 / (1.0 + jnp.exp(-gate))) * up).astype(o_ref.dtype)


def _ffn_gate_up(h, w_gate, w_up, layer, cast_weights=()):
    s, k = h.shape
    f = w_gate.shape[2]
    tm, tf = 2048, 512
    nj = f // tf
    wspec = pl.BlockSpec((None, k, tf), lambda i, j: (layer, 0, j))
    job_in, job_args, job_out, job_shape = _cast_jobs(cast_weights, (s // tm) * nj,
                                                      lambda i, j: i * nj + j)
    return pl.pallas_call(
        functools.partial(_ffn_gate_up_kernel, n_jobs=len(job_args)),
        grid=(s // tm, nj),
        in_specs=[pl.BlockSpec((tm, k), lambda i, j: (i, 0)), wspec, wspec] + job_in,
        out_specs=[pl.BlockSpec((tm, tf), lambda i, j: (i, j))] + job_out,
        out_shape=[jax.ShapeDtypeStruct((s, f), BF16)] + job_shape,
        compiler_params=_params("arbitrary", "arbitrary"),
        name="ffn_gate_up",
    )(h, w_gate, w_up, *job_args)


def kernel(x, mem, positions, mix_norm, mem_norm, w_mem_kv, ffn_norm, w_gate, w_up, w_down,
           attn_w_in, attn_w_out, sgu_w_in, sgu_ln_g, sgu_ln_b, sgu_w_spatial, sgu_b_spatial,
           sgu_w_out, final_norm):
    b, s, d = x.shape
    assert (b, s, d) == (1, SEQ, D_MODEL) and mem.shape == (1, MEM_LEN, D_MODEL)
    xs = x.reshape(s, d)
    mems = mem.reshape(MEM_LEN, d)

    inv_freq = ROPE_THETA ** (-jnp.arange(ROT_HALF, dtype=F32) / ROT_HALF)
    invf = jnp.concatenate([inv_freq, inv_freq, jnp.zeros((HEAD_DIM - ROT_DIM,), F32)])
    pos_col, invf = positions.reshape(s, 1), invf.reshape(1, HEAD_DIM)

    def mixer_weights(layer):
        return (attn_w_in, attn_w_out) if layer % 2 == 0 else (sgu_w_in, sgu_w_out)

    w_in_bf, w_out_bf = _cast_bf16(attn_w_in, 0), None
    for i in range(DEPTH):
        j = i // 2
        kv = _mem_kv(mems, mem_norm, w_mem_kv, i)
        if i % 2 == 0:
            casts = [(attn_w_out, j)] if w_out_bf is None else []
            qk, vm, *cast = _attn_in_proj(xs, mix_norm, i, w_in_bf, pos_col, invf, casts)
            w_out_bf = cast[0] if cast else w_out_bf
            mix = _dilated_attention(qk, vm)
            mem_out = _memory_attention(vm, ATTN_W // MEM_W, kv)
        else:
            mix, q_mem = _sgu_in_proj(xs, mix_norm, i, w_in_bf, sgu_ln_g, sgu_ln_b,
                                      sgu_w_spatial, jnp.swapaxes(sgu_b_spatial, 1, 2), j)
            mem_out = _memory_attention(q_mem, 0, kv)
        xs, hn = _out_proj(mix, mem_out, w_out_bf, xs, ffn_norm, i)
        casts = [(w, (i + 1) // 2) for w in mixer_weights(i + 1)] if i + 1 < DEPTH else []
        act, *cast = _ffn_gate_up(hn, w_gate, w_up, i, casts)
        w_in_bf, w_out_bf = cast if cast else (None, None)
        xs = _ffn_down(act, w_down, i, xs, final_g=final_norm if i == DEPTH - 1 else None)
    return xs.reshape(b, s, d)
```

```python
import functools
import math

import jax
import jax.numpy as jnp
from jax import lax
from jax.experimental import pallas as pl
from jax.experimental.pallas import tpu as pltpu

D_MODEL = 2048
SEQ = 8192
DEPTH = 2
MEM_LEN = 256
HEAD_DIM = 128
MEM_HEADS = 4
MEM_W = MEM_HEADS * HEAD_DIM
ATTN_GROUPS = ((128, 1), (512, 4), (2048, 16))
ATTN_HEADS = 12
HEADS_PER_GROUP = 4
ATTN_W = ATTN_HEADS * HEAD_DIM
ATTN_OUT_W = HEADS_PER_GROUP * HEAD_DIM
BLK = 128
SGU_GROUPS = 12
SGU_W = SGU_GROUPS * HEAD_DIM
SGU_CHUNK = 128
ROT_DIM = HEAD_DIM // 4
ROT_HALF = ROT_DIM // 2
ROPE_THETA = 500000.0
FFN_HIDDEN = 5632
NORM_EPS = 1e-6
LN_EPS = 1e-5
NEG_INF = -1e30
ATTN_SCALE = HEAD_DIM ** -0.5

N_BACK = BLK
assert all(w // d == N_BACK for w, d in ATTN_GROUPS)
ATTN_SPAN = max(d for _, d in ATTN_GROUPS) * BLK

V7X_VMEM_LIMIT_BYTES = 56 * 1024 * 1024
V7X_MXU_COLS = 256

F32 = jnp.float32
BF16 = jnp.bfloat16


def _params(*semantics):
    return pltpu.CompilerParams(dimension_semantics=semantics,
                                vmem_limit_bytes=V7X_VMEM_LIMIT_BYTES)


def _rms_rows(x, g):
    ms = jnp.mean(x * x, axis=-1, keepdims=True)
    return x * lax.rsqrt(ms + NORM_EPS) * g


def _gain_spec(layer, k):
    return pl.BlockSpec((None, 1, k), lambda *_: (layer, 0, 0))


def _gain_arg(g):
    return g.reshape(g.shape[0], 1, g.shape[1])


def _rope_tables(pos_ref, invf_ref, c_ref, s1_ref, s2_ref):
    ang = pos_ref[...].astype(F32) * invf_ref[...]
    lane = lax.broadcasted_iota(jnp.int32, ang.shape, 1)
    cos = jnp.cos(ang)
    sin = jnp.sin(ang)
    c_ref[...] = jnp.where(lane < ROT_DIM, cos, 1.0)
    s1_ref[...] = jnp.where(lane < ROT_HALF, 0.0, jnp.where(lane < ROT_DIM, sin, 0.0))
    s2_ref[...] = jnp.where(lane < ROT_HALF, -sin, 0.0)


def _mem_kv_kernel(x_ref, g_ref, w_ref, o_ref):
    h = _rms_rows(x_ref[...], g_ref[...]).astype(BF16)
    o_ref[...] = jnp.dot(h, w_ref[...].astype(BF16), preferred_element_type=F32)


def _mem_kv(x, g, w, layer):
    m, k = x.shape
    n = w.shape[2]
    tn = 512
    assert n % tn == 0
    return pl.pallas_call(
        _mem_kv_kernel,
        grid=(n // tn,),
        in_specs=[pl.BlockSpec((m, k), lambda j: (0, 0)), _gain_spec(layer, k),
                  pl.BlockSpec((None, k, tn), lambda j: (layer, 0, j))],
        out_specs=pl.BlockSpec((m, tn), lambda j: (0, j)),
        out_shape=jax.ShapeDtypeStruct((m, n), F32),
        compiler_params=_params("parallel"),
        name="mem_kv",
    )(x, _gain_arg(g), w)


def _cast_kernel(w_ref, o_ref):
    o_ref[...] = w_ref[...].astype(BF16)


def _cast_bf16(w, layer):
    _, k, n = w.shape
    tk = 256
    assert k % tk == 0
    return pl.pallas_call(
        _cast_kernel,
        grid=(k // tk,),
        in_specs=[pl.BlockSpec((None, tk, n), lambda i: (layer, i, 0))],
        out_specs=pl.BlockSpec((tk, n), lambda i: (i, 0)),
        out_shape=jax.ShapeDtypeStruct((k, n), BF16),
        compiler_params=_params("parallel"),
        name="cast_bf16",
    )(w)


BF16_SUBLANES = 16


def _cast_jobs(weights, n_steps, step_of):
    in_specs, args, out_specs, out_shape = [], [], [], []
    for w, layer in weights:
        _, k, n = w.shape
        slab = -(-(-(-k // n_steps)) // BF16_SUBLANES) * BF16_SUBLANES
        while k % slab:
            slab += BF16_SUBLANES
        n_slabs = k // slab
        in_specs.append(pl.BlockSpec(
            (None, slab, n), lambda *g, layer=layer, n_slabs=n_slabs:
            (layer, jnp.minimum(step_of(*g), n_slabs - 1), 0)))
        out_specs.append(pl.BlockSpec(
            (slab, n), lambda *g, n_slabs=n_slabs: (jnp.minimum(step_of(*g), n_slabs - 1), 0)))
        args.append(w)
        out_shape.append(jax.ShapeDtypeStruct((k, n), BF16))
    return in_specs, args, out_specs, out_shape


def _run_cast_jobs(in_refs, out_refs):
    for src, dst in zip(in_refs, out_refs):
        dst[...] = src[...].astype(BF16)


def _resident_spec(shape):
    return pl.BlockSpec(shape, lambda *_: (0,) * len(shape), pipeline_mode=pl.Buffered(1))


def _gelu_tanh(x):
    return x * (0.5 * (1.0 + jnp.tanh(math.sqrt(2.0 / math.pi) * (x + 0.044715 * (x * x * x)))))


IN_PROJ_ROWS = 512


def _pass(h_ref, w_ref, col):
    return jnp.dot(h_ref[...], w_ref[:, col:col + V7X_MXU_COLS], preferred_element_type=F32)


def _heads(acc):
    return [acc[:, j * HEAD_DIM:(j + 1) * HEAD_DIM] for j in range(acc.shape[1] // HEAD_DIM)]


def _attn_in_proj_kernel(x_ref, g_ref, w_ref, pos_ref, invf_ref, *rest, n_jobs):
    qk_ref, vm_ref = rest[n_jobs:n_jobs + 2]
    h_ref, c_ref, s1_ref, s2_ref = rest[2 * n_jobs + 2:]
    _run_cast_jobs(rest[:n_jobs], rest[n_jobs + 2:2 * n_jobs + 2])
    h_ref[...] = _rms_rows(x_ref[...], g_ref[...]).astype(BF16)
    _rope_tables(pos_ref, invf_ref, c_ref, s1_ref, s2_ref)
    for c in range(2 * ATTN_W // V7X_MXU_COLS):
        for j, t in enumerate(_heads(_pass(h_ref, w_ref, c * V7X_MXU_COLS))):
            lo = c * V7X_MXU_COLS + j * HEAD_DIM
            qk_ref[:, lo:lo + HEAD_DIM] = (t * c_ref[...] + pltpu.roll(t, ROT_HALF, 1) * s1_ref[...]
                                           + pltpu.roll(t, HEAD_DIM - ROT_HALF, 1) * s2_ref[...])
    for c in range((ATTN_W + MEM_W) // V7X_MXU_COLS):
        cols = slice(c * V7X_MXU_COLS, (c + 1) * V7X_MXU_COLS)
        vm_ref[:, cols] = _pass(h_ref, w_ref, 2 * ATTN_W + c * V7X_MXU_COLS)


def _attn_in_proj(x, g, g_layer, w_bf, pos_col, invf, cast_weights=()):
    s, k = x.shape
    tm = IN_PROJ_ROWS
    assert w_bf.shape == (k, 3 * ATTN_W + MEM_W)
    row = lambda cols: pl.BlockSpec((tm, cols), lambda i: (i, 0))
    job_in, job_args, job_out, job_shape = _cast_jobs(cast_weights, s // tm, lambda i: i)
    return pl.pallas_call(
        functools.partial(_attn_in_proj_kernel, n_jobs=len(job_args)),
        grid=(s // tm,),
        in_specs=[row(k), _gain_spec(g_layer, k), _resident_spec(w_bf.shape), row(1),
                  pl.BlockSpec((1, HEAD_DIM), lambda i: (0, 0))] + job_in,
        out_specs=[row(2 * ATTN_W), row(ATTN_W + MEM_W)] + job_out,
        out_shape=[jax.ShapeDtypeStruct((s, 2 * ATTN_W), F32),
                   jax.ShapeDtypeStruct((s, ATTN_W + MEM_W), F32)] + job_shape,
        scratch_shapes=[pltpu.VMEM((tm, k), BF16)] + [pltpu.VMEM((tm, HEAD_DIM), F32)] * 3,
        compiler_params=_params("arbitrary"),
        name="attn_in_proj",
    )(x, _gain_arg(g), w_bf, pos_col, invf, *job_args)


def _sgu_in_proj_kernel(x_ref, g_ref, w_ref, lng_ref, lnb_ref, ws_ref, bt_ref,
                        mix_ref, qm_ref, h_ref, v_scr, vn_scr, wsb_scr):
    @pl.when(pl.program_id(0) == 0)
    def _():
        t = lax.broadcasted_iota(jnp.int32, (SGU_CHUNK, SGU_CHUNK), 0)
        s = lax.broadcasted_iota(jnp.int32, (SGU_CHUNK, SGU_CHUNK), 1)
        for g in range(SGU_GROUPS):
            wsb_scr[g] = jnp.where(t >= s, ws_ref[g], 0.0).astype(BF16)

    h_ref[...] = _rms_rows(x_ref[...], g_ref[...]).astype(BF16)
    for c in range(SGU_W // V7X_MXU_COLS):
        cols = slice(c * V7X_MXU_COLS, (c + 1) * V7X_MXU_COLS)
        v_scr[:, cols] = _gelu_tanh(_pass(h_ref, w_ref, SGU_W + c * V7X_MXU_COLS))
    for c in range(MEM_W // V7X_MXU_COLS):
        cols = slice(c * V7X_MXU_COLS, (c + 1) * V7X_MXU_COLS)
        qm_ref[:, cols] = _pass(h_ref, w_ref, 2 * SGU_W + c * V7X_MXU_COLS)
    v = v_scr[...]
    mu = jnp.mean(v, axis=-1, keepdims=True)
    vc = v - mu
    var = jnp.mean(vc * vc, axis=-1, keepdims=True)
    vn_scr[...] = (vc * lax.rsqrt(var + LN_EPS) * lng_ref[...] + lnb_ref[...]).astype(BF16)

    bt = bt_ref[...]
    n_chunks = mix_ref.shape[0] // SGU_CHUNK
    for c in range(SGU_W // V7X_MXU_COLS):
        for j, u in enumerate(_heads(_gelu_tanh(_pass(h_ref, w_ref, c * V7X_MXU_COLS)))):
            g = c * (V7X_MXU_COLS // HEAD_DIM) + j
            cols = slice(g * HEAD_DIM, (g + 1) * HEAD_DIM)
            vg = jnp.concatenate([vn_scr[r * SGU_CHUNK:(r + 1) * SGU_CHUNK, cols]
                                  for r in range(n_chunks)], axis=1)
            mixed = jnp.dot(wsb_scr[g], vg, preferred_element_type=F32) + bt[:, g:g + 1]
            for r in range(n_chunks):
                rows = slice(r * SGU_CHUNK, (r + 1) * SGU_CHUNK)
                mix_ref[rows, cols] = (u[rows, :] * mixed[:, r * HEAD_DIM:(r + 1) * HEAD_DIM]
                                       ).astype(mix_ref.dtype)


def _sgu_in_proj(x, g, g_layer, w_bf, ln_g, ln_b, w_spatial, b_spatial_t, layer):
    s, k = x.shape
    tm = IN_PROJ_ROWS
    assert w_bf.shape == (k, 2 * SGU_W + MEM_W) and tm % SGU_CHUNK == 0
    row = lambda cols: pl.BlockSpec((tm, cols), lambda i: (i, 0))
    return pl.pallas_call(
        _sgu_in_proj_kernel,
        grid=(s // tm,),
        in_specs=[row(k), _gain_spec(g_layer, k), _resident_spec(w_bf.shape),
                  _gain_spec(layer, SGU_W), _gain_spec(layer, SGU_W),
                  pl.BlockSpec((None, SGU_GROUPS, SGU_CHUNK, SGU_CHUNK), lambda i: (layer, 0, 0, 0)),
                  pl.BlockSpec((None, SGU_CHUNK, SGU_GROUPS), lambda i: (layer, 0, 0))],
        out_specs=[row(SGU_W), row(MEM_W)],
        out_shape=[jax.ShapeDtypeStruct((s, SGU_W), BF16), jax.ShapeDtypeStruct((s, MEM_W), F32)],
        scratch_shapes=[pltpu.VMEM((tm, k), BF16), pltpu.VMEM((tm, SGU_W), F32),
                        pltpu.VMEM((tm, SGU_W), BF16),
                        pltpu.VMEM((SGU_GROUPS, SGU_CHUNK, SGU_CHUNK), BF16)],
        compiler_params=_params("arbitrary"),
        name="sgu_in_proj",
    )(x, _gain_arg(g), w_bf, _gain_arg(ln_g), _gain_arg(ln_b), w_spatial, b_spatial_t)


def _mem_attn_kernel(q_ref, kv_ref, o_ref):
    for h in range(MEM_HEADS):
        sl = slice(h * HEAD_DIM, (h + 1) * HEAD_DIM)
        k = kv_ref[:, sl].astype(BF16)
        v = kv_ref[:, MEM_W + h * HEAD_DIM:MEM_W + (h + 1) * HEAD_DIM].astype(BF16)
        s = lax.dot_general(q_ref[:, sl].astype(BF16), k, (((1,), (1,)), ((), ())),
                            preferred_element_type=F32) * ATTN_SCALE
        m = jnp.max(s, axis=-1, keepdims=True)
        p = jnp.exp(s - m)
        l = jnp.sum(p, axis=-1, keepdims=True)
        o = jnp.dot(p.astype(BF16), v, preferred_element_type=F32)
        o_ref[:, sl] = (o * (1.0 / l)).astype(o_ref.dtype)


def _memory_attention(proj, q_col_block, kv):
    s = proj.shape[0]
    tm = 1024
    return pl.pallas_call(
        _mem_attn_kernel,
        grid=(s // tm,),
        in_specs=[pl.BlockSpec((tm, MEM_W), lambda i: (i, q_col_block)),
                  pl.BlockSpec((MEM_LEN, 2 * MEM_W), lambda i: (0, 0))],
        out_specs=pl.BlockSpec((tm, MEM_W), lambda i: (i, 0)),
        out_shape=jax.ShapeDtypeStruct((s, MEM_W), BF16),
        compiler_params=_params("parallel"),
        name="memory_attention",
    )(proj, kv)


def _out_proj_kernel(a_ref, b_ref, w_ref, x_ref, g_ref, o_ref, hn_ref):
    ka = a_ref.shape[1]
    width = o_ref.shape[1]
    n_chunk = 512
    ssq = jnp.zeros((o_ref.shape[0], 1), F32)
    for c in range(width // n_chunk):
        cols = slice(c * n_chunk, (c + 1) * n_chunk)
        y = (x_ref[:, cols]
             + jnp.dot(a_ref[...], w_ref[:ka, cols], preferred_element_type=F32)
             + jnp.dot(b_ref[...], w_ref[ka:, cols], preferred_element_type=F32))
        o_ref[:, cols] = y
        ssq = ssq + jnp.sum(y * y, axis=-1, keepdims=True)
    scale = lax.rsqrt(ssq * (1.0 / width) + NORM_EPS)
    for c in range(width // n_chunk):
        cols = slice(c * n_chunk, (c + 1) * n_chunk)
        hn_ref[:, cols] = (o_ref[:, cols] * scale * g_ref[:, cols]).astype(BF16)


def _out_proj(a, b, w_bf, x, g, g_layer):
    s, ka = a.shape
    kb = b.shape[1]
    width = x.shape[1]
    tm = 512
    assert w_bf.shape == (ka + kb, width)
    row = lambda cols: pl.BlockSpec((tm, cols), lambda i: (i, 0))
    return pl.pallas_call(
        _out_proj_kernel,
        grid=(s // tm,),
        in_specs=[row(ka), row(kb), _resident_spec((ka + kb, width)), row(width),
                  _gain_spec(g_layer, width)],
        out_specs=[row(width), row(width)],
        out_shape=[jax.ShapeDtypeStruct((s, width), F32), jax.ShapeDtypeStruct((s, width), BF16)],
        compiler_params=_params("parallel"),
        name="out_proj",
    )(a, b, w_bf, x, _gain_arg(g))


def _rows(ref, start, size, stride):
    if stride == 1:
        return ref[pl.ds(start, size), :]
    return ref[pl.ds(start, size, stride=stride), :]


ATTN_UNITS_PER_BATCH = 16
V7X_FREE_SUBLANE_STRIDE = 4


def _presplit(d):
    return d // V7X_FREE_SUBLANE_STRIDE if d > V7X_FREE_SUBLANE_STRIDE else 1


def _attn_batch(units):
    scores = [lax.dot_general(q.astype(BF16), kc.astype(BF16), (((1,), (1,)), ((), ())),
                              preferred_element_type=F32) * ATTN_SCALE + bias
              for q, kc, _, bias in units]
    probs = []
    for s in scores:
        m = jnp.max(s, axis=-1, keepdims=True)
        p = jnp.exp(s - m)
        probs.append((p, m, jnp.sum(p, axis=-1, keepdims=True)))
    outs = []
    for (p, m, l), (_, _, vc, _) in zip(probs, units):
        o = jnp.dot(p.astype(BF16), vc.astype(BF16), preferred_element_type=F32)
        outs.append((o * (1.0 / l), m + jnp.log(l)))
    return outs


def _attn_kernel(*refs):
    ngrp = len(ATTN_GROUPS)
    in_refs = refs[:5 * ngrp]
    o_ref = refs[5 * ngrp]
    pos = 5 * ngrp + 1
    o_scr, l_scr = refs[pos:pos + ngrp], refs[pos + ngrp:pos + 2 * ngrp]
    k_scr, v_scr = refs[pos + 2 * ngrp:pos + 3 * ngrp], refs[pos + 3 * ngrp:pos + 4 * ngrp]
    split_refs = refs[pos + 4 * ngrp:pos + 4 * ngrp + 3]
    bias_ref = refs[-1]

    qi = lax.broadcasted_iota(jnp.int32, (BLK, 2 * BLK), 0)
    ki = lax.broadcasted_iota(jnp.int32, (BLK, 2 * BLK), 1)
    lo = jnp.where(pl.program_id(0) == 0, BLK, 0)
    bias_ref[0] = jnp.where((ki >= qi) & (ki <= qi + N_BACK), 0.0, NEG_INF)
    bias_ref[1] = jnp.where((ki >= jnp.maximum(qi, lo)) & (ki <= qi + N_BACK), 0.0, NEG_INF)

    for g, (_, d) in enumerate(ATTN_GROUPS):
        q_ref, k_ref, v_ref, kp_ref, vp_ref = in_refs[5 * g:5 * g + 5]
        og, lg, kcat, vcat = o_scr[g], l_scr[g], k_scr[g], v_scr[g]
        sub = d * BLK
        n_units = ATTN_SPAN // BLK
        assert n_units % ATTN_UNITS_PER_BATCH == 0
        f = _presplit(d)
        d2 = d // f
        klen, qlen, sublen = (sub + ATTN_SPAN) // f, ATTN_SPAN // f, sub // f

        for rf in range(f):
            for src_p, src, cat in ((kp_ref, k_ref, kcat), (vp_ref, v_ref, vcat)):
                cat[pl.ds(rf * klen, sublen), :] = _rows(src_p, rf, sublen, f)
                cat[pl.ds(rf * klen + sublen, qlen), :] = _rows(src, rf, qlen, f)
            if f > 1:
                split_refs[0][pl.ds(rf * qlen, qlen), :] = _rows(q_ref, rf, qlen, f)
        q_src, o_dst, l_dst = (split_refs[0], split_refs[1], split_refs[2]) if f > 1 else (q_ref, og, lg)

        def body(t, carry, d=d, f=f, d2=d2, klen=klen, qlen=qlen, sublen=sublen,
                 q_src=q_src, kcat=kcat, vcat=vcat, o_dst=o_dst, l_dst=l_dst):
            units, starts = [], []
            for u in range(ATTN_UNITS_PER_BATCH):
                idx = t * ATTN_UNITS_PER_BATCH + u
                r = idx % d
                base = (idx // d) * sublen + r // f
                qstart = (r % f) * qlen + base
                kstart = (r % f) * klen + base
                bias = bias_ref[jnp.where(idx < d, 1, 0)]
                units.append((_rows(q_src, qstart, BLK, d2), _rows(kcat, kstart, 2 * BLK, d2),
                              _rows(vcat, kstart, 2 * BLK, d2), bias))
                starts.append(qstart)
            for qstart, (o, lse) in zip(starts, _attn_batch(units)):
                idx = pl.ds(qstart, BLK) if d2 == 1 else pl.ds(qstart, BLK, stride=d2)
                o_dst[idx, :] = o
                l_dst[idx, :] = jnp.broadcast_to(lse, (BLK, HEAD_DIM))
            return carry

        lax.fori_loop(0, n_units // ATTN_UNITS_PER_BATCH, body, 0)
        if f > 1:
            for rf in range(f):
                og[pl.ds(rf, qlen, stride=f), :] = o_dst[pl.ds(rf * qlen, qlen), :]
                lg[pl.ds(rf, qlen, stride=f), :] = l_dst[pl.ds(rf * qlen, qlen), :]

    lses = [l[...] for l in l_scr]
    mx = functools.reduce(jnp.maximum, lses)
    ws = [jnp.exp(l - mx) for l in lses]
    num = functools.reduce(lambda a, b: a + b, [w * o[...] for w, o in zip(ws, o_scr)])
    den = functools.reduce(lambda a, b: a + b, ws)
    o_ref[...] = (num * (1.0 / den)).astype(o_ref.dtype)


def _dilated_attention(qk, vm):
    s = qk.shape[0]
    assert s % ATTN_SPAN == 0
    hpg = HEADS_PER_GROUP
    in_specs, args = [], []
    for g, (_, d) in enumerate(ATTN_GROUPS):
        sub = d * BLK
        ratio = ATTN_SPAN // sub
        qc, kc, vc = g * hpg, ATTN_HEADS + g * hpg, g * hpg
        cur = lambda col: pl.BlockSpec((ATTN_SPAN, HEAD_DIM), lambda i, h, col=col: (i, col + h))
        prev = lambda col, ratio=ratio, sub=sub: pl.BlockSpec(
            (sub, HEAD_DIM), lambda i, h, col=col, ratio=ratio: (jnp.maximum(i * ratio - 1, 0), col + h))
        in_specs += [cur(qc), cur(kc), cur(vc), prev(kc), prev(vc)]
        args += [qk, qk, vm, qk, vm]
    ngrp = len(ATTN_GROUPS)
    return pl.pallas_call(
        _attn_kernel,
        grid=(s // ATTN_SPAN, hpg),
        in_specs=in_specs,
        out_specs=pl.BlockSpec((ATTN_SPAN, HEAD_DIM), lambda i, h: (i, h)),
        out_shape=jax.ShapeDtypeStruct((s, ATTN_OUT_W), BF16),
        scratch_shapes=([pltpu.VMEM((ATTN_SPAN, HEAD_DIM), F32)] * (2 * ngrp)
                        + [pltpu.VMEM((d * BLK + ATTN_SPAN, HEAD_DIM), F32) for _, d in ATTN_GROUPS] * 2
                        + [pltpu.VMEM((ATTN_SPAN, HEAD_DIM), F32)] * 3
                        + [pltpu.VMEM((2, BLK, 2 * BLK), F32)]),
        compiler_params=_params("parallel", "arbitrary"),
        name="dilated_attention",
    )(*args)


def _ffn_down_kernel(a_ref, w_ref, x_ref, *rest, nk, final_norm):
    if final_norm:
        g_ref, o_ref = rest
    else:
        (o_ref,) = rest
    k = pl.program_id(1)
    cw = x_ref.shape[1]

    def accumulate(first):
        for c in range(o_ref.shape[1] // cw):
            cols = slice(c * cw, (c + 1) * cw)
            part = jnp.dot(a_ref[...], w_ref[:, cols].astype(BF16), preferred_element_type=F32)
            part = part + jnp.where(k == c, x_ref[...], 0.0)
            if first:
                o_ref[:, cols] = part
            else:
                o_ref[:, cols] += part

    pl.when(k == 0)(lambda: accumulate(True))
    pl.when(k > 0)(lambda: accumulate(False))

    if final_norm:
        @pl.when(k == nk - 1)
        def _():
            o_ref[...] = _rms_rows(o_ref[...], g_ref[...])


def _ffn_down(a, w, layer, x, final_g=None):
    s, width = x.shape
    tm, tk, cw = 2048, 512, 256
    nk = a.shape[1] // tk
    n_res = width // cw
    assert a.shape[1] % tk == 0 and w.shape[1:] == (nk * tk, width)
    assert s % tm == 0 and width % cw == 0 and n_res <= nk
    in_specs = [pl.BlockSpec((tm, tk), lambda i, k: (i, k)),
                pl.BlockSpec((None, tk, width), lambda i, k: (layer, k, 0)),
                pl.BlockSpec((tm, cw), lambda i, k: (i, jnp.minimum(k, n_res - 1)))]
    args = [a, w, x]
    if final_g is not None:
        in_specs.append(_gain_spec(0, width))
        args.append(final_g.reshape(1, 1, width))
    return pl.pallas_call(
        functools.partial(_ffn_down_kernel, nk=nk, final_norm=final_g is not None),
        grid=(s // tm, nk),
        in_specs=in_specs,
        out_specs=pl.BlockSpec((tm, width), lambda i, k: (i, 0)),
        out_shape=jax.ShapeDtypeStruct((s, width), F32),
        compiler_params=_params("parallel", "arbitrary"),
        name="ffn_down",
    )(*args)


def _ffn_down_bf16_kernel(a_ref, w_ref, x_ref, o_ref):
    for c in range(o_ref.shape[1] // V7X_MXU_COLS):
        cols = slice(c * V7X_MXU_COLS, (c + 1) * V7X_MXU_COLS)
        o_ref[:, cols] = x_ref[:, cols] + jnp.dot(a_ref[...], w_ref[:, cols],
                                                  preferred_element_type=F32)


def _ffn_down_bf16(a, w_bf, x):
    s, f = a.shape
    width = x.shape[1]
    tm, tn = 1024, 512
    assert w_bf.shape == (f, width) and s % tm == 0 and width % tn == 0
    return pl.pallas_call(
        _ffn_down_bf16_kernel,
        grid=(s // tm, width // tn),
        in_specs=[pl.BlockSpec((tm, f), lambda i, j: (i, 0)),
                  pl.BlockSpec((f, tn), lambda i, j: (0, j)),
                  pl.BlockSpec((tm, tn), lambda i, j: (i, j))],
        out_specs=pl.BlockSpec((tm, tn), lambda i, j: (i, j)),
        out_shape=jax.ShapeDtypeStruct((s, width), F32),
        compiler_params=_params("parallel", "arbitrary"),
        name="ffn_down_bf16",
    )(a, w_bf, x)


def _ffn_gate_up_kernel(h_ref, wg_ref, wu_ref, *rest, n_jobs):
    o_ref = rest[n_jobs]
    _run_cast_jobs(rest[:n_jobs], rest[n_jobs + 1:])
    for c in range(o_ref.shape[1] // V7X_MXU_COLS):
        cols = slice(c * V7X_MXU_COLS, (c + 1) * V7X_MXU_COLS)
        gate = jnp.dot(h_ref[...], wg_ref[:, cols].astype(BF16), preferred_element_type=F32)
        up = jnp.dot(h_ref[...], wu_ref[:, cols].astype(BF16), preferred_element_type=F32)
        o_ref[:, cols] = (gate * (1.0 / (1.0 + jnp.exp(-gate))) * up).astype(o_ref.dtype)


def _ffn_gate_up(h, w_gate, w_up, layer, cast_weights=()):
    s, k = h.shape
    f = w_gate.shape[2]
    tm, tf = 2048, 512
    nj = f // tf
    wspec = pl.BlockSpec((None, k, tf), lambda i, j: (layer, 0, j))
    job_in, job_args, job_out, job_shape = _cast_jobs(cast_weights, (s // tm) * nj,
                                                      lambda i, j: i * nj + j)
    return pl.pallas_call(
        functools.partial(_ffn_gate_up_kernel, n_jobs=len(job_args)),
        grid=(s // tm, nj),
        in_specs=[pl.BlockSpec((tm, k), lambda i, j: (i, 0)), wspec, wspec] + job_in,
        out_specs=[pl.BlockSpec((tm, tf), lambda i, j: (i, j))] + job_out,
        out_shape=[jax.ShapeDtypeStruct((s, f), BF16)] + job_shape,
        compiler_params=_params("arbitrary", "arbitrary"),
        name="ffn_gate_up",
    )(h, w_gate, w_up, *job_args)


def kernel(x, mem, positions, mix_norm, mem_norm, w_mem_kv, ffn_norm, w_gate, w_up, w_down,
           attn_w_in, attn_w_out, sgu_w_in, sgu_ln_g, sgu_ln_b, sgu_w_spatial, sgu_b_spatial,
           sgu_w_out, final_norm):
    b, s, d = x.shape
    assert (b, s, d) == (1, SEQ, D_MODEL) and mem.shape == (1, MEM_LEN, D_MODEL)
    xs = x.reshape(s, d)
    mems = mem.reshape(MEM_LEN, d)

    inv_freq = ROPE_THETA ** (-jnp.arange(ROT_HALF, dtype=F32) / ROT_HALF)
    invf = jnp.concatenate([inv_freq, inv_freq, jnp.zeros((HEAD_DIM - ROT_DIM,), F32)])
    pos_col, invf = positions.reshape(s, 1), invf.reshape(1, HEAD_DIM)

    def mixer_weights(layer):
        return (attn_w_in, attn_w_out) if layer % 2 == 0 else (sgu_w_in, sgu_w_out)

    w_in_bf, w_out_bf = _cast_bf16(attn_w_in, 0), None
    for i in range(DEPTH):
        j = i // 2
        kv = _mem_kv(mems, mem_norm, w_mem_kv, i)
        if i % 2 == 0:
            casts = [(attn_w_out, j)] if w_out_bf is None else []
            qk, vm, *cast = _attn_in_proj(xs, mix_norm, i, w_in_bf, pos_col, invf, casts)
            w_out_bf = cast[0] if cast else w_out_bf
            mix = _dilated_attention(qk, vm)
            mem_out = _memory_attention(vm, ATTN_W // MEM_W, kv)
        else:
            mix, q_mem = _sgu_in_proj(xs, mix_norm, i, w_in_bf, sgu_ln_g, sgu_ln_b,
                                      sgu_w_spatial, jnp.swapaxes(sgu_b_spatial, 1, 2), j)
            mem_out = _memory_attention(q_mem, 0, kv)
        xs, hn = _out_proj(mix, mem_out, w_out_bf, xs, ffn_norm, i)
        if i + 1 < DEPTH:
            casts = [(w, (i + 1) // 2) for w in mixer_weights(i + 1)] + [(w_down, i)]
            act, w_in_bf, w_out_bf, w_down_bf = _ffn_gate_up(hn, w_gate, w_up, i, casts)
            xs = _ffn_down_bf16(act, w_down_bf, xs)
        else:
            act, = _ffn_gate_up(hn, w_gate, w_up, i)
            xs = _ffn_down(act, w_down, i, xs, final_g=final_norm)
    return xs.reshape(b, s, d)
```

```python
import functools
import math

import jax
import jax.numpy as jnp
from jax import lax
from jax.experimental import pallas as pl
from jax.experimental.pallas import tpu as pltpu

D_MODEL = 2048
SEQ = 8192
DEPTH = 2
MEM_LEN = 256
HEAD_DIM = 128
MEM_HEADS = 4
MEM_W = MEM_HEADS * HEAD_DIM
ATTN_GROUPS = ((128, 1), (512, 4), (2048, 16))
ATTN_HEADS = 12
HEADS_PER_GROUP = 4
ATTN_W = ATTN_HEADS * HEAD_DIM
ATTN_OUT_W = HEADS_PER_GROUP * HEAD_DIM
BLK = 128
SGU_GROUPS = 12
SGU_W = SGU_GROUPS * HEAD_DIM
SGU_CHUNK = 128
ROT_DIM = HEAD_DIM // 4
ROT_HALF = ROT_DIM // 2
ROPE_THETA = 500000.0
FFN_HIDDEN = 5632
NORM_EPS = 1e-6
LN_EPS = 1e-5
NEG_INF = -1e30
ATTN_SCALE = HEAD_DIM ** -0.5

N_BACK = BLK
assert all(w // d == N_BACK for w, d in ATTN_GROUPS)
ATTN_SPAN = max(d for _, d in ATTN_GROUPS) * BLK

V7X_VMEM_LIMIT_BYTES = 56 * 1024 * 1024
V7X_MXU_COLS = 256

F32 = jnp.float32
BF16 = jnp.bfloat16


def _params(*semantics):
    return pltpu.CompilerParams(dimension_semantics=semantics,
                                vmem_limit_bytes=V7X_VMEM_LIMIT_BYTES)


def _rms_rows(x, g):
    ms = jnp.mean(x * x, axis=-1, keepdims=True)
    return x * lax.rsqrt(ms + NORM_EPS) * g


def _gain_spec(layer, k):
    return pl.BlockSpec((None, 1, k), lambda *_: (layer, 0, 0))


def _gain_arg(g):
    return g.reshape(g.shape[0], 1, g.shape[1])


def _rope_tables(pos_ref, invf_ref, c_ref, s1_ref, s2_ref):
    ang = pos_ref[...].astype(F32) * invf_ref[...]
    lane = lax.broadcasted_iota(jnp.int32, ang.shape, 1)
    cos = jnp.cos(ang)
    sin = jnp.sin(ang)
    c_ref[...] = jnp.where(lane < ROT_DIM, cos, 1.0)
    s1_ref[...] = jnp.where(lane < ROT_HALF, 0.0, jnp.where(lane < ROT_DIM, sin, 0.0))
    s2_ref[...] = jnp.where(lane < ROT_HALF, -sin, 0.0)


def _mem_kv_kernel(x_ref, g_ref, w_ref, o_ref):
    h = _rms_rows(x_ref[...], g_ref[...]).astype(BF16)
    o_ref[...] = jnp.dot(h, w_ref[...].astype(BF16), preferred_element_type=F32)


def _mem_kv(x, g, w, layer):
    m, k = x.shape
    n = w.shape[2]
    tn = 512
    assert n % tn == 0
    return pl.pallas_call(
        _mem_kv_kernel,
        grid=(n // tn,),
        in_specs=[pl.BlockSpec((m, k), lambda j: (0, 0)), _gain_spec(layer, k),
                  pl.BlockSpec((None, k, tn), lambda j: (layer, 0, j))],
        out_specs=pl.BlockSpec((m, tn), lambda j: (0, j)),
        out_shape=jax.ShapeDtypeStruct((m, n), F32),
        compiler_params=_params("parallel"),
        name="mem_kv",
    )(x, _gain_arg(g), w)


def _cast_kernel(w_ref, o_ref):
    o_ref[...] = w_ref[...].astype(BF16)


def _cast_bf16(w, layer):
    _, k, n = w.shape
    tk = 256
    assert k % tk == 0
    return pl.pallas_call(
        _cast_kernel,
        grid=(k // tk,),
        in_specs=[pl.BlockSpec((None, tk, n), lambda i: (layer, i, 0))],
        out_specs=pl.BlockSpec((tk, n), lambda i: (i, 0)),
        out_shape=jax.ShapeDtypeStruct((k, n), BF16),
        compiler_params=_params("parallel"),
        name="cast_bf16",
    )(w)


BF16_SUBLANES = 16


def _cast_jobs(weights, n_steps, step_of):
    in_specs, args, out_specs, out_shape = [], [], [], []
    for w, layer in weights:
        _, k, n = w.shape
        slab = -(-(-(-k // n_steps)) // BF16_SUBLANES) * BF16_SUBLANES
        while k % slab:
            slab += BF16_SUBLANES
        n_slabs = k // slab
        in_specs.append(pl.BlockSpec(
            (None, slab, n), lambda *g, layer=layer, n_slabs=n_slabs:
            (layer, jnp.minimum(step_of(*g), n_slabs - 1), 0)))
        out_specs.append(pl.BlockSpec(
            (slab, n), lambda *g, n_slabs=n_slabs: (jnp.minimum(step_of(*g), n_slabs - 1), 0)))
        args.append(w)
        out_shape.append(jax.ShapeDtypeStruct((k, n), BF16))
    return in_specs, args, out_specs, out_shape


def _run_cast_jobs(in_refs, out_refs):
    for src, dst in zip(in_refs, out_refs):
        dst[...] = src[...].astype(BF16)


def _resident_spec(shape):
    return pl.BlockSpec(shape, lambda *_: (0,) * len(shape), pipeline_mode=pl.Buffered(1))


def _gelu_tanh(x):
    return x * (0.5 * (1.0 + jnp.tanh(math.sqrt(2.0 / math.pi) * (x + 0.044715 * (x * x * x)))))


IN_PROJ_ROWS = 512


def _pass(h_ref, w_ref, col):
    return jnp.dot(h_ref[...], w_ref[:, col:col + V7X_MXU_COLS], preferred_element_type=F32)


def _heads(acc):
    return [acc[:, j * HEAD_DIM:(j + 1) * HEAD_DIM] for j in range(acc.shape[1] // HEAD_DIM)]


def _attn_in_proj_kernel(x_ref, g_ref, w_ref, pos_ref, invf_ref, *rest, n_jobs):
    qk_ref, vm_ref = rest[n_jobs:n_jobs + 2]
    h_ref, c_ref, s1_ref, s2_ref = rest[2 * n_jobs + 2:]
    _run_cast_jobs(rest[:n_jobs], rest[n_jobs + 2:2 * n_jobs + 2])
    h_ref[...] = _rms_rows(x_ref[...], g_ref[...]).astype(BF16)
    _rope_tables(pos_ref, invf_ref, c_ref, s1_ref, s2_ref)
    for c in range(2 * ATTN_W // V7X_MXU_COLS):
        for j, t in enumerate(_heads(_pass(h_ref, w_ref, c * V7X_MXU_COLS))):
            lo = c * V7X_MXU_COLS + j * HEAD_DIM
            qk_ref[:, lo:lo + HEAD_DIM] = (t * c_ref[...] + pltpu.roll(t, ROT_HALF, 1) * s1_ref[...]
                                           + pltpu.roll(t, HEAD_DIM - ROT_HALF, 1) * s2_ref[...])
    for c in range((ATTN_W + MEM_W) // V7X_MXU_COLS):
        cols = slice(c * V7X_MXU_COLS, (c + 1) * V7X_MXU_COLS)
        vm_ref[:, cols] = _pass(h_ref, w_ref, 2 * ATTN_W + c * V7X_MXU_COLS)


def _attn_in_proj(x, g, g_layer, w_bf, pos_col, invf, cast_weights=()):
    s, k = x.shape
    tm = IN_PROJ_ROWS
    assert w_bf.shape == (k, 3 * ATTN_W + MEM_W)
    row = lambda cols: pl.BlockSpec((tm, cols), lambda i: (i, 0))
    job_in, job_args, job_out, job_shape = _cast_jobs(cast_weights, s // tm, lambda i: i)
    return pl.pallas_call(
        functools.partial(_attn_in_proj_kernel, n_jobs=len(job_args)),
        grid=(s // tm,),
        in_specs=[row(k), _gain_spec(g_layer, k), _resident_spec(w_bf.shape), row(1),
                  pl.BlockSpec((1, HEAD_DIM), lambda i: (0, 0))] + job_in,
        out_specs=[row(2 * ATTN_W), row(ATTN_W + MEM_W)] + job_out,
        out_shape=[jax.ShapeDtypeStruct((s, 2 * ATTN_W), F32),
                   jax.ShapeDtypeStruct((s, ATTN_W + MEM_W), F32)] + job_shape,
        scratch_shapes=[pltpu.VMEM((tm, k), BF16)] + [pltpu.VMEM((tm, HEAD_DIM), F32)] * 3,
        compiler_params=_params("arbitrary"),
        name="attn_in_proj",
    )(x, _gain_arg(g), w_bf, pos_col, invf, *job_args)


def _sgu_in_proj_kernel(x_ref, g_ref, w_ref, lng_ref, lnb_ref, ws_ref, bt_ref,
                        mix_ref, qm_ref, h_ref, v_scr, vn_scr, wsb_scr):
    @pl.when(pl.program_id(0) == 0)
    def _():
        t = lax.broadcasted_iota(jnp.int32, (SGU_CHUNK, SGU_CHUNK), 0)
        s = lax.broadcasted_iota(jnp.int32, (SGU_CHUNK, SGU_CHUNK), 1)
        for g in range(SGU_GROUPS):
            wsb_scr[g] = jnp.where(t >= s, ws_ref[g], 0.0).astype(BF16)

    h_ref[...] = _rms_rows(x_ref[...], g_ref[...]).astype(BF16)
    for c in range(SGU_W // V7X_MXU_COLS):
        cols = slice(c * V7X_MXU_COLS, (c + 1) * V7X_MXU_COLS)
        v_scr[:, cols] = _gelu_tanh(_pass(h_ref, w_ref, SGU_W + c * V7X_MXU_COLS))
    for c in range(MEM_W // V7X_MXU_COLS):
        cols = slice(c * V7X_MXU_COLS, (c + 1) * V7X_MXU_COLS)
        qm_ref[:, cols] = _pass(h_ref, w_ref, 2 * SGU_W + c * V7X_MXU_COLS)
    v = v_scr[...]
    mu = jnp.mean(v, axis=-1, keepdims=True)
    vc = v - mu
    var = jnp.mean(vc * vc, axis=-1, keepdims=True)
    vn_scr[...] = (vc * lax.rsqrt(var + LN_EPS) * lng_ref[...] + lnb_ref[...]).astype(BF16)

    bt = bt_ref[...]
    n_chunks = mix_ref.shape[0] // SGU_CHUNK
    for c in range(SGU_W // V7X_MXU_COLS):
        for j, u in enumerate(_heads(_gelu_tanh(_pass(h_ref, w_ref, c * V7X_MXU_COLS)))):
            g = c * (V7X_MXU_COLS // HEAD_DIM) + j
            cols = slice(g * HEAD_DIM, (g + 1) * HEAD_DIM)
            vg = jnp.concatenate([vn_scr[r * SGU_CHUNK:(r + 1) * SGU_CHUNK, cols]
                                  for r in range(n_chunks)], axis=1)
            mixed = jnp.dot(wsb_scr[g], vg, preferred_element_type=F32) + bt[:, g:g + 1]
            for r in range(n_chunks):
                rows = slice(r * SGU_CHUNK, (r + 1) * SGU_CHUNK)
                mix_ref[rows, cols] = (u[rows, :] * mixed[:, r * HEAD_DIM:(r + 1) * HEAD_DIM]
                                       ).astype(mix_ref.dtype)


def _sgu_in_proj(x, g, g_layer, w_bf, ln_g, ln_b, w_spatial, b_spatial_t, layer):
    s, k = x.shape
    tm = IN_PROJ_ROWS
    assert w_bf.shape == (k, 2 * SGU_W + MEM_W) and tm % SGU_CHUNK == 0
    row = lambda cols: pl.BlockSpec((tm, cols), lambda i: (i, 0))
    return pl.pallas_call(
        _sgu_in_proj_kernel,
        grid=(s // tm,),
        in_specs=[row(k), _gain_spec(g_layer, k), _resident_spec(w_bf.shape),
                  _gain_spec(layer, SGU_W), _gain_spec(layer, SGU_W),
                  pl.BlockSpec((None, SGU_GROUPS, SGU_CHUNK, SGU_CHUNK), lambda i: (layer, 0, 0, 0)),
                  pl.BlockSpec((None, SGU_CHUNK, SGU_GROUPS), lambda i: (layer, 0, 0))],
        out_specs=[row(SGU_W), row(MEM_W)],
        out_shape=[jax.ShapeDtypeStruct((s, SGU_W), BF16), jax.ShapeDtypeStruct((s, MEM_W), F32)],
        scratch_shapes=[pltpu.VMEM((tm, k), BF16), pltpu.VMEM((tm, SGU_W), F32),
                        pltpu.VMEM((tm, SGU_W), BF16),
                        pltpu.VMEM((SGU_GROUPS, SGU_CHUNK, SGU_CHUNK), BF16)],
        compiler_params=_params("arbitrary"),
        name="sgu_in_proj",
    )(x, _gain_arg(g), w_bf, _gain_arg(ln_g), _gain_arg(ln_b), w_spatial, b_spatial_t)


def _mem_attn_kernel(q_ref, kv_ref, o_ref):
    for h in range(MEM_HEADS):
        sl = slice(h * HEAD_DIM, (h + 1) * HEAD_DIM)
        k = kv_ref[:, sl].astype(BF16)
        v = kv_ref[:, MEM_W + h * HEAD_DIM:MEM_W + (h + 1) * HEAD_DIM].astype(BF16)
        s = lax.dot_general(q_ref[:, sl].astype(BF16), k, (((1,), (1,)), ((), ())),
                            preferred_element_type=F32) * ATTN_SCALE
        m = jnp.max(s, axis=-1, keepdims=True)
        p = jnp.exp(s - m)
        l = jnp.sum(p, axis=-1, keepdims=True)
        o = jnp.dot(p.astype(BF16), v, preferred_element_type=F32)
        o_ref[:, sl] = (o * (1.0 / l)).astype(o_ref.dtype)


def _memory_attention(proj, q_col_block, kv):
    s = proj.shape[0]
    tm = 1024
    return pl.pallas_call(
        _mem_attn_kernel,
        grid=(s // tm,),
        in_specs=[pl.BlockSpec((tm, MEM_W), lambda i: (i, q_col_block)),
                  pl.BlockSpec((MEM_LEN, 2 * MEM_W), lambda i: (0, 0))],
        out_specs=pl.BlockSpec((tm, MEM_W), lambda i: (i, 0)),
        out_shape=jax.ShapeDtypeStruct((s, MEM_W), BF16),
        compiler_params=_params("parallel"),
        name="memory_attention",
    )(proj, kv)


def _out_proj_kernel(a_ref, b_ref, w_ref, x_ref, g_ref, o_ref, hn_ref):
    ka = a_ref.shape[1]
    width = o_ref.shape[1]
    n_chunk = 512
    ssq = jnp.zeros((o_ref.shape[0], 1), F32)
    for c in range(width // n_chunk):
        cols = slice(c * n_chunk, (c + 1) * n_chunk)
        y = (x_ref[:, cols]
             + jnp.dot(a_ref[...], w_ref[:ka, cols], preferred_element_type=F32)
             + jnp.dot(b_ref[...], w_ref[ka:, cols], preferred_element_type=F32))
        o_ref[:, cols] = y
        ssq = ssq + jnp.sum(y * y, axis=-1, keepdims=True)
    scale = lax.rsqrt(ssq * (1.0 / width) + NORM_EPS)
    for c in range(width // n_chunk):
        cols = slice(c * n_chunk, (c + 1) * n_chunk)
        hn_ref[:, cols] = (o_ref[:, cols] * scale * g_ref[:, cols]).astype(BF16)


def _out_proj(a, b, w_bf, x, g, g_layer):
    s, ka = a.shape
    kb = b.shape[1]
    width = x.shape[1]
    tm = 512
    assert w_bf.shape == (ka + kb, width)
    row = lambda cols: pl.BlockSpec((tm, cols), lambda i: (i, 0))
    return pl.pallas_call(
        _out_proj_kernel,
        grid=(s // tm,),
        in_specs=[row(ka), row(kb), _resident_spec((ka + kb, width)), row(width),
                  _gain_spec(g_layer, width)],
        out_specs=[row(width), row(width)],
        out_shape=[jax.ShapeDtypeStruct((s, width), F32), jax.ShapeDtypeStruct((s, width), BF16)],
        compiler_params=_params("parallel"),
        name="out_proj",
    )(a, b, w_bf, x, _gain_arg(g))


def _rows(ref, start, size, stride):
    if stride == 1:
        return ref[pl.ds(start, size), :]
    return ref[pl.ds(start, size, stride=stride), :]


ATTN_UNITS_PER_BATCH = 16
V7X_FREE_SUBLANE_STRIDE = 4


def _presplit(d):
    return d // V7X_FREE_SUBLANE_STRIDE if d > V7X_FREE_SUBLANE_STRIDE else 1


def _attn_batch(units):
    scores = [lax.dot_general(q.astype(BF16), kc.astype(BF16), (((1,), (1,)), ((), ())),
                              preferred_element_type=F32) * ATTN_SCALE + bias
              for q, kc, _, bias in units]
    probs = []
    for s in scores:
        m = jnp.max(s, axis=-1, keepdims=True)
        p = jnp.exp(s - m)
        probs.append((p, m, jnp.sum(p, axis=-1, keepdims=True)))
    outs = []
    for (p, m, l), (_, _, vc, _) in zip(probs, units):
        o = jnp.dot(p.astype(BF16), vc.astype(BF16), preferred_element_type=F32)
        outs.append((o * (1.0 / l), m + jnp.log(l)))
    return outs


def _attn_kernel(*refs):
    ngrp = len(ATTN_GROUPS)
    in_refs = refs[:5 * ngrp]
    o_ref = refs[5 * ngrp]
    pos = 5 * ngrp + 1
    o_scr, l_scr = refs[pos:pos + ngrp], refs[pos + ngrp:pos + 2 * ngrp]
    k_scr, v_scr = refs[pos + 2 * ngrp:pos + 3 * ngrp], refs[pos + 3 * ngrp:pos + 4 * ngrp]
    split_refs = refs[pos + 4 * ngrp:pos + 4 * ngrp + 3]
    bias_ref = refs[-1]

    qi = lax.broadcasted_iota(jnp.int32, (BLK, 2 * BLK), 0)
    ki = lax.broadcasted_iota(jnp.int32, (BLK, 2 * BLK), 1)
    lo = jnp.where(pl.program_id(0) == 0, BLK, 0)
    bias_ref[0] = jnp.where((ki >= qi) & (ki <= qi + N_BACK), 0.0, NEG_INF)
    bias_ref[1] = jnp.where((ki >= jnp.maximum(qi, lo)) & (ki <= qi + N_BACK), 0.0, NEG_INF)

    for g, (_, d) in enumerate(ATTN_GROUPS):
        q_ref, k_ref, v_ref, kp_ref, vp_ref = in_refs[5 * g:5 * g + 5]
        og, lg, kcat, vcat = o_scr[g], l_scr[g], k_scr[g], v_scr[g]
        sub = d * BLK
        n_units = ATTN_SPAN // BLK
        assert n_units % ATTN_UNITS_PER_BATCH == 0
        f = _presplit(d)
        d2 = d // f
        klen, qlen, sublen = (sub + ATTN_SPAN) // f, ATTN_SPAN // f, sub // f

        for rf in range(f):
            for src_p, src, cat in ((kp_ref, k_ref, kcat), (vp_ref, v_ref, vcat)):
                cat[pl.ds(rf * klen, sublen), :] = _rows(src_p, rf, sublen, f)
                cat[pl.ds(rf * klen + sublen, qlen), :] = _rows(src, rf, qlen, f)
            if f > 1:
                split_refs[0][pl.ds(rf * qlen, qlen), :] = _rows(q_ref, rf, qlen, f)
        q_src, o_dst, l_dst = (split_refs[0], split_refs[1], split_refs[2]) if f > 1 else (q_ref, og, lg)

        def body(t, carry, d=d, f=f, d2=d2, klen=klen, qlen=qlen, sublen=sublen,
                 q_src=q_src, kcat=kcat, vcat=vcat, o_dst=o_dst, l_dst=l_dst):
            units, starts = [], []
            for u in range(ATTN_UNITS_PER_BATCH):
                idx = t * ATTN_UNITS_PER_BATCH + u
                r = idx % d
                base = (idx // d) * sublen + r // f
                qstart = (r % f) * qlen + base
                kstart = (r % f) * klen + base
                bias = bias_ref[jnp.where(idx < d, 1, 0)]
                units.append((_rows(q_src, qstart, BLK, d2), _rows(kcat, kstart, 2 * BLK, d2),
                              _rows(vcat, kstart, 2 * BLK, d2), bias))
                starts.append(qstart)
            for qstart, (o, lse) in zip(starts, _attn_batch(units)):
                idx = pl.ds(qstart, BLK) if d2 == 1 else pl.ds(qstart, BLK, stride=d2)
                o_dst[idx, :] = o
                l_dst[idx, :] = jnp.broadcast_to(lse, (BLK, HEAD_DIM))
            return carry

        lax.fori_loop(0, n_units // ATTN_UNITS_PER_BATCH, body, 0)
        if f > 1:
            for rf in range(f):
                og[pl.ds(rf, qlen, stride=f), :] = o_dst[pl.ds(rf * qlen, qlen), :]
                lg[pl.ds(rf, qlen, stride=f), :] = l_dst[pl.ds(rf * qlen, qlen), :]

    lses = [l[...] for l in l_scr]
    mx = functools.reduce(jnp.maximum, lses)
    ws = [jnp.exp(l - mx) for l in lses]
    num = functools.reduce(lambda a, b: a + b, [w * o[...] for w, o in zip(ws, o_scr)])
    den = functools.reduce(lambda a, b: a + b, ws)
    o_ref[...] = (num * (1.0 / den)).astype(o_ref.dtype)


def _dilated_attention(qk, vm):
    s = qk.shape[0]
    assert s % ATTN_SPAN == 0
    hpg = HEADS_PER_GROUP
    in_specs, args = [], []
    for g, (_, d) in enumerate(ATTN_GROUPS):
        sub = d * BLK
        ratio = ATTN_SPAN // sub
        qc, kc, vc = g * hpg, ATTN_HEADS + g * hpg, g * hpg
        cur = lambda col: pl.BlockSpec((ATTN_SPAN, HEAD_DIM), lambda i, h, col=col: (i, col + h))
        prev = lambda col, ratio=ratio, sub=sub: pl.BlockSpec(
            (sub, HEAD_DIM), lambda i, h, col=col, ratio=ratio: (jnp.maximum(i * ratio - 1, 0), col + h))
        in_specs += [cur(qc), cur(kc), cur(vc), prev(kc), prev(vc)]
        args += [qk, qk, vm, qk, vm]
    ngrp = len(ATTN_GROUPS)
    return pl.pallas_call(
        _attn_kernel,
        grid=(s // ATTN_SPAN, hpg),
        in_specs=in_specs,
        out_specs=pl.BlockSpec((ATTN_SPAN, HEAD_DIM), lambda i, h: (i, h)),
        out_shape=jax.ShapeDtypeStruct((s, ATTN_OUT_W), BF16),
        scratch_shapes=([pltpu.VMEM((ATTN_SPAN, HEAD_DIM), F32)] * (2 * ngrp)
                        + [pltpu.VMEM((d * BLK + ATTN_SPAN, HEAD_DIM), F32) for _, d in ATTN_GROUPS] * 2
                        + [pltpu.VMEM((ATTN_SPAN, HEAD_DIM), F32)] * 3
                        + [pltpu.VMEM((2, BLK, 2 * BLK), F32)]),
        compiler_params=_params("parallel", "arbitrary"),
        name="dilated_attention",
    )(*args)


def _ffn_down_rows_kernel(a_ref, w_ref, x_ref, *rest, final_norm):
    if final_norm:
        g_ref, o_ref = rest
    else:
        (o_ref,) = rest
    width = o_ref.shape[1]
    ssq = jnp.zeros((o_ref.shape[0], 1), F32)
    for c in range(width // V7X_MXU_COLS):
        cols = slice(c * V7X_MXU_COLS, (c + 1) * V7X_MXU_COLS)
        y = x_ref[:, cols] + jnp.dot(a_ref[...], w_ref[:, cols], preferred_element_type=F32)
        o_ref[:, cols] = y
        if final_norm:
            ssq = ssq + jnp.sum(y * y, axis=-1, keepdims=True)
    if final_norm:
        scale = lax.rsqrt(ssq * (1.0 / width) + NORM_EPS)
        for c in range(width // V7X_MXU_COLS):
            cols = slice(c * V7X_MXU_COLS, (c + 1) * V7X_MXU_COLS)
            o_ref[:, cols] = o_ref[:, cols] * scale * g_ref[:, cols]


def _ffn_down_rows(a, w_bf, x, final_g=None):
    s, f = a.shape
    width = x.shape[1]
    tm = 512
    assert w_bf.shape == (f, width) and s % tm == 0
    row = lambda cols: pl.BlockSpec((tm, cols), lambda i: (i, 0))
    in_specs = [row(f), _resident_spec(w_bf.shape), row(width)]
    args = [a, w_bf, x]
    if final_g is not None:
        in_specs.append(_gain_spec(0, width))
        args.append(final_g.reshape(1, 1, width))
    return pl.pallas_call(
        functools.partial(_ffn_down_rows_kernel, final_norm=final_g is not None),
        grid=(s // tm,),
        in_specs=in_specs,
        out_specs=row(width),
        out_shape=jax.ShapeDtypeStruct((s, width), F32),
        compiler_params=_params("parallel"),
        name="ffn_down_rows",
    )(*args)


def _ffn_gate_up_kernel(h_ref, wg_ref, wu_ref, *rest, n_jobs):
    o_ref = rest[n_jobs]
    _run_cast_jobs(rest[:n_jobs], rest[n_jobs + 1:])
    for c in range(o_ref.shape[1] // V7X_MXU_COLS):
        cols = slice(c * V7X_MXU_COLS, (c + 1) * V7X_MXU_COLS)
        gate = jnp.dot(h_ref[...], wg_ref[:, cols].astype(BF16), preferred_element_type=F32)
        up = jnp.dot(h_ref[...], wu_ref[:, cols].astype(BF16), preferred_element_type=F32)
        o_ref[:, cols] = (gate * (1.0 / (1.0 + jnp.exp(-gate))) * up).astype(o_ref.dtype)


def _ffn_gate_up(h, w_gate, w_up, layer, cast_weights=()):
    s, k = h.shape
    f = w_gate.shape[2]
    tm, tf = 2048, 512
    nj = f // tf
    wspec = pl.BlockSpec((None, k, tf), lambda i, j: (layer, 0, j))
    job_in, job_args, job_out, job_shape = _cast_jobs(cast_weights, (s // tm) * nj,
                                                      lambda i, j: i * nj + j)
    return pl.pallas_call(
        functools.partial(_ffn_gate_up_kernel, n_jobs=len(job_args)),
        grid=(s // tm, nj),
        in_specs=[pl.BlockSpec((tm, k), lambda i, j: (i, 0)), wspec, wspec] + job_in,
        out_specs=[pl.BlockSpec((tm, tf), lambda i, j: (i, j))] + job_out,
        out_shape=[jax.ShapeDtypeStruct((s, f), BF16)] + job_shape,
        compiler_params=_params("arbitrary", "arbitrary"),
        name="ffn_gate_up",
    )(h, w_gate, w_up, *job_args)


def kernel(x, mem, positions, mix_norm, mem_norm, w_mem_kv, ffn_norm, w_gate, w_up, w_down,
           attn_w_in, attn_w_out, sgu_w_in, sgu_ln_g, sgu_ln_b, sgu_w_spatial, sgu_b_spatial,
           sgu_w_out, final_norm):
    b, s, d = x.shape
    assert (b, s, d) == (1, SEQ, D_MODEL) and mem.shape == (1, MEM_LEN, D_MODEL)
    xs = x.reshape(s, d)
    mems = mem.reshape(MEM_LEN, d)

    inv_freq = ROPE_THETA ** (-jnp.arange(ROT_HALF, dtype=F32) / ROT_HALF)
    invf = jnp.concatenate([inv_freq, inv_freq, jnp.zeros((HEAD_DIM - ROT_DIM,), F32)])
    pos_col, invf = positions.reshape(s, 1), invf.reshape(1, HEAD_DIM)

    def mixer_weights(layer):
        return (attn_w_in, attn_w_out) if layer % 2 == 0 else (sgu_w_in, sgu_w_out)

    w_in_bf, w_out_bf = _cast_bf16(attn_w_in, 0), None
    for i in range(DEPTH):
        j = i // 2
        kv = _mem_kv(mems, mem_norm, w_mem_kv, i)
        if i % 2 == 0:
            casts = [(attn_w_out, j)] if w_out_bf is None else []
            qk, vm, *cast = _attn_in_proj(xs, mix_norm, i, w_in_bf, pos_col, invf, casts)
            w_out_bf = cast[0] if cast else w_out_bf
            mix = _dilated_attention(qk, vm)
            mem_out = _memory_attention(vm, ATTN_W // MEM_W, kv)
        else:
            mix, q_mem = _sgu_in_proj(xs, mix_norm, i, w_in_bf, sgu_ln_g, sgu_ln_b,
                                      sgu_w_spatial, jnp.swapaxes(sgu_b_spatial, 1, 2), j)
            mem_out = _memory_attention(q_mem, 0, kv)
        xs, hn = _out_proj(mix, mem_out, w_out_bf, xs, ffn_norm, i)
        last = i + 1 == DEPTH
        casts = [(w_down, i)] + ([] if last else [(w, (i + 1) // 2) for w in mixer_weights(i + 1)])
        act, w_down_bf, *next_bf = _ffn_gate_up(hn, w_gate, w_up, i, casts)
        w_in_bf, w_out_bf = next_bf if next_bf else (None, None)
        xs = _ffn_down_rows(act, w_down_bf, xs, final_g=final_norm if last else None)
    return xs.reshape(b, s, d)
```

```python
import functools
import math

import jax
import jax.numpy as jnp
from jax import lax
from jax.experimental import pallas as pl
from jax.experimental.pallas import tpu as pltpu

D_MODEL = 2048
SEQ = 8192
DEPTH = 2
MEM_LEN = 256
HEAD_DIM = 128
MEM_HEADS = 4
MEM_W = MEM_HEADS * HEAD_DIM
ATTN_GROUPS = ((128, 1), (512, 4), (2048, 16))
ATTN_HEADS = 12
HEADS_PER_GROUP = 4
ATTN_W = ATTN_HEADS * HEAD_DIM
ATTN_OUT_W = HEADS_PER_GROUP * HEAD_DIM
BLK = 128
SGU_GROUPS = 12
SGU_W = SGU_GROUPS * HEAD_DIM
SGU_CHUNK = 128
ROT_DIM = HEAD_DIM // 4
ROT_HALF = ROT_DIM // 2
ROPE_THETA = 500000.0
FFN_HIDDEN = 5632
NORM_EPS = 1e-6
LN_EPS = 1e-5
NEG_INF = -1e30
ATTN_SCALE = HEAD_DIM ** -0.5

N_BACK = BLK
assert all(w // d == N_BACK for w, d in ATTN_GROUPS)
ATTN_SPAN = max(d for _, d in ATTN_GROUPS) * BLK

V7X_VMEM_LIMIT_BYTES = 56 * 1024 * 1024
V7X_MXU_COLS = 256

F32 = jnp.float32
BF16 = jnp.bfloat16


def _params(*semantics):
    return pltpu.CompilerParams(dimension_semantics=semantics,
                                vmem_limit_bytes=V7X_VMEM_LIMIT_BYTES)


def _rms_rows(x, g):
    ms = jnp.mean(x * x, axis=-1, keepdims=True)
    return x * lax.rsqrt(ms + NORM_EPS) * g


def _gain_spec(layer, k):
    return pl.BlockSpec((None, 1, k), lambda *_: (layer, 0, 0))


def _gain_arg(g):
    return g.reshape(g.shape[0], 1, g.shape[1])


def _rope_tables(pos_ref, invf_ref, c_ref, s1_ref, s2_ref):
    ang = pos_ref[...].astype(F32) * invf_ref[...]
    lane = lax.broadcasted_iota(jnp.int32, ang.shape, 1)
    cos = jnp.cos(ang)
    sin = jnp.sin(ang)
    c_ref[...] = jnp.where(lane < ROT_DIM, cos, 1.0)
    s1_ref[...] = jnp.where(lane < ROT_HALF, 0.0, jnp.where(lane < ROT_DIM, sin, 0.0))
    s2_ref[...] = jnp.where(lane < ROT_HALF, -sin, 0.0)


def _mem_kv_kernel(x_ref, g_ref, w_ref, o_ref):
    h = _rms_rows(x_ref[...], g_ref[...]).astype(BF16)
    o_ref[...] = jnp.dot(h, w_ref[...].astype(BF16), preferred_element_type=F32)


def _mem_kv(x, g, w):
    m, k = x.shape
    layers, _, n = w.shape
    tn = 512
    assert n % tn == 0
    return pl.pallas_call(
        _mem_kv_kernel,
        grid=(layers, n // tn),
        in_specs=[pl.BlockSpec((m, k), lambda l, j: (0, 0)),
                  pl.BlockSpec((None, 1, k), lambda l, j: (l, 0, 0)),
                  pl.BlockSpec((None, k, tn), lambda l, j: (l, 0, j))],
        out_specs=pl.BlockSpec((None, m, tn), lambda l, j: (l, 0, j)),
        out_shape=jax.ShapeDtypeStruct((layers, m, n), F32),
        compiler_params=_params("parallel", "parallel"),
        name="mem_kv",
    )(x, _gain_arg(g), w)


def _cast_kernel(w_ref, o_ref):
    o_ref[...] = w_ref[...].astype(BF16)


def _cast_bf16(w, layer):
    _, k, n = w.shape
    tk = 256
    assert k % tk == 0
    return pl.pallas_call(
        _cast_kernel,
        grid=(k // tk,),
        in_specs=[pl.BlockSpec((None, tk, n), lambda i: (layer, i, 0))],
        out_specs=pl.BlockSpec((tk, n), lambda i: (i, 0)),
        out_shape=jax.ShapeDtypeStruct((k, n), BF16),
        compiler_params=_params("parallel"),
        name="cast_bf16",
    )(w)


BF16_SUBLANES = 16


def _cast_jobs(weights, n_steps, step_of):
    in_specs, args, out_specs, out_shape = [], [], [], []
    for w, layer in weights:
        _, k, n = w.shape
        slab = -(-(-(-k // n_steps)) // BF16_SUBLANES) * BF16_SUBLANES
        while k % slab:
            slab += BF16_SUBLANES
        n_slabs = k // slab
        in_specs.append(pl.BlockSpec(
            (None, slab, n), lambda *g, layer=layer, n_slabs=n_slabs:
            (layer, jnp.minimum(step_of(*g), n_slabs - 1), 0)))
        out_specs.append(pl.BlockSpec(
            (slab, n), lambda *g, n_slabs=n_slabs: (jnp.minimum(step_of(*g), n_slabs - 1), 0)))
        args.append(w)
        out_shape.append(jax.ShapeDtypeStruct((k, n), BF16))
    return in_specs, args, out_specs, out_shape


def _run_cast_jobs(in_refs, out_refs):
    for src, dst in zip(in_refs, out_refs):
        dst[...] = src[...].astype(BF16)


def _resident_spec(shape):
    return pl.BlockSpec(shape, lambda *_: (0,) * len(shape), pipeline_mode=pl.Buffered(1))


def _gelu_tanh(x):
    return x * (0.5 * (1.0 + jnp.tanh(math.sqrt(2.0 / math.pi) * (x + 0.044715 * (x * x * x)))))


IN_PROJ_ROWS = 512


def _pass(h_ref, w_ref, col):
    return jnp.dot(h_ref[...], w_ref[:, col:col + V7X_MXU_COLS], preferred_element_type=F32)


def _heads(acc):
    return [acc[:, j * HEAD_DIM:(j + 1) * HEAD_DIM] for j in range(acc.shape[1] // HEAD_DIM)]


def _attn_in_proj_kernel(x_ref, g_ref, w_ref, pos_ref, invf_ref, *rest, n_jobs):
    qk_ref, vm_ref = rest[n_jobs:n_jobs + 2]
    h_ref, c_ref, s1_ref, s2_ref = rest[2 * n_jobs + 2:]
    _run_cast_jobs(rest[:n_jobs], rest[n_jobs + 2:2 * n_jobs + 2])
    h_ref[...] = _rms_rows(x_ref[...], g_ref[...]).astype(BF16)
    _rope_tables(pos_ref, invf_ref, c_ref, s1_ref, s2_ref)
    for c in range(2 * ATTN_W // V7X_MXU_COLS):
        for j, t in enumerate(_heads(_pass(h_ref, w_ref, c * V7X_MXU_COLS))):
            lo = c * V7X_MXU_COLS + j * HEAD_DIM
            qk_ref[:, lo:lo + HEAD_DIM] = (t * c_ref[...] + pltpu.roll(t, ROT_HALF, 1) * s1_ref[...]
                                           + pltpu.roll(t, HEAD_DIM - ROT_HALF, 1) * s2_ref[...])
    for c in range((ATTN_W + MEM_W) // V7X_MXU_COLS):
        cols = slice(c * V7X_MXU_COLS, (c + 1) * V7X_MXU_COLS)
        vm_ref[:, cols] = _pass(h_ref, w_ref, 2 * ATTN_W + c * V7X_MXU_COLS)


def _attn_in_proj(x, g, g_layer, w_bf, pos_col, invf, cast_weights=()):
    s, k = x.shape
    tm = IN_PROJ_ROWS
    assert w_bf.shape == (k, 3 * ATTN_W + MEM_W)
    row = lambda cols: pl.BlockSpec((tm, cols), lambda i: (i, 0))
    job_in, job_args, job_out, job_shape = _cast_jobs(cast_weights, s // tm, lambda i: i)
    return pl.pallas_call(
        functools.partial(_attn_in_proj_kernel, n_jobs=len(job_args)),
        grid=(s // tm,),
        in_specs=[row(k), _gain_spec(g_layer, k), _resident_spec(w_bf.shape), row(1),
                  pl.BlockSpec((1, HEAD_DIM), lambda i: (0, 0))] + job_in,
        out_specs=[row(2 * ATTN_W), row(ATTN_W + MEM_W)] + job_out,
        out_shape=[jax.ShapeDtypeStruct((s, 2 * ATTN_W), F32),
                   jax.ShapeDtypeStruct((s, ATTN_W + MEM_W), F32)] + job_shape,
        scratch_shapes=[pltpu.VMEM((tm, k), BF16)] + [pltpu.VMEM((tm, HEAD_DIM), F32)] * 3,
        compiler_params=_params("arbitrary"),
        name="attn_in_proj",
    )(x, _gain_arg(g), w_bf, pos_col, invf, *job_args)


def _sgu_in_proj_kernel(x_ref, g_ref, w_ref, lng_ref, lnb_ref, ws_ref, bt_ref,
                        mix_ref, qm_ref, h_ref, v_scr, vn_scr, wsb_scr):
    @pl.when(pl.program_id(0) == 0)
    def _():
        t = lax.broadcasted_iota(jnp.int32, (SGU_CHUNK, SGU_CHUNK), 0)
        s = lax.broadcasted_iota(jnp.int32, (SGU_CHUNK, SGU_CHUNK), 1)
        for g in range(SGU_GROUPS):
            wsb_scr[g] = jnp.where(t >= s, ws_ref[g], 0.0).astype(BF16)

    h_ref[...] = _rms_rows(x_ref[...], g_ref[...]).astype(BF16)
    for c in range(SGU_W // V7X_MXU_COLS):
        cols = slice(c * V7X_MXU_COLS, (c + 1) * V7X_MXU_COLS)
        v_scr[:, cols] = _gelu_tanh(_pass(h_ref, w_ref, SGU_W + c * V7X_MXU_COLS))
    for c in range(MEM_W // V7X_MXU_COLS):
        cols = slice(c * V7X_MXU_COLS, (c + 1) * V7X_MXU_COLS)
        qm_ref[:, cols] = _pass(h_ref, w_ref, 2 * SGU_W + c * V7X_MXU_COLS)
    v = v_scr[...]
    mu = jnp.mean(v, axis=-1, keepdims=True)
    vc = v - mu
    var = jnp.mean(vc * vc, axis=-1, keepdims=True)
    vn_scr[...] = (vc * lax.rsqrt(var + LN_EPS) * lng_ref[...] + lnb_ref[...]).astype(BF16)

    bt = bt_ref[...]
    n_chunks = mix_ref.shape[0] // SGU_CHUNK
    for c in range(SGU_W // V7X_MXU_COLS):
        for j, u in enumerate(_heads(_gelu_tanh(_pass(h_ref, w_ref, c * V7X_MXU_COLS)))):
            g = c * (V7X_MXU_COLS // HEAD_DIM) + j
            cols = slice(g * HEAD_DIM, (g + 1) * HEAD_DIM)
            vg = jnp.concatenate([vn_scr[r * SGU_CHUNK:(r + 1) * SGU_CHUNK, cols]
                                  for r in range(n_chunks)], axis=1)
            mixed = jnp.dot(wsb_scr[g], vg, preferred_element_type=F32) + bt[:, g:g + 1]
            for r in range(n_chunks):
                rows = slice(r * SGU_CHUNK, (r + 1) * SGU_CHUNK)
                mix_ref[rows, cols] = (u[rows, :] * mixed[:, r * HEAD_DIM:(r + 1) * HEAD_DIM]
                                       ).astype(mix_ref.dtype)


def _sgu_in_proj(x, g, g_layer, w_bf, ln_g, ln_b, w_spatial, b_spatial_t, layer):
    s, k = x.shape
    tm = IN_PROJ_ROWS
    assert w_bf.shape == (k, 2 * SGU_W + MEM_W) and tm % SGU_CHUNK == 0
    row = lambda cols: pl.BlockSpec((tm, cols), lambda i: (i, 0))
    return pl.pallas_call(
        _sgu_in_proj_kernel,
        grid=(s // tm,),
        in_specs=[row(k), _gain_spec(g_layer, k), _resident_spec(w_bf.shape),
                  _gain_spec(layer, SGU_W), _gain_spec(layer, SGU_W),
                  pl.BlockSpec((None, SGU_GROUPS, SGU_CHUNK, SGU_CHUNK), lambda i: (layer, 0, 0, 0)),
                  pl.BlockSpec((None, SGU_CHUNK, SGU_GROUPS), lambda i: (layer, 0, 0))],
        out_specs=[row(SGU_W), row(MEM_W)],
        out_shape=[jax.ShapeDtypeStruct((s, SGU_W), BF16), jax.ShapeDtypeStruct((s, MEM_W), F32)],
        scratch_shapes=[pltpu.VMEM((tm, k), BF16), pltpu.VMEM((tm, SGU_W), F32),
                        pltpu.VMEM((tm, SGU_W), BF16),
                        pltpu.VMEM((SGU_GROUPS, SGU_CHUNK, SGU_CHUNK), BF16)],
        compiler_params=_params("arbitrary"),
        name="sgu_in_proj",
    )(x, _gain_arg(g), w_bf, _gain_arg(ln_g), _gain_arg(ln_b), w_spatial, b_spatial_t)


def _mem_attn_kernel(q_ref, kv_ref, o_ref):
    for h in range(MEM_HEADS):
        sl = slice(h * HEAD_DIM, (h + 1) * HEAD_DIM)
        k = kv_ref[:, sl].astype(BF16)
        v = kv_ref[:, MEM_W + h * HEAD_DIM:MEM_W + (h + 1) * HEAD_DIM].astype(BF16)
        s = lax.dot_general(q_ref[:, sl].astype(BF16), k, (((1,), (1,)), ((), ())),
                            preferred_element_type=F32) * ATTN_SCALE
        m = jnp.max(s, axis=-1, keepdims=True)
        p = jnp.exp(s - m)
        l = jnp.sum(p, axis=-1, keepdims=True)
        o = jnp.dot(p.astype(BF16), v, preferred_element_type=F32)
        o_ref[:, sl] = (o * (1.0 / l)).astype(o_ref.dtype)


def _memory_attention(proj, q_col_block, kv, layer):
    s = proj.shape[0]
    tm = 1024
    return pl.pallas_call(
        _mem_attn_kernel,
        grid=(s // tm,),
        in_specs=[pl.BlockSpec((tm, MEM_W), lambda i: (i, q_col_block)),
                  pl.BlockSpec((None, MEM_LEN, 2 * MEM_W), lambda i: (layer, 0, 0))],
        out_specs=pl.BlockSpec((tm, MEM_W), lambda i: (i, 0)),
        out_shape=jax.ShapeDtypeStruct((s, MEM_W), BF16),
        compiler_params=_params("parallel"),
        name="memory_attention",
    )(proj, kv)


def _out_proj_kernel(a_ref, b_ref, w_ref, x_ref, g_ref, o_ref, hn_ref):
    ka = a_ref.shape[1]
    width = o_ref.shape[1]
    n_chunk = 512
    ssq = jnp.zeros((o_ref.shape[0], 1), F32)
    for c in range(width // n_chunk):
        cols = slice(c * n_chunk, (c + 1) * n_chunk)
        y = (x_ref[:, cols]
             + jnp.dot(a_ref[...], w_ref[:ka, cols], preferred_element_type=F32)
             + jnp.dot(b_ref[...], w_ref[ka:, cols], preferred_element_type=F32))
        o_ref[:, cols] = y
        ssq = ssq + jnp.sum(y * y, axis=-1, keepdims=True)
    scale = lax.rsqrt(ssq * (1.0 / width) + NORM_EPS)
    for c in range(width // n_chunk):
        cols = slice(c * n_chunk, (c + 1) * n_chunk)
        hn_ref[:, cols] = (o_ref[:, cols] * scale * g_ref[:, cols]).astype(BF16)


def _out_proj(a, b, w_bf, x, g, g_layer):
    s, ka = a.shape
    kb = b.shape[1]
    width = x.shape[1]
    tm = 512
    assert w_bf.shape == (ka + kb, width)
    row = lambda cols: pl.BlockSpec((tm, cols), lambda i: (i, 0))
    return pl.pallas_call(
        _out_proj_kernel,
        grid=(s // tm,),
        in_specs=[row(ka), row(kb), _resident_spec((ka + kb, width)), row(width),
                  _gain_spec(g_layer, width)],
        out_specs=[row(width), row(width)],
        out_shape=[jax.ShapeDtypeStruct((s, width), F32), jax.ShapeDtypeStruct((s, width), BF16)],
        compiler_params=_params("parallel"),
        name="out_proj",
    )(a, b, w_bf, x, _gain_arg(g))


def _rows(ref, start, size, stride):
    if stride == 1:
        return ref[pl.ds(start, size), :]
    return ref[pl.ds(start, size, stride=stride), :]


V7X_FREE_SUBLANE_STRIDE = 4


def _presplit(d):
    return d // V7X_FREE_SUBLANE_STRIDE if d > V7X_FREE_SUBLANE_STRIDE else 1


def _attn_batch(units):
    scores = [lax.dot_general(q.astype(BF16), kc.astype(BF16), (((1,), (1,)), ((), ())),
                              preferred_element_type=F32) * ATTN_SCALE + bias
              for q, kc, _, bias in units]
    probs = []
    for s in scores:
        m = jnp.max(s, axis=-1, keepdims=True)
        p = jnp.exp(s - m)
        probs.append((p, m, jnp.sum(p, axis=-1, keepdims=True)))
    outs = []
    for (p, m, l), (_, _, vc, _) in zip(probs, units):
        o = jnp.dot(p.astype(BF16), vc.astype(BF16), preferred_element_type=F32)
        outs.append((o * (1.0 / l), m + jnp.log(l)))
    return outs


def _attn_kernel(*refs):
    ngrp = len(ATTN_GROUPS)
    in_refs = refs[:5 * ngrp]
    o_ref = refs[5 * ngrp]
    pos = 5 * ngrp + 1
    o_scr, l_scr = refs[pos:pos + ngrp], refs[pos + ngrp:pos + 2 * ngrp]
    split_refs = refs[pos + 2 * ngrp:pos + 2 * ngrp + 7]
    bias_ref = refs[-1]

    qi = lax.broadcasted_iota(jnp.int32, (BLK, 2 * BLK), 0)
    ki = lax.broadcasted_iota(jnp.int32, (BLK, 2 * BLK), 1)
    lo = jnp.where(pl.program_id(0) == 0, BLK, 0)
    bias_ref[0] = jnp.where((ki >= qi) & (ki <= qi + N_BACK), 0.0, NEG_INF)
    bias_ref[1] = jnp.where((ki >= jnp.maximum(qi, lo)) & (ki <= qi + N_BACK), 0.0, NEG_INF)

    for g, (_, d) in enumerate(ATTN_GROUPS):
        q_ref, k_ref, v_ref, kp_ref, vp_ref = in_refs[5 * g:5 * g + 5]
        og, lg = o_scr[g], l_scr[g]
        sub = d * BLK
        f = _presplit(d)
        d2 = d // f
        qlen, sublen = ATTN_SPAN // f, sub // f
        if f > 1:
            q_src, kp_src, k_src, vp_src, v_src, o_dst, l_dst = split_refs
            for rf in range(f):
                for src, dst, n in ((q_ref, q_src, qlen), (k_ref, k_src, qlen), (v_ref, v_src, qlen),
                                    (kp_ref, kp_src, sublen), (vp_ref, vp_src, sublen)):
                    dst[pl.ds(rf * n, n), :] = _rows(src, rf, n, f)
        else:
            q_src, kp_src, k_src, vp_src, v_src, o_dst, l_dst = (
                q_ref, kp_ref, k_ref, vp_ref, v_ref, og, lg)

        def key_rows(prev_src, src, c, rf, r2):
            if c == 0:
                return jnp.concatenate([_rows(prev_src, rf * sublen + r2, BLK, d2),
                                        _rows(src, rf * qlen + r2, BLK, d2)], axis=0)
            return _rows(src, rf * qlen + (c - 1) * sublen + r2, 2 * BLK, d2)

        units, starts = [], []
        for idx in range(ATTN_SPAN // BLK):
            c, r = divmod(idx, d)
            rf, r2 = r % f, r // f
            qstart = rf * qlen + c * sublen + r2
            units.append((_rows(q_src, qstart, BLK, d2), key_rows(kp_src, k_src, c, rf, r2),
                          key_rows(vp_src, v_src, c, rf, r2), bias_ref[1 if c == 0 else 0]))
            starts.append(qstart)
        for qstart, (o, lse) in zip(starts, _attn_batch(units)):
            rows = pl.ds(qstart, BLK) if d2 == 1 else pl.ds(qstart, BLK, stride=d2)
            o_dst[rows, :] = o
            l_dst[rows, :] = jnp.broadcast_to(lse, (BLK, HEAD_DIM))
        if f > 1:
            for rf in range(f):
                og[pl.ds(rf, qlen, stride=f), :] = o_dst[pl.ds(rf * qlen, qlen), :]
                lg[pl.ds(rf, qlen, stride=f), :] = l_dst[pl.ds(rf * qlen, qlen), :]

    lses = [l[...] for l in l_scr]
    mx = functools.reduce(jnp.maximum, lses)
    ws = [jnp.exp(l - mx) for l in lses]
    num = functools.reduce(lambda a, b: a + b, [w * o[...] for w, o in zip(ws, o_scr)])
    den = functools.reduce(lambda a, b: a + b, ws)
    o_ref[...] = (num * (1.0 / den)).astype(o_ref.dtype)


def _dilated_attention(qk, vm):
    s = qk.shape[0]
    assert s % ATTN_SPAN == 0
    hpg = HEADS_PER_GROUP
    in_specs, args = [], []
    for g, (_, d) in enumerate(ATTN_GROUPS):
        sub = d * BLK
        ratio = ATTN_SPAN // sub
        qc, kc, vc = g * hpg, ATTN_HEADS + g * hpg, g * hpg
        cur = lambda col: pl.BlockSpec((ATTN_SPAN, HEAD_DIM), lambda i, h, col=col: (i, col + h))
        prev = lambda col, ratio=ratio, sub=sub: pl.BlockSpec(
            (sub, HEAD_DIM), lambda i, h, col=col, ratio=ratio: (jnp.maximum(i * ratio - 1, 0), col + h))
        in_specs += [cur(qc), cur(kc), cur(vc), prev(kc), prev(vc)]
        args += [qk, qk, vm, qk, vm]
    ngrp = len(ATTN_GROUPS)
    return pl.pallas_call(
        _attn_kernel,
        grid=(s // ATTN_SPAN, hpg),
        in_specs=in_specs,
        out_specs=pl.BlockSpec((ATTN_SPAN, HEAD_DIM), lambda i, h: (i, h)),
        out_shape=jax.ShapeDtypeStruct((s, ATTN_OUT_W), BF16),
        scratch_shapes=([pltpu.VMEM((ATTN_SPAN, HEAD_DIM), F32)] * (2 * ngrp)
                        + [pltpu.VMEM((ATTN_SPAN, HEAD_DIM), F32)] * 7
                        + [pltpu.VMEM((2, BLK, 2 * BLK), F32)]),
        compiler_params=_params("parallel", "arbitrary"),
        name="dilated_attention",
    )(*args)


def _ffn_down_rows_kernel(a_ref, w_ref, x_ref, *rest, final_norm):
    if final_norm:
        g_ref, o_ref = rest
    else:
        (o_ref,) = rest
    width = o_ref.shape[1]
    ssq = jnp.zeros((o_ref.shape[0], 1), F32)
    for c in range(width // V7X_MXU_COLS):
        cols = slice(c * V7X_MXU_COLS, (c + 1) * V7X_MXU_COLS)
        y = x_ref[:, cols] + jnp.dot(a_ref[...], w_ref[:, cols], preferred_element_type=F32)
        o_ref[:, cols] = y
        if final_norm:
            ssq = ssq + jnp.sum(y * y, axis=-1, keepdims=True)
    if final_norm:
        scale = lax.rsqrt(ssq * (1.0 / width) + NORM_EPS)
        for c in range(width // V7X_MXU_COLS):
            cols = slice(c * V7X_MXU_COLS, (c + 1) * V7X_MXU_COLS)
            o_ref[:, cols] = o_ref[:, cols] * scale * g_ref[:, cols]


def _ffn_down_rows(a, w_bf, x, final_g=None):
    s, f = a.shape
    width = x.shape[1]
    tm = 512
    assert w_bf.shape == (f, width) and s % tm == 0
    row = lambda cols: pl.BlockSpec((tm, cols), lambda i: (i, 0))
    in_specs = [row(f), _resident_spec(w_bf.shape), row(width)]
    args = [a, w_bf, x]
    if final_g is not None:
        in_specs.append(_gain_spec(0, width))
        args.append(final_g.reshape(1, 1, width))
    return pl.pallas_call(
        functools.partial(_ffn_down_rows_kernel, final_norm=final_g is not None),
        grid=(s // tm,),
        in_specs=in_specs,
        out_specs=row(width),
        out_shape=jax.ShapeDtypeStruct((s, width), F32),
        compiler_params=_params("parallel"),
        name="ffn_down_rows",
    )(*args)


def _ffn_gate_up_kernel(h_ref, wg_ref, wu_ref, *rest, n_jobs):
    o_ref = rest[n_jobs]
    _run_cast_jobs(rest[:n_jobs], rest[n_jobs + 1:])
    for c in range(o_ref.shape[1] // V7X_MXU_COLS):
        cols = slice(c * V7X_MXU_COLS, (c + 1) * V7X_MXU_COLS)
        gate = jnp.dot(h_ref[...], wg_ref[:, cols].astype(BF16), preferred_element_type=F32)
        up = jnp.dot(h_ref[...], wu_ref[:, cols].astype(BF16), preferred_element_type=F32)
        o_ref[:, cols] = (gate * (1.0 / (1.0 + jnp.exp(-gate))) * up).astype(o_ref.dtype)


def _ffn_gate_up(h, w_gate, w_up, layer, cast_weights=()):
    s, k = h.shape
    f = w_gate.shape[2]
    tm, tf = 2048, 512
    nj = f // tf
    wspec = pl.BlockSpec((None, k, tf), lambda i, j: (layer, 0, j))
    job_in, job_args, job_out, job_shape = _cast_jobs(cast_weights, (s // tm) * nj,
                                                      lambda i, j: i * nj + j)
    return pl.pallas_call(
        functools.partial(_ffn_gate_up_kernel, n_jobs=len(job_args)),
        grid=(s // tm, nj),
        in_specs=[pl.BlockSpec((tm, k), lambda i, j: (i, 0)), wspec, wspec] + job_in,
        out_specs=[pl.BlockSpec((tm, tf), lambda i, j: (i, j))] + job_out,
        out_shape=[jax.ShapeDtypeStruct((s, f), BF16)] + job_shape,
        compiler_params=_params("arbitrary", "arbitrary"),
        name="ffn_gate_up",
    )(h, w_gate, w_up, *job_args)


def kernel(x, mem, positions, mix_norm, mem_norm, w_mem_kv, ffn_norm, w_gate, w_up, w_down,
           attn_w_in, attn_w_out, sgu_w_in, sgu_ln_g, sgu_ln_b, sgu_w_spatial, sgu_b_spatial,
           sgu_w_out, final_norm):
    b, s, d = x.shape
    assert (b, s, d) == (1, SEQ, D_MODEL) and mem.shape == (1, MEM_LEN, D_MODEL)
    xs = x.reshape(s, d)
    mems = mem.reshape(MEM_LEN, d)

    inv_freq = ROPE_THETA ** (-jnp.arange(ROT_HALF, dtype=F32) / ROT_HALF)
    invf = jnp.concatenate([inv_freq, inv_freq, jnp.zeros((HEAD_DIM - ROT_DIM,), F32)])
    pos_col, invf = positions.reshape(s, 1), invf.reshape(1, HEAD_DIM)

    def mixer_weights(layer):
        return (attn_w_in, attn_w_out) if layer % 2 == 0 else (sgu_w_in, sgu_w_out)

    w_in_bf, w_out_bf = _cast_bf16(attn_w_in, 0), None
    kv = _mem_kv(mems, mem_norm, w_mem_kv)
    for i in range(DEPTH):
        j = i // 2
        if i % 2 == 0:
            casts = [(attn_w_out, j)] if w_out_bf is None else []
            qk, vm, *cast = _attn_in_proj(xs, mix_norm, i, w_in_bf, pos_col, invf, casts)
            w_out_bf = cast[0] if cast else w_out_bf
            mix = _dilated_attention(qk, vm)
            mem_out = _memory_attention(vm, ATTN_W // MEM_W, kv, i)
        else:
            mix, q_mem = _sgu_in_proj(xs, mix_norm, i, w_in_bf, sgu_ln_g, sgu_ln_b,
                                      sgu_w_spatial, jnp.swapaxes(sgu_b_spatial, 1, 2), j)
            mem_out = _memory_attention(q_mem, 0, kv, i)
        xs, hn = _out_proj(mix, mem_out, w_out_bf, xs, ffn_norm, i)
        last = i + 1 == DEPTH
        casts = [(w_down, i)] + ([] if last else [(w, (i + 1) // 2) for w in mixer_weights(i + 1)])
        act, w_down_bf, *next_bf = _ffn_gate_up(hn, w_gate, w_up, i, casts)
        w_in_bf, w_out_bf = next_bf if next_bf else (None, None)
        xs = _ffn_down_rows(act, w_down_bf, xs, final_g=final_norm if last else None)
    return xs.reshape(b, s, d)
```

```python
import functools
import math

import jax
import jax.numpy as jnp
from jax import lax
from jax.experimental import pallas as pl
from jax.experimental.pallas import tpu as pltpu

D_MODEL = 2048
SEQ = 8192
DEPTH = 2
MEM_LEN = 256
HEAD_DIM = 128
MEM_HEADS = 4
MEM_W = MEM_HEADS * HEAD_DIM
ATTN_GROUPS = ((128, 1), (512, 4), (2048, 16))
ATTN_HEADS = 12
HEADS_PER_GROUP = 4
ATTN_W = ATTN_HEADS * HEAD_DIM
ATTN_OUT_W = HEADS_PER_GROUP * HEAD_DIM
BLK = 128
SGU_GROUPS = 12
SGU_W = SGU_GROUPS * HEAD_DIM
SGU_CHUNK = 128
ROT_DIM = HEAD_DIM // 4
ROT_HALF = ROT_DIM // 2
ROPE_THETA = 500000.0
NORM_EPS = 1e-6
LN_EPS = 1e-5
NEG_INF = -1e30
ATTN_SCALE = HEAD_DIM ** -0.5

N_BACK = BLK
assert all(w // d == N_BACK for w, d in ATTN_GROUPS)
ATTN_SPAN = max(d for _, d in ATTN_GROUPS) * BLK

V7X_VMEM_LIMIT_BYTES = 56 * 1024 * 1024
V7X_MXU_COLS = 256

ROW_STEP = 512
GATE_UP_ROWS, GATE_UP_COLS = 2048, 512
MEM_ATTN_ROWS = 1024
MEM_KV_COLS = 512
CAST_ROWS = 256

F32 = jnp.float32
BF16 = jnp.bfloat16


def _params(*semantics):
    return pltpu.CompilerParams(dimension_semantics=semantics,
                                vmem_limit_bytes=V7X_VMEM_LIMIT_BYTES)


def _rms_rows(x, g):
    ms = jnp.mean(x * x, axis=-1, keepdims=True)
    return x * lax.rsqrt(ms + NORM_EPS) * g


def _gain_spec(layer, k):
    return pl.BlockSpec((None, 1, k), lambda *_: (layer, 0, 0))


def _gain_arg(g):
    return g.reshape(g.shape[0], 1, g.shape[1])


ROPE_PACK = HEAD_DIM // ROT_DIM


def _rope_tables(pos_ref, invf_ref, c_ref, s1_ref, s2_ref):
    n = pos_ref.shape[0] // ROPE_PACK
    lane = lax.broadcasted_iota(jnp.int32, (n, HEAD_DIM), 1)
    group = lane // ROT_DIM
    pos = jnp.zeros((n, HEAD_DIM), F32)
    for q in range(ROPE_PACK):
        pos = jnp.where(group == q, pos_ref[q * n:(q + 1) * n, :].astype(F32), pos)
    ang = pos * invf_ref[...]
    cos, sin = jnp.cos(ang), jnp.sin(ang)
    for q in range(ROPE_PACK):
        cq = cos if q == 0 else pltpu.roll(cos, HEAD_DIM - q * ROT_DIM, 1)
        sq = sin if q == 0 else pltpu.roll(sin, HEAD_DIM - q * ROT_DIM, 1)
        rows = slice(q * n, (q + 1) * n)
        c_ref[rows, :] = jnp.where(lane < ROT_DIM, cq, 1.0)
        s1_ref[rows, :] = jnp.where(lane < ROT_HALF, 0.0, jnp.where(lane < ROT_DIM, sq, 0.0))
        s2_ref[rows, :] = jnp.where(lane < ROT_HALF, -sq, 0.0)


def _mem_kv_kernel(x_ref, g_ref, w_ref, o_ref):
    h = _rms_rows(x_ref[...], g_ref[...]).astype(BF16)
    o_ref[...] = jnp.dot(h, w_ref[...].astype(BF16), preferred_element_type=F32)


def _mem_kv(x, g, w):
    m, k = x.shape
    layers, _, n = w.shape
    tn = MEM_KV_COLS
    assert n % tn == 0
    return pl.pallas_call(
        _mem_kv_kernel,
        grid=(layers, n // tn),
        in_specs=[pl.BlockSpec((m, k), lambda l, j: (0, 0)),
                  pl.BlockSpec((None, 1, k), lambda l, j: (l, 0, 0)),
                  pl.BlockSpec((None, k, tn), lambda l, j: (l, 0, j))],
        out_specs=pl.BlockSpec((None, m, tn), lambda l, j: (l, 0, j)),
        out_shape=jax.ShapeDtypeStruct((layers, m, n), F32),
        compiler_params=_params("parallel", "parallel"),
        name="mem_kv",
    )(x, _gain_arg(g), w)


def _cast_kernel(w_ref, o_ref):
    o_ref[...] = w_ref[...].astype(BF16)


def _cast_bf16(w, layer):
    _, k, n = w.shape
    tk = CAST_ROWS
    assert k % tk == 0
    return pl.pallas_call(
        _cast_kernel,
        grid=(k // tk,),
        in_specs=[pl.BlockSpec((None, tk, n), lambda i: (layer, i, 0))],
        out_specs=pl.BlockSpec((tk, n), lambda i: (i, 0)),
        out_shape=jax.ShapeDtypeStruct((k, n), BF16),
        compiler_params=_params("parallel"),
        name="cast_bf16",
    )(w)


BF16_SUBLANES = 16


def _cast_jobs(weights, n_steps, step_of):
    in_specs, args, out_specs, out_shape = [], [], [], []
    for w, layer in weights:
        _, k, n = w.shape
        slab = -(-(-(-k // n_steps)) // BF16_SUBLANES) * BF16_SUBLANES
        while k % slab:
            slab += BF16_SUBLANES
        n_slabs = k // slab
        in_specs.append(pl.BlockSpec(
            (None, slab, n), lambda *g, layer=layer, n_slabs=n_slabs:
            (layer, jnp.minimum(step_of(*g), n_slabs - 1), 0)))
        out_specs.append(pl.BlockSpec(
            (slab, n), lambda *g, n_slabs=n_slabs: (jnp.minimum(step_of(*g), n_slabs - 1), 0)))
        args.append(w)
        out_shape.append(jax.ShapeDtypeStruct((k, n), BF16))
    return in_specs, args, out_specs, out_shape


def _run_cast_jobs(in_refs, out_refs):
    for src, dst in zip(in_refs, out_refs):
        dst[...] = src[...].astype(BF16)


def _resident_spec(shape):
    return pl.BlockSpec(shape, lambda *_: (0,) * len(shape), pipeline_mode=pl.Buffered(1))


def _gelu_tanh(x):
    return x * (0.5 * (1.0 + jnp.tanh(math.sqrt(2.0 / math.pi) * (x + 0.044715 * (x * x * x)))))


def _pass(h_ref, w_ref, col):
    return jnp.dot(h_ref[...], w_ref[:, col:col + V7X_MXU_COLS], preferred_element_type=F32)


def _heads(acc):
    return [acc[:, j * HEAD_DIM:(j + 1) * HEAD_DIM] for j in range(acc.shape[1] // HEAD_DIM)]


def _attn_in_proj_kernel(x_ref, g_ref, w_ref, pos_ref, invf_ref, *rest, n_jobs):
    qk_ref, vm_ref = rest[n_jobs:n_jobs + 2]
    h_ref, c_ref, s1_ref, s2_ref = rest[2 * n_jobs + 2:]
    _run_cast_jobs(rest[:n_jobs], rest[n_jobs + 2:2 * n_jobs + 2])
    h_ref[...] = _rms_rows(x_ref[...], g_ref[...]).astype(BF16)
    _rope_tables(pos_ref, invf_ref, c_ref, s1_ref, s2_ref)
    for c in range(2 * ATTN_W // V7X_MXU_COLS):
        for j, t in enumerate(_heads(_pass(h_ref, w_ref, c * V7X_MXU_COLS))):
            lo = c * V7X_MXU_COLS + j * HEAD_DIM
            qk_ref[:, lo:lo + HEAD_DIM] = (t * c_ref[...] + pltpu.roll(t, ROT_HALF, 1) * s1_ref[...]
                                           + pltpu.roll(t, HEAD_DIM - ROT_HALF, 1) * s2_ref[...])
    for c in range((ATTN_W + MEM_W) // V7X_MXU_COLS):
        cols = slice(c * V7X_MXU_COLS, (c + 1) * V7X_MXU_COLS)
        vm_ref[:, cols] = _pass(h_ref, w_ref, 2 * ATTN_W + c * V7X_MXU_COLS)


def _attn_in_proj(x, g, g_layer, w_bf, pos_col, invf, cast_weights=()):
    s, k = x.shape
    tm = ROW_STEP
    assert w_bf.shape == (k, 3 * ATTN_W + MEM_W)
    row = lambda cols: pl.BlockSpec((tm, cols), lambda i: (i, 0))
    job_in, job_args, job_out, job_shape = _cast_jobs(cast_weights, s // tm, lambda i: i)
    return pl.pallas_call(
        functools.partial(_attn_in_proj_kernel, n_jobs=len(job_args)),
        grid=(s // tm,),
        in_specs=[row(k), _gain_spec(g_layer, k), _resident_spec(w_bf.shape), row(1),
                  pl.BlockSpec((1, HEAD_DIM), lambda i: (0, 0))] + job_in,
        out_specs=[row(2 * ATTN_W), row(ATTN_W + MEM_W)] + job_out,
        out_shape=[jax.ShapeDtypeStruct((s, 2 * ATTN_W), F32),
                   jax.ShapeDtypeStruct((s, ATTN_W + MEM_W), F32)] + job_shape,
        scratch_shapes=[pltpu.VMEM((tm, k), BF16)] + [pltpu.VMEM((tm, HEAD_DIM), F32)] * 3,
        compiler_params=_params("arbitrary"),
        name="attn_in_proj",
    )(x, _gain_arg(g), w_bf, pos_col, invf, *job_args)


def _sgu_in_proj_kernel(x_ref, g_ref, w_ref, lng_ref, lnb_ref, ws_ref, bt_ref,
                        mix_ref, qm_ref, h_ref, v_scr, vn_scr, wsb_scr):
    @pl.when(pl.program_id(0) == 0)
    def _():
        t = lax.broadcasted_iota(jnp.int32, (SGU_CHUNK, SGU_CHUNK), 0)
        s = lax.broadcasted_iota(jnp.int32, (SGU_CHUNK, SGU_CHUNK), 1)
        for g in range(SGU_GROUPS):
            wsb_scr[g] = jnp.where(t >= s, ws_ref[g], 0.0).astype(BF16)

    h_ref[...] = _rms_rows(x_ref[...], g_ref[...]).astype(BF16)
    for c in range(SGU_W // V7X_MXU_COLS):
        cols = slice(c * V7X_MXU_COLS, (c + 1) * V7X_MXU_COLS)
        v_scr[:, cols] = _gelu_tanh(_pass(h_ref, w_ref, SGU_W + c * V7X_MXU_COLS))
    for c in range(MEM_W // V7X_MXU_COLS):
        cols = slice(c * V7X_MXU_COLS, (c + 1) * V7X_MXU_COLS)
        qm_ref[:, cols] = _pass(h_ref, w_ref, 2 * SGU_W + c * V7X_MXU_COLS)
    v = v_scr[...]
    mu = jnp.mean(v, axis=-1, keepdims=True)
    vc = v - mu
    var = jnp.mean(vc * vc, axis=-1, keepdims=True)
    vn_scr[...] = (vc * lax.rsqrt(var + LN_EPS) * lng_ref[...] + lnb_ref[...]).astype(BF16)

    bt = bt_ref[...]
    n_chunks = mix_ref.shape[0] // SGU_CHUNK
    for c in range(SGU_W // V7X_MXU_COLS):
        for j, u in enumerate(_heads(_gelu_tanh(_pass(h_ref, w_ref, c * V7X_MXU_COLS)))):
            g = c * (V7X_MXU_COLS // HEAD_DIM) + j
            cols = slice(g * HEAD_DIM, (g + 1) * HEAD_DIM)
            vg = jnp.concatenate([vn_scr[r * SGU_CHUNK:(r + 1) * SGU_CHUNK, cols]
                                  for r in range(n_chunks)], axis=1)
            mixed = jnp.dot(wsb_scr[g], vg, preferred_element_type=F32) + bt[:, g:g + 1]
            for r in range(n_chunks):
                rows = slice(r * SGU_CHUNK, (r + 1) * SGU_CHUNK)
                mix_ref[rows, cols] = (u[rows, :] * mixed[:, r * HEAD_DIM:(r + 1) * HEAD_DIM]
                                       ).astype(mix_ref.dtype)


def _sgu_in_proj(x, g, g_layer, w_bf, ln_g, ln_b, w_spatial, b_spatial_t, layer):
    s, k = x.shape
    tm = ROW_STEP
    assert w_bf.shape == (k, 2 * SGU_W + MEM_W) and tm % SGU_CHUNK == 0
    row = lambda cols: pl.BlockSpec((tm, cols), lambda i: (i, 0))
    return pl.pallas_call(
        _sgu_in_proj_kernel,
        grid=(s // tm,),
        in_specs=[row(k), _gain_spec(g_layer, k), _resident_spec(w_bf.shape),
                  _gain_spec(layer, SGU_W), _gain_spec(layer, SGU_W),
                  pl.BlockSpec((None, SGU_GROUPS, SGU_CHUNK, SGU_CHUNK), lambda i: (layer, 0, 0, 0)),
                  pl.BlockSpec((None, SGU_CHUNK, SGU_GROUPS), lambda i: (layer, 0, 0))],
        out_specs=[row(SGU_W), row(MEM_W)],
        out_shape=[jax.ShapeDtypeStruct((s, SGU_W), BF16), jax.ShapeDtypeStruct((s, MEM_W), F32)],
        scratch_shapes=[pltpu.VMEM((tm, k), BF16), pltpu.VMEM((tm, SGU_W), F32),
                        pltpu.VMEM((tm, SGU_W), BF16),
                        pltpu.VMEM((SGU_GROUPS, SGU_CHUNK, SGU_CHUNK), BF16)],
        compiler_params=_params("arbitrary"),
        name="sgu_in_proj",
    )(x, _gain_arg(g), w_bf, _gain_arg(ln_g), _gain_arg(ln_b), w_spatial, b_spatial_t)


def _mem_attn_kernel(q_ref, kv_ref, o_ref):
    for h in range(MEM_HEADS):
        sl = slice(h * HEAD_DIM, (h + 1) * HEAD_DIM)
        k = kv_ref[:, sl].astype(BF16)
        v = kv_ref[:, MEM_W + h * HEAD_DIM:MEM_W + (h + 1) * HEAD_DIM].astype(BF16)
        s = lax.dot_general(q_ref[:, sl].astype(BF16), k, (((1,), (1,)), ((), ())),
                            preferred_element_type=F32) * ATTN_SCALE
        m = jnp.max(s, axis=-1, keepdims=True)
        p = jnp.exp(s - m)
        l = jnp.sum(p, axis=-1, keepdims=True)
        o = jnp.dot(p.astype(BF16), v, preferred_element_type=F32)
        o_ref[:, sl] = (o * (1.0 / l)).astype(o_ref.dtype)


def _memory_attention(proj, q_col_block, kv, layer):
    s = proj.shape[0]
    tm = MEM_ATTN_ROWS
    return pl.pallas_call(
        _mem_attn_kernel,
        grid=(s // tm,),
        in_specs=[pl.BlockSpec((tm, MEM_W), lambda i: (i, q_col_block)),
                  pl.BlockSpec((None, MEM_LEN, 2 * MEM_W), lambda i: (layer, 0, 0))],
        out_specs=pl.BlockSpec((tm, MEM_W), lambda i: (i, 0)),
        out_shape=jax.ShapeDtypeStruct((s, MEM_W), BF16),
        compiler_params=_params("parallel"),
        name="memory_attention",
    )(proj, kv)


def _out_proj_kernel(a_ref, b_ref, w_ref, x_ref, g_ref, o_ref, hn_ref):
    ka = a_ref.shape[1]
    width = o_ref.shape[1]
    n_chunk = 512
    ssq = jnp.zeros((o_ref.shape[0], 1), F32)
    for c in range(width // n_chunk):
        cols = slice(c * n_chunk, (c + 1) * n_chunk)
        y = (x_ref[:, cols]
             + jnp.dot(a_ref[...], w_ref[:ka, cols], preferred_element_type=F32)
             + jnp.dot(b_ref[...], w_ref[ka:, cols], preferred_element_type=F32))
        o_ref[:, cols] = y
        ssq = ssq + jnp.sum(y * y, axis=-1, keepdims=True)
    scale = lax.rsqrt(ssq * (1.0 / width) + NORM_EPS)
    for c in range(width // n_chunk):
        cols = slice(c * n_chunk, (c + 1) * n_chunk)
        hn_ref[:, cols] = (o_ref[:, cols] * scale * g_ref[:, cols]).astype(BF16)


def _out_proj(a, b, w_bf, x, g, g_layer):
    s, ka = a.shape
    kb = b.shape[1]
    width = x.shape[1]
    tm = ROW_STEP
    assert w_bf.shape == (ka + kb, width)
    row = lambda cols: pl.BlockSpec((tm, cols), lambda i: (i, 0))
    return pl.pallas_call(
        _out_proj_kernel,
        grid=(s // tm,),
        in_specs=[row(ka), row(kb), _resident_spec((ka + kb, width)), row(width),
                  _gain_spec(g_layer, width)],
        out_specs=[row(width), row(width)],
        out_shape=[jax.ShapeDtypeStruct((s, width), F32), jax.ShapeDtypeStruct((s, width), BF16)],
        compiler_params=_params("parallel"),
        name="out_proj",
    )(a, b, w_bf, x, _gain_arg(g))


def _rows(ref, start, size, stride):
    if stride == 1:
        return ref[pl.ds(start, size), :]
    return ref[pl.ds(start, size, stride=stride), :]


V7X_FREE_SUBLANE_STRIDE = 4


def _presplit(d):
    return d // V7X_FREE_SUBLANE_STRIDE if d > V7X_FREE_SUBLANE_STRIDE else 1


def _attn_batch(units):
    scores = [lax.dot_general(q.astype(BF16), kc.astype(BF16), (((1,), (1,)), ((), ())),
                              preferred_element_type=F32) * ATTN_SCALE + bias
              for q, kc, _, bias in units]
    probs = []
    for s in scores:
        m = jnp.max(s, axis=-1, keepdims=True)
        p = jnp.exp(s - m)
        probs.append((p, m, jnp.sum(p, axis=-1, keepdims=True)))
    outs = []
    for (p, m, l), (_, _, vc, _) in zip(probs, units):
        o = jnp.dot(p.astype(BF16), vc.astype(BF16), preferred_element_type=F32)
        outs.append((o * (1.0 / l), m + jnp.log(l)))
    return outs


def _attn_kernel(*refs):
    ngrp = len(ATTN_GROUPS)
    in_refs = refs[:5 * ngrp]
    o_ref = refs[5 * ngrp]
    pos = 5 * ngrp + 1
    o_scr, l_scr = refs[pos:pos + ngrp], refs[pos + ngrp:pos + 2 * ngrp]
    split_refs = refs[pos + 2 * ngrp:pos + 2 * ngrp + 7]
    bias_ref = refs[-1]

    qi = lax.broadcasted_iota(jnp.int32, (BLK, 2 * BLK), 0)
    ki = lax.broadcasted_iota(jnp.int32, (BLK, 2 * BLK), 1)
    lo = jnp.where(pl.program_id(0) == 0, BLK, 0)
    bias_ref[0] = jnp.where((ki >= qi) & (ki <= qi + N_BACK), 0.0, NEG_INF)
    bias_ref[1] = jnp.where((ki >= jnp.maximum(qi, lo)) & (ki <= qi + N_BACK), 0.0, NEG_INF)

    for g, (_, d) in enumerate(ATTN_GROUPS):
        q_ref, k_ref, v_ref, kp_ref, vp_ref = in_refs[5 * g:5 * g + 5]
        og, lg = o_scr[g], l_scr[g]
        sub = d * BLK
        f = _presplit(d)
        d2 = d // f
        qlen, sublen = ATTN_SPAN // f, sub // f
        if f > 1:
            q_src, kp_src, k_src, vp_src, v_src, o_dst, l_dst = split_refs
            for rf in range(f):
                for src, dst, n in ((q_ref, q_src, qlen), (k_ref, k_src, qlen), (v_ref, v_src, qlen),
                                    (kp_ref, kp_src, sublen), (vp_ref, vp_src, sublen)):
                    dst[pl.ds(rf * n, n), :] = _rows(src, rf, n, f)
        else:
            q_src, kp_src, k_src, vp_src, v_src, o_dst, l_dst = (
                q_ref, kp_ref, k_ref, vp_ref, v_ref, og, lg)

        def key_rows(prev_src, src, c, rf, r2):
            if c == 0:
                return jnp.concatenate([_rows(prev_src, rf * sublen + r2, BLK, d2),
                                        _rows(src, rf * qlen + r2, BLK, d2)], axis=0)
            return _rows(src, rf * qlen + (c - 1) * sublen + r2, 2 * BLK, d2)

        units, starts = [], []
        for idx in range(ATTN_SPAN // BLK):
            c, r = divmod(idx, d)
            rf, r2 = r % f, r // f
            qstart = rf * qlen + c * sublen + r2
            units.append((_rows(q_src, qstart, BLK, d2), key_rows(kp_src, k_src, c, rf, r2),
                          key_rows(vp_src, v_src, c, rf, r2), bias_ref[1 if c == 0 else 0]))
            starts.append(qstart)
        for qstart, (o, lse) in zip(starts, _attn_batch(units)):
            rows = pl.ds(qstart, BLK) if d2 == 1 else pl.ds(qstart, BLK, stride=d2)
            o_dst[rows, :] = o
            l_dst[rows, :] = jnp.broadcast_to(lse, (BLK, HEAD_DIM))
        if f > 1:
            for rf in range(f):
                og[pl.ds(rf, qlen, stride=f), :] = o_dst[pl.ds(rf * qlen, qlen), :]
                lg[pl.ds(rf, qlen, stride=f), :] = l_dst[pl.ds(rf * qlen, qlen), :]

    lses = [l[...] for l in l_scr]
    mx = functools.reduce(jnp.maximum, lses)
    ws = [jnp.exp(l - mx) for l in lses]
    num = functools.reduce(lambda a, b: a + b, [w * o[...] for w, o in zip(ws, o_scr)])
    den = functools.reduce(lambda a, b: a + b, ws)
    o_ref[...] = (num * (1.0 / den)).astype(o_ref.dtype)


def _dilated_attention(qk, vm):
    s = qk.shape[0]
    assert s % ATTN_SPAN == 0
    hpg = HEADS_PER_GROUP
    in_specs, args = [], []
    for g, (_, d) in enumerate(ATTN_GROUPS):
        sub = d * BLK
        ratio = ATTN_SPAN // sub
        qc, kc, vc = g * hpg, ATTN_HEADS + g * hpg, g * hpg
        cur = lambda col: pl.BlockSpec((ATTN_SPAN, HEAD_DIM), lambda i, h, col=col: (i, col + h))
        prev = lambda col, ratio=ratio, sub=sub: pl.BlockSpec(
            (sub, HEAD_DIM), lambda i, h, col=col, ratio=ratio: (jnp.maximum(i * ratio - 1, 0), col + h))
        in_specs += [cur(qc), cur(kc), cur(vc), prev(kc), prev(vc)]
        args += [qk, qk, vm, qk, vm]
    ngrp = len(ATTN_GROUPS)
    return pl.pallas_call(
        _attn_kernel,
        grid=(s // ATTN_SPAN, hpg),
        in_specs=in_specs,
        out_specs=pl.BlockSpec((ATTN_SPAN, HEAD_DIM), lambda i, h: (i, h)),
        out_shape=jax.ShapeDtypeStruct((s, ATTN_OUT_W), BF16),
        scratch_shapes=([pltpu.VMEM((ATTN_SPAN, HEAD_DIM), F32)] * (2 * ngrp)
                        + [pltpu.VMEM((ATTN_SPAN, HEAD_DIM), F32)] * 7
                        + [pltpu.VMEM((2, BLK, 2 * BLK), F32)]),
        compiler_params=_params("parallel", "arbitrary"),
        name="dilated_attention",
    )(*args)


def _ffn_down_rows_kernel(a_ref, w_ref, x_ref, *rest, final_norm):
    if final_norm:
        g_ref, o_ref = rest
    else:
        (o_ref,) = rest
    width = o_ref.shape[1]
    ssq = jnp.zeros((o_ref.shape[0], 1), F32)
    for c in range(width // V7X_MXU_COLS):
        cols = slice(c * V7X_MXU_COLS, (c + 1) * V7X_MXU_COLS)
        y = x_ref[:, cols] + jnp.dot(a_ref[...], w_ref[:, cols], preferred_element_type=F32)
        o_ref[:, cols] = y
        if final_norm:
            ssq = ssq + jnp.sum(y * y, axis=-1, keepdims=True)
    if final_norm:
        scale = lax.rsqrt(ssq * (1.0 / width) + NORM_EPS)
        for c in range(width // V7X_MXU_COLS):
            cols = slice(c * V7X_MXU_COLS, (c + 1) * V7X_MXU_COLS)
            o_ref[:, cols] = o_ref[:, cols] * scale * g_ref[:, cols]


def _ffn_down_rows(a, w_bf, x, final_g=None):
    s, f = a.shape
    width = x.shape[1]
    tm = ROW_STEP
    assert w_bf.shape == (f, width) and s % tm == 0
    row = lambda cols: pl.BlockSpec((tm, cols), lambda i: (i, 0))
    in_specs = [row(f), _resident_spec(w_bf.shape), row(width)]
    args = [a, w_bf, x]
    if final_g is not None:
        in_specs.append(_gain_spec(0, width))
        args.append(final_g.reshape(1, 1, width))
    return pl.pallas_call(
        functools.partial(_ffn_down_rows_kernel, final_norm=final_g is not None),
        grid=(s // tm,),
        in_specs=in_specs,
        out_specs=row(width),
        out_shape=jax.ShapeDtypeStruct((s, width), F32),
        compiler_params=_params("parallel"),
        name="ffn_down_rows",
    )(*args)


def _ffn_gate_up_kernel(h_ref, wg_ref, wu_ref, *rest, n_jobs):
    o_ref = rest[n_jobs]
    _run_cast_jobs(rest[:n_jobs], rest[n_jobs + 1:])
    for c in range(o_ref.shape[1] // V7X_MXU_COLS):
        cols = slice(c * V7X_MXU_COLS, (c + 1) * V7X_MXU_COLS)
        gate = jnp.dot(h_ref[...], wg_ref[:, cols].astype(BF16), preferred_element_type=F32)
        up = jnp.dot(h_ref[...], wu_ref[:, cols].astype(BF16), preferred_element_type=F32)
        o_ref[:, cols] = (gate * (1.0 / (1.0 + jnp.exp(-gate))) * up).astype(o_ref.dtype)


def _ffn_gate_up(h, w_gate, w_up, layer, cast_weights=()):
    s, k = h.shape
    f = w_gate.shape[2]
    tm, tf = GATE_UP_ROWS, GATE_UP_COLS
    nj = f // tf
    wspec = pl.BlockSpec((None, k, tf), lambda i, j: (layer, 0, j))
    job_in, job_args, job_out, job_shape = _cast_jobs(cast_weights, (s // tm) * nj,
                                                      lambda i, j: i * nj + j)
    return pl.pallas_call(
        functools.partial(_ffn_gate_up_kernel, n_jobs=len(job_args)),
        grid=(s // tm, nj),
        in_specs=[pl.BlockSpec((tm, k), lambda i, j: (i, 0)), wspec, wspec] + job_in,
        out_specs=[pl.BlockSpec((tm, tf), lambda i, j: (i, j))] + job_out,
        out_shape=[jax.ShapeDtypeStruct((s, f), BF16)] + job_shape,
        compiler_params=_params("arbitrary", "arbitrary"),
        name="ffn_gate_up",
    )(h, w_gate, w_up, *job_args)


def kernel(x, mem, positions, mix_norm, mem_norm, w_mem_kv, ffn_norm, w_gate, w_up, w_down,
           attn_w_in, attn_w_out, sgu_w_in, sgu_ln_g, sgu_ln_b, sgu_w_spatial, sgu_b_spatial,
           sgu_w_out, final_norm):
    b, s, d = x.shape
    assert (b, s, d) == (1, SEQ, D_MODEL) and mem.shape == (1, MEM_LEN, D_MODEL)
    xs = x.reshape(s, d)
    mems = mem.reshape(MEM_LEN, d)

    inv_freq = ROPE_THETA ** (-jnp.arange(ROT_HALF, dtype=F32) / ROT_HALF)
    invf = jnp.tile(jnp.concatenate([inv_freq, inv_freq]), ROPE_PACK)
    pos_col, invf = positions.reshape(s, 1), invf.reshape(1, HEAD_DIM)

    def mixer_weights(layer):
        return (attn_w_in, attn_w_out) if layer % 2 == 0 else (sgu_w_in, sgu_w_out)

    w_in_bf, w_out_bf = _cast_bf16(attn_w_in, 0), None
    kv = _mem_kv(mems, mem_norm, w_mem_kv)
    for i in range(DEPTH):
        j = i // 2
        if i % 2 == 0:
            casts = [(attn_w_out, j)] if w_out_bf is None else []
            qk, vm, *cast = _attn_in_proj(xs, mix_norm, i, w_in_bf, pos_col, invf, casts)
            w_out_bf = cast[0] if cast else w_out_bf
            mix = _dilated_attention(qk, vm)
            mem_out = _memory_attention(vm, ATTN_W // MEM_W, kv, i)
        else:
            mix, q_mem = _sgu_in_proj(xs, mix_norm, i, w_in_bf, sgu_ln_g, sgu_ln_b,
                                      sgu_w_spatial, jnp.swapaxes(sgu_b_spatial, 1, 2), j)
            mem_out = _memory_attention(q_mem, 0, kv, i)
        xs, hn = _out_proj(mix, mem_out, w_out_bf, xs, ffn_norm, i)
        last = i + 1 == DEPTH
        casts = [(w_down, i)] + ([] if last else [(w, (i + 1) // 2) for w in mixer_weights(i + 1)])
        act, w_down_bf, *next_bf = _ffn_gate_up(hn, w_gate, w_up, i, casts)
        w_in_bf, w_out_bf = next_bf if next_bf else (None, None)
        xs = _ffn_down_rows(act, w_down_bf, xs, final_g=final_norm if last else None)
    return xs.reshape(b, s, d)
```

```python
import functools
import math

import jax
import jax.numpy as jnp
from jax import lax
from jax.experimental import pallas as pl
from jax.experimental.pallas import tpu as pltpu

D_MODEL = 2048
SEQ = 8192
DEPTH = 2
MEM_LEN = 256
HEAD_DIM = 128
MEM_HEADS = 4
MEM_W = MEM_HEADS * HEAD_DIM
ATTN_GROUPS = ((128, 1), (512, 4), (2048, 16))
ATTN_HEADS = 12
HEADS_PER_GROUP = 4
ATTN_W = ATTN_HEADS * HEAD_DIM
ATTN_OUT_W = HEADS_PER_GROUP * HEAD_DIM
BLK = 128
SGU_GROUPS = 12
SGU_W = SGU_GROUPS * HEAD_DIM
SGU_CHUNK = 128
ROT_DIM = HEAD_DIM // 4
ROT_HALF = ROT_DIM // 2
ROPE_THETA = 500000.0
NORM_EPS = 1e-6
LN_EPS = 1e-5
NEG_INF = -1e30
ATTN_SCALE = HEAD_DIM ** -0.5

N_BACK = BLK
assert all(w // d == N_BACK for w, d in ATTN_GROUPS)
ATTN_SPAN = max(d for _, d in ATTN_GROUPS) * BLK

V7X_VMEM_LIMIT_BYTES = 56 * 1024 * 1024
V7X_MXU_COLS = 256

ROW_STEP = 512
GATE_UP_ROWS, GATE_UP_COLS = 2048, 512
MEM_ATTN_ROWS = 2048
MEM_KV_COLS = 512
CAST_ROWS = 512

F32 = jnp.float32
BF16 = jnp.bfloat16


def _params(*semantics):
    return pltpu.CompilerParams(dimension_semantics=semantics,
                                vmem_limit_bytes=V7X_VMEM_LIMIT_BYTES)


def _rms_rows(x, g):
    ms = jnp.mean(x * x, axis=-1, keepdims=True)
    return x * lax.rsqrt(ms + NORM_EPS) * g


def _gain_spec(layer, k):
    return pl.BlockSpec((None, 1, k), lambda *_: (layer, 0, 0))


def _gain_arg(g):
    return g.reshape(g.shape[0], 1, g.shape[1])


ROPE_PACK = HEAD_DIM // ROT_DIM


def _rope_tables(pos_ref, invf_ref, c_ref, s1_ref, s2_ref):
    n = pos_ref.shape[0] // ROPE_PACK
    lane = lax.broadcasted_iota(jnp.int32, (n, HEAD_DIM), 1)
    group = lane // ROT_DIM
    pos = jnp.zeros((n, HEAD_DIM), F32)
    for q in range(ROPE_PACK):
        pos = jnp.where(group == q, pos_ref[q * n:(q + 1) * n, :].astype(F32), pos)
    ang = pos * invf_ref[...]
    cos, sin = jnp.cos(ang), jnp.sin(ang)
    for q in range(ROPE_PACK):
        cq = cos if q == 0 else pltpu.roll(cos, HEAD_DIM - q * ROT_DIM, 1)
        sq = sin if q == 0 else pltpu.roll(sin, HEAD_DIM - q * ROT_DIM, 1)
        rows = slice(q * n, (q + 1) * n)
        c_ref[rows, :] = jnp.where(lane < ROT_DIM, cq, 1.0)
        s1_ref[rows, :] = jnp.where(lane < ROT_HALF, 0.0, jnp.where(lane < ROT_DIM, sq, 0.0))
        s2_ref[rows, :] = jnp.where(lane < ROT_HALF, -sq, 0.0)


def _mem_kv_kernel(x_ref, g_ref, w_ref, o_ref):
    h = _rms_rows(x_ref[...], g_ref[...]).astype(BF16)
    o_ref[...] = jnp.dot(h, w_ref[...].astype(BF16), preferred_element_type=F32)


def _mem_kv(x, g, w):
    m, k = x.shape
    layers, _, n = w.shape
    tn = MEM_KV_COLS
    assert n % tn == 0
    return pl.pallas_call(
        _mem_kv_kernel,
        grid=(layers, n // tn),
        in_specs=[pl.BlockSpec((m, k), lambda l, j: (0, 0)),
                  pl.BlockSpec((None, 1, k), lambda l, j: (l, 0, 0)),
                  pl.BlockSpec((None, k, tn), lambda l, j: (l, 0, j))],
        out_specs=pl.BlockSpec((None, m, tn), lambda l, j: (l, 0, j)),
        out_shape=jax.ShapeDtypeStruct((layers, m, n), F32),
        compiler_params=_params("parallel", "parallel"),
        name="mem_kv",
    )(x, _gain_arg(g), w)


def _cast_kernel(w_ref, o_ref):
    o_ref[...] = w_ref[...].astype(BF16)


def _cast_bf16(w, layer):
    _, k, n = w.shape
    tk = CAST_ROWS
    assert k % tk == 0
    return pl.pallas_call(
        _cast_kernel,
        grid=(k // tk,),
        in_specs=[pl.BlockSpec((None, tk, n), lambda i: (layer, i, 0))],
        out_specs=pl.BlockSpec((tk, n), lambda i: (i, 0)),
        out_shape=jax.ShapeDtypeStruct((k, n), BF16),
        compiler_params=_params("parallel"),
        name="cast_bf16",
    )(w)


BF16_SUBLANES = 16


def _cast_jobs(weights, n_steps, step_of):
    in_specs, args, out_specs, out_shape = [], [], [], []
    for w, layer in weights:
        _, k, n = w.shape
        slab = -(-(-(-k // n_steps)) // BF16_SUBLANES) * BF16_SUBLANES
        while k % slab:
            slab += BF16_SUBLANES
        n_slabs = k // slab
        in_specs.append(pl.BlockSpec(
            (None, slab, n), lambda *g, layer=layer, n_slabs=n_slabs:
            (layer, jnp.minimum(step_of(*g), n_slabs - 1), 0)))
        out_specs.append(pl.BlockSpec(
            (slab, n), lambda *g, n_slabs=n_slabs: (jnp.minimum(step_of(*g), n_slabs - 1), 0)))
        args.append(w)
        out_shape.append(jax.ShapeDtypeStruct((k, n), BF16))
    return in_specs, args, out_specs, out_shape


def _run_cast_jobs(in_refs, out_refs):
    for src, dst in zip(in_refs, out_refs):
        dst[...] = src[...].astype(BF16)


def _resident_spec(shape):
    return pl.BlockSpec(shape, lambda *_: (0,) * len(shape), pipeline_mode=pl.Buffered(1))


def _gelu_tanh(x):
    return x * (0.5 * (1.0 + jnp.tanh(math.sqrt(2.0 / math.pi) * (x + 0.044715 * (x * x * x)))))


def _pass(h_ref, w_ref, col):
    return jnp.dot(h_ref[...], w_ref[:, col:col + V7X_MXU_COLS], preferred_element_type=F32)


def _heads(acc):
    return [acc[:, j * HEAD_DIM:(j + 1) * HEAD_DIM] for j in range(acc.shape[1] // HEAD_DIM)]


def _attn_in_proj_kernel(x_ref, g_ref, w_ref, pos_ref, invf_ref, *rest, n_jobs):
    qk_ref, vm_ref = rest[n_jobs:n_jobs + 2]
    h_ref, c_ref, s1_ref, s2_ref = rest[2 * n_jobs + 2:]
    _run_cast_jobs(rest[:n_jobs], rest[n_jobs + 2:2 * n_jobs + 2])
    h_ref[...] = _rms_rows(x_ref[...], g_ref[...]).astype(BF16)
    _rope_tables(pos_ref, invf_ref, c_ref, s1_ref, s2_ref)
    for c in range(2 * ATTN_W // V7X_MXU_COLS):
        for j, t in enumerate(_heads(_pass(h_ref, w_ref, c * V7X_MXU_COLS))):
            lo = c * V7X_MXU_COLS + j * HEAD_DIM
            qk_ref[:, lo:lo + HEAD_DIM] = (t * c_ref[...] + pltpu.roll(t, ROT_HALF, 1) * s1_ref[...]
                                           + pltpu.roll(t, HEAD_DIM - ROT_HALF, 1) * s2_ref[...])
    for c in range((ATTN_W + MEM_W) // V7X_MXU_COLS):
        cols = slice(c * V7X_MXU_COLS, (c + 1) * V7X_MXU_COLS)
        vm_ref[:, cols] = _pass(h_ref, w_ref, 2 * ATTN_W + c * V7X_MXU_COLS)


def _attn_in_proj(x, g, g_layer, w_bf, pos_col, invf, cast_weights=()):
    s, k = x.shape
    tm = ROW_STEP
    assert w_bf.shape == (k, 3 * ATTN_W + MEM_W)
    row = lambda cols: pl.BlockSpec((tm, cols), lambda i: (i, 0))
    job_in, job_args, job_out, job_shape = _cast_jobs(cast_weights, s // tm, lambda i: i)
    return pl.pallas_call(
        functools.partial(_attn_in_proj_kernel, n_jobs=len(job_args)),
        grid=(s // tm,),
        in_specs=[row(k), _gain_spec(g_layer, k), _resident_spec(w_bf.shape), row(1),
                  pl.BlockSpec((1, HEAD_DIM), lambda i: (0, 0))] + job_in,
        out_specs=[row(2 * ATTN_W), row(ATTN_W + MEM_W)] + job_out,
        out_shape=[jax.ShapeDtypeStruct((s, 2 * ATTN_W), F32),
                   jax.ShapeDtypeStruct((s, ATTN_W + MEM_W), F32)] + job_shape,
        scratch_shapes=[pltpu.VMEM((tm, k), BF16)] + [pltpu.VMEM((tm, HEAD_DIM), F32)] * 3,
        compiler_params=_params("arbitrary"),
        name="attn_in_proj",
    )(x, _gain_arg(g), w_bf, pos_col, invf, *job_args)


def _sgu_in_proj_kernel(x_ref, g_ref, w_ref, lng_ref, lnb_ref, ws_ref, bt_ref,
                        mix_ref, qm_ref, h_ref, v_scr, vn_scr, wsb_scr):
    @pl.when(pl.program_id(0) == 0)
    def _():
        t = lax.broadcasted_iota(jnp.int32, (SGU_CHUNK, SGU_CHUNK), 0)
        s = lax.broadcasted_iota(jnp.int32, (SGU_CHUNK, SGU_CHUNK), 1)
        for g in range(SGU_GROUPS):
            wsb_scr[g] = jnp.where(t >= s, ws_ref[g], 0.0).astype(BF16)

    h_ref[...] = _rms_rows(x_ref[...], g_ref[...]).astype(BF16)
    for c in range(SGU_W // V7X_MXU_COLS):
        cols = slice(c * V7X_MXU_COLS, (c + 1) * V7X_MXU_COLS)
        v_scr[:, cols] = _gelu_tanh(_pass(h_ref, w_ref, SGU_W + c * V7X_MXU_COLS))
    for c in range(MEM_W // V7X_MXU_COLS):
        cols = slice(c * V7X_MXU_COLS, (c + 1) * V7X_MXU_COLS)
        qm_ref[:, cols] = _pass(h_ref, w_ref, 2 * SGU_W + c * V7X_MXU_COLS)
    v = v_scr[...]
    mu = jnp.mean(v, axis=-1, keepdims=True)
    vc = v - mu
    var = jnp.mean(vc * vc, axis=-1, keepdims=True)
    vn_scr[...] = (vc * lax.rsqrt(var + LN_EPS) * lng_ref[...] + lnb_ref[...]).astype(BF16)

    bt = bt_ref[...]
    n_chunks = mix_ref.shape[0] // SGU_CHUNK
    for c in range(SGU_W // V7X_MXU_COLS):
        for j, u in enumerate(_heads(_gelu_tanh(_pass(h_ref, w_ref, c * V7X_MXU_COLS)))):
            g = c * (V7X_MXU_COLS // HEAD_DIM) + j
            cols = slice(g * HEAD_DIM, (g + 1) * HEAD_DIM)
            vg = jnp.concatenate([vn_scr[r * SGU_CHUNK:(r + 1) * SGU_CHUNK, cols]
                                  for r in range(n_chunks)], axis=1)
            mixed = jnp.dot(wsb_scr[g], vg, preferred_element_type=F32) + bt[:, g:g + 1]
            for r in range(n_chunks):
                rows = slice(r * SGU_CHUNK, (r + 1) * SGU_CHUNK)
                mix_ref[rows, cols] = (u[rows, :] * mixed[:, r * HEAD_DIM:(r + 1) * HEAD_DIM]
                                       ).astype(mix_ref.dtype)


def _sgu_in_proj(x, g, g_layer, w_bf, ln_g, ln_b, w_spatial, b_spatial_t, layer):
    s, k = x.shape
    tm = ROW_STEP
    assert w_bf.shape == (k, 2 * SGU_W + MEM_W) and tm % SGU_CHUNK == 0
    row = lambda cols: pl.BlockSpec((tm, cols), lambda i: (i, 0))
    return pl.pallas_call(
        _sgu_in_proj_kernel,
        grid=(s // tm,),
        in_specs=[row(k), _gain_spec(g_layer, k), _resident_spec(w_bf.shape),
                  _gain_spec(layer, SGU_W), _gain_spec(layer, SGU_W),
                  pl.BlockSpec((None, SGU_GROUPS, SGU_CHUNK, SGU_CHUNK), lambda i: (layer, 0, 0, 0)),
                  pl.BlockSpec((None, SGU_CHUNK, SGU_GROUPS), lambda i: (layer, 0, 0))],
        out_specs=[row(SGU_W), row(MEM_W)],
        out_shape=[jax.ShapeDtypeStruct((s, SGU_W), BF16), jax.ShapeDtypeStruct((s, MEM_W), F32)],
        scratch_shapes=[pltpu.VMEM((tm, k), BF16), pltpu.VMEM((tm, SGU_W), F32),
                        pltpu.VMEM((tm, SGU_W), BF16),
                        pltpu.VMEM((SGU_GROUPS, SGU_CHUNK, SGU_CHUNK), BF16)],
        compiler_params=_params("arbitrary"),
        name="sgu_in_proj",
    )(x, _gain_arg(g), w_bf, _gain_arg(ln_g), _gain_arg(ln_b), w_spatial, b_spatial_t)


def _mem_attn_kernel(q_ref, kv_ref, o_ref):
    for h in range(MEM_HEADS):
        sl = slice(h * HEAD_DIM, (h + 1) * HEAD_DIM)
        k = kv_ref[:, sl].astype(BF16)
        v = kv_ref[:, MEM_W + h * HEAD_DIM:MEM_W + (h + 1) * HEAD_DIM].astype(BF16)
        s = lax.dot_general(q_ref[:, sl].astype(BF16), k, (((1,), (1,)), ((), ())),
                            preferred_element_type=F32) * ATTN_SCALE
        m = jnp.max(s, axis=-1, keepdims=True)
        p = jnp.exp(s - m)
        l = jnp.sum(p, axis=-1, keepdims=True)
        o = jnp.dot(p.astype(BF16), v, preferred_element_type=F32)
        o_ref[:, sl] = (o * (1.0 / l)).astype(o_ref.dtype)


def _memory_attention(proj, q_col_block, kv, layer):
    s = proj.shape[0]
    tm = MEM_ATTN_ROWS
    return pl.pallas_call(
        _mem_attn_kernel,
        grid=(s // tm,),
        in_specs=[pl.BlockSpec((tm, MEM_W), lambda i: (i, q_col_block)),
                  pl.BlockSpec((None, MEM_LEN, 2 * MEM_W), lambda i: (layer, 0, 0))],
        out_specs=pl.BlockSpec((tm, MEM_W), lambda i: (i, 0)),
        out_shape=jax.ShapeDtypeStruct((s, MEM_W), BF16),
        compiler_params=_params("parallel"),
        name="memory_attention",
    )(proj, kv)


def _out_proj_kernel(a_ref, b_ref, w_ref, x_ref, g_ref, o_ref, hn_ref):
    ka = a_ref.shape[1]
    width = o_ref.shape[1]
    n_chunk = 512
    ssq = jnp.zeros((o_ref.shape[0], 1), F32)
    for c in range(width // n_chunk):
        cols = slice(c * n_chunk, (c + 1) * n_chunk)
        y = (x_ref[:, cols]
             + jnp.dot(a_ref[...], w_ref[:ka, cols], preferred_element_type=F32)
             + jnp.dot(b_ref[...], w_ref[ka:, cols], preferred_element_type=F32))
        o_ref[:, cols] = y
        ssq = ssq + jnp.sum(y * y, axis=-1, keepdims=True)
    scale = lax.rsqrt(ssq * (1.0 / width) + NORM_EPS)
    for c in range(width // n_chunk):
        cols = slice(c * n_chunk, (c + 1) * n_chunk)
        hn_ref[:, cols] = (o_ref[:, cols] * scale * g_ref[:, cols]).astype(BF16)


def _out_proj(a, b, w_bf, x, g, g_layer):
    s, ka = a.shape
    kb = b.shape[1]
    width = x.shape[1]
    tm = ROW_STEP
    assert w_bf.shape == (ka + kb, width)
    row = lambda cols: pl.BlockSpec((tm, cols), lambda i: (i, 0))
    return pl.pallas_call(
        _out_proj_kernel,
        grid=(s // tm,),
        in_specs=[row(ka), row(kb), _resident_spec((ka + kb, width)), row(width),
                  _gain_spec(g_layer, width)],
        out_specs=[row(width), row(width)],
        out_shape=[jax.ShapeDtypeStruct((s, width), F32), jax.ShapeDtypeStruct((s, width), BF16)],
        compiler_params=_params("parallel"),
        name="out_proj",
    )(a, b, w_bf, x, _gain_arg(g))


def _rows(ref, start, size, stride):
    if stride == 1:
        return ref[pl.ds(start, size), :]
    return ref[pl.ds(start, size, stride=stride), :]


V7X_FREE_SUBLANE_STRIDE = 4


def _presplit(d):
    return d // V7X_FREE_SUBLANE_STRIDE if d > V7X_FREE_SUBLANE_STRIDE else 1


def _attn_batch(units):
    scores = [lax.dot_general(q.astype(BF16), kc.astype(BF16), (((1,), (1,)), ((), ())),
                              preferred_element_type=F32) * ATTN_SCALE + bias
              for q, kc, _, bias in units]
    probs = []
    for s in scores:
        m = jnp.max(s, axis=-1, keepdims=True)
        p = jnp.exp(s - m)
        probs.append((p, m, jnp.sum(p, axis=-1, keepdims=True)))
    outs = []
    for (p, m, l), (_, _, vc, _) in zip(probs, units):
        o = jnp.dot(p.astype(BF16), vc.astype(BF16), preferred_element_type=F32)
        outs.append((o * (1.0 / l), m + jnp.log(l)))
    return outs


def _attn_kernel(*refs):
    ngrp = len(ATTN_GROUPS)
    in_refs = refs[:5 * ngrp]
    o_ref = refs[5 * ngrp]
    pos = 5 * ngrp + 1
    o_scr, l_scr = refs[pos:pos + ngrp], refs[pos + ngrp:pos + 2 * ngrp]
    split_refs = refs[pos + 2 * ngrp:pos + 2 * ngrp + 7]
    bias_ref = refs[-1]

    qi = lax.broadcasted_iota(jnp.int32, (BLK, 2 * BLK), 0)
    ki = lax.broadcasted_iota(jnp.int32, (BLK, 2 * BLK), 1)
    lo = jnp.where(pl.program_id(0) == 0, BLK, 0)
    bias_ref[0] = jnp.where((ki >= qi) & (ki <= qi + N_BACK), 0.0, NEG_INF)
    bias_ref[1] = jnp.where((ki >= jnp.maximum(qi, lo)) & (ki <= qi + N_BACK), 0.0, NEG_INF)

    for g, (_, d) in enumerate(ATTN_GROUPS):
        q_ref, k_ref, v_ref, kp_ref, vp_ref = in_refs[5 * g:5 * g + 5]
        og, lg = o_scr[g], l_scr[g]
        sub = d * BLK
        f = _presplit(d)
        d2 = d // f
        qlen, sublen = ATTN_SPAN // f, sub // f
        if f > 1:
            q_src, kp_src, k_src, vp_src, v_src, o_dst, l_dst = split_refs
            for rf in range(f):
                for src, dst, n in ((q_ref, q_src, qlen), (k_ref, k_src, qlen), (v_ref, v_src, qlen),
                                    (kp_ref, kp_src, sublen), (vp_ref, vp_src, sublen)):
                    dst[pl.ds(rf * n, n), :] = _rows(src, rf, n, f)
        else:
            q_src, kp_src, k_src, vp_src, v_src, o_dst, l_dst = (
                q_ref, kp_ref, k_ref, vp_ref, v_ref, og, lg)

        def key_rows(prev_src, src, c, rf, r2):
            if c == 0:
                return jnp.concatenate([_rows(prev_src, rf * sublen + r2, BLK, d2),
                                        _rows(src, rf * qlen + r2, BLK, d2)], axis=0)
            return _rows(src, rf * qlen + (c - 1) * sublen + r2, 2 * BLK, d2)

        units, starts = [], []
        for idx in range(ATTN_SPAN // BLK):
            c, r = divmod(idx, d)
            rf, r2 = r % f, r // f
            qstart = rf * qlen + c * sublen + r2
            units.append((_rows(q_src, qstart, BLK, d2), key_rows(kp_src, k_src, c, rf, r2),
                          key_rows(vp_src, v_src, c, rf, r2), bias_ref[1 if c == 0 else 0]))
            starts.append(qstart)
        for qstart, (o, lse) in zip(starts, _attn_batch(units)):
            rows = pl.ds(qstart, BLK) if d2 == 1 else pl.ds(qstart, BLK, stride=d2)
            o_dst[rows, :] = o
            l_dst[rows, :] = jnp.broadcast_to(lse, (BLK, HEAD_DIM))
        if f > 1:
            for rf in range(f):
                og[pl.ds(rf, qlen, stride=f), :] = o_dst[pl.ds(rf * qlen, qlen), :]
                lg[pl.ds(rf, qlen, stride=f), :] = l_dst[pl.ds(rf * qlen, qlen), :]

    lses = [l[...] for l in l_scr]
    mx = functools.reduce(jnp.maximum, lses)
    ws = [jnp.exp(l - mx) for l in lses]
    num = functools.reduce(lambda a, b: a + b, [w * o[...] for w, o in zip(ws, o_scr)])
    den = functools.reduce(lambda a, b: a + b, ws)
    o_ref[...] = (num * (1.0 / den)).astype(o_ref.dtype)


def _dilated_attention(qk, vm):
    s = qk.shape[0]
    assert s % ATTN_SPAN == 0
    hpg = HEADS_PER_GROUP
    in_specs, args = [], []
    for g, (_, d) in enumerate(ATTN_GROUPS):
        sub = d * BLK
        ratio = ATTN_SPAN // sub
        qc, kc, vc = g * hpg, ATTN_HEADS + g * hpg, g * hpg
        cur = lambda col: pl.BlockSpec((ATTN_SPAN, HEAD_DIM), lambda i, h, col=col: (i, col + h))
        prev = lambda col, ratio=ratio, sub=sub: pl.BlockSpec(
            (sub, HEAD_DIM), lambda i, h, col=col, ratio=ratio: (jnp.maximum(i * ratio - 1, 0), col + h))
        in_specs += [cur(qc), cur(kc), cur(vc), prev(kc), prev(vc)]
        args += [qk, qk, vm, qk, vm]
    ngrp = len(ATTN_GROUPS)
    return pl.pallas_call(
        _attn_kernel,
        grid=(s // ATTN_SPAN, hpg),
        in_specs=in_specs,
        out_specs=pl.BlockSpec((ATTN_SPAN, HEAD_DIM), lambda i, h: (i, h)),
        out_shape=jax.ShapeDtypeStruct((s, ATTN_OUT_W), BF16),
        scratch_shapes=([pltpu.VMEM((ATTN_SPAN, HEAD_DIM), F32)] * (2 * ngrp)
                        + [pltpu.VMEM((ATTN_SPAN, HEAD_DIM), F32)] * 7
                        + [pltpu.VMEM((2, BLK, 2 * BLK), F32)]),
        compiler_params=_params("parallel", "arbitrary"),
        name="dilated_attention",
    )(*args)


def _ffn_down_rows_kernel(a_ref, w_ref, x_ref, *rest, final_norm):
    if final_norm:
        g_ref, o_ref = rest
    else:
        (o_ref,) = rest
    width = o_ref.shape[1]
    ssq = jnp.zeros((o_ref.shape[0], 1), F32)
    for c in range(width // V7X_MXU_COLS):
        cols = slice(c * V7X_MXU_COLS, (c + 1) * V7X_MXU_COLS)
        y = x_ref[:, cols] + jnp.dot(a_ref[...], w_ref[:, cols], preferred_element_type=F32)
        o_ref[:, cols] = y
        if final_norm:
            ssq = ssq + jnp.sum(y * y, axis=-1, keepdims=True)
    if final_norm:
        scale = lax.rsqrt(ssq * (1.0 / width) + NORM_EPS)
        for c in range(width // V7X_MXU_COLS):
            cols = slice(c * V7X_MXU_COLS, (c + 1) * V7X_MXU_COLS)
            o_ref[:, cols] = o_ref[:, cols] * scale * g_ref[:, cols]


def _ffn_down_rows(a, w_bf, x, final_g=None):
    s, f = a.shape
    width = x.shape[1]
    tm = ROW_STEP
    assert w_bf.shape == (f, width) and s % tm == 0
    row = lambda cols: pl.BlockSpec((tm, cols), lambda i: (i, 0))
    in_specs = [row(f), _resident_spec(w_bf.shape), row(width)]
    args = [a, w_bf, x]
    if final_g is not None:
        in_specs.append(_gain_spec(0, width))
        args.append(final_g.reshape(1, 1, width))
    return pl.pallas_call(
        functools.partial(_ffn_down_rows_kernel, final_norm=final_g is not None),
        grid=(s // tm,),
        in_specs=in_specs,
        out_specs=row(width),
        out_shape=jax.ShapeDtypeStruct((s, width), F32),
        compiler_params=_params("parallel"),
        name="ffn_down_rows",
    )(*args)


def _ffn_gate_up_kernel(h_ref, wg_ref, wu_ref, *rest, n_jobs):
    o_ref = rest[n_jobs]
    _run_cast_jobs(rest[:n_jobs], rest[n_jobs + 1:])
    for c in range(o_ref.shape[1] // V7X_MXU_COLS):
        cols = slice(c * V7X_MXU_COLS, (c + 1) * V7X_MXU_COLS)
        gate = jnp.dot(h_ref[...], wg_ref[:, cols].astype(BF16), preferred_element_type=F32)
        up = jnp.dot(h_ref[...], wu_ref[:, cols].astype(BF16), preferred_element_type=F32)
        o_ref[:, cols] = (gate * (1.0 / (1.0 + jnp.exp(-gate))) * up).astype(o_ref.dtype)


def _ffn_gate_up(h, w_gate, w_up, layer, cast_weights=()):
    s, k = h.shape
    f = w_gate.shape[2]
    tm, tf = GATE_UP_ROWS, GATE_UP_COLS
    nj = f // tf
    wspec = pl.BlockSpec((None, k, tf), lambda i, j: (layer, 0, j))
    job_in, job_args, job_out, job_shape = _cast_jobs(cast_weights, (s // tm) * nj,
                                                      lambda i, j: i * nj + j)
    return pl.pallas_call(
        functools.partial(_ffn_gate_up_kernel, n_jobs=len(job_args)),
        grid=(s // tm, nj),
        in_specs=[pl.BlockSpec((tm, k), lambda i, j: (i, 0)), wspec, wspec] + job_in,
        out_specs=[pl.BlockSpec((tm, tf), lambda i, j: (i, j))] + job_out,
        out_shape=[jax.ShapeDtypeStruct((s, f), BF16)] + job_shape,
        compiler_params=_params("arbitrary", "arbitrary"),
        name="ffn_gate_up",
    )(h, w_gate, w_up, *job_args)


def kernel(x, mem, positions, mix_norm, mem_norm, w_mem_kv, ffn_norm, w_gate, w_up, w_down,
           attn_w_in, attn_w_out, sgu_w_in, sgu_ln_g, sgu_ln_b, sgu_w_spatial, sgu_b_spatial,
           sgu_w_out, final_norm):
    b, s, d = x.shape
    assert (b, s, d) == (1, SEQ, D_MODEL) and mem.shape == (1, MEM_LEN, D_MODEL)
    xs = x.reshape(s, d)
    mems = mem.reshape(MEM_LEN, d)

    inv_freq = ROPE_THETA ** (-jnp.arange(ROT_HALF, dtype=F32) / ROT_HALF)
    invf = jnp.tile(jnp.concatenate([inv_freq, inv_freq]), ROPE_PACK)
    pos_col, invf = positions.reshape(s, 1), invf.reshape(1, HEAD_DIM)

    def mixer_weights(layer):
        return (attn_w_in, attn_w_out) if layer % 2 == 0 else (sgu_w_in, sgu_w_out)

    w_in_bf, w_out_bf = _cast_bf16(attn_w_in, 0), None
    kv = _mem_kv(mems, mem_norm, w_mem_kv)
    for i in range(DEPTH):
        j = i // 2
        if i % 2 == 0:
            casts = [(attn_w_out, j)] if w_out_bf is None else []
            qk, vm, *cast = _attn_in_proj(xs, mix_norm, i, w_in_bf, pos_col, invf, casts)
            w_out_bf = cast[0] if cast else w_out_bf
            mix = _dilated_attention(qk, vm)
            mem_out = _memory_attention(vm, ATTN_W // MEM_W, kv, i)
        else:
            mix, q_mem = _sgu_in_proj(xs, mix_norm, i, w_in_bf, sgu_ln_g, sgu_ln_b,
                                      sgu_w_spatial, jnp.swapaxes(sgu_b_spatial, 1, 2), j)
            mem_out = _memory_attention(q_mem, 0, kv, i)
        xs, hn = _out_proj(mix, mem_out, w_out_bf, xs, ffn_norm, i)
        last = i + 1 == DEPTH
        casts = [(w_down, i)] + ([] if last else [(w, (i + 1) // 2) for w in mixer_weights(i + 1)])
        act, w_down_bf, *next_bf = _ffn_gate_up(hn, w_gate, w_up, i, casts)
        w_in_bf, w_out_bf = next_bf if next_bf else (None, None)
        xs = _ffn_down_rows(act, w_down_bf, xs, final_g=final_norm if last else None)
    return xs.reshape(b, s, d)
```

```python
import functools
import math

import jax
import jax.numpy as jnp
from jax import lax
from jax.experimental import pallas as pl
from jax.experimental.pallas import tpu as pltpu

D_MODEL = 2048
SEQ = 8192
DEPTH = 2
MEM_LEN = 256
HEAD_DIM = 128
MEM_HEADS = 4
MEM_W = MEM_HEADS * HEAD_DIM
ATTN_GROUPS = ((128, 1), (512, 4), (2048, 16))
ATTN_HEADS = 12
HEADS_PER_GROUP = 4
ATTN_W = ATTN_HEADS * HEAD_DIM
ATTN_OUT_W = HEADS_PER_GROUP * HEAD_DIM
BLK = 128
SGU_GROUPS = 12
SGU_W = SGU_GROUPS * HEAD_DIM
SGU_CHUNK = 128
ROT_DIM = HEAD_DIM // 4
ROT_HALF = ROT_DIM // 2
ROPE_THETA = 500000.0
NORM_EPS = 1e-6
LN_EPS = 1e-5
NEG_INF = -1e30
ATTN_SCALE = HEAD_DIM ** -0.5

N_BACK = BLK
assert all(w // d == N_BACK for w, d in ATTN_GROUPS)
ATTN_SPAN = max(d for _, d in ATTN_GROUPS) * BLK

V7X_VMEM_LIMIT_BYTES = 56 * 1024 * 1024
V7X_MXU_COLS = 256

ROW_STEP = 512
GATE_UP_ROWS, GATE_UP_COLS = 2048, 512
MEM_ATTN_ROWS = 2048
MEM_KV_COLS = 512

F32 = jnp.float32
BF16 = jnp.bfloat16


def _params(*semantics):
    return pltpu.CompilerParams(dimension_semantics=semantics,
                                vmem_limit_bytes=V7X_VMEM_LIMIT_BYTES)


def _rms_rows(x, g):
    ms = jnp.mean(x * x, axis=-1, keepdims=True)
    return x * lax.rsqrt(ms + NORM_EPS) * g


def _gain_spec(layer, k):
    return pl.BlockSpec((None, 1, k), lambda *_: (layer, 0, 0))


def _gain_arg(g):
    return g.reshape(g.shape[0], 1, g.shape[1])


ROPE_PACK = HEAD_DIM // ROT_DIM


def _rope_tables(pos_ref, invf_ref, c_ref, s1_ref, s2_ref):
    n = pos_ref.shape[0] // ROPE_PACK
    lane = lax.broadcasted_iota(jnp.int32, (n, HEAD_DIM), 1)
    group = lane // ROT_DIM
    pos = jnp.zeros((n, HEAD_DIM), F32)
    for q in range(ROPE_PACK):
        pos = jnp.where(group == q, pos_ref[q * n:(q + 1) * n, :].astype(F32), pos)
    ang = pos * invf_ref[...]
    cos, sin = jnp.cos(ang), jnp.sin(ang)
    for q in range(ROPE_PACK):
        cq = cos if q == 0 else pltpu.roll(cos, HEAD_DIM - q * ROT_DIM, 1)
        sq = sin if q == 0 else pltpu.roll(sin, HEAD_DIM - q * ROT_DIM, 1)
        rows = slice(q * n, (q + 1) * n)
        c_ref[rows, :] = jnp.where(lane < ROT_DIM, cq, 1.0)
        s1_ref[rows, :] = jnp.where(lane < ROT_HALF, 0.0, jnp.where(lane < ROT_DIM, sq, 0.0))
        s2_ref[rows, :] = jnp.where(lane < ROT_HALF, -sq, 0.0)


BF16_SUBLANES = 16


def _cast_jobs(weights, n_steps, step_of):
    in_specs, args, out_specs, out_shape = [], [], [], []
    for w, layer in weights:
        _, k, n = w.shape
        slab = -(-(-(-k // n_steps)) // BF16_SUBLANES) * BF16_SUBLANES
        while k % slab:
            slab += BF16_SUBLANES
        n_slabs = k // slab
        in_specs.append(pl.BlockSpec(
            (None, slab, n), lambda *g, layer=layer, n_slabs=n_slabs:
            (layer, jnp.minimum(step_of(*g), n_slabs - 1), 0)))
        out_specs.append(pl.BlockSpec(
            (slab, n), lambda *g, n_slabs=n_slabs: (jnp.minimum(step_of(*g), n_slabs - 1), 0)))
        args.append(w)
        out_shape.append(jax.ShapeDtypeStruct((k, n), BF16))
    return in_specs, args, out_specs, out_shape


def _run_cast_jobs(in_refs, out_refs):
    for src, dst in zip(in_refs, out_refs):
        dst[...] = src[...].astype(BF16)


def _mem_kv_kernel(x_ref, g_ref, w_ref, *rest, n_jobs):
    o_ref = rest[n_jobs]
    _run_cast_jobs(rest[:n_jobs], rest[n_jobs + 1:])
    h = _rms_rows(x_ref[...], g_ref[...]).astype(BF16)
    o_ref[...] = jnp.dot(h, w_ref[...].astype(BF16), preferred_element_type=F32)


def _mem_kv(x, g, w, cast_weights=()):
    m, k = x.shape
    layers, _, n = w.shape
    tn = MEM_KV_COLS
    assert n % tn == 0
    nt = n // tn
    job_in, job_args, job_out, job_shape = _cast_jobs(cast_weights, layers * nt, lambda l, j: l * nt + j)
    return pl.pallas_call(
        functools.partial(_mem_kv_kernel, n_jobs=len(job_args)),
        grid=(layers, nt),
        in_specs=[pl.BlockSpec((m, k), lambda l, j: (0, 0)),
                  pl.BlockSpec((None, 1, k), lambda l, j: (l, 0, 0)),
                  pl.BlockSpec((None, k, tn), lambda l, j: (l, 0, j))] + job_in,
        out_specs=[pl.BlockSpec((None, m, tn), lambda l, j: (l, 0, j))] + job_out,
        out_shape=[jax.ShapeDtypeStruct((layers, m, n), F32)] + job_shape,
        compiler_params=_params("arbitrary", "arbitrary"),
        name="mem_kv",
    )(x, _gain_arg(g), w, *job_args)


def _resident_spec(shape):
    return pl.BlockSpec(shape, lambda *_: (0,) * len(shape), pipeline_mode=pl.Buffered(1))


def _gelu_tanh(x):
    return x * (0.5 * (1.0 + jnp.tanh(math.sqrt(2.0 / math.pi) * (x + 0.044715 * (x * x * x)))))


def _pass(h_ref, w_ref, col):
    return jnp.dot(h_ref[...], w_ref[:, col:col + V7X_MXU_COLS], preferred_element_type=F32)


def _heads(acc):
    return [acc[:, j * HEAD_DIM:(j + 1) * HEAD_DIM] for j in range(acc.shape[1] // HEAD_DIM)]


def _attn_in_proj_kernel(x_ref, g_ref, w_ref, pos_ref, invf_ref, *rest, n_jobs):
    qk_ref, vm_ref = rest[n_jobs:n_jobs + 2]
    h_ref, c_ref, s1_ref, s2_ref = rest[2 * n_jobs + 2:]
    _run_cast_jobs(rest[:n_jobs], rest[n_jobs + 2:2 * n_jobs + 2])
    h_ref[...] = _rms_rows(x_ref[...], g_ref[...]).astype(BF16)
    _rope_tables(pos_ref, invf_ref, c_ref, s1_ref, s2_ref)
    for c in range(2 * ATTN_W // V7X_MXU_COLS):
        for j, t in enumerate(_heads(_pass(h_ref, w_ref, c * V7X_MXU_COLS))):
            lo = c * V7X_MXU_COLS + j * HEAD_DIM
            qk_ref[:, lo:lo + HEAD_DIM] = (t * c_ref[...] + pltpu.roll(t, ROT_HALF, 1) * s1_ref[...]
                                           + pltpu.roll(t, HEAD_DIM - ROT_HALF, 1) * s2_ref[...])
    for c in range((ATTN_W + MEM_W) // V7X_MXU_COLS):
        cols = slice(c * V7X_MXU_COLS, (c + 1) * V7X_MXU_COLS)
        vm_ref[:, cols] = _pass(h_ref, w_ref, 2 * ATTN_W + c * V7X_MXU_COLS)


def _attn_in_proj(x, g, g_layer, w_bf, pos_col, invf, cast_weights=()):
    s, k = x.shape
    tm = ROW_STEP
    assert w_bf.shape == (k, 3 * ATTN_W + MEM_W)
    row = lambda cols: pl.BlockSpec((tm, cols), lambda i: (i, 0))
    job_in, job_args, job_out, job_shape = _cast_jobs(cast_weights, s // tm, lambda i: i)
    return pl.pallas_call(
        functools.partial(_attn_in_proj_kernel, n_jobs=len(job_args)),
        grid=(s // tm,),
        in_specs=[row(k), _gain_spec(g_layer, k), _resident_spec(w_bf.shape), row(1),
                  pl.BlockSpec((1, HEAD_DIM), lambda i: (0, 0))] + job_in,
        out_specs=[row(2 * ATTN_W), row(ATTN_W + MEM_W)] + job_out,
        out_shape=[jax.ShapeDtypeStruct((s, 2 * ATTN_W), F32),
                   jax.ShapeDtypeStruct((s, ATTN_W + MEM_W), F32)] + job_shape,
        scratch_shapes=[pltpu.VMEM((tm, k), BF16)] + [pltpu.VMEM((tm, HEAD_DIM), F32)] * 3,
        compiler_params=_params("arbitrary"),
        name="attn_in_proj",
    )(x, _gain_arg(g), w_bf, pos_col, invf, *job_args)


def _sgu_in_proj_kernel(x_ref, g_ref, w_ref, lng_ref, lnb_ref, ws_ref, bt_ref,
                        mix_ref, qm_ref, h_ref, v_scr, vn_scr, wsb_scr):
    @pl.when(pl.program_id(0) == 0)
    def _():
        t = lax.broadcasted_iota(jnp.int32, (SGU_CHUNK, SGU_CHUNK), 0)
        s = lax.broadcasted_iota(jnp.int32, (SGU_CHUNK, SGU_CHUNK), 1)
        for g in range(SGU_GROUPS):
            wsb_scr[g] = jnp.where(t >= s, ws_ref[g], 0.0).astype(BF16)

    h_ref[...] = _rms_rows(x_ref[...], g_ref[...]).astype(BF16)
    for c in range(SGU_W // V7X_MXU_COLS):
        cols = slice(c * V7X_MXU_COLS, (c + 1) * V7X_MXU_COLS)
        v_scr[:, cols] = _gelu_tanh(_pass(h_ref, w_ref, SGU_W + c * V7X_MXU_COLS))
    for c in range(MEM_W // V7X_MXU_COLS):
        cols = slice(c * V7X_MXU_COLS, (c + 1) * V7X_MXU_COLS)
        qm_ref[:, cols] = _pass(h_ref, w_ref, 2 * SGU_W + c * V7X_MXU_COLS)
    v = v_scr[...]
    mu = jnp.mean(v, axis=-1, keepdims=True)
    vc = v - mu
    var = jnp.mean(vc * vc, axis=-1, keepdims=True)
    vn_scr[...] = (vc * lax.rsqrt(var + LN_EPS) * lng_ref[...] + lnb_ref[...]).astype(BF16)

    bt = bt_ref[...]
    n_chunks = mix_ref.shape[0] // SGU_CHUNK
    for c in range(SGU_W // V7X_MXU_COLS):
        for j, u in enumerate(_heads(_gelu_tanh(_pass(h_ref, w_ref, c * V7X_MXU_COLS)))):
            g = c * (V7X_MXU_COLS // HEAD_DIM) + j
            cols = slice(g * HEAD_DIM, (g + 1) * HEAD_DIM)
            vg = jnp.concatenate([vn_scr[r * SGU_CHUNK:(r + 1) * SGU_CHUNK, cols]
                                  for r in range(n_chunks)], axis=1)
            mixed = jnp.dot(wsb_scr[g], vg, preferred_element_type=F32) + bt[:, g:g + 1]
            for r in range(n_chunks):
                rows = slice(r * SGU_CHUNK, (r + 1) * SGU_CHUNK)
                mix_ref[rows, cols] = (u[rows, :] * mixed[:, r * HEAD_DIM:(r + 1) * HEAD_DIM]
                                       ).astype(mix_ref.dtype)


def _sgu_in_proj(x, g, g_layer, w_bf, ln_g, ln_b, w_spatial, b_spatial_t, layer):
    s, k = x.shape
    tm = ROW_STEP
    assert w_bf.shape == (k, 2 * SGU_W + MEM_W) and tm % SGU_CHUNK == 0
    row = lambda cols: pl.BlockSpec((tm, cols), lambda i: (i, 0))
    return pl.pallas_call(
        _sgu_in_proj_kernel,
        grid=(s // tm,),
        in_specs=[row(k), _gain_spec(g_layer, k), _resident_spec(w_bf.shape),
                  _gain_spec(layer, SGU_W), _gain_spec(layer, SGU_W),
                  pl.BlockSpec((None, SGU_GROUPS, SGU_CHUNK, SGU_CHUNK), lambda i: (layer, 0, 0, 0)),
                  pl.BlockSpec((None, SGU_CHUNK, SGU_GROUPS), lambda i: (layer, 0, 0))],
        out_specs=[row(SGU_W), row(MEM_W)],
        out_shape=[jax.ShapeDtypeStruct((s, SGU_W), BF16), jax.ShapeDtypeStruct((s, MEM_W), F32)],
        scratch_shapes=[pltpu.VMEM((tm, k), BF16), pltpu.VMEM((tm, SGU_W), F32),
                        pltpu.VMEM((tm, SGU_W), BF16),
                        pltpu.VMEM((SGU_GROUPS, SGU_CHUNK, SGU_CHUNK), BF16)],
        compiler_params=_params("arbitrary"),
        name="sgu_in_proj",
    )(x, _gain_arg(g), w_bf, _gain_arg(ln_g), _gain_arg(ln_b), w_spatial, b_spatial_t)


def _mem_attn_kernel(q_ref, kv_ref, o_ref):
    for h in range(MEM_HEADS):
        sl = slice(h * HEAD_DIM, (h + 1) * HEAD_DIM)
        k = kv_ref[:, sl].astype(BF16)
        v = kv_ref[:, MEM_W + h * HEAD_DIM:MEM_W + (h + 1) * HEAD_DIM].astype(BF16)
        s = lax.dot_general(q_ref[:, sl].astype(BF16), k, (((1,), (1,)), ((), ())),
                            preferred_element_type=F32) * ATTN_SCALE
        m = jnp.max(s, axis=-1, keepdims=True)
        p = jnp.exp(s - m)
        l = jnp.sum(p, axis=-1, keepdims=True)
        o = jnp.dot(p.astype(BF16), v, preferred_element_type=F32)
        o_ref[:, sl] = (o * (1.0 / l)).astype(o_ref.dtype)


def _memory_attention(proj, q_col_block, kv, layer):
    s = proj.shape[0]
    tm = MEM_ATTN_ROWS
    return pl.pallas_call(
        _mem_attn_kernel,
        grid=(s // tm,),
        in_specs=[pl.BlockSpec((tm, MEM_W), lambda i: (i, q_col_block)),
                  pl.BlockSpec((None, MEM_LEN, 2 * MEM_W), lambda i: (layer, 0, 0))],
        out_specs=pl.BlockSpec((tm, MEM_W), lambda i: (i, 0)),
        out_shape=jax.ShapeDtypeStruct((s, MEM_W), BF16),
        compiler_params=_params("parallel"),
        name="memory_attention",
    )(proj, kv)


def _out_proj_kernel(a_ref, b_ref, w_ref, x_ref, g_ref, o_ref, hn_ref):
    ka = a_ref.shape[1]
    width = o_ref.shape[1]
    n_chunk = 512
    ssq = jnp.zeros((o_ref.shape[0], 1), F32)
    for c in range(width // n_chunk):
        cols = slice(c * n_chunk, (c + 1) * n_chunk)
        y = (x_ref[:, cols]
             + jnp.dot(a_ref[...], w_ref[:ka, cols], preferred_element_type=F32)
             + jnp.dot(b_ref[...], w_ref[ka:, cols], preferred_element_type=F32))
        o_ref[:, cols] = y
        ssq = ssq + jnp.sum(y * y, axis=-1, keepdims=True)
    scale = lax.rsqrt(ssq * (1.0 / width) + NORM_EPS)
    for c in range(width // n_chunk):
        cols = slice(c * n_chunk, (c + 1) * n_chunk)
        hn_ref[:, cols] = (o_ref[:, cols] * scale * g_ref[:, cols]).astype(BF16)


def _out_proj(a, b, w_bf, x, g, g_layer):
    s, ka = a.shape
    kb = b.shape[1]
    width = x.shape[1]
    tm = ROW_STEP
    assert w_bf.shape == (ka + kb, width)
    row = lambda cols: pl.BlockSpec((tm, cols), lambda i: (i, 0))
    return pl.pallas_call(
        _out_proj_kernel,
        grid=(s // tm,),
        in_specs=[row(ka), row(kb), _resident_spec((ka + kb, width)), row(width),
                  _gain_spec(g_layer, width)],
        out_specs=[row(width), row(width)],
        out_shape=[jax.ShapeDtypeStruct((s, width), F32), jax.ShapeDtypeStruct((s, width), BF16)],
        compiler_params=_params("parallel"),
        name="out_proj",
    )(a, b, w_bf, x, _gain_arg(g))


def _rows(ref, start, size, stride):
    if stride == 1:
        return ref[pl.ds(start, size), :]
    return ref[pl.ds(start, size, stride=stride), :]


V7X_FREE_SUBLANE_STRIDE = 4


def _presplit(d):
    return d // V7X_FREE_SUBLANE_STRIDE if d > V7X_FREE_SUBLANE_STRIDE else 1


def _attn_batch(units):
    scores = [lax.dot_general(q.astype(BF16), kc.astype(BF16), (((1,), (1,)), ((), ())),
                              preferred_element_type=F32) * ATTN_SCALE + bias
              for q, kc, _, bias in units]
    probs = []
    for s in scores:
        m = jnp.max(s, axis=-1, keepdims=True)
        p = jnp.exp(s - m)
        probs.append((p, m, jnp.sum(p, axis=-1, keepdims=True)))
    outs = []
    for (p, m, l), (_, _, vc, _) in zip(probs, units):
        o = jnp.dot(p.astype(BF16), vc.astype(BF16), preferred_element_type=F32)
        outs.append((o * (1.0 / l), m + jnp.log(l)))
    return outs


def _attn_kernel(*refs):
    ngrp = len(ATTN_GROUPS)
    in_refs = refs[:5 * ngrp]
    o_ref = refs[5 * ngrp]
    pos = 5 * ngrp + 1
    o_scr, l_scr = refs[pos:pos + ngrp], refs[pos + ngrp:pos + 2 * ngrp]
    split_refs = refs[pos + 2 * ngrp:pos + 2 * ngrp + 7]
    bias_ref = refs[-1]

    qi = lax.broadcasted_iota(jnp.int32, (BLK, 2 * BLK), 0)
    ki = lax.broadcasted_iota(jnp.int32, (BLK, 2 * BLK), 1)
    lo = jnp.where(pl.program_id(0) == 0, BLK, 0)
    bias_ref[0] = jnp.where((ki >= qi) & (ki <= qi + N_BACK), 0.0, NEG_INF)
    bias_ref[1] = jnp.where((ki >= jnp.maximum(qi, lo)) & (ki <= qi + N_BACK), 0.0, NEG_INF)

    for g, (_, d) in enumerate(ATTN_GROUPS):
        q_ref, k_ref, v_ref, kp_ref, vp_ref = in_refs[5 * g:5 * g + 5]
        og, lg = o_scr[g], l_scr[g]
        sub = d * BLK
        f = _presplit(d)
        d2 = d // f
        qlen, sublen = ATTN_SPAN // f, sub // f
        if f > 1:
            q_src, kp_src, k_src, vp_src, v_src, o_dst, l_dst = split_refs
            for rf in range(f):
                for src, dst, n in ((q_ref, q_src, qlen), (k_ref, k_src, qlen), (v_ref, v_src, qlen),
                                    (kp_ref, kp_src, sublen), (vp_ref, vp_src, sublen)):
                    dst[pl.ds(rf * n, n), :] = _rows(src, rf, n, f)
        else:
            q_src, kp_src, k_src, vp_src, v_src, o_dst, l_dst = (
                q_ref, kp_ref, k_ref, vp_ref, v_ref, og, lg)

        def key_rows(prev_src, src, c, rf, r2):
            if c == 0:
                return jnp.concatenate([_rows(prev_src, rf * sublen + r2, BLK, d2),
                                        _rows(src, rf * qlen + r2, BLK, d2)], axis=0)
            return _rows(src, rf * qlen + (c - 1) * sublen + r2, 2 * BLK, d2)

        units, starts = [], []
        for idx in range(ATTN_SPAN // BLK):
            c, r = divmod(idx, d)
            rf, r2 = r % f, r // f
            qstart = rf * qlen + c * sublen + r2
            units.append((_rows(q_src, qstart, BLK, d2), key_rows(kp_src, k_src, c, rf, r2),
                          key_rows(vp_src, v_src, c, rf, r2), bias_ref[1 if c == 0 else 0]))
            starts.append(qstart)
        for qstart, (o, lse) in zip(starts, _attn_batch(units)):
            rows = pl.ds(qstart, BLK) if d2 == 1 else pl.ds(qstart, BLK, stride=d2)
            o_dst[rows, :] = o
            l_dst[rows, :] = jnp.broadcast_to(lse, (BLK, HEAD_DIM))
        if f > 1:
            for rf in range(f):
                og[pl.ds(rf, qlen, stride=f), :] = o_dst[pl.ds(rf * qlen, qlen), :]
                lg[pl.ds(rf, qlen, stride=f), :] = l_dst[pl.ds(rf * qlen, qlen), :]

    lses = [l[...] for l in l_scr]
    mx = functools.reduce(jnp.maximum, lses)
    ws = [jnp.exp(l - mx) for l in lses]
    num = functools.reduce(lambda a, b: a + b, [w * o[...] for w, o in zip(ws, o_scr)])
    den = functools.reduce(lambda a, b: a + b, ws)
    o_ref[...] = (num * (1.0 / den)).astype(o_ref.dtype)


def _dilated_attention(qk, vm):
    s = qk.shape[0]
    assert s % ATTN_SPAN == 0
    hpg = HEADS_PER_GROUP
    in_specs, args = [], []
    for g, (_, d) in enumerate(ATTN_GROUPS):
        sub = d * BLK
        ratio = ATTN_SPAN // sub
        qc, kc, vc = g * hpg, ATTN_HEADS + g * hpg, g * hpg
        cur = lambda col: pl.BlockSpec((ATTN_SPAN, HEAD_DIM), lambda i, h, col=col: (i, col + h))
        prev = lambda col, ratio=ratio, sub=sub: pl.BlockSpec(
            (sub, HEAD_DIM), lambda i, h, col=col, ratio=ratio: (jnp.maximum(i * ratio - 1, 0), col + h))
        in_specs += [cur(qc), cur(kc), cur(vc), prev(kc), prev(vc)]
        args += [qk, qk, vm, qk, vm]
    ngrp = len(ATTN_GROUPS)
    return pl.pallas_call(
        _attn_kernel,
        grid=(s // ATTN_SPAN, hpg),
        in_specs=in_specs,
        out_specs=pl.BlockSpec((ATTN_SPAN, HEAD_DIM), lambda i, h: (i, h)),
        out_shape=jax.ShapeDtypeStruct((s, ATTN_OUT_W), BF16),
        scratch_shapes=([pltpu.VMEM((ATTN_SPAN, HEAD_DIM), F32)] * (2 * ngrp)
                        + [pltpu.VMEM((ATTN_SPAN, HEAD_DIM), F32)] * 7
                        + [pltpu.VMEM((2, BLK, 2 * BLK), F32)]),
        compiler_params=_params("parallel", "arbitrary"),
        name="dilated_attention",
    )(*args)


def _ffn_down_rows_kernel(a_ref, w_ref, x_ref, *rest, final_norm):
    if final_norm:
        g_ref, o_ref = rest
    else:
        (o_ref,) = rest
    width = o_ref.shape[1]
    ssq = jnp.zeros((o_ref.shape[0], 1), F32)
    for c in range(width // V7X_MXU_COLS):
        cols = slice(c * V7X_MXU_COLS, (c + 1) * V7X_MXU_COLS)
        y = x_ref[:, cols] + jnp.dot(a_ref[...], w_ref[:, cols], preferred_element_type=F32)
        o_ref[:, cols] = y
        if final_norm:
            ssq = ssq + jnp.sum(y * y, axis=-1, keepdims=True)
    if final_norm:
        scale = lax.rsqrt(ssq * (1.0 / width) + NORM_EPS)
        for c in range(width // V7X_MXU_COLS):
            cols = slice(c * V7X_MXU_COLS, (c + 1) * V7X_MXU_COLS)
            o_ref[:, cols] = o_ref[:, cols] * scale * g_ref[:, cols]


def _ffn_down_rows(a, w_bf, x, final_g=None):
    s, f = a.shape
    width = x.shape[1]
    tm = ROW_STEP
    assert w_bf.shape == (f, width) and s % tm == 0
    row = lambda cols: pl.BlockSpec((tm, cols), lambda i: (i, 0))
    in_specs = [row(f), _resident_spec(w_bf.shape), row(width)]
    args = [a, w_bf, x]
    if final_g is not None:
        in_specs.append(_gain_spec(0, width))
        args.append(final_g.reshape(1, 1, width))
    return pl.pallas_call(
        functools.partial(_ffn_down_rows_kernel, final_norm=final_g is not None),
        grid=(s // tm,),
        in_specs=in_specs,
        out_specs=row(width),
        out_shape=jax.ShapeDtypeStruct((s, width), F32),
        compiler_params=_params("parallel"),
        name="ffn_down_rows",
    )(*args)


def _ffn_gate_up_kernel(h_ref, wg_ref, wu_ref, *rest, n_jobs):
    o_ref = rest[n_jobs]
    _run_cast_jobs(rest[:n_jobs], rest[n_jobs + 1:])
    for c in range(o_ref.shape[1] // V7X_MXU_COLS):
        cols = slice(c * V7X_MXU_COLS, (c + 1) * V7X_MXU_COLS)
        gate = jnp.dot(h_ref[...], wg_ref[:, cols].astype(BF16), preferred_element_type=F32)
        up = jnp.dot(h_ref[...], wu_ref[:, cols].astype(BF16), preferred_element_type=F32)
        o_ref[:, cols] = (gate * (1.0 / (1.0 + jnp.exp(-gate))) * up).astype(o_ref.dtype)


def _ffn_gate_up(h, w_gate, w_up, layer, cast_weights=()):
    s, k = h.shape
    f = w_gate.shape[2]
    tm, tf = GATE_UP_ROWS, GATE_UP_COLS
    nj = f // tf
    wspec = pl.BlockSpec((None, k, tf), lambda i, j: (layer, 0, j))
    job_in, job_args, job_out, job_shape = _cast_jobs(cast_weights, (s // tm) * nj,
                                                      lambda i, j: i * nj + j)
    return pl.pallas_call(
        functools.partial(_ffn_gate_up_kernel, n_jobs=len(job_args)),
        grid=(s // tm, nj),
        in_specs=[pl.BlockSpec((tm, k), lambda i, j: (i, 0)), wspec, wspec] + job_in,
        out_specs=[pl.BlockSpec((tm, tf), lambda i, j: (i, j))] + job_out,
        out_shape=[jax.ShapeDtypeStruct((s, f), BF16)] + job_shape,
        compiler_params=_params("arbitrary", "arbitrary"),
        name="ffn_gate_up",
    )(h, w_gate, w_up, *job_args)


def kernel(x, mem, positions, mix_norm, mem_norm, w_mem_kv, ffn_norm, w_gate, w_up, w_down,
           attn_w_in, attn_w_out, sgu_w_in, sgu_ln_g, sgu_ln_b, sgu_w_spatial, sgu_b_spatial,
           sgu_w_out, final_norm):
    b, s, d = x.shape
    assert (b, s, d) == (1, SEQ, D_MODEL) and mem.shape == (1, MEM_LEN, D_MODEL)
    xs = x.reshape(s, d)
    mems = mem.reshape(MEM_LEN, d)

    inv_freq = ROPE_THETA ** (-jnp.arange(ROT_HALF, dtype=F32) / ROT_HALF)
    invf = jnp.tile(jnp.concatenate([inv_freq, inv_freq]), ROPE_PACK)
    pos_col, invf = positions.reshape(s, 1), invf.reshape(1, HEAD_DIM)

    def mixer_weights(layer):
        return (attn_w_in, attn_w_out) if layer % 2 == 0 else (sgu_w_in, sgu_w_out)

    kv, w_in_bf = _mem_kv(mems, mem_norm, w_mem_kv, [(attn_w_in, 0)])
    w_out_bf = None
    for i in range(DEPTH):
        j = i // 2
        if i % 2 == 0:
            casts = [(attn_w_out, j)] if w_out_bf is None else []
            qk, vm, *cast = _attn_in_proj(xs, mix_norm, i, w_in_bf, pos_col, invf, casts)
            w_out_bf = cast[0] if cast else w_out_bf
            mix = _dilated_attention(qk, vm)
            mem_out = _memory_attention(vm, ATTN_W // MEM_W, kv, i)
        else:
            mix, q_mem = _sgu_in_proj(xs, mix_norm, i, w_in_bf, sgu_ln_g, sgu_ln_b,
                                      sgu_w_spatial, jnp.swapaxes(sgu_b_spatial, 1, 2), j)
            mem_out = _memory_attention(q_mem, 0, kv, i)
        xs, hn = _out_proj(mix, mem_out, w_out_bf, xs, ffn_norm, i)
        last = i + 1 == DEPTH
        casts = [(w_down, i)] + ([] if last else [(w, (i + 1) // 2) for w in mixer_weights(i + 1)])
        act, w_down_bf, *next_bf = _ffn_gate_up(hn, w_gate, w_up, i, casts)
        w_in_bf, w_out_bf = next_bf if next_bf else (None, None)
        xs = _ffn_down_rows(act, w_down_bf, xs, final_g=final_norm if last else None)
    return xs.reshape(b, s, d)
```

```python
import functools
import math

import jax
import jax.numpy as jnp
from jax import lax
from jax.experimental import pallas as pl
from jax.experimental.pallas import tpu as pltpu

D_MODEL = 2048
SEQ = 8192
DEPTH = 2
MEM_LEN = 256
HEAD_DIM = 128
MEM_HEADS = 4
MEM_W = MEM_HEADS * HEAD_DIM
ATTN_GROUPS = ((128, 1), (512, 4), (2048, 16))
ATTN_HEADS = 12
HEADS_PER_GROUP = 4
ATTN_W = ATTN_HEADS * HEAD_DIM
ATTN_OUT_W = HEADS_PER_GROUP * HEAD_DIM
BLK = 128
SGU_GROUPS = 12
SGU_W = SGU_GROUPS * HEAD_DIM
SGU_CHUNK = 128
ROT_DIM = HEAD_DIM // 4
ROT_HALF = ROT_DIM // 2
ROPE_THETA = 500000.0
NORM_EPS = 1e-6
LN_EPS = 1e-5
NEG_INF = -1e30
ATTN_SCALE = HEAD_DIM ** -0.5

N_BACK = BLK
assert all(w // d == N_BACK for w, d in ATTN_GROUPS)
ATTN_SPAN = max(d for _, d in ATTN_GROUPS) * BLK

V7X_VMEM_LIMIT_BYTES = 56 * 1024 * 1024
V7X_MXU_COLS = 256

ROW_STEP = 512
GATE_UP_ROWS, GATE_UP_COLS = 2048, 512
MEM_KV_COLS = 512

F32 = jnp.float32
BF16 = jnp.bfloat16


def _params(*semantics):
    return pltpu.CompilerParams(dimension_semantics=semantics,
                                vmem_limit_bytes=V7X_VMEM_LIMIT_BYTES)


def _rms_rows(x, g):
    ms = jnp.mean(x * x, axis=-1, keepdims=True)
    return x * lax.rsqrt(ms + NORM_EPS) * g


def _gain_spec(layer, k):
    return pl.BlockSpec((None, 1, k), lambda *_: (layer, 0, 0))


def _gain_arg(g):
    return g.reshape(g.shape[0], 1, g.shape[1])


ROPE_PACK = HEAD_DIM // ROT_DIM


def _rope_tables(pos_ref, invf_ref, c_ref, s1_ref, s2_ref):
    n = pos_ref.shape[0] // ROPE_PACK
    lane = lax.broadcasted_iota(jnp.int32, (n, HEAD_DIM), 1)
    group = lane // ROT_DIM
    pos = jnp.zeros((n, HEAD_DIM), F32)
    for q in range(ROPE_PACK):
        pos = jnp.where(group == q, pos_ref[q * n:(q + 1) * n, :].astype(F32), pos)
    ang = pos * invf_ref[...]
    cos, sin = jnp.cos(ang), jnp.sin(ang)
    for q in range(ROPE_PACK):
        cq = cos if q == 0 else pltpu.roll(cos, HEAD_DIM - q * ROT_DIM, 1)
        sq = sin if q == 0 else pltpu.roll(sin, HEAD_DIM - q * ROT_DIM, 1)
        rows = slice(q * n, (q + 1) * n)
        c_ref[rows, :] = jnp.where(lane < ROT_DIM, cq, 1.0)
        s1_ref[rows, :] = jnp.where(lane < ROT_HALF, 0.0, jnp.where(lane < ROT_DIM, sq, 0.0))
        s2_ref[rows, :] = jnp.where(lane < ROT_HALF, -sq, 0.0)


BF16_SUBLANES = 16


def _cast_jobs(weights, n_steps, step_of):
    in_specs, args, out_specs, out_shape = [], [], [], []
    for w, layer in weights:
        _, k, n = w.shape
        slab = -(-(-(-k // n_steps)) // BF16_SUBLANES) * BF16_SUBLANES
        while k % slab:
            slab += BF16_SUBLANES
        n_slabs = k // slab
        in_specs.append(pl.BlockSpec(
            (None, slab, n), lambda *g, layer=layer, n_slabs=n_slabs:
            (layer, jnp.minimum(step_of(*g), n_slabs - 1), 0)))
        out_specs.append(pl.BlockSpec(
            (slab, n), lambda *g, n_slabs=n_slabs: (jnp.minimum(step_of(*g), n_slabs - 1), 0)))
        args.append(w)
        out_shape.append(jax.ShapeDtypeStruct((k, n), BF16))
    return in_specs, args, out_specs, out_shape


def _run_cast_jobs(in_refs, out_refs):
    for src, dst in zip(in_refs, out_refs):
        dst[...] = src[...].astype(BF16)


def _mem_kv_kernel(x_ref, g_ref, w_ref, *rest, n_jobs):
    o_ref = rest[n_jobs]
    _run_cast_jobs(rest[:n_jobs], rest[n_jobs + 1:])
    h = _rms_rows(x_ref[...], g_ref[...]).astype(BF16)
    o_ref[...] = jnp.dot(h, w_ref[...].astype(BF16), preferred_element_type=F32)


def _mem_kv(x, g, w, cast_weights=()):
    m, k = x.shape
    layers, _, n = w.shape
    tn = MEM_KV_COLS
    assert n % tn == 0
    nt = n // tn
    job_in, job_args, job_out, job_shape = _cast_jobs(cast_weights, layers * nt, lambda l, j: l * nt + j)
    return pl.pallas_call(
        functools.partial(_mem_kv_kernel, n_jobs=len(job_args)),
        grid=(layers, nt),
        in_specs=[pl.BlockSpec((m, k), lambda l, j: (0, 0)),
                  pl.BlockSpec((None, 1, k), lambda l, j: (l, 0, 0)),
                  pl.BlockSpec((None, k, tn), lambda l, j: (l, 0, j))] + job_in,
        out_specs=[pl.BlockSpec((None, m, tn), lambda l, j: (l, 0, j))] + job_out,
        out_shape=[jax.ShapeDtypeStruct((layers, m, n), F32)] + job_shape,
        compiler_params=_params("arbitrary", "arbitrary"),
        name="mem_kv",
    )(x, _gain_arg(g), w, *job_args)


def _resident_spec(shape):
    return pl.BlockSpec(shape, lambda *_: (0,) * len(shape), pipeline_mode=pl.Buffered(1))


def _gelu_tanh(x):
    return x * (0.5 * (1.0 + jnp.tanh(math.sqrt(2.0 / math.pi) * (x + 0.044715 * (x * x * x)))))


def _pass(h_ref, w_ref, col):
    return jnp.dot(h_ref[...], w_ref[:, col:col + V7X_MXU_COLS], preferred_element_type=F32)


def _heads(acc):
    return [acc[:, j * HEAD_DIM:(j + 1) * HEAD_DIM] for j in range(acc.shape[1] // HEAD_DIM)]


def _attn_in_proj_kernel(x_ref, g_ref, w_ref, pos_ref, invf_ref, *rest, n_jobs):
    qk_ref, vm_ref = rest[n_jobs:n_jobs + 2]
    h_ref, c_ref, s1_ref, s2_ref = rest[2 * n_jobs + 2:]
    _run_cast_jobs(rest[:n_jobs], rest[n_jobs + 2:2 * n_jobs + 2])
    h_ref[...] = _rms_rows(x_ref[...], g_ref[...]).astype(BF16)
    _rope_tables(pos_ref, invf_ref, c_ref, s1_ref, s2_ref)
    for c in range(2 * ATTN_W // V7X_MXU_COLS):
        for j, t in enumerate(_heads(_pass(h_ref, w_ref, c * V7X_MXU_COLS))):
            lo = c * V7X_MXU_COLS + j * HEAD_DIM
            qk_ref[:, lo:lo + HEAD_DIM] = (t * c_ref[...] + pltpu.roll(t, ROT_HALF, 1) * s1_ref[...]
                                           + pltpu.roll(t, HEAD_DIM - ROT_HALF, 1) * s2_ref[...])
    for c in range((ATTN_W + MEM_W) // V7X_MXU_COLS):
        cols = slice(c * V7X_MXU_COLS, (c + 1) * V7X_MXU_COLS)
        vm_ref[:, cols] = _pass(h_ref, w_ref, 2 * ATTN_W + c * V7X_MXU_COLS)


def _attn_in_proj(x, g, g_layer, w_bf, pos_col, invf, cast_weights=()):
    s, k = x.shape
    tm = ROW_STEP
    assert w_bf.shape == (k, 3 * ATTN_W + MEM_W)
    row = lambda cols: pl.BlockSpec((tm, cols), lambda i: (i, 0))
    job_in, job_args, job_out, job_shape = _cast_jobs(cast_weights, s // tm, lambda i: i)
    return pl.pallas_call(
        functools.partial(_attn_in_proj_kernel, n_jobs=len(job_args)),
        grid=(s // tm,),
        in_specs=[row(k), _gain_spec(g_layer, k), _resident_spec(w_bf.shape), row(1),
                  pl.BlockSpec((1, HEAD_DIM), lambda i: (0, 0))] + job_in,
        out_specs=[row(2 * ATTN_W), row(ATTN_W + MEM_W)] + job_out,
        out_shape=[jax.ShapeDtypeStruct((s, 2 * ATTN_W), F32),
                   jax.ShapeDtypeStruct((s, ATTN_W + MEM_W), F32)] + job_shape,
        scratch_shapes=[pltpu.VMEM((tm, k), BF16)] + [pltpu.VMEM((tm, HEAD_DIM), F32)] * 3,
        compiler_params=_params("arbitrary"),
        name="attn_in_proj",
    )(x, _gain_arg(g), w_bf, pos_col, invf, *job_args)


def _sgu_in_proj_kernel(x_ref, g_ref, w_ref, lng_ref, lnb_ref, ws_ref, bt_ref,
                        mix_ref, qm_ref, h_ref, v_scr, vn_scr, wsb_scr):
    @pl.when(pl.program_id(0) == 0)
    def _():
        t = lax.broadcasted_iota(jnp.int32, (SGU_CHUNK, SGU_CHUNK), 0)
        s = lax.broadcasted_iota(jnp.int32, (SGU_CHUNK, SGU_CHUNK), 1)
        for g in range(SGU_GROUPS):
            wsb_scr[g] = jnp.where(t >= s, ws_ref[g], 0.0).astype(BF16)

    h_ref[...] = _rms_rows(x_ref[...], g_ref[...]).astype(BF16)
    for c in range(SGU_W // V7X_MXU_COLS):
        cols = slice(c * V7X_MXU_COLS, (c + 1) * V7X_MXU_COLS)
        v_scr[:, cols] = _gelu_tanh(_pass(h_ref, w_ref, SGU_W + c * V7X_MXU_COLS))
    for c in range(MEM_W // V7X_MXU_COLS):
        cols = slice(c * V7X_MXU_COLS, (c + 1) * V7X_MXU_COLS)
        qm_ref[:, cols] = _pass(h_ref, w_ref, 2 * SGU_W + c * V7X_MXU_COLS)
    v = v_scr[...]
    mu = jnp.mean(v, axis=-1, keepdims=True)
    vc = v - mu
    var = jnp.mean(vc * vc, axis=-1, keepdims=True)
    vn_scr[...] = (vc * lax.rsqrt(var + LN_EPS) * lng_ref[...] + lnb_ref[...]).astype(BF16)

    bt = bt_ref[...]
    n_chunks = mix_ref.shape[0] // SGU_CHUNK
    for c in range(SGU_W // V7X_MXU_COLS):
        for j, u in enumerate(_heads(_gelu_tanh(_pass(h_ref, w_ref, c * V7X_MXU_COLS)))):
            g = c * (V7X_MXU_COLS // HEAD_DIM) + j
            cols = slice(g * HEAD_DIM, (g + 1) * HEAD_DIM)
            vg = jnp.concatenate([vn_scr[r * SGU_CHUNK:(r + 1) * SGU_CHUNK, cols]
                                  for r in range(n_chunks)], axis=1)
            mixed = jnp.dot(wsb_scr[g], vg, preferred_element_type=F32) + bt[:, g:g + 1]
            for r in range(n_chunks):
                rows = slice(r * SGU_CHUNK, (r + 1) * SGU_CHUNK)
                mix_ref[rows, cols] = (u[rows, :] * mixed[:, r * HEAD_DIM:(r + 1) * HEAD_DIM]
                                       ).astype(mix_ref.dtype)


def _sgu_in_proj(x, g, g_layer, w_bf, ln_g, ln_b, w_spatial, b_spatial_t, layer):
    s, k = x.shape
    tm = ROW_STEP
    assert w_bf.shape == (k, 2 * SGU_W + MEM_W) and tm % SGU_CHUNK == 0
    row = lambda cols: pl.BlockSpec((tm, cols), lambda i: (i, 0))
    return pl.pallas_call(
        _sgu_in_proj_kernel,
        grid=(s // tm,),
        in_specs=[row(k), _gain_spec(g_layer, k), _resident_spec(w_bf.shape),
                  _gain_spec(layer, SGU_W), _gain_spec(layer, SGU_W),
                  pl.BlockSpec((None, SGU_GROUPS, SGU_CHUNK, SGU_CHUNK), lambda i: (layer, 0, 0, 0)),
                  pl.BlockSpec((None, SGU_CHUNK, SGU_GROUPS), lambda i: (layer, 0, 0))],
        out_specs=[row(SGU_W), row(MEM_W)],
        out_shape=[jax.ShapeDtypeStruct((s, SGU_W), BF16), jax.ShapeDtypeStruct((s, MEM_W), F32)],
        scratch_shapes=[pltpu.VMEM((tm, k), BF16), pltpu.VMEM((tm, SGU_W), F32),
                        pltpu.VMEM((tm, SGU_W), BF16),
                        pltpu.VMEM((SGU_GROUPS, SGU_CHUNK, SGU_CHUNK), BF16)],
        compiler_params=_params("arbitrary"),
        name="sgu_in_proj",
    )(x, _gain_arg(g), w_bf, _gain_arg(ln_g), _gain_arg(ln_b), w_spatial, b_spatial_t)


def _out_proj_kernel(a_ref, q_ref, kv_ref, w_ref, x_ref, g_ref, o_ref, hn_ref):
    ka = a_ref.shape[1]
    width = o_ref.shape[1]
    n_chunk = 512
    chunks = [slice(c * n_chunk, (c + 1) * n_chunk) for c in range(width // n_chunk)]

    probs = []
    for h in range(MEM_HEADS):
        sl = slice(h * HEAD_DIM, (h + 1) * HEAD_DIM)
        s = lax.dot_general(q_ref[:, sl].astype(BF16), kv_ref[:, sl].astype(BF16),
                            (((1,), (1,)), ((), ())), preferred_element_type=F32) * ATTN_SCALE
        p = jnp.exp(s - jnp.max(s, axis=-1, keepdims=True))
        probs.append((p.astype(BF16), jnp.sum(p, axis=-1, keepdims=True)))
    for cols in chunks:
        o_ref[:, cols] = x_ref[:, cols] + jnp.dot(a_ref[...], w_ref[:ka, cols],
                                                  preferred_element_type=F32)
    heads = []
    for h, (p, l) in enumerate(probs):
        v = kv_ref[:, MEM_W + h * HEAD_DIM:MEM_W + (h + 1) * HEAD_DIM].astype(BF16)
        heads.append((jnp.dot(p, v, preferred_element_type=F32) * (1.0 / l)).astype(BF16))
    b = jnp.concatenate(heads, axis=1)

    ssq = jnp.zeros((o_ref.shape[0], 1), F32)
    for cols in chunks:
        y = o_ref[:, cols] + jnp.dot(b, w_ref[ka:, cols], preferred_element_type=F32)
        o_ref[:, cols] = y
        ssq = ssq + jnp.sum(y * y, axis=-1, keepdims=True)
    scale = lax.rsqrt(ssq * (1.0 / width) + NORM_EPS)
    for cols in chunks:
        hn_ref[:, cols] = (o_ref[:, cols] * scale * g_ref[:, cols]).astype(BF16)


def _out_proj(a, q_src, q_col_block, kv, kv_layer, w_bf, x, g, g_layer):
    s, ka = a.shape
    width = x.shape[1]
    tm = ROW_STEP
    assert w_bf.shape == (ka + MEM_W, width)
    row = lambda cols: pl.BlockSpec((tm, cols), lambda i: (i, 0))
    return pl.pallas_call(
        _out_proj_kernel,
        grid=(s // tm,),
        in_specs=[row(ka), pl.BlockSpec((tm, MEM_W), lambda i: (i, q_col_block)),
                  pl.BlockSpec((None, MEM_LEN, 2 * MEM_W), lambda i: (kv_layer, 0, 0)),
                  _resident_spec((ka + MEM_W, width)), row(width), _gain_spec(g_layer, width)],
        out_specs=[row(width), row(width)],
        out_shape=[jax.ShapeDtypeStruct((s, width), F32), jax.ShapeDtypeStruct((s, width), BF16)],
        compiler_params=_params("parallel"),
        name="out_proj",
    )(a, q_src, kv, w_bf, x, _gain_arg(g))


def _rows(ref, start, size, stride):
    if stride == 1:
        return ref[pl.ds(start, size), :]
    return ref[pl.ds(start, size, stride=stride), :]


V7X_FREE_SUBLANE_STRIDE = 4


def _presplit(d):
    return d // V7X_FREE_SUBLANE_STRIDE if d > V7X_FREE_SUBLANE_STRIDE else 1


def _attn_batch(units):
    scores = [lax.dot_general(q.astype(BF16), kc.astype(BF16), (((1,), (1,)), ((), ())),
                              preferred_element_type=F32) * ATTN_SCALE + bias
              for q, kc, _, bias in units]
    probs = []
    for s in scores:
        m = jnp.max(s, axis=-1, keepdims=True)
        p = jnp.exp(s - m)
        probs.append((p, m, jnp.sum(p, axis=-1, keepdims=True)))
    outs = []
    for (p, m, l), (_, _, vc, _) in zip(probs, units):
        o = jnp.dot(p.astype(BF16), vc.astype(BF16), preferred_element_type=F32)
        outs.append((o * (1.0 / l), m + jnp.log(l)))
    return outs


def _attn_kernel(*refs):
    ngrp = len(ATTN_GROUPS)
    in_refs = refs[:5 * ngrp]
    o_ref = refs[5 * ngrp]
    pos = 5 * ngrp + 1
    o_scr, l_scr = refs[pos:pos + ngrp], refs[pos + ngrp:pos + 2 * ngrp]
    split_refs = refs[pos + 2 * ngrp:pos + 2 * ngrp + 7]
    bias_ref = refs[-1]

    qi = lax.broadcasted_iota(jnp.int32, (BLK, 2 * BLK), 0)
    ki = lax.broadcasted_iota(jnp.int32, (BLK, 2 * BLK), 1)
    lo = jnp.where(pl.program_id(0) == 0, BLK, 0)
    bias_ref[0] = jnp.where((ki >= qi) & (ki <= qi + N_BACK), 0.0, NEG_INF)
    bias_ref[1] = jnp.where((ki >= jnp.maximum(qi, lo)) & (ki <= qi + N_BACK), 0.0, NEG_INF)

    for g, (_, d) in enumerate(ATTN_GROUPS):
        q_ref, k_ref, v_ref, kp_ref, vp_ref = in_refs[5 * g:5 * g + 5]
        og, lg = o_scr[g], l_scr[g]
        sub = d * BLK
        f = _presplit(d)
        d2 = d // f
        qlen, sublen = ATTN_SPAN // f, sub // f
        if f > 1:
            q_src, kp_src, k_src, vp_src, v_src, o_dst, l_dst = split_refs
            for rf in range(f):
                for src, dst, n in ((q_ref, q_src, qlen), (k_ref, k_src, qlen), (v_ref, v_src, qlen),
                                    (kp_ref, kp_src, sublen), (vp_ref, vp_src, sublen)):
                    dst[pl.ds(rf * n, n), :] = _rows(src, rf, n, f)
        else:
            q_src, kp_src, k_src, vp_src, v_src, o_dst, l_dst = (
                q_ref, kp_ref, k_ref, vp_ref, v_ref, og, lg)

        def key_rows(prev_src, src, c, rf, r2):
            if c == 0:
                return jnp.concatenate([_rows(prev_src, rf * sublen + r2, BLK, d2),
                                        _rows(src, rf * qlen + r2, BLK, d2)], axis=0)
            return _rows(src, rf * qlen + (c - 1) * sublen + r2, 2 * BLK, d2)

        units, starts = [], []
        for idx in range(ATTN_SPAN // BLK):
            c, r = divmod(idx, d)
            rf, r2 = r % f, r // f
            qstart = rf * qlen + c * sublen + r2
            units.append((_rows(q_src, qstart, BLK, d2), key_rows(kp_src, k_src, c, rf, r2),
                          key_rows(vp_src, v_src, c, rf, r2), bias_ref[1 if c == 0 else 0]))
            starts.append(qstart)
        for qstart, (o, lse) in zip(starts, _attn_batch(units)):
            rows = pl.ds(qstart, BLK) if d2 == 1 else pl.ds(qstart, BLK, stride=d2)
            o_dst[rows, :] = o
            l_dst[rows, :] = jnp.broadcast_to(lse, (BLK, HEAD_DIM))
        if f > 1:
            for rf in range(f):
                og[pl.ds(rf, qlen, stride=f), :] = o_dst[pl.ds(rf * qlen, qlen), :]
                lg[pl.ds(rf, qlen, stride=f), :] = l_dst[pl.ds(rf * qlen, qlen), :]

    lses = [l[...] for l in l_scr]
    mx = functools.reduce(jnp.maximum, lses)
    ws = [jnp.exp(l - mx) for l in lses]
    num = functools.reduce(lambda a, b: a + b, [w * o[...] for w, o in zip(ws, o_scr)])
    den = functools.reduce(lambda a, b: a + b, ws)
    o_ref[...] = (num * (1.0 / den)).astype(o_ref.dtype)


def _dilated_attention(qk, vm):
    s = qk.shape[0]
    assert s % ATTN_SPAN == 0
    hpg = HEADS_PER_GROUP
    in_specs, args = [], []
    for g, (_, d) in enumerate(ATTN_GROUPS):
        sub = d * BLK
        ratio = ATTN_SPAN // sub
        qc, kc, vc = g * hpg, ATTN_HEADS + g * hpg, g * hpg
        cur = lambda col: pl.BlockSpec((ATTN_SPAN, HEAD_DIM), lambda i, h, col=col: (i, col + h))
        prev = lambda col, ratio=ratio, sub=sub: pl.BlockSpec(
            (sub, HEAD_DIM), lambda i, h, col=col, ratio=ratio: (jnp.maximum(i * ratio - 1, 0), col + h))
        in_specs += [cur(qc), cur(kc), cur(vc), prev(kc), prev(vc)]
        args += [qk, qk, vm, qk, vm]
    ngrp = len(ATTN_GROUPS)
    return pl.pallas_call(
        _attn_kernel,
        grid=(s // ATTN_SPAN, hpg),
        in_specs=in_specs,
        out_specs=pl.BlockSpec((ATTN_SPAN, HEAD_DIM), lambda i, h: (i, h)),
        out_shape=jax.ShapeDtypeStruct((s, ATTN_OUT_W), BF16),
        scratch_shapes=([pltpu.VMEM((ATTN_SPAN, HEAD_DIM), F32)] * (2 * ngrp)
                        + [pltpu.VMEM((ATTN_SPAN, HEAD_DIM), F32)] * 7
                        + [pltpu.VMEM((2, BLK, 2 * BLK), F32)]),
        compiler_params=_params("parallel", "arbitrary"),
        name="dilated_attention",
    )(*args)


def _ffn_down_rows_kernel(a_ref, w_ref, x_ref, *rest, final_norm):
    if final_norm:
        g_ref, o_ref = rest
    else:
        (o_ref,) = rest
    width = o_ref.shape[1]
    ssq = jnp.zeros((o_ref.shape[0], 1), F32)
    for c in range(width // V7X_MXU_COLS):
        cols = slice(c * V7X_MXU_COLS, (c + 1) * V7X_MXU_COLS)
        y = x_ref[:, cols] + jnp.dot(a_ref[...], w_ref[:, cols], preferred_element_type=F32)
        o_ref[:, cols] = y
        if final_norm:
            ssq = ssq + jnp.sum(y * y, axis=-1, keepdims=True)
    if final_norm:
        scale = lax.rsqrt(ssq * (1.0 / width) + NORM_EPS)
        for c in range(width // V7X_MXU_COLS):
            cols = slice(c * V7X_MXU_COLS, (c + 1) * V7X_MXU_COLS)
            o_ref[:, cols] = o_ref[:, cols] * scale * g_ref[:, cols]


def _ffn_down_rows(a, w_bf, x, final_g=None):
    s, f = a.shape
    width = x.shape[1]
    tm = ROW_STEP
    assert w_bf.shape == (f, width) and s % tm == 0
    row = lambda cols: pl.BlockSpec((tm, cols), lambda i: (i, 0))
    in_specs = [row(f), _resident_spec(w_bf.shape), row(width)]
    args = [a, w_bf, x]
    if final_g is not None:
        in_specs.append(_gain_spec(0, width))
        args.append(final_g.reshape(1, 1, width))
    return pl.pallas_call(
        functools.partial(_ffn_down_rows_kernel, final_norm=final_g is not None),
        grid=(s // tm,),
        in_specs=in_specs,
        out_specs=row(width),
        out_shape=jax.ShapeDtypeStruct((s, width), F32),
        compiler_params=_params("parallel"),
        name="ffn_down_rows",
    )(*args)


def _ffn_gate_up_kernel(h_ref, wg_ref, wu_ref, *rest, n_jobs):
    o_ref = rest[n_jobs]
    _run_cast_jobs(rest[:n_jobs], rest[n_jobs + 1:])
    for c in range(o_ref.shape[1] // V7X_MXU_COLS):
        cols = slice(c * V7X_MXU_COLS, (c + 1) * V7X_MXU_COLS)
        gate = jnp.dot(h_ref[...], wg_ref[:, cols].astype(BF16), preferred_element_type=F32)
        up = jnp.dot(h_ref[...], wu_ref[:, cols].astype(BF16), preferred_element_type=F32)
        o_ref[:, cols] = (gate * (1.0 / (1.0 + jnp.exp(-gate))) * up).astype(o_ref.dtype)


def _ffn_gate_up(h, w_gate, w_up, layer, cast_weights=()):
    s, k = h.shape
    f = w_gate.shape[2]
    tm, tf = GATE_UP_ROWS, GATE_UP_COLS
    nj = f // tf
    wspec = pl.BlockSpec((None, k, tf), lambda i, j: (layer, 0, j))
    job_in, job_args, job_out, job_shape = _cast_jobs(cast_weights, (s // tm) * nj,
                                                      lambda i, j: i * nj + j)
    return pl.pallas_call(
        functools.partial(_ffn_gate_up_kernel, n_jobs=len(job_args)),
        grid=(s // tm, nj),
        in_specs=[pl.BlockSpec((tm, k), lambda i, j: (i, 0)), wspec, wspec] + job_in,
        out_specs=[pl.BlockSpec((tm, tf), lambda i, j: (i, j))] + job_out,
        out_shape=[jax.ShapeDtypeStruct((s, f), BF16)] + job_shape,
        compiler_params=_params("arbitrary", "arbitrary"),
        name="ffn_gate_up",
    )(h, w_gate, w_up, *job_args)


def kernel(x, mem, positions, mix_norm, mem_norm, w_mem_kv, ffn_norm, w_gate, w_up, w_down,
           attn_w_in, attn_w_out, sgu_w_in, sgu_ln_g, sgu_ln_b, sgu_w_spatial, sgu_b_spatial,
           sgu_w_out, final_norm):
    b, s, d = x.shape
    assert (b, s, d) == (1, SEQ, D_MODEL) and mem.shape == (1, MEM_LEN, D_MODEL)
    xs = x.reshape(s, d)
    mems = mem.reshape(MEM_LEN, d)

    inv_freq = ROPE_THETA ** (-jnp.arange(ROT_HALF, dtype=F32) / ROT_HALF)
    invf = jnp.tile(jnp.concatenate([inv_freq, inv_freq]), ROPE_PACK)
    pos_col, invf = positions.reshape(s, 1), invf.reshape(1, HEAD_DIM)

    def mixer_weights(layer):
        return (attn_w_in, attn_w_out) if layer % 2 == 0 else (sgu_w_in, sgu_w_out)

    kv, w_in_bf = _mem_kv(mems, mem_norm, w_mem_kv, [(attn_w_in, 0)])
    w_out_bf = None
    for i in range(DEPTH):
        j = i // 2
        if i % 2 == 0:
            casts = [(attn_w_out, j)] if w_out_bf is None else []
            qk, vm, *cast = _attn_in_proj(xs, mix_norm, i, w_in_bf, pos_col, invf, casts)
            w_out_bf = cast[0] if cast else w_out_bf
            mix = _dilated_attention(qk, vm)
            q_src, q_block = vm, ATTN_W // MEM_W
        else:
            mix, q_src = _sgu_in_proj(xs, mix_norm, i, w_in_bf, sgu_ln_g, sgu_ln_b,
                                      sgu_w_spatial, jnp.swapaxes(sgu_b_spatial, 1, 2), j)
            q_block = 0
        xs, hn = _out_proj(mix, q_src, q_block, kv, i, w_out_bf, xs, ffn_norm, i)
        last = i + 1 == DEPTH
        casts = [(w_down, i)] + ([] if last else [(w, (i + 1) // 2) for w in mixer_weights(i + 1)])
        act, w_down_bf, *next_bf = _ffn_gate_up(hn, w_gate, w_up, i, casts)
        w_in_bf, w_out_bf = next_bf if next_bf else (None, None)
        xs = _ffn_down_rows(act, w_down_bf, xs, final_g=final_norm if last else None)
    return xs.reshape(b, s, d)
```

```python
import functools
import math

import jax
import jax.numpy as jnp
from jax import lax
from jax.experimental import pallas as pl
from jax.experimental.pallas import tpu as pltpu

D_MODEL = 2048
SEQ = 8192
DEPTH = 2
MEM_LEN = 256
HEAD_DIM = 128
MEM_HEADS = 4
MEM_W = MEM_HEADS * HEAD_DIM
ATTN_GROUPS = ((128, 1), (512, 4), (2048, 16))
ATTN_HEADS = 12
HEADS_PER_GROUP = 4
ATTN_W = ATTN_HEADS * HEAD_DIM
ATTN_OUT_W = HEADS_PER_GROUP * HEAD_DIM
BLK = 128
SGU_GROUPS = 12
SGU_W = SGU_GROUPS * HEAD_DIM
SGU_CHUNK = 128
ROT_DIM = HEAD_DIM // 4
ROT_HALF = ROT_DIM // 2
ROPE_THETA = 500000.0
NORM_EPS = 1e-6
LN_EPS = 1e-5
NEG_INF = -1e30
ATTN_SCALE = HEAD_DIM ** -0.5

N_BACK = BLK
assert all(w // d == N_BACK for w, d in ATTN_GROUPS)
ATTN_SPAN = max(d for _, d in ATTN_GROUPS) * BLK

V7X_VMEM_LIMIT_BYTES = 56 * 1024 * 1024
V7X_MXU_COLS = 256

ROW_STEP = 512
GATE_UP_ROWS, GATE_UP_COLS = 2048, 512
MEM_KV_COLS = 512

F32 = jnp.float32
BF16 = jnp.bfloat16


def _params(*semantics):
    return pltpu.CompilerParams(dimension_semantics=semantics,
                                vmem_limit_bytes=V7X_VMEM_LIMIT_BYTES)


def _rms_rows(x, g):
    ms = jnp.mean(x * x, axis=-1, keepdims=True)
    return x * lax.rsqrt(ms + NORM_EPS) * g


def _gain_spec(layer, k):
    return pl.BlockSpec((None, 1, k), lambda *_: (layer, 0, 0))


def _gain_arg(g):
    return g.reshape(g.shape[0], 1, g.shape[1])


ROPE_PACK = HEAD_DIM // ROT_DIM


def _rope_tables(pos_ref, invf_ref, c_ref, s1_ref, s2_ref):
    n = pos_ref.shape[0] // ROPE_PACK
    lane = lax.broadcasted_iota(jnp.int32, (n, HEAD_DIM), 1)
    group = lane // ROT_DIM
    pos = jnp.zeros((n, HEAD_DIM), F32)
    for q in range(ROPE_PACK):
        pos = jnp.where(group == q, pos_ref[q * n:(q + 1) * n, :].astype(F32), pos)
    ang = pos * invf_ref[...]
    cos, sin = jnp.cos(ang), jnp.sin(ang)
    for q in range(ROPE_PACK):
        cq = cos if q == 0 else pltpu.roll(cos, HEAD_DIM - q * ROT_DIM, 1)
        sq = sin if q == 0 else pltpu.roll(sin, HEAD_DIM - q * ROT_DIM, 1)
        rows = slice(q * n, (q + 1) * n)
        c_ref[rows, :] = jnp.where(lane < ROT_DIM, cq, 1.0)
        s1_ref[rows, :] = jnp.where(lane < ROT_HALF, 0.0, jnp.where(lane < ROT_DIM, sq, 0.0))
        s2_ref[rows, :] = jnp.where(lane < ROT_HALF, -sq, 0.0)


BF16_SUBLANES = 16


def _cast_jobs(weights, n_steps, step_of):
    in_specs, args, out_specs, out_shape = [], [], [], []
    for w, layer in weights:
        _, k, n = w.shape
        slab = -(-(-(-k // n_steps)) // BF16_SUBLANES) * BF16_SUBLANES
        while k % slab:
            slab += BF16_SUBLANES
        n_slabs = k // slab
        in_specs.append(pl.BlockSpec(
            (None, slab, n), lambda *g, layer=layer, n_slabs=n_slabs:
            (layer, jnp.minimum(step_of(*g), n_slabs - 1), 0)))
        out_specs.append(pl.BlockSpec(
            (slab, n), lambda *g, n_slabs=n_slabs: (jnp.minimum(step_of(*g), n_slabs - 1), 0)))
        args.append(w)
        out_shape.append(jax.ShapeDtypeStruct((k, n), BF16))
    return in_specs, args, out_specs, out_shape


def _run_cast_jobs(in_refs, out_refs):
    for src, dst in zip(in_refs, out_refs):
        dst[...] = src[...].astype(BF16)


def _mem_kv_kernel(x_ref, g_ref, w_ref, *rest, n_jobs):
    o_ref = rest[n_jobs]
    _run_cast_jobs(rest[:n_jobs], rest[n_jobs + 1:])
    h = _rms_rows(x_ref[...], g_ref[...]).astype(BF16)
    o_ref[...] = jnp.dot(h, w_ref[...].astype(BF16), preferred_element_type=F32)


def _mem_kv(x, g, w, cast_weights=()):
    m, k = x.shape
    layers, _, n = w.shape
    tn = MEM_KV_COLS
    assert n % tn == 0
    nt = n // tn
    job_in, job_args, job_out, job_shape = _cast_jobs(cast_weights, layers * nt, lambda l, j: l * nt + j)
    return pl.pallas_call(
        functools.partial(_mem_kv_kernel, n_jobs=len(job_args)),
        grid=(layers, nt),
        in_specs=[pl.BlockSpec((m, k), lambda l, j: (0, 0)),
                  pl.BlockSpec((None, 1, k), lambda l, j: (l, 0, 0)),
                  pl.BlockSpec((None, k, tn), lambda l, j: (l, 0, j))] + job_in,
        out_specs=[pl.BlockSpec((None, m, tn), lambda l, j: (l, 0, j))] + job_out,
        out_shape=[jax.ShapeDtypeStruct((layers, m, n), F32)] + job_shape,
        compiler_params=_params("arbitrary", "arbitrary"),
        name="mem_kv",
    )(x, _gain_arg(g), w, *job_args)


def _resident_spec(shape):
    return pl.BlockSpec(shape, lambda *_: (0,) * len(shape), pipeline_mode=pl.Buffered(1))


def _gelu_tanh(x):
    return x * (0.5 * (1.0 + jnp.tanh(math.sqrt(2.0 / math.pi) * (x + 0.044715 * (x * x * x)))))


def _pass(h_ref, w_ref, col):
    return jnp.dot(h_ref[...], w_ref[:, col:col + V7X_MXU_COLS], preferred_element_type=F32)


def _heads(acc):
    return [acc[:, j * HEAD_DIM:(j + 1) * HEAD_DIM] for j in range(acc.shape[1] // HEAD_DIM)]


def _attn_in_proj_kernel(x_ref, g_ref, w_ref, pos_ref, invf_ref, *rest, n_jobs):
    qk_ref, vm_ref = rest[n_jobs:n_jobs + 2]
    h_ref, c_ref, s1_ref, s2_ref = rest[2 * n_jobs + 2:]
    _run_cast_jobs(rest[:n_jobs], rest[n_jobs + 2:2 * n_jobs + 2])
    h_ref[...] = _rms_rows(x_ref[...], g_ref[...]).astype(BF16)
    _rope_tables(pos_ref, invf_ref, c_ref, s1_ref, s2_ref)
    for c in range(2 * ATTN_W // V7X_MXU_COLS):
        for j, t in enumerate(_heads(_pass(h_ref, w_ref, c * V7X_MXU_COLS))):
            lo = c * V7X_MXU_COLS + j * HEAD_DIM
            qk_ref[:, lo:lo + HEAD_DIM] = (t * c_ref[...] + pltpu.roll(t, ROT_HALF, 1) * s1_ref[...]
                                           + pltpu.roll(t, HEAD_DIM - ROT_HALF, 1) * s2_ref[...])
    for c in range((ATTN_W + MEM_W) // V7X_MXU_COLS):
        cols = slice(c * V7X_MXU_COLS, (c + 1) * V7X_MXU_COLS)
        vm_ref[:, cols] = _pass(h_ref, w_ref, 2 * ATTN_W + c * V7X_MXU_COLS)


def _attn_in_proj(x, g, g_layer, w_bf, pos_col, invf, cast_weights=()):
    s, k = x.shape
    tm = ROW_STEP
    assert w_bf.shape == (k, 3 * ATTN_W + MEM_W)
    row = lambda cols: pl.BlockSpec((tm, cols), lambda i: (i, 0))
    job_in, job_args, job_out, job_shape = _cast_jobs(cast_weights, s // tm, lambda i: i)
    return pl.pallas_call(
        functools.partial(_attn_in_proj_kernel, n_jobs=len(job_args)),
        grid=(s // tm,),
        in_specs=[row(k), _gain_spec(g_layer, k), _resident_spec(w_bf.shape), row(1),
                  pl.BlockSpec((1, HEAD_DIM), lambda i: (0, 0))] + job_in,
        out_specs=[row(2 * ATTN_W), row(ATTN_W + MEM_W)] + job_out,
        out_shape=[jax.ShapeDtypeStruct((s, 2 * ATTN_W), F32),
                   jax.ShapeDtypeStruct((s, ATTN_W + MEM_W), F32)] + job_shape,
        scratch_shapes=[pltpu.VMEM((tm, k), BF16)] + [pltpu.VMEM((tm, HEAD_DIM), F32)] * 3,
        compiler_params=_params("arbitrary"),
        name="attn_in_proj",
    )(x, _gain_arg(g), w_bf, pos_col, invf, *job_args)


def _memory_scores(q_heads, kv_ref):
    probs = []
    for h, q in enumerate(q_heads):
        sl = slice(h * HEAD_DIM, (h + 1) * HEAD_DIM)
        s = lax.dot_general(q.astype(BF16), kv_ref[:, sl].astype(BF16), (((1,), (1,)), ((), ())),
                            preferred_element_type=F32) * ATTN_SCALE
        p = jnp.exp(s - jnp.max(s, axis=-1, keepdims=True))
        probs.append((p.astype(BF16), jnp.sum(p, axis=-1, keepdims=True)))
    return probs


def _memory_values(probs, kv_ref):
    heads = []
    for h, (p, l) in enumerate(probs):
        v = kv_ref[:, MEM_W + h * HEAD_DIM:MEM_W + (h + 1) * HEAD_DIM].astype(BF16)
        heads.append((jnp.dot(p, v, preferred_element_type=F32) * (1.0 / l)).astype(BF16))
    return jnp.concatenate(heads, axis=1)


OUT_PROJ_COLS = 512


def _project_out(mix, probs, kv_ref, w_ref, x_ref, g_ref, o_ref, hn_ref):
    ka = mix.shape[1]
    width = o_ref.shape[1]
    chunks = [slice(c * OUT_PROJ_COLS, (c + 1) * OUT_PROJ_COLS) for c in range(width // OUT_PROJ_COLS)]
    for cols in chunks:
        o_ref[:, cols] = x_ref[:, cols] + jnp.dot(mix, w_ref[:ka, cols], preferred_element_type=F32)
    b = _memory_values(probs, kv_ref)
    ssq = jnp.zeros((o_ref.shape[0], 1), F32)
    for cols in chunks:
        y = o_ref[:, cols] + jnp.dot(b, w_ref[ka:, cols], preferred_element_type=F32)
        o_ref[:, cols] = y
        ssq = ssq + jnp.sum(y * y, axis=-1, keepdims=True)
    scale = lax.rsqrt(ssq * (1.0 / width) + NORM_EPS)
    for cols in chunks:
        hn_ref[:, cols] = (o_ref[:, cols] * scale * g_ref[:, cols]).astype(BF16)


def _sgu_mixer_kernel(x_ref, g_ref, w_ref, lng_ref, lnb_ref, ws_ref, bt_ref, kv_ref, wo_ref, gf_ref,
                      o_ref, hn_ref, h_ref, v_scr, vn_scr, mix_scr, wsb_scr):
    @pl.when(pl.program_id(0) == 0)
    def _():
        t = lax.broadcasted_iota(jnp.int32, (SGU_CHUNK, SGU_CHUNK), 0)
        s = lax.broadcasted_iota(jnp.int32, (SGU_CHUNK, SGU_CHUNK), 1)
        for g in range(SGU_GROUPS):
            wsb_scr[g] = jnp.where(t >= s, ws_ref[g], 0.0).astype(BF16)

    h_ref[...] = _rms_rows(x_ref[...], g_ref[...]).astype(BF16)
    for c in range(SGU_W // V7X_MXU_COLS):
        cols = slice(c * V7X_MXU_COLS, (c + 1) * V7X_MXU_COLS)
        v_scr[:, cols] = _gelu_tanh(_pass(h_ref, w_ref, SGU_W + c * V7X_MXU_COLS))
    q_heads = []
    for c in range(MEM_W // V7X_MXU_COLS):
        q_heads += _heads(_pass(h_ref, w_ref, 2 * SGU_W + c * V7X_MXU_COLS))
    probs = _memory_scores(q_heads, kv_ref)
    v = v_scr[...]
    mu = jnp.mean(v, axis=-1, keepdims=True)
    vc = v - mu
    var = jnp.mean(vc * vc, axis=-1, keepdims=True)
    vn_scr[...] = (vc * lax.rsqrt(var + LN_EPS) * lng_ref[...] + lnb_ref[...]).astype(BF16)

    bt = bt_ref[...]
    n_chunks = mix_scr.shape[0] // SGU_CHUNK
    for c in range(SGU_W // V7X_MXU_COLS):
        for j, u in enumerate(_heads(_gelu_tanh(_pass(h_ref, w_ref, c * V7X_MXU_COLS)))):
            g = c * (V7X_MXU_COLS // HEAD_DIM) + j
            cols = slice(g * HEAD_DIM, (g + 1) * HEAD_DIM)
            vg = jnp.concatenate([vn_scr[r * SGU_CHUNK:(r + 1) * SGU_CHUNK, cols]
                                  for r in range(n_chunks)], axis=1)
            mixed = jnp.dot(wsb_scr[g], vg, preferred_element_type=F32) + bt[:, g:g + 1]
            for r in range(n_chunks):
                rows = slice(r * SGU_CHUNK, (r + 1) * SGU_CHUNK)
                mix_scr[rows, cols] = (u[rows, :] * mixed[:, r * HEAD_DIM:(r + 1) * HEAD_DIM]
                                       ).astype(mix_scr.dtype)

    _project_out(mix_scr[...], probs, kv_ref, wo_ref, x_ref, gf_ref, o_ref, hn_ref)


def _sgu_mixer(x, g, g_layer, w_bf, ln_g, ln_b, w_spatial, b_spatial_t, layer, kv, kv_layer,
               w_out_bf, g_ffn, g_ffn_layer):
    s, k = x.shape
    tm = ROW_STEP
    assert w_bf.shape == (k, 2 * SGU_W + MEM_W) and tm % SGU_CHUNK == 0
    assert w_out_bf.shape == (SGU_W + MEM_W, k)
    row = lambda cols: pl.BlockSpec((tm, cols), lambda i: (i, 0))
    return pl.pallas_call(
        _sgu_mixer_kernel,
        grid=(s // tm,),
        in_specs=[row(k), _gain_spec(g_layer, k), _resident_spec(w_bf.shape),
                  _gain_spec(layer, SGU_W), _gain_spec(layer, SGU_W),
                  pl.BlockSpec((None, SGU_GROUPS, SGU_CHUNK, SGU_CHUNK), lambda i: (layer, 0, 0, 0)),
                  pl.BlockSpec((None, SGU_CHUNK, SGU_GROUPS), lambda i: (layer, 0, 0)),
                  pl.BlockSpec((None, MEM_LEN, 2 * MEM_W), lambda i: (kv_layer, 0, 0)),
                  _resident_spec(w_out_bf.shape), _gain_spec(g_ffn_layer, k)],
        out_specs=[row(k), row(k)],
        out_shape=[jax.ShapeDtypeStruct((s, k), F32), jax.ShapeDtypeStruct((s, k), BF16)],
        scratch_shapes=[pltpu.VMEM((tm, k), BF16), pltpu.VMEM((tm, SGU_W), F32),
                        pltpu.VMEM((tm, SGU_W), BF16), pltpu.VMEM((tm, SGU_W), BF16),
                        pltpu.VMEM((SGU_GROUPS, SGU_CHUNK, SGU_CHUNK), BF16)],
        compiler_params=_params("arbitrary"),
        name="sgu_mixer",
    )(x, _gain_arg(g), w_bf, _gain_arg(ln_g), _gain_arg(ln_b), w_spatial, b_spatial_t, kv,
      w_out_bf, _gain_arg(g_ffn))


def _out_proj_kernel(a_ref, q_ref, kv_ref, w_ref, x_ref, g_ref, o_ref, hn_ref):
    probs = _memory_scores([q_ref[:, h * HEAD_DIM:(h + 1) * HEAD_DIM] for h in range(MEM_HEADS)],
                           kv_ref)
    _project_out(a_ref[...], probs, kv_ref, w_ref, x_ref, g_ref, o_ref, hn_ref)


def _out_proj(a, q_src, q_col_block, kv, kv_layer, w_bf, x, g, g_layer):
    s, ka = a.shape
    width = x.shape[1]
    tm = ROW_STEP
    assert w_bf.shape == (ka + MEM_W, width)
    row = lambda cols: pl.BlockSpec((tm, cols), lambda i: (i, 0))
    return pl.pallas_call(
        _out_proj_kernel,
        grid=(s // tm,),
        in_specs=[row(ka), pl.BlockSpec((tm, MEM_W), lambda i: (i, q_col_block)),
                  pl.BlockSpec((None, MEM_LEN, 2 * MEM_W), lambda i: (kv_layer, 0, 0)),
                  _resident_spec((ka + MEM_W, width)), row(width), _gain_spec(g_layer, width)],
        out_specs=[row(width), row(width)],
        out_shape=[jax.ShapeDtypeStruct((s, width), F32), jax.ShapeDtypeStruct((s, width), BF16)],
        compiler_params=_params("parallel"),
        name="out_proj",
    )(a, q_src, kv, w_bf, x, _gain_arg(g))


def _rows(ref, start, size, stride):
    if stride == 1:
        return ref[pl.ds(start, size), :]
    return ref[pl.ds(start, size, stride=stride), :]


V7X_FREE_SUBLANE_STRIDE = 4


def _presplit(d):
    return d // V7X_FREE_SUBLANE_STRIDE if d > V7X_FREE_SUBLANE_STRIDE else 1


def _attn_batch(units):
    scores = [lax.dot_general(q.astype(BF16), kc.astype(BF16), (((1,), (1,)), ((), ())),
                              preferred_element_type=F32) * ATTN_SCALE + bias
              for q, kc, _, bias in units]
    probs = []
    for s in scores:
        m = jnp.max(s, axis=-1, keepdims=True)
        p = jnp.exp(s - m)
        probs.append((p, m, jnp.sum(p, axis=-1, keepdims=True)))
    outs = []
    for (p, m, l), (_, _, vc, _) in zip(probs, units):
        o = jnp.dot(p.astype(BF16), vc.astype(BF16), preferred_element_type=F32)
        outs.append((o * (1.0 / l), m + jnp.log(l)))
    return outs


def _attn_kernel(*refs):
    ngrp = len(ATTN_GROUPS)
    in_refs = refs[:5 * ngrp]
    o_ref = refs[5 * ngrp]
    pos = 5 * ngrp + 1
    o_scr, l_scr = refs[pos:pos + ngrp], refs[pos + ngrp:pos + 2 * ngrp]
    split_refs = refs[pos + 2 * ngrp:pos + 2 * ngrp + 7]
    bias_ref = refs[-1]

    qi = lax.broadcasted_iota(jnp.int32, (BLK, 2 * BLK), 0)
    ki = lax.broadcasted_iota(jnp.int32, (BLK, 2 * BLK), 1)
    lo = jnp.where(pl.program_id(0) == 0, BLK, 0)
    bias_ref[0] = jnp.where((ki >= qi) & (ki <= qi + N_BACK), 0.0, NEG_INF)
    bias_ref[1] = jnp.where((ki >= jnp.maximum(qi, lo)) & (ki <= qi + N_BACK), 0.0, NEG_INF)

    for g, (_, d) in enumerate(ATTN_GROUPS):
        q_ref, k_ref, v_ref, kp_ref, vp_ref = in_refs[5 * g:5 * g + 5]
        og, lg = o_scr[g], l_scr[g]
        sub = d * BLK
        f = _presplit(d)
        d2 = d // f
        qlen, sublen = ATTN_SPAN // f, sub // f
        if f > 1:
            q_src, kp_src, k_src, vp_src, v_src, o_dst, l_dst = split_refs
            for rf in range(f):
                for src, dst, n in ((q_ref, q_src, qlen), (k_ref, k_src, qlen), (v_ref, v_src, qlen),
                                    (kp_ref, kp_src, sublen), (vp_ref, vp_src, sublen)):
                    dst[pl.ds(rf * n, n), :] = _rows(src, rf, n, f)
        else:
            q_src, kp_src, k_src, vp_src, v_src, o_dst, l_dst = (
                q_ref, kp_ref, k_ref, vp_ref, v_ref, og, lg)

        def key_rows(prev_src, src, c, rf, r2):
            if c == 0:
                return jnp.concatenate([_rows(prev_src, rf * sublen + r2, BLK, d2),
                                        _rows(src, rf * qlen + r2, BLK, d2)], axis=0)
            return _rows(src, rf * qlen + (c - 1) * sublen + r2, 2 * BLK, d2)

        units, starts = [], []
        for idx in range(ATTN_SPAN // BLK):
            c, r = divmod(idx, d)
            rf, r2 = r % f, r // f
            qstart = rf * qlen + c * sublen + r2
            units.append((_rows(q_src, qstart, BLK, d2), key_rows(kp_src, k_src, c, rf, r2),
                          key_rows(vp_src, v_src, c, rf, r2), bias_ref[1 if c == 0 else 0]))
            starts.append(qstart)
        for qstart, (o, lse) in zip(starts, _attn_batch(units)):
            rows = pl.ds(qstart, BLK) if d2 == 1 else pl.ds(qstart, BLK, stride=d2)
            o_dst[rows, :] = o
            l_dst[rows, :] = jnp.broadcast_to(lse, (BLK, HEAD_DIM))
        if f > 1:
            for rf in range(f):
                og[pl.ds(rf, qlen, stride=f), :] = o_dst[pl.ds(rf * qlen, qlen), :]
                lg[pl.ds(rf, qlen, stride=f), :] = l_dst[pl.ds(rf * qlen, qlen), :]

    lses = [l[...] for l in l_scr]
    mx = functools.reduce(jnp.maximum, lses)
    ws = [jnp.exp(l - mx) for l in lses]
    num = functools.reduce(lambda a, b: a + b, [w * o[...] for w, o in zip(ws, o_scr)])
    den = functools.reduce(lambda a, b: a + b, ws)
    o_ref[...] = (num * (1.0 / den)).astype(o_ref.dtype)


def _dilated_attention(qk, vm):
    s = qk.shape[0]
    assert s % ATTN_SPAN == 0
    hpg = HEADS_PER_GROUP
    in_specs, args = [], []
    for g, (_, d) in enumerate(ATTN_GROUPS):
        sub = d * BLK
        ratio = ATTN_SPAN // sub
        qc, kc, vc = g * hpg, ATTN_HEADS + g * hpg, g * hpg
        cur = lambda col: pl.BlockSpec((ATTN_SPAN, HEAD_DIM), lambda i, h, col=col: (i, col + h))
        prev = lambda col, ratio=ratio, sub=sub: pl.BlockSpec(
            (sub, HEAD_DIM), lambda i, h, col=col, ratio=ratio: (jnp.maximum(i * ratio - 1, 0), col + h))
        in_specs += [cur(qc), cur(kc), cur(vc), prev(kc), prev(vc)]
        args += [qk, qk, vm, qk, vm]
    ngrp = len(ATTN_GROUPS)
    return pl.pallas_call(
        _attn_kernel,
        grid=(s // ATTN_SPAN, hpg),
        in_specs=in_specs,
        out_specs=pl.BlockSpec((ATTN_SPAN, HEAD_DIM), lambda i, h: (i, h)),
        out_shape=jax.ShapeDtypeStruct((s, ATTN_OUT_W), BF16),
        scratch_shapes=([pltpu.VMEM((ATTN_SPAN, HEAD_DIM), F32)] * (2 * ngrp)
                        + [pltpu.VMEM((ATTN_SPAN, HEAD_DIM), F32)] * 7
                        + [pltpu.VMEM((2, BLK, 2 * BLK), F32)]),
        compiler_params=_params("parallel", "arbitrary"),
        name="dilated_attention",
    )(*args)


def _ffn_down_rows_kernel(a_ref, w_ref, x_ref, *rest, final_norm):
    if final_norm:
        g_ref, o_ref = rest
    else:
        (o_ref,) = rest
    width = o_ref.shape[1]
    ssq = jnp.zeros((o_ref.shape[0], 1), F32)
    for c in range(width // V7X_MXU_COLS):
        cols = slice(c * V7X_MXU_COLS, (c + 1) * V7X_MXU_COLS)
        y = x_ref[:, cols] + jnp.dot(a_ref[...], w_ref[:, cols], preferred_element_type=F32)
        o_ref[:, cols] = y
        if final_norm:
            ssq = ssq + jnp.sum(y * y, axis=-1, keepdims=True)
    if final_norm:
        scale = lax.rsqrt(ssq * (1.0 / width) + NORM_EPS)
        for c in range(width // V7X_MXU_COLS):
            cols = slice(c * V7X_MXU_COLS, (c + 1) * V7X_MXU_COLS)
            o_ref[:, cols] = o_ref[:, cols] * scale * g_ref[:, cols]


def _ffn_down_rows(a, w_bf, x, final_g=None):
    s, f = a.shape
    width = x.shape[1]
    tm = ROW_STEP
    assert w_bf.shape == (f, width) and s % tm == 0
    row = lambda cols: pl.BlockSpec((tm, cols), lambda i: (i, 0))
    in_specs = [row(f), _resident_spec(w_bf.shape), row(width)]
    args = [a, w_bf, x]
    if final_g is not None:
        in_specs.append(_gain_spec(0, width))
        args.append(final_g.reshape(1, 1, width))
    return pl.pallas_call(
        functools.partial(_ffn_down_rows_kernel, final_norm=final_g is not None),
        grid=(s // tm,),
        in_specs=in_specs,
        out_specs=row(width),
        out_shape=jax.ShapeDtypeStruct((s, width), F32),
        compiler_params=_params("parallel"),
        name="ffn_down_rows",
    )(*args)


def _ffn_gate_up_kernel(h_ref, wg_ref, wu_ref, *rest, n_jobs):
    o_ref = rest[n_jobs]
    _run_cast_jobs(rest[:n_jobs], rest[n_jobs + 1:])
    for c in range(o_ref.shape[1] // V7X_MXU_COLS):
        cols = slice(c * V7X_MXU_COLS, (c + 1) * V7X_MXU_COLS)
        gate = jnp.dot(h_ref[...], wg_ref[:, cols].astype(BF16), preferred_element_type=F32)
        up = jnp.dot(h_ref[...], wu_ref[:, cols].astype(BF16), preferred_element_type=F32)
        o_ref[:, cols] = (gate * (1.0 / (1.0 + jnp.exp(-gate))) * up).astype(o_ref.dtype)


def _ffn_gate_up(h, w_gate, w_up, layer, cast_weights=()):
    s, k = h.shape
    f = w_gate.shape[2]
    tm, tf = GATE_UP_ROWS, GATE_UP_COLS
    nj = f // tf
    wspec = pl.BlockSpec((None, k, tf), lambda i, j: (layer, 0, j))
    job_in, job_args, job_out, job_shape = _cast_jobs(cast_weights, (s // tm) * nj,
                                                      lambda i, j: i * nj + j)
    return pl.pallas_call(
        functools.partial(_ffn_gate_up_kernel, n_jobs=len(job_args)),
        grid=(s // tm, nj),
        in_specs=[pl.BlockSpec((tm, k), lambda i, j: (i, 0)), wspec, wspec] + job_in,
        out_specs=[pl.BlockSpec((tm, tf), lambda i, j: (i, j))] + job_out,
        out_shape=[jax.ShapeDtypeStruct((s, f), BF16)] + job_shape,
        compiler_params=_params("arbitrary", "arbitrary"),
        name="ffn_gate_up",
    )(h, w_gate, w_up, *job_args)


def kernel(x, mem, positions, mix_norm, mem_norm, w_mem_kv, ffn_norm, w_gate, w_up, w_down,
           attn_w_in, attn_w_out, sgu_w_in, sgu_ln_g, sgu_ln_b, sgu_w_spatial, sgu_b_spatial,
           sgu_w_out, final_norm):
    b, s, d = x.shape
    assert (b, s, d) == (1, SEQ, D_MODEL) and mem.shape == (1, MEM_LEN, D_MODEL)
    xs = x.reshape(s, d)
    mems = mem.reshape(MEM_LEN, d)

    inv_freq = ROPE_THETA ** (-jnp.arange(ROT_HALF, dtype=F32) / ROT_HALF)
    invf = jnp.tile(jnp.concatenate([inv_freq, inv_freq]), ROPE_PACK)
    pos_col, invf = positions.reshape(s, 1), invf.reshape(1, HEAD_DIM)

    def mixer_weights(layer):
        return (attn_w_in, attn_w_out) if layer % 2 == 0 else (sgu_w_in, sgu_w_out)

    kv, w_in_bf = _mem_kv(mems, mem_norm, w_mem_kv, [(attn_w_in, 0)])
    w_out_bf = None
    for i in range(DEPTH):
        j = i // 2
        if i % 2 == 0:
            casts = [(attn_w_out, j)] if w_out_bf is None else []
            qk, vm, *cast = _attn_in_proj(xs, mix_norm, i, w_in_bf, pos_col, invf, casts)
            w_out_bf = cast[0] if cast else w_out_bf
            mix = _dilated_attention(qk, vm)
            xs, hn = _out_proj(mix, vm, ATTN_W // MEM_W, kv, i, w_out_bf, xs, ffn_norm, i)
        else:
            xs, hn = _sgu_mixer(xs, mix_norm, i, w_in_bf, sgu_ln_g, sgu_ln_b, sgu_w_spatial,
                                jnp.swapaxes(sgu_b_spatial, 1, 2), j, kv, i, w_out_bf, ffn_norm, i)
        last = i + 1 == DEPTH
        casts = [(w_down, i)] + ([] if last else [(w, (i + 1) // 2) for w in mixer_weights(i + 1)])
        act, w_down_bf, *next_bf = _ffn_gate_up(hn, w_gate, w_up, i, casts)
        w_in_bf, w_out_bf = next_bf if next_bf else (None, None)
        xs = _ffn_down_rows(act, w_down_bf, xs, final_g=final_norm if last else None)
    return xs.reshape(b, s, d)
```

```python
import functools
import math

import jax
import jax.numpy as jnp
from jax import lax
from jax.experimental import pallas as pl
from jax.experimental.pallas import tpu as pltpu

D_MODEL = 2048
SEQ = 8192
DEPTH = 2
MEM_LEN = 256
HEAD_DIM = 128
MEM_HEADS = 4
MEM_W = MEM_HEADS * HEAD_DIM
ATTN_GROUPS = ((128, 1), (512, 4), (2048, 16))
ATTN_HEADS = 12
HEADS_PER_GROUP = 4
ATTN_W = ATTN_HEADS * HEAD_DIM
ATTN_OUT_W = HEADS_PER_GROUP * HEAD_DIM
BLK = 128
SGU_GROUPS = 12
SGU_W = SGU_GROUPS * HEAD_DIM
SGU_CHUNK = 128
ROT_DIM = HEAD_DIM // 4
ROT_HALF = ROT_DIM // 2
ROPE_THETA = 500000.0
NORM_EPS = 1e-6
LN_EPS = 1e-5
NEG_INF = -1e30
ATTN_SCALE = HEAD_DIM ** -0.5

N_BACK = BLK
assert all(w // d == N_BACK for w, d in ATTN_GROUPS)
ATTN_SPAN = max(d for _, d in ATTN_GROUPS) * BLK

V7X_VMEM_LIMIT_BYTES = 56 * 1024 * 1024
V7X_MXU_COLS = 256

ROW_STEP = 512
GATE_UP_ROWS, GATE_UP_COLS = 2048, 512
MEM_KV_COLS = 512

F32 = jnp.float32
BF16 = jnp.bfloat16


def _params(*semantics):
    return pltpu.CompilerParams(dimension_semantics=semantics,
                                vmem_limit_bytes=V7X_VMEM_LIMIT_BYTES)


def _rms_rows(x, g):
    ms = jnp.mean(x * x, axis=-1, keepdims=True)
    return x * lax.rsqrt(ms + NORM_EPS) * g


def _gain_spec(layer, k):
    return pl.BlockSpec((None, 1, k), lambda *_: (layer, 0, 0))


def _gain_arg(g):
    return g.reshape(g.shape[0], 1, g.shape[1])


ROPE_PACK = HEAD_DIM // ROT_DIM


def _rope_tables(pos_ref, invf_ref, c_ref, s1_ref, s2_ref):
    n = pos_ref.shape[0] // ROPE_PACK
    lane = lax.broadcasted_iota(jnp.int32, (n, HEAD_DIM), 1)
    group = lane // ROT_DIM
    pos = jnp.zeros((n, HEAD_DIM), F32)
    for q in range(ROPE_PACK):
        pos = jnp.where(group == q, pos_ref[q * n:(q + 1) * n, :].astype(F32), pos)
    ang = pos * invf_ref[...]
    cos, sin = jnp.cos(ang), jnp.sin(ang)
    for q in range(ROPE_PACK):
        cq = cos if q == 0 else pltpu.roll(cos, HEAD_DIM - q * ROT_DIM, 1)
        sq = sin if q == 0 else pltpu.roll(sin, HEAD_DIM - q * ROT_DIM, 1)
        rows = slice(q * n, (q + 1) * n)
        c_ref[rows, :] = jnp.where(lane < ROT_DIM, cq, 1.0)
        s1_ref[rows, :] = jnp.where(lane < ROT_HALF, 0.0, jnp.where(lane < ROT_DIM, sq, 0.0))
        s2_ref[rows, :] = jnp.where(lane < ROT_HALF, -sq, 0.0)


BF16_SUBLANES = 16


def _cast_jobs(weights, n_steps, step_of):
    in_specs, args, out_specs, out_shape = [], [], [], []
    for w, layer in weights:
        _, k, n = w.shape
        slab = -(-(-(-k // n_steps)) // BF16_SUBLANES) * BF16_SUBLANES
        while k % slab:
            slab += BF16_SUBLANES
        n_slabs = k // slab
        in_specs.append(pl.BlockSpec(
            (None, slab, n), lambda *g, layer=layer, n_slabs=n_slabs:
            (layer, jnp.minimum(step_of(*g), n_slabs - 1), 0)))
        out_specs.append(pl.BlockSpec(
            (slab, n), lambda *g, n_slabs=n_slabs: (jnp.minimum(step_of(*g), n_slabs - 1), 0)))
        args.append(w)
        out_shape.append(jax.ShapeDtypeStruct((k, n), BF16))
    return in_specs, args, out_specs, out_shape


def _run_cast_jobs(in_refs, out_refs):
    for src, dst in zip(in_refs, out_refs):
        dst[...] = src[...].astype(BF16)


def _mem_kv_kernel(x_ref, g_ref, w_ref, *rest, n_jobs):
    o_ref = rest[n_jobs]
    _run_cast_jobs(rest[:n_jobs], rest[n_jobs + 1:])
    h = _rms_rows(x_ref[...], g_ref[...]).astype(BF16)
    o_ref[...] = jnp.dot(h, w_ref[...].astype(BF16), preferred_element_type=F32)


def _mem_kv(x, g, w, cast_weights=()):
    m, k = x.shape
    layers, _, n = w.shape
    tn = MEM_KV_COLS
    assert n % tn == 0
    nt = n // tn
    job_in, job_args, job_out, job_shape = _cast_jobs(cast_weights, layers * nt, lambda l, j: l * nt + j)
    return pl.pallas_call(
        functools.partial(_mem_kv_kernel, n_jobs=len(job_args)),
        grid=(layers, nt),
        in_specs=[pl.BlockSpec((m, k), lambda l, j: (0, 0)),
                  pl.BlockSpec((None, 1, k), lambda l, j: (l, 0, 0)),
                  pl.BlockSpec((None, k, tn), lambda l, j: (l, 0, j))] + job_in,
        out_specs=[pl.BlockSpec((None, m, tn), lambda l, j: (l, 0, j))] + job_out,
        out_shape=[jax.ShapeDtypeStruct((layers, m, n), F32)] + job_shape,
        compiler_params=_params("arbitrary", "arbitrary"),
        name="mem_kv",
    )(x, _gain_arg(g), w, *job_args)


def _resident_spec(shape):
    return pl.BlockSpec(shape, lambda *_: (0,) * len(shape), pipeline_mode=pl.Buffered(1))


def _gelu_tanh(x):
    return x * (0.5 * (1.0 + jnp.tanh(math.sqrt(2.0 / math.pi) * (x + 0.044715 * (x * x * x)))))


def _pass(h_ref, w_ref, col):
    return jnp.dot(h_ref[...], w_ref[:, col:col + V7X_MXU_COLS], preferred_element_type=F32)


def _heads(acc):
    return [acc[:, j * HEAD_DIM:(j + 1) * HEAD_DIM] for j in range(acc.shape[1] // HEAD_DIM)]


def _attn_in_proj_kernel(x_ref, g_ref, w_ref, pos_ref, invf_ref, *rest, n_jobs):
    qk_ref, vm_ref = rest[n_jobs:n_jobs + 2]
    h_ref, c_ref, s1_ref, s2_ref = rest[2 * n_jobs + 2:]
    _run_cast_jobs(rest[:n_jobs], rest[n_jobs + 2:2 * n_jobs + 2])
    h_ref[...] = _rms_rows(x_ref[...], g_ref[...]).astype(BF16)
    _rope_tables(pos_ref, invf_ref, c_ref, s1_ref, s2_ref)
    for c in range(2 * ATTN_W // V7X_MXU_COLS):
        for j, t in enumerate(_heads(_pass(h_ref, w_ref, c * V7X_MXU_COLS))):
            lo = c * V7X_MXU_COLS + j * HEAD_DIM
            qk_ref[:, lo:lo + HEAD_DIM] = (t * c_ref[...] + pltpu.roll(t, ROT_HALF, 1) * s1_ref[...]
                                           + pltpu.roll(t, HEAD_DIM - ROT_HALF, 1) * s2_ref[...])
    for c in range((ATTN_W + MEM_W) // V7X_MXU_COLS):
        cols = slice(c * V7X_MXU_COLS, (c + 1) * V7X_MXU_COLS)
        vm_ref[:, cols] = _pass(h_ref, w_ref, 2 * ATTN_W + c * V7X_MXU_COLS)


def _attn_in_proj(x, g, g_layer, w_bf, pos_col, invf, cast_weights=()):
    s, k = x.shape
    tm = ROW_STEP
    assert w_bf.shape == (k, 3 * ATTN_W + MEM_W)
    row = lambda cols: pl.BlockSpec((tm, cols), lambda i: (i, 0))
    job_in, job_args, job_out, job_shape = _cast_jobs(cast_weights, s // tm, lambda i: i)
    return pl.pallas_call(
        functools.partial(_attn_in_proj_kernel, n_jobs=len(job_args)),
        grid=(s // tm,),
        in_specs=[row(k), _gain_spec(g_layer, k), _resident_spec(w_bf.shape), row(1),
                  pl.BlockSpec((1, HEAD_DIM), lambda i: (0, 0))] + job_in,
        out_specs=[row(2 * ATTN_W), row(ATTN_W + MEM_W)] + job_out,
        out_shape=[jax.ShapeDtypeStruct((s, 2 * ATTN_W), F32),
                   jax.ShapeDtypeStruct((s, ATTN_W + MEM_W), F32)] + job_shape,
        scratch_shapes=[pltpu.VMEM((tm, k), BF16)] + [pltpu.VMEM((tm, HEAD_DIM), F32)] * 3,
        compiler_params=_params("arbitrary"),
        name="attn_in_proj",
    )(x, _gain_arg(g), w_bf, pos_col, invf, *job_args)


def _memory_scores(q_heads, kv_ref):
    probs = []
    for h, q in enumerate(q_heads):
        sl = slice(h * HEAD_DIM, (h + 1) * HEAD_DIM)
        s = lax.dot_general(q.astype(BF16), kv_ref[:, sl].astype(BF16), (((1,), (1,)), ((), ())),
                            preferred_element_type=F32) * ATTN_SCALE
        p = jnp.exp(s - jnp.max(s, axis=-1, keepdims=True))
        probs.append((p.astype(BF16), jnp.sum(p, axis=-1, keepdims=True)))
    return probs


def _memory_values(probs, kv_ref):
    heads = []
    for h, (p, l) in enumerate(probs):
        v = kv_ref[:, MEM_W + h * HEAD_DIM:MEM_W + (h + 1) * HEAD_DIM].astype(BF16)
        heads.append((jnp.dot(p, v, preferred_element_type=F32) * (1.0 / l)).astype(BF16))
    return jnp.concatenate(heads, axis=1)


OUT_PROJ_COLS = 512


def _project_out(mix, probs, kv_ref, w_ref, x_ref, g_ref, o_ref, hn_ref):
    ka = mix.shape[1]
    width = o_ref.shape[1]
    chunks = [slice(c * OUT_PROJ_COLS, (c + 1) * OUT_PROJ_COLS) for c in range(width // OUT_PROJ_COLS)]
    for cols in chunks:
        o_ref[:, cols] = x_ref[:, cols] + jnp.dot(mix, w_ref[:ka, cols], preferred_element_type=F32)
    b = _memory_values(probs, kv_ref)
    ssq = jnp.zeros((o_ref.shape[0], 1), F32)
    for cols in chunks:
        y = o_ref[:, cols] + jnp.dot(b, w_ref[ka:, cols], preferred_element_type=F32)
        o_ref[:, cols] = y
        ssq = ssq + jnp.sum(y * y, axis=-1, keepdims=True)
    scale = lax.rsqrt(ssq * (1.0 / width) + NORM_EPS)
    for cols in chunks:
        hn_ref[:, cols] = (o_ref[:, cols] * scale * g_ref[:, cols]).astype(BF16)


def _sgu_mixer_kernel(x_ref, g_ref, w_ref, lng_ref, lnb_ref, ws_ref, bt_ref, kv_ref, wo_ref, gf_ref,
                      o_ref, hn_ref, h_ref, v_scr, vn_scr, mix_scr, wsb_scr):
    @pl.when(pl.program_id(0) == 0)
    def _():
        t = lax.broadcasted_iota(jnp.int32, (SGU_CHUNK, SGU_CHUNK), 0)
        s = lax.broadcasted_iota(jnp.int32, (SGU_CHUNK, SGU_CHUNK), 1)
        for g in range(SGU_GROUPS):
            wsb_scr[g] = jnp.where(t >= s, ws_ref[g], 0.0).astype(BF16)

    h_ref[...] = _rms_rows(x_ref[...], g_ref[...]).astype(BF16)
    for c in range(SGU_W // V7X_MXU_COLS):
        cols = slice(c * V7X_MXU_COLS, (c + 1) * V7X_MXU_COLS)
        v_scr[:, cols] = _gelu_tanh(_pass(h_ref, w_ref, SGU_W + c * V7X_MXU_COLS))
    q_heads = []
    for c in range(MEM_W // V7X_MXU_COLS):
        q_heads += _heads(_pass(h_ref, w_ref, 2 * SGU_W + c * V7X_MXU_COLS))
    probs = _memory_scores(q_heads, kv_ref)
    v = v_scr[...]
    mu = jnp.mean(v, axis=-1, keepdims=True)
    vc = v - mu
    var = jnp.mean(vc * vc, axis=-1, keepdims=True)
    vn_scr[...] = (vc * lax.rsqrt(var + LN_EPS) * lng_ref[...] + lnb_ref[...]).astype(BF16)

    bt = bt_ref[...]
    n_chunks = mix_scr.shape[0] // SGU_CHUNK
    for c in range(SGU_W // V7X_MXU_COLS):
        for j, u in enumerate(_heads(_gelu_tanh(_pass(h_ref, w_ref, c * V7X_MXU_COLS)))):
            g = c * (V7X_MXU_COLS // HEAD_DIM) + j
            cols = slice(g * HEAD_DIM, (g + 1) * HEAD_DIM)
            vg = jnp.concatenate([vn_scr[r * SGU_CHUNK:(r + 1) * SGU_CHUNK, cols]
                                  for r in range(n_chunks)], axis=1)
            mixed = jnp.dot(wsb_scr[g], vg, preferred_element_type=F32) + bt[:, g:g + 1]
            for r in range(n_chunks):
                rows = slice(r * SGU_CHUNK, (r + 1) * SGU_CHUNK)
                mix_scr[rows, cols] = (u[rows, :] * mixed[:, r * HEAD_DIM:(r + 1) * HEAD_DIM]
                                       ).astype(mix_scr.dtype)

    _project_out(mix_scr[...], probs, kv_ref, wo_ref, x_ref, gf_ref, o_ref, hn_ref)


def _sgu_mixer(x, g, g_layer, w_bf, ln_g, ln_b, w_spatial, b_spatial_t, layer, kv, kv_layer,
               w_out_bf, g_ffn, g_ffn_layer):
    s, k = x.shape
    tm = ROW_STEP
    assert w_bf.shape == (k, 2 * SGU_W + MEM_W) and tm % SGU_CHUNK == 0
    assert w_out_bf.shape == (SGU_W + MEM_W, k)
    row = lambda cols: pl.BlockSpec((tm, cols), lambda i: (i, 0))
    return pl.pallas_call(
        _sgu_mixer_kernel,
        grid=(s // tm,),
        in_specs=[row(k), _gain_spec(g_layer, k), _resident_spec(w_bf.shape),
                  _gain_spec(layer, SGU_W), _gain_spec(layer, SGU_W),
                  pl.BlockSpec((None, SGU_GROUPS, SGU_CHUNK, SGU_CHUNK), lambda i: (layer, 0, 0, 0)),
                  pl.BlockSpec((None, SGU_CHUNK, SGU_GROUPS), lambda i: (layer, 0, 0)),
                  pl.BlockSpec((None, MEM_LEN, 2 * MEM_W), lambda i: (kv_layer, 0, 0)),
                  _resident_spec(w_out_bf.shape), _gain_spec(g_ffn_layer, k)],
        out_specs=[row(k), row(k)],
        out_shape=[jax.ShapeDtypeStruct((s, k), F32), jax.ShapeDtypeStruct((s, k), BF16)],
        scratch_shapes=[pltpu.VMEM((tm, k), BF16), pltpu.VMEM((tm, SGU_W), F32),
                        pltpu.VMEM((tm, SGU_W), BF16), pltpu.VMEM((tm, SGU_W), BF16),
                        pltpu.VMEM((SGU_GROUPS, SGU_CHUNK, SGU_CHUNK), BF16)],
        compiler_params=_params("arbitrary"),
        name="sgu_mixer",
    )(x, _gain_arg(g), w_bf, _gain_arg(ln_g), _gain_arg(ln_b), w_spatial, b_spatial_t, kv,
      w_out_bf, _gain_arg(g_ffn))


def _out_proj_kernel(a_ref, q_ref, kv_ref, w_ref, x_ref, g_ref, o_ref, hn_ref):
    probs = _memory_scores([q_ref[:, h * HEAD_DIM:(h + 1) * HEAD_DIM] for h in range(MEM_HEADS)],
                           kv_ref)
    _project_out(a_ref[...], probs, kv_ref, w_ref, x_ref, g_ref, o_ref, hn_ref)


def _out_proj(a, q_src, q_col_block, kv, kv_layer, w_bf, x, g, g_layer):
    s, ka = a.shape
    width = x.shape[1]
    tm = ROW_STEP
    assert w_bf.shape == (ka + MEM_W, width)
    row = lambda cols: pl.BlockSpec((tm, cols), lambda i: (i, 0))
    return pl.pallas_call(
        _out_proj_kernel,
        grid=(s // tm,),
        in_specs=[row(ka), pl.BlockSpec((tm, MEM_W), lambda i: (i, q_col_block)),
                  pl.BlockSpec((None, MEM_LEN, 2 * MEM_W), lambda i: (kv_layer, 0, 0)),
                  _resident_spec((ka + MEM_W, width)), row(width), _gain_spec(g_layer, width)],
        out_specs=[row(width), row(width)],
        out_shape=[jax.ShapeDtypeStruct((s, width), F32), jax.ShapeDtypeStruct((s, width), BF16)],
        compiler_params=_params("parallel"),
        name="out_proj",
    )(a, q_src, kv, w_bf, x, _gain_arg(g))


def _rows(ref, start, size, stride):
    if stride == 1:
        return ref[pl.ds(start, size), :]
    return ref[pl.ds(start, size, stride=stride), :]


V7X_FREE_SUBLANE_STRIDE = 4


def _presplit(d):
    return d // V7X_FREE_SUBLANE_STRIDE if d > V7X_FREE_SUBLANE_STRIDE else 1


def _attn_batch(units):
    scores = [lax.dot_general(q.astype(BF16), kc.astype(BF16), (((1,), (1,)), ((), ())),
                              preferred_element_type=F32) * ATTN_SCALE + bias
              for q, kc, _, bias in units]
    probs = []
    for s in scores:
        m = jnp.max(s, axis=-1, keepdims=True)
        p = jnp.exp(s - m)
        probs.append((p, m, jnp.sum(p, axis=-1, keepdims=True)))
    outs = []
    for (p, m, l), (_, _, vc, _) in zip(probs, units):
        o = jnp.dot(p.astype(BF16), vc.astype(BF16), preferred_element_type=F32)
        outs.append((o * (1.0 / l), m + jnp.log(l)))
    return outs


def _attn_kernel(*refs):
    ngrp = len(ATTN_GROUPS)
    in_refs = refs[:5 * ngrp]
    o_ref = refs[5 * ngrp]
    pos = 5 * ngrp + 1
    o_scr, l_scr = refs[pos:pos + ngrp], refs[pos + ngrp:pos + 2 * ngrp]
    split_refs = refs[pos + 2 * ngrp:pos + 2 * ngrp + 7]
    bias_ref = refs[-1]

    qi = lax.broadcasted_iota(jnp.int32, (BLK, 2 * BLK), 0)
    ki = lax.broadcasted_iota(jnp.int32, (BLK, 2 * BLK), 1)
    lo = jnp.where(pl.program_id(0) == 0, BLK, 0)
    bias_ref[0] = jnp.where((ki >= qi) & (ki <= qi + N_BACK), 0.0, NEG_INF)
    bias_ref[1] = jnp.where((ki >= jnp.maximum(qi, lo)) & (ki <= qi + N_BACK), 0.0, NEG_INF)

    for g, (_, d) in enumerate(ATTN_GROUPS):
        q_ref, k_ref, v_ref, kp_ref, vp_ref = in_refs[5 * g:5 * g + 5]
        og, lg = o_scr[g], l_scr[g]
        sub = d * BLK
        f = _presplit(d)
        d2 = d // f
        qlen, sublen = ATTN_SPAN // f, sub // f
        if f > 1:
            q_src, kp_src, k_src, vp_src, v_src, o_dst, l_dst = split_refs
            for rf in range(f):
                for src, dst, n in ((q_ref, q_src, qlen), (k_ref, k_src, qlen), (v_ref, v_src, qlen),
                                    (kp_ref, kp_src, sublen), (vp_ref, vp_src, sublen)):
                    dst[pl.ds(rf * n, n), :] = _rows(src, rf, n, f)
        else:
            q_src, kp_src, k_src, vp_src, v_src, o_dst, l_dst = (
                q_ref, kp_ref, k_ref, vp_ref, v_ref, og, lg)

        def key_rows(prev_src, src, c, rf, r2):
            if c == 0:
                return jnp.concatenate([_rows(prev_src, rf * sublen + r2, BLK, d2),
                                        _rows(src, rf * qlen + r2, BLK, d2)], axis=0)
            return _rows(src, rf * qlen + (c - 1) * sublen + r2, 2 * BLK, d2)

        units, starts = [], []
        for idx in range(ATTN_SPAN // BLK):
            c, r = divmod(idx, d)
            rf, r2 = r % f, r // f
            qstart = rf * qlen + c * sublen + r2
            units.append((_rows(q_src, qstart, BLK, d2), key_rows(kp_src, k_src, c, rf, r2),
                          key_rows(vp_src, v_src, c, rf, r2), bias_ref[1 if c == 0 else 0]))
            starts.append(qstart)
        for qstart, (o, lse) in zip(starts, _attn_batch(units)):
            rows = pl.ds(qstart, BLK) if d2 == 1 else pl.ds(qstart, BLK, stride=d2)
            o_dst[rows, :] = o
            l_dst[rows, :] = jnp.broadcast_to(lse, (BLK, HEAD_DIM))
        if f > 1:
            for rf in range(f):
                og[pl.ds(rf, qlen, stride=f), :] = o_dst[pl.ds(rf * qlen, qlen), :]
                lg[pl.ds(rf, qlen, stride=f), :] = l_dst[pl.ds(rf * qlen, qlen), :]

    lses = [l[...] for l in l_scr]
    mx = functools.reduce(jnp.maximum, lses)
    ws = [jnp.exp(l - mx) for l in lses]
    num = functools.reduce(lambda a, b: a + b, [w * o[...] for w, o in zip(ws, o_scr)])
    den = functools.reduce(lambda a, b: a + b, ws)
    o_ref[...] = (num * (1.0 / den)).astype(o_ref.dtype)


def _dilated_attention(qk, vm):
    s = qk.shape[0]
    assert s % ATTN_SPAN == 0
    hpg = HEADS_PER_GROUP
    in_specs, args = [], []
    for g, (_, d) in enumerate(ATTN_GROUPS):
        sub = d * BLK
        ratio = ATTN_SPAN // sub
        qc, kc, vc = g * hpg, ATTN_HEADS + g * hpg, g * hpg
        cur = lambda col: pl.BlockSpec((ATTN_SPAN, HEAD_DIM), lambda i, h, col=col: (i, col + h))
        prev = lambda col, ratio=ratio, sub=sub: pl.BlockSpec(
            (sub, HEAD_DIM), lambda i, h, col=col, ratio=ratio: (jnp.maximum(i * ratio - 1, 0), col + h))
        in_specs += [cur(qc), cur(kc), cur(vc), prev(kc), prev(vc)]
        args += [qk, qk, vm, qk, vm]
    ngrp = len(ATTN_GROUPS)
    return pl.pallas_call(
        _attn_kernel,
        grid=(s // ATTN_SPAN, hpg),
        in_specs=in_specs,
        out_specs=pl.BlockSpec((ATTN_SPAN, HEAD_DIM), lambda i, h: (i, h)),
        out_shape=jax.ShapeDtypeStruct((s, ATTN_OUT_W), BF16),
        scratch_shapes=([pltpu.VMEM((ATTN_SPAN, HEAD_DIM), F32)] * (2 * ngrp)
                        + [pltpu.VMEM((ATTN_SPAN, HEAD_DIM), F32)] * 7
                        + [pltpu.VMEM((2, BLK, 2 * BLK), F32)]),
        compiler_params=_params("parallel", "arbitrary"),
        name="dilated_attention",
    )(*args)


def _weight_pass_copy(w_hbm, w_ref, sems, c):
    cols = pl.ds(c * V7X_MXU_COLS, V7X_MXU_COLS)
    return pltpu.make_async_copy(w_hbm.at[:, cols], w_ref.at[:, cols], sems.at[c])


def _ffn_down_rows_kernel(a_ref, w_hbm, x_ref, *rest, final_norm):
    if final_norm:
        g_ref, o_ref, w_ref, sems, ssq_ref = rest
    else:
        o_ref, w_ref, sems, ssq_ref = rest
    width = o_ref.shape[1]
    n_pass = width // V7X_MXU_COLS
    t = pl.program_id(0)

    def one_pass(c):
        cols = slice(c * V7X_MXU_COLS, (c + 1) * V7X_MXU_COLS)
        y = x_ref[:, cols] + jnp.dot(a_ref[...], w_ref[:, cols], preferred_element_type=F32)
        o_ref[:, cols] = y
        return jnp.sum(y * y, axis=-1, keepdims=True) if final_norm else None

    def normalise(ssq):
        scale = lax.rsqrt(ssq * (1.0 / width) + NORM_EPS)
        for c in range(n_pass):
            cols = slice(c * V7X_MXU_COLS, (c + 1) * V7X_MXU_COLS)
            o_ref[:, cols] = o_ref[:, cols] * scale * g_ref[:, cols]

    @pl.when(t == 0)
    def _():
        for c in range(n_pass):
            _weight_pass_copy(w_hbm, w_ref, sems, c).start()

    for c in range(n_pass):
        @pl.when(t == c)
        def _(c=c):
            _weight_pass_copy(w_hbm, w_ref, sems, c).wait()
            part = one_pass(c)
            if final_norm:
                ssq = part if c == 0 else ssq_ref[...] + part
                if c == n_pass - 1:
                    normalise(ssq)
                else:
                    ssq_ref[...] = ssq

    @pl.when(t >= n_pass)
    def _():
        ssq = None
        for c in range(n_pass):
            part = one_pass(c)
            if final_norm:
                ssq = part if c == 0 else ssq + part
        if final_norm:
            normalise(ssq)


def _ffn_down_rows(a, w_bf, x, final_g=None):
    s, f = a.shape
    width = x.shape[1]
    tm = ROW_STEP
    n_pass = width // V7X_MXU_COLS
    assert w_bf.shape == (f, width) and s % tm == 0 and width % V7X_MXU_COLS == 0
    row = lambda cols: pl.BlockSpec((tm, cols), lambda t: (jnp.maximum(t - (n_pass - 1), 0), 0))
    in_specs = [row(f), pl.BlockSpec(memory_space=pl.ANY), row(width)]
    args = [a, w_bf, x]
    if final_g is not None:
        in_specs.append(_gain_spec(0, width))
        args.append(final_g.reshape(1, 1, width))
    return pl.pallas_call(
        functools.partial(_ffn_down_rows_kernel, final_norm=final_g is not None),
        grid=(s // tm + n_pass - 1,),
        in_specs=in_specs,
        out_specs=row(width),
        out_shape=jax.ShapeDtypeStruct((s, width), F32),
        scratch_shapes=[pltpu.VMEM((f, width), BF16), pltpu.SemaphoreType.DMA((n_pass,)),
                        pltpu.VMEM((tm, 1), F32)],
        compiler_params=_params("arbitrary"),
        name="ffn_down_rows",
    )(*args)


def _ffn_gate_up_kernel(h_ref, wg_ref, wu_ref, *rest, n_jobs):
    o_ref = rest[n_jobs]
    _run_cast_jobs(rest[:n_jobs], rest[n_jobs + 1:])
    for c in range(o_ref.shape[1] // V7X_MXU_COLS):
        cols = slice(c * V7X_MXU_COLS, (c + 1) * V7X_MXU_COLS)
        gate = jnp.dot(h_ref[...], wg_ref[:, cols].astype(BF16), preferred_element_type=F32)
        up = jnp.dot(h_ref[...], wu_ref[:, cols].astype(BF16), preferred_element_type=F32)
        o_ref[:, cols] = (gate * (1.0 / (1.0 + jnp.exp(-gate))) * up).astype(o_ref.dtype)


def _ffn_gate_up(h, w_gate, w_up, layer, cast_weights=()):
    s, k = h.shape
    f = w_gate.shape[2]
    tm, tf = GATE_UP_ROWS, GATE_UP_COLS
    nj = f // tf
    wspec = pl.BlockSpec((None, k, tf), lambda i, j: (layer, 0, j))
    job_in, job_args, job_out, job_shape = _cast_jobs(cast_weights, (s // tm) * nj,
                                                      lambda i, j: i * nj + j)
    return pl.pallas_call(
        functools.partial(_ffn_gate_up_kernel, n_jobs=len(job_args)),
        grid=(s // tm, nj),
        in_specs=[pl.BlockSpec((tm, k), lambda i, j: (i, 0)), wspec, wspec] + job_in,
        out_specs=[pl.BlockSpec((tm, tf), lambda i, j: (i, j))] + job_out,
        out_shape=[jax.ShapeDtypeStruct((s, f), BF16)] + job_shape,
        compiler_params=_params("arbitrary", "arbitrary"),
        name="ffn_gate_up",
    )(h, w_gate, w_up, *job_args)


def kernel(x, mem, positions, mix_norm, mem_norm, w_mem_kv, ffn_norm, w_gate, w_up, w_down,
           attn_w_in, attn_w_out, sgu_w_in, sgu_ln_g, sgu_ln_b, sgu_w_spatial, sgu_b_spatial,
           sgu_w_out, final_norm):
    b, s, d = x.shape
    assert (b, s, d) == (1, SEQ, D_MODEL) and mem.shape == (1, MEM_LEN, D_MODEL)
    xs = x.reshape(s, d)
    mems = mem.reshape(MEM_LEN, d)

    inv_freq = ROPE_THETA ** (-jnp.arange(ROT_HALF, dtype=F32) / ROT_HALF)
    invf = jnp.tile(jnp.concatenate([inv_freq, inv_freq]), ROPE_PACK)
    pos_col, invf = positions.reshape(s, 1), invf.reshape(1, HEAD_DIM)

    def mixer_weights(layer):
        return (attn_w_in, attn_w_out) if layer % 2 == 0 else (sgu_w_in, sgu_w_out)

    kv, w_in_bf = _mem_kv(mems, mem_norm, w_mem_kv, [(attn_w_in, 0)])
    w_out_bf = None
    for i in range(DEPTH):
        j = i // 2
        if i % 2 == 0:
            casts = [(attn_w_out, j)] if w_out_bf is None else []
            qk, vm, *cast = _attn_in_proj(xs, mix_norm, i, w_in_bf, pos_col, invf, casts)
            w_out_bf = cast[0] if cast else w_out_bf
            mix = _dilated_attention(qk, vm)
            xs, hn = _out_proj(mix, vm, ATTN_W // MEM_W, kv, i, w_out_bf, xs, ffn_norm, i)
        else:
            xs, hn = _sgu_mixer(xs, mix_norm, i, w_in_bf, sgu_ln_g, sgu_ln_b, sgu_w_spatial,
                                jnp.swapaxes(sgu_b_spatial, 1, 2), j, kv, i, w_out_bf, ffn_norm, i)
        last = i + 1 == DEPTH
        casts = [(w_down, i)] + ([] if last else [(w, (i + 1) // 2) for w in mixer_weights(i + 1)])
        act, w_down_bf, *next_bf = _ffn_gate_up(hn, w_gate, w_up, i, casts)
        w_in_bf, w_out_bf = next_bf if next_bf else (None, None)
        xs = _ffn_down_rows(act, w_down_bf, xs, final_g=final_norm if last else None)
    return xs.reshape(b, s, d)
```

```python
import functools
import math

import jax
import jax.numpy as jnp
from jax import lax
from jax.experimental import pallas as pl
from jax.experimental.pallas import tpu as pltpu

D_MODEL = 2048
SEQ = 8192
DEPTH = 2
MEM_LEN = 256
HEAD_DIM = 128
MEM_HEADS = 4
MEM_W = MEM_HEADS * HEAD_DIM
ATTN_GROUPS = ((128, 1), (512, 4), (2048, 16))
ATTN_HEADS = 12
HEADS_PER_GROUP = 4
ATTN_W = ATTN_HEADS * HEAD_DIM
ATTN_OUT_W = HEADS_PER_GROUP * HEAD_DIM
BLK = 128
SGU_GROUPS = 12
SGU_W = SGU_GROUPS * HEAD_DIM
SGU_CHUNK = 128
ROT_DIM = HEAD_DIM // 4
ROT_HALF = ROT_DIM // 2
ROPE_THETA = 500000.0
NORM_EPS = 1e-6
LN_EPS = 1e-5
NEG_INF = -1e30
ATTN_SCALE_LOG2 = HEAD_DIM ** -0.5 * math.log2(math.e)

N_BACK = BLK
assert all(w // d == N_BACK for w, d in ATTN_GROUPS)
ATTN_SPAN = max(d for _, d in ATTN_GROUPS) * BLK

V7X_VMEM_LIMIT_BYTES = 56 * 1024 * 1024
V7X_MXU_COLS = 256

ROW_STEP = 512
GATE_UP_ROWS, GATE_UP_COLS = 2048, 512
MEM_KV_COLS = 512

F32 = jnp.float32
BF16 = jnp.bfloat16


def _params(*semantics):
    return pltpu.CompilerParams(dimension_semantics=semantics,
                                vmem_limit_bytes=V7X_VMEM_LIMIT_BYTES)


def _rms_rows(x, g):
    ms = jnp.mean(x * x, axis=-1, keepdims=True)
    return x * lax.rsqrt(ms + NORM_EPS) * g


def _gain_spec(layer, k):
    return pl.BlockSpec((None, 1, k), lambda *_: (layer, 0, 0))


def _gain_arg(g):
    return g.reshape(g.shape[0], 1, g.shape[1])


ROPE_PACK = HEAD_DIM // ROT_DIM


def _rope_tables(pos_ref, invf_ref, c_ref, s1_ref, s2_ref):
    n = pos_ref.shape[0] // ROPE_PACK
    lane = lax.broadcasted_iota(jnp.int32, (n, HEAD_DIM), 1)
    group = lane // ROT_DIM
    pos = jnp.zeros((n, HEAD_DIM), F32)
    for q in range(ROPE_PACK):
        pos = jnp.where(group == q, pos_ref[q * n:(q + 1) * n, :].astype(F32), pos)
    ang = pos * invf_ref[...]
    cos, sin = jnp.cos(ang), jnp.sin(ang)
    for q in range(ROPE_PACK):
        cq = cos if q == 0 else pltpu.roll(cos, HEAD_DIM - q * ROT_DIM, 1)
        sq = sin if q == 0 else pltpu.roll(sin, HEAD_DIM - q * ROT_DIM, 1)
        rows = slice(q * n, (q + 1) * n)
        c_ref[rows, :] = jnp.where(lane < ROT_DIM, cq, 1.0)
        s1_ref[rows, :] = jnp.where(lane < ROT_HALF, 0.0, jnp.where(lane < ROT_DIM, sq, 0.0))
        s2_ref[rows, :] = jnp.where(lane < ROT_HALF, -sq, 0.0)


BF16_SUBLANES = 16


def _cast_jobs(weights, n_steps, step_of):
    in_specs, args, out_specs, out_shape = [], [], [], []
    for w, layer in weights:
        _, k, n = w.shape
        slab = -(-(-(-k // n_steps)) // BF16_SUBLANES) * BF16_SUBLANES
        while k % slab:
            slab += BF16_SUBLANES
        n_slabs = k // slab
        in_specs.append(pl.BlockSpec(
            (None, slab, n), lambda *g, layer=layer, n_slabs=n_slabs:
            (layer, jnp.minimum(step_of(*g), n_slabs - 1), 0)))
        out_specs.append(pl.BlockSpec(
            (slab, n), lambda *g, n_slabs=n_slabs: (jnp.minimum(step_of(*g), n_slabs - 1), 0)))
        args.append(w)
        out_shape.append(jax.ShapeDtypeStruct((k, n), BF16))
    return in_specs, args, out_specs, out_shape


def _run_cast_jobs(in_refs, out_refs):
    for src, dst in zip(in_refs, out_refs):
        dst[...] = src[...].astype(BF16)


def _mem_kv_kernel(x_ref, g_ref, w_ref, *rest, n_jobs):
    o_ref = rest[n_jobs]
    _run_cast_jobs(rest[:n_jobs], rest[n_jobs + 1:])
    h = _rms_rows(x_ref[...], g_ref[...]).astype(BF16)
    o_ref[...] = jnp.dot(h, w_ref[...].astype(BF16), preferred_element_type=F32)


def _mem_kv(x, g, w, cast_weights=()):
    m, k = x.shape
    layers, _, n = w.shape
    tn = MEM_KV_COLS
    assert n % tn == 0
    nt = n // tn
    job_in, job_args, job_out, job_shape = _cast_jobs(cast_weights, layers * nt, lambda l, j: l * nt + j)
    return pl.pallas_call(
        functools.partial(_mem_kv_kernel, n_jobs=len(job_args)),
        grid=(layers, nt),
        in_specs=[pl.BlockSpec((m, k), lambda l, j: (0, 0)),
                  pl.BlockSpec((None, 1, k), lambda l, j: (l, 0, 0)),
                  pl.BlockSpec((None, k, tn), lambda l, j: (l, 0, j))] + job_in,
        out_specs=[pl.BlockSpec((None, m, tn), lambda l, j: (l, 0, j))] + job_out,
        out_shape=[jax.ShapeDtypeStruct((layers, m, n), F32)] + job_shape,
        compiler_params=_params("arbitrary", "arbitrary"),
        name="mem_kv",
    )(x, _gain_arg(g), w, *job_args)


def _resident_spec(shape):
    return pl.BlockSpec(shape, lambda *_: (0,) * len(shape), pipeline_mode=pl.Buffered(1))


def _gelu_tanh(x):
    return x * (0.5 * (1.0 + jnp.tanh(math.sqrt(2.0 / math.pi) * (x + 0.044715 * (x * x * x)))))


def _pass(h_ref, w_ref, col):
    return jnp.dot(h_ref[...], w_ref[:, col:col + V7X_MXU_COLS], preferred_element_type=F32)


def _heads(acc):
    return [acc[:, j * HEAD_DIM:(j + 1) * HEAD_DIM] for j in range(acc.shape[1] // HEAD_DIM)]


def _attn_in_proj_kernel(x_ref, g_ref, w_ref, pos_ref, invf_ref, *rest, n_jobs):
    qk_ref, vm_ref = rest[n_jobs:n_jobs + 2]
    h_ref, c_ref, s1_ref, s2_ref = rest[2 * n_jobs + 2:]
    _run_cast_jobs(rest[:n_jobs], rest[n_jobs + 2:2 * n_jobs + 2])
    h_ref[...] = _rms_rows(x_ref[...], g_ref[...]).astype(BF16)
    _rope_tables(pos_ref, invf_ref, c_ref, s1_ref, s2_ref)
    for c in range(2 * ATTN_W // V7X_MXU_COLS):
        for j, t in enumerate(_heads(_pass(h_ref, w_ref, c * V7X_MXU_COLS))):
            lo = c * V7X_MXU_COLS + j * HEAD_DIM
            qk_ref[:, lo:lo + HEAD_DIM] = (t * c_ref[...] + pltpu.roll(t, ROT_HALF, 1) * s1_ref[...]
                                           + pltpu.roll(t, HEAD_DIM - ROT_HALF, 1) * s2_ref[...])
    for c in range((ATTN_W + MEM_W) // V7X_MXU_COLS):
        cols = slice(c * V7X_MXU_COLS, (c + 1) * V7X_MXU_COLS)
        vm_ref[:, cols] = _pass(h_ref, w_ref, 2 * ATTN_W + c * V7X_MXU_COLS)


def _attn_in_proj(x, g, g_layer, w_bf, pos_col, invf, cast_weights=()):
    s, k = x.shape
    tm = ROW_STEP
    assert w_bf.shape == (k, 3 * ATTN_W + MEM_W)
    row = lambda cols: pl.BlockSpec((tm, cols), lambda i: (i, 0))
    job_in, job_args, job_out, job_shape = _cast_jobs(cast_weights, s // tm, lambda i: i)
    return pl.pallas_call(
        functools.partial(_attn_in_proj_kernel, n_jobs=len(job_args)),
        grid=(s // tm,),
        in_specs=[row(k), _gain_spec(g_layer, k), _resident_spec(w_bf.shape), row(1),
                  pl.BlockSpec((1, HEAD_DIM), lambda i: (0, 0))] + job_in,
        out_specs=[row(2 * ATTN_W), row(ATTN_W + MEM_W)] + job_out,
        out_shape=[jax.ShapeDtypeStruct((s, 2 * ATTN_W), F32),
                   jax.ShapeDtypeStruct((s, ATTN_W + MEM_W), F32)] + job_shape,
        scratch_shapes=[pltpu.VMEM((tm, k), BF16)] + [pltpu.VMEM((tm, HEAD_DIM), F32)] * 3,
        compiler_params=_params("arbitrary"),
        name="attn_in_proj",
    )(x, _gain_arg(g), w_bf, pos_col, invf, *job_args)


def _memory_scores(q_heads, kv_ref):
    probs = []
    for h, q in enumerate(q_heads):
        sl = slice(h * HEAD_DIM, (h + 1) * HEAD_DIM)
        s = lax.dot_general((q * ATTN_SCALE_LOG2).astype(BF16), kv_ref[:, sl].astype(BF16),
                            (((1,), (1,)), ((), ())), preferred_element_type=F32)
        p = jnp.exp2(s - jnp.max(s, axis=-1, keepdims=True))
        probs.append((p.astype(BF16), jnp.sum(p, axis=-1, keepdims=True)))
    return probs


def _memory_values(probs, kv_ref):
    heads = []
    for h, (p, l) in enumerate(probs):
        v = kv_ref[:, MEM_W + h * HEAD_DIM:MEM_W + (h + 1) * HEAD_DIM].astype(BF16)
        heads.append((jnp.dot(p, v, preferred_element_type=F32) * (1.0 / l)).astype(BF16))
    return jnp.concatenate(heads, axis=1)


OUT_PROJ_COLS = 512


def _project_out(mix, probs, kv_ref, w_ref, x_ref, g_ref, o_ref, hn_ref):
    ka = mix.shape[1]
    width = o_ref.shape[1]
    chunks = [slice(c * OUT_PROJ_COLS, (c + 1) * OUT_PROJ_COLS) for c in range(width // OUT_PROJ_COLS)]
    for cols in chunks:
        o_ref[:, cols] = x_ref[:, cols] + jnp.dot(mix, w_ref[:ka, cols], preferred_element_type=F32)
    b = _memory_values(probs, kv_ref)
    ssq = jnp.zeros((o_ref.shape[0], 1), F32)
    for cols in chunks:
        y = o_ref[:, cols] + jnp.dot(b, w_ref[ka:, cols], preferred_element_type=F32)
        o_ref[:, cols] = y
        ssq = ssq + jnp.sum(y * y, axis=-1, keepdims=True)
    scale = lax.rsqrt(ssq * (1.0 / width) + NORM_EPS)
    for cols in chunks:
        hn_ref[:, cols] = (o_ref[:, cols] * scale * g_ref[:, cols]).astype(BF16)


def _sgu_mixer_kernel(x_ref, g_ref, w_ref, lng_ref, lnb_ref, ws_ref, bt_ref, kv_ref, wo_ref, gf_ref,
                      o_ref, hn_ref, h_ref, v_scr, vn_scr, mix_scr, wsb_scr):
    @pl.when(pl.program_id(0) == 0)
    def _():
        t = lax.broadcasted_iota(jnp.int32, (SGU_CHUNK, SGU_CHUNK), 0)
        s = lax.broadcasted_iota(jnp.int32, (SGU_CHUNK, SGU_CHUNK), 1)
        for g in range(SGU_GROUPS):
            wsb_scr[g] = jnp.where(t >= s, ws_ref[g], 0.0).astype(BF16)

    h_ref[...] = _rms_rows(x_ref[...], g_ref[...]).astype(BF16)
    for c in range(SGU_W // V7X_MXU_COLS):
        cols = slice(c * V7X_MXU_COLS, (c + 1) * V7X_MXU_COLS)
        v_scr[:, cols] = _gelu_tanh(_pass(h_ref, w_ref, SGU_W + c * V7X_MXU_COLS))
    q_heads = []
    for c in range(MEM_W // V7X_MXU_COLS):
        q_heads += _heads(_pass(h_ref, w_ref, 2 * SGU_W + c * V7X_MXU_COLS))
    probs = _memory_scores(q_heads, kv_ref)
    v = v_scr[...]
    mu = jnp.mean(v, axis=-1, keepdims=True)
    vc = v - mu
    var = jnp.mean(vc * vc, axis=-1, keepdims=True)
    vn_scr[...] = (vc * lax.rsqrt(var + LN_EPS) * lng_ref[...] + lnb_ref[...]).astype(BF16)

    bt = bt_ref[...]
    n_chunks = mix_scr.shape[0] // SGU_CHUNK
    for c in range(SGU_W // V7X_MXU_COLS):
        for j, u in enumerate(_heads(_gelu_tanh(_pass(h_ref, w_ref, c * V7X_MXU_COLS)))):
            g = c * (V7X_MXU_COLS // HEAD_DIM) + j
            cols = slice(g * HEAD_DIM, (g + 1) * HEAD_DIM)
            vg = jnp.concatenate([vn_scr[r * SGU_CHUNK:(r + 1) * SGU_CHUNK, cols]
                                  for r in range(n_chunks)], axis=1)
            mixed = jnp.dot(wsb_scr[g], vg, preferred_element_type=F32) + bt[:, g:g + 1]
            for r in range(n_chunks):
                rows = slice(r * SGU_CHUNK, (r + 1) * SGU_CHUNK)
                mix_scr[rows, cols] = (u[rows, :] * mixed[:, r * HEAD_DIM:(r + 1) * HEAD_DIM]
                                       ).astype(mix_scr.dtype)

    _project_out(mix_scr[...], probs, kv_ref, wo_ref, x_ref, gf_ref, o_ref, hn_ref)


def _sgu_mixer(x, g, g_layer, w_bf, ln_g, ln_b, w_spatial, b_spatial_t, layer, kv, kv_layer,
               w_out_bf, g_ffn, g_ffn_layer):
    s, k = x.shape
    tm = ROW_STEP
    assert w_bf.shape == (k, 2 * SGU_W + MEM_W) and tm % SGU_CHUNK == 0
    assert w_out_bf.shape == (SGU_W + MEM_W, k)
    row = lambda cols: pl.BlockSpec((tm, cols), lambda i: (i, 0))
    return pl.pallas_call(
        _sgu_mixer_kernel,
        grid=(s // tm,),
        in_specs=[row(k), _gain_spec(g_layer, k), _resident_spec(w_bf.shape),
                  _gain_spec(layer, SGU_W), _gain_spec(layer, SGU_W),
                  pl.BlockSpec((None, SGU_GROUPS, SGU_CHUNK, SGU_CHUNK), lambda i: (layer, 0, 0, 0)),
                  pl.BlockSpec((None, SGU_CHUNK, SGU_GROUPS), lambda i: (layer, 0, 0)),
                  pl.BlockSpec((None, MEM_LEN, 2 * MEM_W), lambda i: (kv_layer, 0, 0)),
                  _resident_spec(w_out_bf.shape), _gain_spec(g_ffn_layer, k)],
        out_specs=[row(k), row(k)],
        out_shape=[jax.ShapeDtypeStruct((s, k), F32), jax.ShapeDtypeStruct((s, k), BF16)],
        scratch_shapes=[pltpu.VMEM((tm, k), BF16), pltpu.VMEM((tm, SGU_W), F32),
                        pltpu.VMEM((tm, SGU_W), BF16), pltpu.VMEM((tm, SGU_W), BF16),
                        pltpu.VMEM((SGU_GROUPS, SGU_CHUNK, SGU_CHUNK), BF16)],
        compiler_params=_params("arbitrary"),
        name="sgu_mixer",
    )(x, _gain_arg(g), w_bf, _gain_arg(ln_g), _gain_arg(ln_b), w_spatial, b_spatial_t, kv,
      w_out_bf, _gain_arg(g_ffn))


def _out_proj_kernel(a_ref, q_ref, kv_ref, w_ref, x_ref, g_ref, o_ref, hn_ref):
    probs = _memory_scores([q_ref[:, h * HEAD_DIM:(h + 1) * HEAD_DIM] for h in range(MEM_HEADS)],
                           kv_ref)
    _project_out(a_ref[...], probs, kv_ref, w_ref, x_ref, g_ref, o_ref, hn_ref)


def _out_proj(a, q_src, q_col_block, kv, kv_layer, w_bf, x, g, g_layer):
    s, ka = a.shape
    width = x.shape[1]
    tm = ROW_STEP
    assert w_bf.shape == (ka + MEM_W, width)
    row = lambda cols: pl.BlockSpec((tm, cols), lambda i: (i, 0))
    return pl.pallas_call(
        _out_proj_kernel,
        grid=(s // tm,),
        in_specs=[row(ka), pl.BlockSpec((tm, MEM_W), lambda i: (i, q_col_block)),
                  pl.BlockSpec((None, MEM_LEN, 2 * MEM_W), lambda i: (kv_layer, 0, 0)),
                  _resident_spec((ka + MEM_W, width)), row(width), _gain_spec(g_layer, width)],
        out_specs=[row(width), row(width)],
        out_shape=[jax.ShapeDtypeStruct((s, width), F32), jax.ShapeDtypeStruct((s, width), BF16)],
        compiler_params=_params("parallel"),
        name="out_proj",
    )(a, q_src, kv, w_bf, x, _gain_arg(g))


def _rows(ref, start, size, stride):
    if stride == 1:
        return ref[pl.ds(start, size), :]
    return ref[pl.ds(start, size, stride=stride), :]


V7X_FREE_SUBLANE_STRIDE = 4


def _presplit(d):
    return d // V7X_FREE_SUBLANE_STRIDE if d > V7X_FREE_SUBLANE_STRIDE else 1


def _attn_batch(units):
    scores = [lax.dot_general((q * ATTN_SCALE_LOG2).astype(BF16), kc.astype(BF16),
                              (((1,), (1,)), ((), ())), preferred_element_type=F32) + bias
              for q, kc, _, bias in units]
    probs = []
    for s in scores:
        m = jnp.max(s, axis=-1, keepdims=True)
        p = jnp.exp2(s - m)
        probs.append((p, m, jnp.sum(p, axis=-1, keepdims=True)))
    outs = []
    for (p, m, l), (_, _, vc, _) in zip(probs, units):
        o = jnp.dot(p.astype(BF16), vc.astype(BF16), preferred_element_type=F32)
        outs.append((o * (1.0 / l), m + jnp.log2(l)))
    return outs


def _attn_kernel(*refs):
    ngrp = len(ATTN_GROUPS)
    in_refs = refs[:5 * ngrp]
    o_ref = refs[5 * ngrp]
    pos = 5 * ngrp + 1
    o_scr, l_scr = refs[pos:pos + ngrp], refs[pos + ngrp:pos + 2 * ngrp]
    split_refs = refs[pos + 2 * ngrp:pos + 2 * ngrp + 7]
    bias_ref = refs[-1]

    qi = lax.broadcasted_iota(jnp.int32, (BLK, 2 * BLK), 0)
    ki = lax.broadcasted_iota(jnp.int32, (BLK, 2 * BLK), 1)
    lo = jnp.where(pl.program_id(0) == 0, BLK, 0)
    bias_ref[0] = jnp.where((ki >= qi) & (ki <= qi + N_BACK), 0.0, NEG_INF)
    bias_ref[1] = jnp.where((ki >= jnp.maximum(qi, lo)) & (ki <= qi + N_BACK), 0.0, NEG_INF)

    for g, (_, d) in enumerate(ATTN_GROUPS):
        q_ref, k_ref, v_ref, kp_ref, vp_ref = in_refs[5 * g:5 * g + 5]
        og, lg = o_scr[g], l_scr[g]
        sub = d * BLK
        f = _presplit(d)
        d2 = d // f
        qlen, sublen = ATTN_SPAN // f, sub // f
        if f > 1:
            q_src, kp_src, k_src, vp_src, v_src, o_dst, l_dst = split_refs
            for rf in range(f):
                for src, dst, n in ((q_ref, q_src, qlen), (k_ref, k_src, qlen), (v_ref, v_src, qlen),
                                    (kp_ref, kp_src, sublen), (vp_ref, vp_src, sublen)):
                    dst[pl.ds(rf * n, n), :] = _rows(src, rf, n, f)
        else:
            q_src, kp_src, k_src, vp_src, v_src, o_dst, l_dst = (
                q_ref, kp_ref, k_ref, vp_ref, v_ref, og, lg)

        def key_rows(prev_src, src, c, rf, r2):
            if c == 0:
                return jnp.concatenate([_rows(prev_src, rf * sublen + r2, BLK, d2),
                                        _rows(src, rf * qlen + r2, BLK, d2)], axis=0)
            return _rows(src, rf * qlen + (c - 1) * sublen + r2, 2 * BLK, d2)

        units, starts = [], []
        for idx in range(ATTN_SPAN // BLK):
            c, r = divmod(idx, d)
            rf, r2 = r % f, r // f
            qstart = rf * qlen + c * sublen + r2
            units.append((_rows(q_src, qstart, BLK, d2), key_rows(kp_src, k_src, c, rf, r2),
                          key_rows(vp_src, v_src, c, rf, r2), bias_ref[1 if c == 0 else 0]))
            starts.append(qstart)
        for qstart, (o, lse) in zip(starts, _attn_batch(units)):
            rows = pl.ds(qstart, BLK) if d2 == 1 else pl.ds(qstart, BLK, stride=d2)
            o_dst[rows, :] = o
            l_dst[rows, :] = jnp.broadcast_to(lse, (BLK, HEAD_DIM))
        if f > 1:
            for rf in range(f):
                og[pl.ds(rf, qlen, stride=f), :] = o_dst[pl.ds(rf * qlen, qlen), :]
                lg[pl.ds(rf, qlen, stride=f), :] = l_dst[pl.ds(rf * qlen, qlen), :]

    lses = [l[...] for l in l_scr]
    mx = functools.reduce(jnp.maximum, lses)
    ws = [jnp.exp2(l - mx) for l in lses]
    num = functools.reduce(lambda a, b: a + b, [w * o[...] for w, o in zip(ws, o_scr)])
    den = functools.reduce(lambda a, b: a + b, ws)
    o_ref[...] = (num * (1.0 / den)).astype(o_ref.dtype)


def _dilated_attention(qk, vm):
    s = qk.shape[0]
    assert s % ATTN_SPAN == 0
    hpg = HEADS_PER_GROUP
    in_specs, args = [], []
    for g, (_, d) in enumerate(ATTN_GROUPS):
        sub = d * BLK
        ratio = ATTN_SPAN // sub
        qc, kc, vc = g * hpg, ATTN_HEADS + g * hpg, g * hpg
        cur = lambda col: pl.BlockSpec((ATTN_SPAN, HEAD_DIM), lambda i, h, col=col: (i, col + h))
        prev = lambda col, ratio=ratio, sub=sub: pl.BlockSpec(
            (sub, HEAD_DIM), lambda i, h, col=col, ratio=ratio: (jnp.maximum(i * ratio - 1, 0), col + h))
        in_specs += [cur(qc), cur(kc), cur(vc), prev(kc), prev(vc)]
        args += [qk, qk, vm, qk, vm]
    ngrp = len(ATTN_GROUPS)
    return pl.pallas_call(
        _attn_kernel,
        grid=(s // ATTN_SPAN, hpg),
        in_specs=in_specs,
        out_specs=pl.BlockSpec((ATTN_SPAN, HEAD_DIM), lambda i, h: (i, h)),
        out_shape=jax.ShapeDtypeStruct((s, ATTN_OUT_W), BF16),
        scratch_shapes=([pltpu.VMEM((ATTN_SPAN, HEAD_DIM), F32)] * (2 * ngrp)
                        + [pltpu.VMEM((ATTN_SPAN, HEAD_DIM), F32)] * 7
                        + [pltpu.VMEM((2, BLK, 2 * BLK), F32)]),
        compiler_params=_params("parallel", "arbitrary"),
        name="dilated_attention",
    )(*args)


def _ffn_down_rows_kernel(a_ref, w_ref, x_ref, *rest, final_norm):
    if final_norm:
        g_ref, o_ref = rest
    else:
        (o_ref,) = rest
    width = o_ref.shape[1]
    ssq = jnp.zeros((o_ref.shape[0], 1), F32)
    for c in range(width // V7X_MXU_COLS):
        cols = slice(c * V7X_MXU_COLS, (c + 1) * V7X_MXU_COLS)
        y = x_ref[:, cols] + jnp.dot(a_ref[...], w_ref[:, cols], preferred_element_type=F32)
        o_ref[:, cols] = y
        if final_norm:
            ssq = ssq + jnp.sum(y * y, axis=-1, keepdims=True)
    if final_norm:
        scale = lax.rsqrt(ssq * (1.0 / width) + NORM_EPS)
        for c in range(width // V7X_MXU_COLS):
            cols = slice(c * V7X_MXU_COLS, (c + 1) * V7X_MXU_COLS)
            o_ref[:, cols] = o_ref[:, cols] * scale * g_ref[:, cols]


def _ffn_down_rows(a, w_bf, x, final_g=None):
    s, f = a.shape
    width = x.shape[1]
    tm = ROW_STEP
    assert w_bf.shape == (f, width) and s % tm == 0
    row = lambda cols: pl.BlockSpec((tm, cols), lambda i: (i, 0))
    in_specs = [row(f), _resident_spec(w_bf.shape), row(width)]
    args = [a, w_bf, x]
    if final_g is not None:
        in_specs.append(_gain_spec(0, width))
        args.append(final_g.reshape(1, 1, width))
    return pl.pallas_call(
        functools.partial(_ffn_down_rows_kernel, final_norm=final_g is not None),
        grid=(s // tm,),
        in_specs=in_specs,
        out_specs=row(width),
        out_shape=jax.ShapeDtypeStruct((s, width), F32),
        compiler_params=_params("parallel"),
        name="ffn_down_rows",
    )(*args)


def _ffn_gate_up_kernel(h_ref, wg_ref, wu_ref, *rest, n_jobs):
    o_ref = rest[n_jobs]
    _run_cast_jobs(rest[:n_jobs], rest[n_jobs + 1:])
    for c in range(o_ref.shape[1] // V7X_MXU_COLS):
        cols = slice(c * V7X_MXU_COLS, (c + 1) * V7X_MXU_COLS)
        gate = jnp.dot(h_ref[...], wg_ref[:, cols].astype(BF16), preferred_element_type=F32)
        up = jnp.dot(h_ref[...], wu_ref[:, cols].astype(BF16), preferred_element_type=F32)
        o_ref[:, cols] = (gate * (1.0 / (1.0 + jnp.exp(-gate))) * up).astype(o_ref.dtype)


def _ffn_gate_up(h, w_gate, w_up, layer, cast_weights=()):
    s, k = h.shape
    f = w_gate.shape[2]
    tm, tf = GATE_UP_ROWS, GATE_UP_COLS
    nj = f // tf
    wspec = pl.BlockSpec((None, k, tf), lambda i, j: (layer, 0, j))
    job_in, job_args, job_out, job_shape = _cast_jobs(cast_weights, (s // tm) * nj,
                                                      lambda i, j: i * nj + j)
    return pl.pallas_call(
        functools.partial(_ffn_gate_up_kernel, n_jobs=len(job_args)),
        grid=(s // tm, nj),
        in_specs=[pl.BlockSpec((tm, k), lambda i, j: (i, 0)), wspec, wspec] + job_in,
        out_specs=[pl.BlockSpec((tm, tf), lambda i, j: (i, j))] + job_out,
        out_shape=[jax.ShapeDtypeStruct((s, f), BF16)] + job_shape,
        compiler_params=_params("arbitrary", "arbitrary"),
        name="ffn_gate_up",
    )(h, w_gate, w_up, *job_args)


def kernel(x, mem, positions, mix_norm, mem_norm, w_mem_kv, ffn_norm, w_gate, w_up, w_down,
           attn_w_in, attn_w_out, sgu_w_in, sgu_ln_g, sgu_ln_b, sgu_w_spatial, sgu_b_spatial,
           sgu_w_out, final_norm):
    b, s, d = x.shape
    assert (b, s, d) == (1, SEQ, D_MODEL) and mem.shape == (1, MEM_LEN, D_MODEL)
    xs = x.reshape(s, d)
    mems = mem.reshape(MEM_LEN, d)

    inv_freq = ROPE_THETA ** (-jnp.arange(ROT_HALF, dtype=F32) / ROT_HALF)
    invf = jnp.tile(jnp.concatenate([inv_freq, inv_freq]), ROPE_PACK)
    pos_col, invf = positions.reshape(s, 1), invf.reshape(1, HEAD_DIM)

    def mixer_weights(layer):
        return (attn_w_in, attn_w_out) if layer % 2 == 0 else (sgu_w_in, sgu_w_out)

    kv, w_in_bf = _mem_kv(mems, mem_norm, w_mem_kv, [(attn_w_in, 0)])
    w_out_bf = None
    for i in range(DEPTH):
        j = i // 2
        if i % 2 == 0:
            casts = [(attn_w_out, j)] if w_out_bf is None else []
            qk, vm, *cast = _attn_in_proj(xs, mix_norm, i, w_in_bf, pos_col, invf, casts)
            w_out_bf = cast[0] if cast else w_out_bf
            mix = _dilated_attention(qk, vm)
            xs, hn = _out_proj(mix, vm, ATTN_W // MEM_W, kv, i, w_out_bf, xs, ffn_norm, i)
        else:
            xs, hn = _sgu_mixer(xs, mix_norm, i, w_in_bf, sgu_ln_g, sgu_ln_b, sgu_w_spatial,
                                jnp.swapaxes(sgu_b_spatial, 1, 2), j, kv, i, w_out_bf, ffn_norm, i)
        last = i + 1 == DEPTH
        casts = [(w_down, i)] + ([] if last else [(w, (i + 1) // 2) for w in mixer_weights(i + 1)])
        act, w_down_bf, *next_bf = _ffn_gate_up(hn, w_gate, w_up, i, casts)
        w_in_bf, w_out_bf = next_bf if next_bf else (None, None)
        xs = _ffn_down_rows(act, w_down_bf, xs, final_g=final_norm if last else None)
    return xs.reshape(b, s, d)
```

```python
import functools
import math

import jax
import jax.numpy as jnp
from jax import lax
from jax.experimental import pallas as pl
from jax.experimental.pallas import tpu as pltpu

D_MODEL = 2048
SEQ = 8192
DEPTH = 2
MEM_LEN = 256
HEAD_DIM = 128
MEM_HEADS = 4
MEM_W = MEM_HEADS * HEAD_DIM
ATTN_GROUPS = ((128, 1), (512, 4), (2048, 16))
ATTN_HEADS = 12
HEADS_PER_GROUP = 4
ATTN_W = ATTN_HEADS * HEAD_DIM
ATTN_OUT_W = HEADS_PER_GROUP * HEAD_DIM
BLK = 128
SGU_GROUPS = 12
SGU_W = SGU_GROUPS * HEAD_DIM
SGU_CHUNK = 128
ROT_DIM = HEAD_DIM // 4
ROT_HALF = ROT_DIM // 2
ROPE_THETA = 500000.0
NORM_EPS = 1e-6
LN_EPS = 1e-5
NEG_INF = -1e30
ATTN_SCALE_LOG2 = HEAD_DIM ** -0.5 * math.log2(math.e)

N_BACK = BLK
assert all(w // d == N_BACK for w, d in ATTN_GROUPS)
ATTN_SPAN = max(d for _, d in ATTN_GROUPS) * BLK

V7X_VMEM_LIMIT_BYTES = 56 * 1024 * 1024
V7X_MXU_COLS = 256

ROW_STEP = 512
GATE_UP_ROWS, GATE_UP_COLS = 2048, 512
MEM_KV_COLS = 512

F32 = jnp.float32
BF16 = jnp.bfloat16


def _params(*semantics):
    return pltpu.CompilerParams(dimension_semantics=semantics,
                                vmem_limit_bytes=V7X_VMEM_LIMIT_BYTES)


def _rms_rows(x, g):
    ms = jnp.mean(x * x, axis=-1, keepdims=True)
    return x * lax.rsqrt(ms + NORM_EPS) * g


def _gain_spec(layer, k):
    return pl.BlockSpec((None, 1, k), lambda *_: (layer, 0, 0))


def _gain_arg(g):
    return g.reshape(g.shape[0], 1, g.shape[1])


ROPE_PACK = HEAD_DIM // ROT_DIM


def _rope_tables(pos_ref, invf_ref, c_ref, s1_ref, s2_ref):
    n = pos_ref.shape[0] // ROPE_PACK
    lane = lax.broadcasted_iota(jnp.int32, (n, HEAD_DIM), 1)
    group = lane // ROT_DIM
    pos = jnp.zeros((n, HEAD_DIM), F32)
    for q in range(ROPE_PACK):
        pos = jnp.where(group == q, pos_ref[q * n:(q + 1) * n, :].astype(F32), pos)
    ang = pos * invf_ref[...]
    cos, sin = jnp.cos(ang), jnp.sin(ang)
    for q in range(ROPE_PACK):
        cq = cos if q == 0 else pltpu.roll(cos, HEAD_DIM - q * ROT_DIM, 1)
        sq = sin if q == 0 else pltpu.roll(sin, HEAD_DIM - q * ROT_DIM, 1)
        rows = slice(q * n, (q + 1) * n)
        c_ref[rows, :] = jnp.where(lane < ROT_DIM, cq, 1.0)
        s1_ref[rows, :] = jnp.where(lane < ROT_HALF, 0.0, jnp.where(lane < ROT_DIM, sq, 0.0))
        s2_ref[rows, :] = jnp.where(lane < ROT_HALF, -sq, 0.0)


BF16_SUBLANES = 16


def _cast_jobs(weights, n_steps, step_of):
    in_specs, args, out_specs, out_shape = [], [], [], []
    for w, layer in weights:
        _, k, n = w.shape
        slab = -(-(-(-k // n_steps)) // BF16_SUBLANES) * BF16_SUBLANES
        while k % slab:
            slab += BF16_SUBLANES
        n_slabs = k // slab
        in_specs.append(pl.BlockSpec(
            (None, slab, n), lambda *g, layer=layer, n_slabs=n_slabs:
            (layer, jnp.minimum(step_of(*g), n_slabs - 1), 0)))
        out_specs.append(pl.BlockSpec(
            (slab, n), lambda *g, n_slabs=n_slabs: (jnp.minimum(step_of(*g), n_slabs - 1), 0)))
        args.append(w)
        out_shape.append(jax.ShapeDtypeStruct((k, n), BF16))
    return in_specs, args, out_specs, out_shape


def _run_cast_jobs(in_refs, out_refs):
    for src, dst in zip(in_refs, out_refs):
        dst[...] = src[...].astype(BF16)


def _mem_kv_kernel(x_ref, g_ref, w_ref, *rest, n_jobs):
    o_ref = rest[n_jobs]
    _run_cast_jobs(rest[:n_jobs], rest[n_jobs + 1:])
    h = _rms_rows(x_ref[...], g_ref[...]).astype(BF16)
    o_ref[...] = jnp.dot(h, w_ref[...].astype(BF16), preferred_element_type=F32)


def _mem_kv(x, g, w, cast_weights=()):
    m, k = x.shape
    layers, _, n = w.shape
    tn = MEM_KV_COLS
    assert n % tn == 0
    nt = n // tn
    job_in, job_args, job_out, job_shape = _cast_jobs(cast_weights, layers * nt, lambda l, j: l * nt + j)
    return pl.pallas_call(
        functools.partial(_mem_kv_kernel, n_jobs=len(job_args)),
        grid=(layers, nt),
        in_specs=[pl.BlockSpec((m, k), lambda l, j: (0, 0)),
                  pl.BlockSpec((None, 1, k), lambda l, j: (l, 0, 0)),
                  pl.BlockSpec((None, k, tn), lambda l, j: (l, 0, j))] + job_in,
        out_specs=[pl.BlockSpec((None, m, tn), lambda l, j: (l, 0, j))] + job_out,
        out_shape=[jax.ShapeDtypeStruct((layers, m, n), F32)] + job_shape,
        compiler_params=_params("arbitrary", "arbitrary"),
        name="mem_kv",
    )(x, _gain_arg(g), w, *job_args)


def _resident_spec(shape):
    return pl.BlockSpec(shape, lambda *_: (0,) * len(shape), pipeline_mode=pl.Buffered(1))


def _gelu_tanh(x):
    return x * (0.5 * (1.0 + jnp.tanh(math.sqrt(2.0 / math.pi) * (x + 0.044715 * (x * x * x)))))


def _pass(h_ref, w_ref, col):
    return jnp.dot(h_ref[...], w_ref[:, col:col + V7X_MXU_COLS], preferred_element_type=F32)


def _heads(acc):
    return [acc[:, j * HEAD_DIM:(j + 1) * HEAD_DIM] for j in range(acc.shape[1] // HEAD_DIM)]


def _attn_in_proj_kernel(x_ref, g_ref, w_ref, pos_ref, invf_ref, *rest, n_jobs):
    qk_ref, vm_ref = rest[n_jobs:n_jobs + 2]
    h_ref, c_ref, s1_ref, s2_ref = rest[2 * n_jobs + 2:]
    _run_cast_jobs(rest[:n_jobs], rest[n_jobs + 2:2 * n_jobs + 2])
    h_ref[...] = _rms_rows(x_ref[...], g_ref[...]).astype(BF16)
    _rope_tables(pos_ref, invf_ref, c_ref, s1_ref, s2_ref)
    for c in range(2 * ATTN_W // V7X_MXU_COLS):
        for j, t in enumerate(_heads(_pass(h_ref, w_ref, c * V7X_MXU_COLS))):
            lo = c * V7X_MXU_COLS + j * HEAD_DIM
            qk_ref[:, lo:lo + HEAD_DIM] = (t * c_ref[...] + pltpu.roll(t, ROT_HALF, 1) * s1_ref[...]
                                           + pltpu.roll(t, HEAD_DIM - ROT_HALF, 1) * s2_ref[...])
    for c in range((ATTN_W + MEM_W) // V7X_MXU_COLS):
        cols = slice(c * V7X_MXU_COLS, (c + 1) * V7X_MXU_COLS)
        vm_ref[:, cols] = _pass(h_ref, w_ref, 2 * ATTN_W + c * V7X_MXU_COLS)


def _attn_in_proj(x, g, g_layer, w_bf, pos_col, invf, cast_weights=()):
    s, k = x.shape
    tm = ROW_STEP
    assert w_bf.shape == (k, 3 * ATTN_W + MEM_W)
    row = lambda cols: pl.BlockSpec((tm, cols), lambda i: (i, 0))
    job_in, job_args, job_out, job_shape = _cast_jobs(cast_weights, s // tm, lambda i: i)
    return pl.pallas_call(
        functools.partial(_attn_in_proj_kernel, n_jobs=len(job_args)),
        grid=(s // tm,),
        in_specs=[row(k), _gain_spec(g_layer, k), _resident_spec(w_bf.shape), row(1),
                  pl.BlockSpec((1, HEAD_DIM), lambda i: (0, 0))] + job_in,
        out_specs=[row(2 * ATTN_W), row(ATTN_W + MEM_W)] + job_out,
        out_shape=[jax.ShapeDtypeStruct((s, 2 * ATTN_W), F32),
                   jax.ShapeDtypeStruct((s, ATTN_W + MEM_W), F32)] + job_shape,
        scratch_shapes=[pltpu.VMEM((tm, k), BF16)] + [pltpu.VMEM((tm, HEAD_DIM), F32)] * 3,
        compiler_params=_params("arbitrary"),
        name="attn_in_proj",
    )(x, _gain_arg(g), w_bf, pos_col, invf, *job_args)


def _memory_scores(q_heads, kv_ref):
    probs = []
    for h, q in enumerate(q_heads):
        sl = slice(h * HEAD_DIM, (h + 1) * HEAD_DIM)
        s = lax.dot_general((q * ATTN_SCALE_LOG2).astype(BF16), kv_ref[:, sl].astype(BF16),
                            (((1,), (1,)), ((), ())), preferred_element_type=F32)
        p = jnp.exp2(s - jnp.max(s, axis=-1, keepdims=True))
        probs.append((p.astype(BF16), jnp.sum(p, axis=-1, keepdims=True)))
    return probs


def _memory_values(probs, kv_ref):
    heads = []
    for h, (p, l) in enumerate(probs):
        v = kv_ref[:, MEM_W + h * HEAD_DIM:MEM_W + (h + 1) * HEAD_DIM].astype(BF16)
        heads.append((jnp.dot(p, v, preferred_element_type=F32) * (1.0 / l)).astype(BF16))
    return jnp.concatenate(heads, axis=1)


OUT_PROJ_COLS = 512


def _project_out(mix, probs, kv_ref, w_ref, x_ref, g_ref, o_ref, hn_ref):
    ka = mix.shape[1]
    width = o_ref.shape[1]
    chunks = [slice(c * OUT_PROJ_COLS, (c + 1) * OUT_PROJ_COLS) for c in range(width // OUT_PROJ_COLS)]
    for cols in chunks:
        o_ref[:, cols] = x_ref[:, cols] + jnp.dot(mix, w_ref[:ka, cols], preferred_element_type=F32)
    b = _memory_values(probs, kv_ref)
    ssq = jnp.zeros((o_ref.shape[0], 1), F32)
    for cols in chunks:
        y = o_ref[:, cols] + jnp.dot(b, w_ref[ka:, cols], preferred_element_type=F32)
        o_ref[:, cols] = y
        ssq = ssq + jnp.sum(y * y, axis=-1, keepdims=True)
    scale = lax.rsqrt(ssq * (1.0 / width) + NORM_EPS)
    for cols in chunks:
        hn_ref[:, cols] = (o_ref[:, cols] * scale * g_ref[:, cols]).astype(BF16)


def _sgu_mixer_kernel(x_ref, g_ref, w_ref, lng_ref, lnb_ref, ws_ref, bt_ref, kv_ref, wo_ref, gf_ref,
                      o_ref, hn_ref, h_ref, v_scr, vn_scr, mix_scr, wsb_scr):
    @pl.when(pl.program_id(0) == 0)
    def _():
        t = lax.broadcasted_iota(jnp.int32, (SGU_CHUNK, SGU_CHUNK), 0)
        s = lax.broadcasted_iota(jnp.int32, (SGU_CHUNK, SGU_CHUNK), 1)
        for g in range(SGU_GROUPS):
            wsb_scr[g] = jnp.where(t >= s, ws_ref[g], 0.0).astype(BF16)

    h_ref[...] = _rms_rows(x_ref[...], g_ref[...]).astype(BF16)
    for c in range(SGU_W // V7X_MXU_COLS):
        cols = slice(c * V7X_MXU_COLS, (c + 1) * V7X_MXU_COLS)
        v_scr[:, cols] = _gelu_tanh(_pass(h_ref, w_ref, SGU_W + c * V7X_MXU_COLS))
    q_heads = []
    for c in range(MEM_W // V7X_MXU_COLS):
        q_heads += _heads(_pass(h_ref, w_ref, 2 * SGU_W + c * V7X_MXU_COLS))
    probs = _memory_scores(q_heads, kv_ref)
    v = v_scr[...]
    mu = jnp.mean(v, axis=-1, keepdims=True)
    vc = v - mu
    var = jnp.mean(vc * vc, axis=-1, keepdims=True)
    vn_scr[...] = (vc * lax.rsqrt(var + LN_EPS) * lng_ref[...] + lnb_ref[...]).astype(BF16)

    bt = bt_ref[...]
    n_chunks = mix_scr.shape[0] // SGU_CHUNK
    for c in range(SGU_W // V7X_MXU_COLS):
        for j, u in enumerate(_heads(_gelu_tanh(_pass(h_ref, w_ref, c * V7X_MXU_COLS)))):
            g = c * (V7X_MXU_COLS // HEAD_DIM) + j
            cols = slice(g * HEAD_DIM, (g + 1) * HEAD_DIM)
            vg = jnp.concatenate([vn_scr[r * SGU_CHUNK:(r + 1) * SGU_CHUNK, cols]
                                  for r in range(n_chunks)], axis=1)
            mixed = jnp.dot(wsb_scr[g], vg, preferred_element_type=F32) + bt[:, g:g + 1]
            for r in range(n_chunks):
                rows = slice(r * SGU_CHUNK, (r + 1) * SGU_CHUNK)
                mix_scr[rows, cols] = (u[rows, :] * mixed[:, r * HEAD_DIM:(r + 1) * HEAD_DIM]
                                       ).astype(mix_scr.dtype)

    _project_out(mix_scr[...], probs, kv_ref, wo_ref, x_ref, gf_ref, o_ref, hn_ref)


def _sgu_mixer(x, g, g_layer, w_bf, ln_g, ln_b, w_spatial, b_spatial_t, layer, kv, kv_layer,
               w_out_bf, g_ffn, g_ffn_layer):
    s, k = x.shape
    tm = ROW_STEP
    assert w_bf.shape == (k, 2 * SGU_W + MEM_W) and tm % SGU_CHUNK == 0
    assert w_out_bf.shape == (SGU_W + MEM_W, k)
    row = lambda cols: pl.BlockSpec((tm, cols), lambda i: (i, 0))
    return pl.pallas_call(
        _sgu_mixer_kernel,
        grid=(s // tm,),
        in_specs=[row(k), _gain_spec(g_layer, k), _resident_spec(w_bf.shape),
                  _gain_spec(layer, SGU_W), _gain_spec(layer, SGU_W),
                  pl.BlockSpec((None, SGU_GROUPS, SGU_CHUNK, SGU_CHUNK), lambda i: (layer, 0, 0, 0)),
                  pl.BlockSpec((None, SGU_CHUNK, SGU_GROUPS), lambda i: (layer, 0, 0)),
                  pl.BlockSpec((None, MEM_LEN, 2 * MEM_W), lambda i: (kv_layer, 0, 0)),
                  _resident_spec(w_out_bf.shape), _gain_spec(g_ffn_layer, k)],
        out_specs=[row(k), row(k)],
        out_shape=[jax.ShapeDtypeStruct((s, k), F32), jax.ShapeDtypeStruct((s, k), BF16)],
        scratch_shapes=[pltpu.VMEM((tm, k), BF16), pltpu.VMEM((tm, SGU_W), F32),
                        pltpu.VMEM((tm, SGU_W), BF16), pltpu.VMEM((tm, SGU_W), BF16),
                        pltpu.VMEM((SGU_GROUPS, SGU_CHUNK, SGU_CHUNK), BF16)],
        compiler_params=_params("arbitrary"),
        name="sgu_mixer",
    )(x, _gain_arg(g), w_bf, _gain_arg(ln_g), _gain_arg(ln_b), w_spatial, b_spatial_t, kv,
      w_out_bf, _gain_arg(g_ffn))


def _out_proj_kernel(a_ref, q_ref, kv_ref, w_ref, x_ref, g_ref, o_ref, hn_ref):
    probs = _memory_scores([q_ref[:, h * HEAD_DIM:(h + 1) * HEAD_DIM] for h in range(MEM_HEADS)],
                           kv_ref)
    _project_out(a_ref[...], probs, kv_ref, w_ref, x_ref, g_ref, o_ref, hn_ref)


def _out_proj(a, q_src, q_col_block, kv, kv_layer, w_bf, x, g, g_layer):
    s, ka = a.shape
    width = x.shape[1]
    tm = ROW_STEP
    assert w_bf.shape == (ka + MEM_W, width)
    row = lambda cols: pl.BlockSpec((tm, cols), lambda i: (i, 0))
    return pl.pallas_call(
        _out_proj_kernel,
        grid=(s // tm,),
        in_specs=[row(ka), pl.BlockSpec((tm, MEM_W), lambda i: (i, q_col_block)),
                  pl.BlockSpec((None, MEM_LEN, 2 * MEM_W), lambda i: (kv_layer, 0, 0)),
                  _resident_spec((ka + MEM_W, width)), row(width), _gain_spec(g_layer, width)],
        out_specs=[row(width), row(width)],
        out_shape=[jax.ShapeDtypeStruct((s, width), F32), jax.ShapeDtypeStruct((s, width), BF16)],
        compiler_params=_params("parallel"),
        name="out_proj",
    )(a, q_src, kv, w_bf, x, _gain_arg(g))


def _rows(ref, start, size, stride):
    if stride == 1:
        return ref[pl.ds(start, size), :]
    return ref[pl.ds(start, size, stride=stride), :]


V7X_FREE_SUBLANE_STRIDE = 4


def _presplit(d):
    return d // V7X_FREE_SUBLANE_STRIDE if d > V7X_FREE_SUBLANE_STRIDE else 1


def _attn_batch(units):
    scores = [lax.dot_general((q * ATTN_SCALE_LOG2).astype(BF16), kc.astype(BF16),
                              (((1,), (1,)), ((), ())), preferred_element_type=F32) + bias
              for q, kc, _, bias in units]
    probs = []
    for s in scores:
        m = jnp.max(s, axis=-1, keepdims=True)
        probs.append((jnp.exp2(s - m).astype(BF16), m))
    ones = jnp.ones((2 * BLK, HEAD_DIM), BF16)
    outs = []
    for (p, m), (_, _, vc, _) in zip(probs, units):
        ol = jnp.dot(p, jnp.concatenate([vc.astype(BF16), ones], axis=1),
                     preferred_element_type=F32)
        o, l = ol[:, :HEAD_DIM], ol[:, HEAD_DIM:]
        outs.append((o * (1.0 / l), m + jnp.log2(l)))
    return outs


def _attn_kernel(*refs):
    ngrp = len(ATTN_GROUPS)
    in_refs = refs[:5 * ngrp]
    o_ref = refs[5 * ngrp]
    pos = 5 * ngrp + 1
    o_scr, l_scr = refs[pos:pos + ngrp], refs[pos + ngrp:pos + 2 * ngrp]
    split_refs = refs[pos + 2 * ngrp:pos + 2 * ngrp + 7]
    bias_ref = refs[-1]

    qi = lax.broadcasted_iota(jnp.int32, (BLK, 2 * BLK), 0)
    ki = lax.broadcasted_iota(jnp.int32, (BLK, 2 * BLK), 1)
    lo = jnp.where(pl.program_id(0) == 0, BLK, 0)
    bias_ref[0] = jnp.where((ki >= qi) & (ki <= qi + N_BACK), 0.0, NEG_INF)
    bias_ref[1] = jnp.where((ki >= jnp.maximum(qi, lo)) & (ki <= qi + N_BACK), 0.0, NEG_INF)

    for g, (_, d) in enumerate(ATTN_GROUPS):
        q_ref, k_ref, v_ref, kp_ref, vp_ref = in_refs[5 * g:5 * g + 5]
        og, lg = o_scr[g], l_scr[g]
        sub = d * BLK
        f = _presplit(d)
        d2 = d // f
        qlen, sublen = ATTN_SPAN // f, sub // f
        if f > 1:
            q_src, kp_src, k_src, vp_src, v_src, o_dst, l_dst = split_refs
            for rf in range(f):
                for src, dst, n in ((q_ref, q_src, qlen), (k_ref, k_src, qlen), (v_ref, v_src, qlen),
                                    (kp_ref, kp_src, sublen), (vp_ref, vp_src, sublen)):
                    dst[pl.ds(rf * n, n), :] = _rows(src, rf, n, f)
        else:
            q_src, kp_src, k_src, vp_src, v_src, o_dst, l_dst = (
                q_ref, kp_ref, k_ref, vp_ref, v_ref, og, lg)

        def key_rows(prev_src, src, c, rf, r2):
            if c == 0:
                return jnp.concatenate([_rows(prev_src, rf * sublen + r2, BLK, d2),
                                        _rows(src, rf * qlen + r2, BLK, d2)], axis=0)
            return _rows(src, rf * qlen + (c - 1) * sublen + r2, 2 * BLK, d2)

        units, starts = [], []
        for idx in range(ATTN_SPAN // BLK):
            c, r = divmod(idx, d)
            rf, r2 = r % f, r // f
            qstart = rf * qlen + c * sublen + r2
            units.append((_rows(q_src, qstart, BLK, d2), key_rows(kp_src, k_src, c, rf, r2),
                          key_rows(vp_src, v_src, c, rf, r2), bias_ref[1 if c == 0 else 0]))
            starts.append(qstart)
        for qstart, (o, lse) in zip(starts, _attn_batch(units)):
            rows = pl.ds(qstart, BLK) if d2 == 1 else pl.ds(qstart, BLK, stride=d2)
            o_dst[rows, :] = o
            l_dst[rows, :] = lse
        if f > 1:
            for rf in range(f):
                og[pl.ds(rf, qlen, stride=f), :] = o_dst[pl.ds(rf * qlen, qlen), :]
                lg[pl.ds(rf, qlen, stride=f), :] = l_dst[pl.ds(rf * qlen, qlen), :]

    lses = [l[...] for l in l_scr]
    mx = functools.reduce(jnp.maximum, lses)
    ws = [jnp.exp2(l - mx) for l in lses]
    num = functools.reduce(lambda a, b: a + b, [w * o[...] for w, o in zip(ws, o_scr)])
    den = functools.reduce(lambda a, b: a + b, ws)
    o_ref[...] = (num * (1.0 / den)).astype(o_ref.dtype)


def _dilated_attention(qk, vm):
    s = qk.shape[0]
    assert s % ATTN_SPAN == 0
    hpg = HEADS_PER_GROUP
    in_specs, args = [], []
    for g, (_, d) in enumerate(ATTN_GROUPS):
        sub = d * BLK
        ratio = ATTN_SPAN // sub
        qc, kc, vc = g * hpg, ATTN_HEADS + g * hpg, g * hpg
        cur = lambda col: pl.BlockSpec((ATTN_SPAN, HEAD_DIM), lambda i, h, col=col: (i, col + h))
        prev = lambda col, ratio=ratio, sub=sub: pl.BlockSpec(
            (sub, HEAD_DIM), lambda i, h, col=col, ratio=ratio: (jnp.maximum(i * ratio - 1, 0), col + h))
        in_specs += [cur(qc), cur(kc), cur(vc), prev(kc), prev(vc)]
        args += [qk, qk, vm, qk, vm]
    ngrp = len(ATTN_GROUPS)
    return pl.pallas_call(
        _attn_kernel,
        grid=(s // ATTN_SPAN, hpg),
        in_specs=in_specs,
        out_specs=pl.BlockSpec((ATTN_SPAN, HEAD_DIM), lambda i, h: (i, h)),
        out_shape=jax.ShapeDtypeStruct((s, ATTN_OUT_W), BF16),
        scratch_shapes=([pltpu.VMEM((ATTN_SPAN, HEAD_DIM), F32)] * (2 * ngrp)
                        + [pltpu.VMEM((ATTN_SPAN, HEAD_DIM), F32)] * 7
                        + [pltpu.VMEM((2, BLK, 2 * BLK), F32)]),
        compiler_params=_params("parallel", "arbitrary"),
        name="dilated_attention",
    )(*args)


def _ffn_down_rows_kernel(a_ref, w_ref, x_ref, *rest, final_norm):
    if final_norm:
        g_ref, o_ref = rest
    else:
        (o_ref,) = rest
    width = o_ref.shape[1]
    ssq = jnp.zeros((o_ref.shape[0], 1), F32)
    for c in range(width // V7X_MXU_COLS):
        cols = slice(c * V7X_MXU_COLS, (c + 1) * V7X_MXU_COLS)
        y = x_ref[:, cols] + jnp.dot(a_ref[...], w_ref[:, cols], preferred_element_type=F32)
        o_ref[:, cols] = y
        if final_norm:
            ssq = ssq + jnp.sum(y * y, axis=-1, keepdims=True)
    if final_norm:
        scale = lax.rsqrt(ssq * (1.0 / width) + NORM_EPS)
        for c in range(width // V7X_MXU_COLS):
            cols = slice(c * V7X_MXU_COLS, (c + 1) * V7X_MXU_COLS)
            o_ref[:, cols] = o_ref[:, cols] * scale * g_ref[:, cols]


def _ffn_down_rows(a, w_bf, x, final_g=None):
    s, f = a.shape
    width = x.shape[1]
    tm = ROW_STEP
    assert w_bf.shape == (f, width) and s % tm == 0
    row = lambda cols: pl.BlockSpec((tm, cols), lambda i: (i, 0))
    in_specs = [row(f), _resident_spec(w_bf.shape), row(width)]
    args = [a, w_bf, x]
    if final_g is not None:
        in_specs.append(_gain_spec(0, width))
        args.append(final_g.reshape(1, 1, width))
    return pl.pallas_call(
        functools.partial(_ffn_down_rows_kernel, final_norm=final_g is not None),
        grid=(s // tm,),
        in_specs=in_specs,
        out_specs=row(width),
        out_shape=jax.ShapeDtypeStruct((s, width), F32),
        compiler_params=_params("parallel"),
        name="ffn_down_rows",
    )(*args)


def _ffn_gate_up_kernel(h_ref, wg_ref, wu_ref, *rest, n_jobs):
    o_ref = rest[n_jobs]
    _run_cast_jobs(rest[:n_jobs], rest[n_jobs + 1:])
    for c in range(o_ref.shape[1] // V7X_MXU_COLS):
        cols = slice(c * V7X_MXU_COLS, (c + 1) * V7X_MXU_COLS)
        gate = jnp.dot(h_ref[...], wg_ref[:, cols].astype(BF16), preferred_element_type=F32)
        up = jnp.dot(h_ref[...], wu_ref[:, cols].astype(BF16), preferred_element_type=F32)
        o_ref[:, cols] = (gate * (1.0 / (1.0 + jnp.exp(-gate))) * up).astype(o_ref.dtype)


def _ffn_gate_up(h, w_gate, w_up, layer, cast_weights=()):
    s, k = h.shape
    f = w_gate.shape[2]
    tm, tf = GATE_UP_ROWS, GATE_UP_COLS
    nj = f // tf
    wspec = pl.BlockSpec((None, k, tf), lambda i, j: (layer, 0, j))
    job_in, job_args, job_out, job_shape = _cast_jobs(cast_weights, (s // tm) * nj,
                                                      lambda i, j: i * nj + j)
    return pl.pallas_call(
        functools.partial(_ffn_gate_up_kernel, n_jobs=len(job_args)),
        grid=(s // tm, nj),
        in_specs=[pl.BlockSpec((tm, k), lambda i, j: (i, 0)), wspec, wspec] + job_in,
        out_specs=[pl.BlockSpec((tm, tf), lambda i, j: (i, j))] + job_out,
        out_shape=[jax.ShapeDtypeStruct((s, f), BF16)] + job_shape,
        compiler_params=_params("arbitrary", "arbitrary"),
        name="ffn_gate_up",
    )(h, w_gate, w_up, *job_args)


def kernel(x, mem, positions, mix_norm, mem_norm, w_mem_kv, ffn_norm, w_gate, w_up, w_down,
           attn_w_in, attn_w_out, sgu_w_in, sgu_ln_g, sgu_ln_b, sgu_w_spatial, sgu_b_spatial,
           sgu_w_out, final_norm):
    b, s, d = x.shape
    assert (b, s, d) == (1, SEQ, D_MODEL) and mem.shape == (1, MEM_LEN, D_MODEL)
    xs = x.reshape(s, d)
    mems = mem.reshape(MEM_LEN, d)

    inv_freq = ROPE_THETA ** (-jnp.arange(ROT_HALF, dtype=F32) / ROT_HALF)
    invf = jnp.tile(jnp.concatenate([inv_freq, inv_freq]), ROPE_PACK)
    pos_col, invf = positions.reshape(s, 1), invf.reshape(1, HEAD_DIM)

    def mixer_weights(layer):
        return (attn_w_in, attn_w_out) if layer % 2 == 0 else (sgu_w_in, sgu_w_out)

    kv, w_in_bf = _mem_kv(mems, mem_norm, w_mem_kv, [(attn_w_in, 0)])
    w_out_bf = None
    for i in range(DEPTH):
        j = i // 2
        if i % 2 == 0:
            casts = [(attn_w_out, j)] if w_out_bf is None else []
            qk, vm, *cast = _attn_in_proj(xs, mix_norm, i, w_in_bf, pos_col, invf, casts)
            w_out_bf = cast[0] if cast else w_out_bf
            mix = _dilated_attention(qk, vm)
            xs, hn = _out_proj(mix, vm, ATTN_W // MEM_W, kv, i, w_out_bf, xs, ffn_norm, i)
        else:
            xs, hn = _sgu_mixer(xs, mix_norm, i, w_in_bf, sgu_ln_g, sgu_ln_b, sgu_w_spatial,
                                jnp.swapaxes(sgu_b_spatial, 1, 2), j, kv, i, w_out_bf, ffn_norm, i)
        last = i + 1 == DEPTH
        casts = [(w_down, i)] + ([] if last else [(w, (i + 1) // 2) for w in mixer_weights(i + 1)])
        act, w_down_bf, *next_bf = _ffn_gate_up(hn, w_gate, w_up, i, casts)
        w_in_bf, w_out_bf = next_bf if next_bf else (None, None)
        xs = _ffn_down_rows(act, w_down_bf, xs, final_g=final_norm if last else None)
    return xs.reshape(b, s, d)
```

```python
import functools
import math

import jax
import jax.numpy as jnp
from jax import lax
from jax.experimental import pallas as pl
from jax.experimental.pallas import tpu as pltpu

D_MODEL = 2048
SEQ = 8192
DEPTH = 2
MEM_LEN = 256
HEAD_DIM = 128
MEM_HEADS = 4
MEM_W = MEM_HEADS * HEAD_DIM
ATTN_GROUPS = ((128, 1), (512, 4), (2048, 16))
ATTN_HEADS = 12
HEADS_PER_GROUP = 4
ATTN_W = ATTN_HEADS * HEAD_DIM
ATTN_OUT_W = HEADS_PER_GROUP * HEAD_DIM
BLK = 128
SGU_GROUPS = 12
SGU_W = SGU_GROUPS * HEAD_DIM
SGU_CHUNK = 128
ROT_DIM = HEAD_DIM // 4
ROT_HALF = ROT_DIM // 2
ROPE_THETA = 500000.0
NORM_EPS = 1e-6
LN_EPS = 1e-5
NEG_INF = -1e30
ATTN_SCALE_LOG2 = HEAD_DIM ** -0.5 * math.log2(math.e)

N_BACK = BLK
assert all(w // d == N_BACK for w, d in ATTN_GROUPS)
ATTN_SPAN = max(d for _, d in ATTN_GROUPS) * BLK

V7X_VMEM_LIMIT_BYTES = 56 * 1024 * 1024
V7X_MXU_COLS = 256

ROW_STEP = 512
GATE_UP_ROWS, GATE_UP_COLS = 2048, 512
MEM_KV_COLS = 512

F32 = jnp.float32
BF16 = jnp.bfloat16


def _params(*semantics):
    return pltpu.CompilerParams(dimension_semantics=semantics,
                                vmem_limit_bytes=V7X_VMEM_LIMIT_BYTES)


def _rms_rows(x, g):
    ms = jnp.mean(x * x, axis=-1, keepdims=True)
    return x * lax.rsqrt(ms + NORM_EPS) * g


def _gain_spec(layer, k):
    return pl.BlockSpec((None, 1, k), lambda *_: (layer, 0, 0))


def _gain_arg(g):
    return g.reshape(g.shape[0], 1, g.shape[1])


ROPE_PACK = HEAD_DIM // ROT_DIM


def _rope_tables(pos_ref, invf_ref, c_ref, s1_ref, s2_ref):
    n = pos_ref.shape[0] // ROPE_PACK
    lane = lax.broadcasted_iota(jnp.int32, (n, HEAD_DIM), 1)
    group = lane // ROT_DIM
    pos = jnp.zeros((n, HEAD_DIM), F32)
    for q in range(ROPE_PACK):
        pos = jnp.where(group == q, pos_ref[q * n:(q + 1) * n, :].astype(F32), pos)
    ang = pos * invf_ref[...]
    cos, sin = jnp.cos(ang), jnp.sin(ang)
    for q in range(ROPE_PACK):
        cq = cos if q == 0 else pltpu.roll(cos, HEAD_DIM - q * ROT_DIM, 1)
        sq = sin if q == 0 else pltpu.roll(sin, HEAD_DIM - q * ROT_DIM, 1)
        rows = slice(q * n, (q + 1) * n)
        c_ref[rows, :] = jnp.where(lane < ROT_DIM, cq, 1.0)
        s1_ref[rows, :] = jnp.where(lane < ROT_HALF, 0.0, jnp.where(lane < ROT_DIM, sq, 0.0))
        s2_ref[rows, :] = jnp.where(lane < ROT_HALF, -sq, 0.0)


BF16_SUBLANES = 16


def _cast_jobs(weights, n_steps, step_of):
    in_specs, args, out_specs, out_shape = [], [], [], []
    for w, layer in weights:
        _, k, n = w.shape
        slab = -(-(-(-k // n_steps)) // BF16_SUBLANES) * BF16_SUBLANES
        while k % slab:
            slab += BF16_SUBLANES
        n_slabs = k // slab
        in_specs.append(pl.BlockSpec(
            (None, slab, n), lambda *g, layer=layer, n_slabs=n_slabs:
            (layer, jnp.minimum(step_of(*g), n_slabs - 1), 0)))
        out_specs.append(pl.BlockSpec(
            (slab, n), lambda *g, n_slabs=n_slabs: (jnp.minimum(step_of(*g), n_slabs - 1), 0)))
        args.append(w)
        out_shape.append(jax.ShapeDtypeStruct((k, n), BF16))
    return in_specs, args, out_specs, out_shape


def _run_cast_jobs(in_refs, out_refs):
    for src, dst in zip(in_refs, out_refs):
        dst[...] = src[...].astype(BF16)


def _mem_kv_kernel(x_ref, g_ref, w_ref, *rest, n_jobs):
    o_ref = rest[n_jobs]
    _run_cast_jobs(rest[:n_jobs], rest[n_jobs + 1:])
    h = _rms_rows(x_ref[...], g_ref[...]).astype(BF16)
    o_ref[...] = jnp.dot(h, w_ref[...].astype(BF16), preferred_element_type=F32)


def _mem_kv(x, g, w, cast_weights=()):
    m, k = x.shape
    layers, _, n = w.shape
    tn = MEM_KV_COLS
    assert n % tn == 0
    nt = n // tn
    job_in, job_args, job_out, job_shape = _cast_jobs(cast_weights, layers * nt, lambda l, j: l * nt + j)
    return pl.pallas_call(
        functools.partial(_mem_kv_kernel, n_jobs=len(job_args)),
        grid=(layers, nt),
        in_specs=[pl.BlockSpec((m, k), lambda l, j: (0, 0)),
                  pl.BlockSpec((None, 1, k), lambda l, j: (l, 0, 0)),
                  pl.BlockSpec((None, k, tn), lambda l, j: (l, 0, j))] + job_in,
        out_specs=[pl.BlockSpec((None, m, tn), lambda l, j: (l, 0, j))] + job_out,
        out_shape=[jax.ShapeDtypeStruct((layers, m, n), F32)] + job_shape,
        compiler_params=_params("arbitrary", "arbitrary"),
        name="mem_kv",
    )(x, _gain_arg(g), w, *job_args)


def _resident_spec(shape):
    return pl.BlockSpec(shape, lambda *_: (0,) * len(shape), pipeline_mode=pl.Buffered(1))


def _gelu_tanh(x):
    return x * (0.5 * (1.0 + jnp.tanh(math.sqrt(2.0 / math.pi) * (x + 0.044715 * (x * x * x)))))


def _pass(h_ref, w_ref, col):
    return jnp.dot(h_ref[...], w_ref[:, col:col + V7X_MXU_COLS], preferred_element_type=F32)


def _heads(acc):
    return [acc[:, j * HEAD_DIM:(j + 1) * HEAD_DIM] for j in range(acc.shape[1] // HEAD_DIM)]


def _attn_in_proj_kernel(x_ref, g_ref, w_ref, pos_ref, invf_ref, *rest, n_jobs):
    qk_ref, vm_ref = rest[n_jobs:n_jobs + 2]
    h_ref, c_ref, s1_ref, s2_ref = rest[2 * n_jobs + 2:]
    _run_cast_jobs(rest[:n_jobs], rest[n_jobs + 2:2 * n_jobs + 2])
    h_ref[...] = _rms_rows(x_ref[...], g_ref[...]).astype(BF16)
    _rope_tables(pos_ref, invf_ref, c_ref, s1_ref, s2_ref)
    for c in range(2 * ATTN_W // V7X_MXU_COLS):
        for j, t in enumerate(_heads(_pass(h_ref, w_ref, c * V7X_MXU_COLS))):
            lo = c * V7X_MXU_COLS + j * HEAD_DIM
            qk_ref[:, lo:lo + HEAD_DIM] = (t * c_ref[...] + pltpu.roll(t, ROT_HALF, 1) * s1_ref[...]
                                           + pltpu.roll(t, HEAD_DIM - ROT_HALF, 1) * s2_ref[...])
    for c in range((ATTN_W + MEM_W) // V7X_MXU_COLS):
        cols = slice(c * V7X_MXU_COLS, (c + 1) * V7X_MXU_COLS)
        vm_ref[:, cols] = _pass(h_ref, w_ref, 2 * ATTN_W + c * V7X_MXU_COLS)


def _attn_in_proj(x, g, g_layer, w_bf, pos_col, invf, cast_weights=()):
    s, k = x.shape
    tm = ROW_STEP
    assert w_bf.shape == (k, 3 * ATTN_W + MEM_W)
    row = lambda cols: pl.BlockSpec((tm, cols), lambda i: (i, 0))
    job_in, job_args, job_out, job_shape = _cast_jobs(cast_weights, s // tm, lambda i: i)
    return pl.pallas_call(
        functools.partial(_attn_in_proj_kernel, n_jobs=len(job_args)),
        grid=(s // tm,),
        in_specs=[row(k), _gain_spec(g_layer, k), _resident_spec(w_bf.shape), row(1),
                  pl.BlockSpec((1, HEAD_DIM), lambda i: (0, 0))] + job_in,
        out_specs=[row(2 * ATTN_W), row(ATTN_W + MEM_W)] + job_out,
        out_shape=[jax.ShapeDtypeStruct((s, 2 * ATTN_W), F32),
                   jax.ShapeDtypeStruct((s, ATTN_W + MEM_W), F32)] + job_shape,
        scratch_shapes=[pltpu.VMEM((tm, k), BF16)] + [pltpu.VMEM((tm, HEAD_DIM), F32)] * 3,
        compiler_params=_params("arbitrary"),
        name="attn_in_proj",
    )(x, _gain_arg(g), w_bf, pos_col, invf, *job_args)


def _memory_scores(q_heads, kv_ref):
    probs = []
    for h, q in enumerate(q_heads):
        sl = slice(h * HEAD_DIM, (h + 1) * HEAD_DIM)
        s = lax.dot_general((q * ATTN_SCALE_LOG2).astype(BF16), kv_ref[:, sl].astype(BF16),
                            (((1,), (1,)), ((), ())), preferred_element_type=F32)
        probs.append(jnp.exp2(s - jnp.max(s, axis=-1, keepdims=True)).astype(BF16))
    return probs


def _memory_values(probs, kv_ref):
    ones = jnp.ones((MEM_LEN, HEAD_DIM), BF16)
    heads = []
    for h, p in enumerate(probs):
        v = kv_ref[:, MEM_W + h * HEAD_DIM:MEM_W + (h + 1) * HEAD_DIM].astype(BF16)
        ol = jnp.dot(p, jnp.concatenate([v, ones], axis=1), preferred_element_type=F32)
        heads.append((ol[:, :HEAD_DIM] * (1.0 / ol[:, HEAD_DIM:])).astype(BF16))
    return jnp.concatenate(heads, axis=1)


OUT_PROJ_COLS = 512


def _project_out(mix, probs, kv_ref, w_ref, x_ref, g_ref, o_ref, hn_ref):
    ka = mix.shape[1]
    width = o_ref.shape[1]
    chunks = [slice(c * OUT_PROJ_COLS, (c + 1) * OUT_PROJ_COLS) for c in range(width // OUT_PROJ_COLS)]
    for cols in chunks:
        o_ref[:, cols] = x_ref[:, cols] + jnp.dot(mix, w_ref[:ka, cols], preferred_element_type=F32)
    b = _memory_values(probs, kv_ref)
    ssq = jnp.zeros((o_ref.shape[0], 1), F32)
    for cols in chunks:
        y = o_ref[:, cols] + jnp.dot(b, w_ref[ka:, cols], preferred_element_type=F32)
        o_ref[:, cols] = y
        ssq = ssq + jnp.sum(y * y, axis=-1, keepdims=True)
    scale = lax.rsqrt(ssq * (1.0 / width) + NORM_EPS)
    for cols in chunks:
        hn_ref[:, cols] = (o_ref[:, cols] * scale * g_ref[:, cols]).astype(BF16)


def _sgu_mixer_kernel(x_ref, g_ref, w_ref, lng_ref, lnb_ref, ws_ref, bt_ref, kv_ref, wo_ref, gf_ref,
                      o_ref, hn_ref, h_ref, v_scr, vn_scr, mix_scr, wsb_scr):
    @pl.when(pl.program_id(0) == 0)
    def _():
        t = lax.broadcasted_iota(jnp.int32, (SGU_CHUNK, SGU_CHUNK), 0)
        s = lax.broadcasted_iota(jnp.int32, (SGU_CHUNK, SGU_CHUNK), 1)
        for g in range(SGU_GROUPS):
            wsb_scr[g] = jnp.where(t >= s, ws_ref[g], 0.0).astype(BF16)

    h_ref[...] = _rms_rows(x_ref[...], g_ref[...]).astype(BF16)
    for c in range(SGU_W // V7X_MXU_COLS):
        cols = slice(c * V7X_MXU_COLS, (c + 1) * V7X_MXU_COLS)
        v_scr[:, cols] = _gelu_tanh(_pass(h_ref, w_ref, SGU_W + c * V7X_MXU_COLS))
    q_heads = []
    for c in range(MEM_W // V7X_MXU_COLS):
        q_heads += _heads(_pass(h_ref, w_ref, 2 * SGU_W + c * V7X_MXU_COLS))
    probs = _memory_scores(q_heads, kv_ref)
    v = v_scr[...]
    mu = jnp.mean(v, axis=-1, keepdims=True)
    vc = v - mu
    var = jnp.mean(vc * vc, axis=-1, keepdims=True)
    vn_scr[...] = (vc * lax.rsqrt(var + LN_EPS) * lng_ref[...] + lnb_ref[...]).astype(BF16)

    bt = bt_ref[...]
    n_chunks = mix_scr.shape[0] // SGU_CHUNK
    for c in range(SGU_W // V7X_MXU_COLS):
        for j, u in enumerate(_heads(_gelu_tanh(_pass(h_ref, w_ref, c * V7X_MXU_COLS)))):
            g = c * (V7X_MXU_COLS // HEAD_DIM) + j
            cols = slice(g * HEAD_DIM, (g + 1) * HEAD_DIM)
            vg = jnp.concatenate([vn_scr[r * SGU_CHUNK:(r + 1) * SGU_CHUNK, cols]
                                  for r in range(n_chunks)], axis=1)
            mixed = jnp.dot(wsb_scr[g], vg, preferred_element_type=F32) + bt[:, g:g + 1]
            for r in range(n_chunks):
                rows = slice(r * SGU_CHUNK, (r + 1) * SGU_CHUNK)
                mix_scr[rows, cols] = (u[rows, :] * mixed[:, r * HEAD_DIM:(r + 1) * HEAD_DIM]
                                       ).astype(mix_scr.dtype)

    _project_out(mix_scr[...], probs, kv_ref, wo_ref, x_ref, gf_ref, o_ref, hn_ref)


def _sgu_mixer(x, g, g_layer, w_bf, ln_g, ln_b, w_spatial, b_spatial_t, layer, kv, kv_layer,
               w_out_bf, g_ffn, g_ffn_layer):
    s, k = x.shape
    tm = ROW_STEP
    assert w_bf.shape == (k, 2 * SGU_W + MEM_W) and tm % SGU_CHUNK == 0
    assert w_out_bf.shape == (SGU_W + MEM_W, k)
    row = lambda cols: pl.BlockSpec((tm, cols), lambda i: (i, 0))
    return pl.pallas_call(
        _sgu_mixer_kernel,
        grid=(s // tm,),
        in_specs=[row(k), _gain_spec(g_layer, k), _resident_spec(w_bf.shape),
                  _gain_spec(layer, SGU_W), _gain_spec(layer, SGU_W),
                  pl.BlockSpec((None, SGU_GROUPS, SGU_CHUNK, SGU_CHUNK), lambda i: (layer, 0, 0, 0)),
                  pl.BlockSpec((None, SGU_CHUNK, SGU_GROUPS), lambda i: (layer, 0, 0)),
                  pl.BlockSpec((None, MEM_LEN, 2 * MEM_W), lambda i: (kv_layer, 0, 0)),
                  _resident_spec(w_out_bf.shape), _gain_spec(g_ffn_layer, k)],
        out_specs=[row(k), row(k)],
        out_shape=[jax.ShapeDtypeStruct((s, k), F32), jax.ShapeDtypeStruct((s, k), BF16)],
        scratch_shapes=[pltpu.VMEM((tm, k), BF16), pltpu.VMEM((tm, SGU_W), F32),
                        pltpu.VMEM((tm, SGU_W), BF16), pltpu.VMEM((tm, SGU_W), BF16),
                        pltpu.VMEM((SGU_GROUPS, SGU_CHUNK, SGU_CHUNK), BF16)],
        compiler_params=_params("arbitrary"),
        name="sgu_mixer",
    )(x, _gain_arg(g), w_bf, _gain_arg(ln_g), _gain_arg(ln_b), w_spatial, b_spatial_t, kv,
      w_out_bf, _gain_arg(g_ffn))


def _out_proj_kernel(a_ref, q_ref, kv_ref, w_ref, x_ref, g_ref, o_ref, hn_ref):
    probs = _memory_scores([q_ref[:, h * HEAD_DIM:(h + 1) * HEAD_DIM] for h in range(MEM_HEADS)],
                           kv_ref)
    _project_out(a_ref[...], probs, kv_ref, w_ref, x_ref, g_ref, o_ref, hn_ref)


def _out_proj(a, q_src, q_col_block, kv, kv_layer, w_bf, x, g, g_layer):
    s, ka = a.shape
    width = x.shape[1]
    tm = ROW_STEP
    assert w_bf.shape == (ka + MEM_W, width)
    row = lambda cols: pl.BlockSpec((tm, cols), lambda i: (i, 0))
    return pl.pallas_call(
        _out_proj_kernel,
        grid=(s // tm,),
        in_specs=[row(ka), pl.BlockSpec((tm, MEM_W), lambda i: (i, q_col_block)),
                  pl.BlockSpec((None, MEM_LEN, 2 * MEM_W), lambda i: (kv_layer, 0, 0)),
                  _resident_spec((ka + MEM_W, width)), row(width), _gain_spec(g_layer, width)],
        out_specs=[row(width), row(width)],
        out_shape=[jax.ShapeDtypeStruct((s, width), F32), jax.ShapeDtypeStruct((s, width), BF16)],
        compiler_params=_params("parallel"),
        name="out_proj",
    )(a, q_src, kv, w_bf, x, _gain_arg(g))


def _rows(ref, start, size, stride):
    if stride == 1:
        return ref[pl.ds(start, size), :]
    return ref[pl.ds(start, size, stride=stride), :]


V7X_FREE_SUBLANE_STRIDE = 4


def _presplit(d):
    return d // V7X_FREE_SUBLANE_STRIDE if d > V7X_FREE_SUBLANE_STRIDE else 1


ATTN_SUB_BATCH = 8


def _attn_batch(units):
    scores = [lax.dot_general((q * ATTN_SCALE_LOG2).astype(BF16), kc.astype(BF16),
                              (((1,), (1,)), ((), ())), preferred_element_type=F32) + bias
              for q, kc, _, bias in units]
    probs = []
    for s in scores:
        m = jnp.max(s, axis=-1, keepdims=True)
        probs.append((jnp.exp2(s - m).astype(BF16), m))
    ones = jnp.ones((2 * BLK, HEAD_DIM), BF16)
    outs = []
    for (p, m), (_, _, vc, _) in zip(probs, units):
        ol = jnp.dot(p, jnp.concatenate([vc.astype(BF16), ones], axis=1),
                     preferred_element_type=F32)
        o, l = ol[:, :HEAD_DIM], ol[:, HEAD_DIM:]
        outs.append((o * (1.0 / l), m + jnp.log2(l)))
    return outs


def _attn_kernel(*refs):
    ngrp = len(ATTN_GROUPS)
    in_refs = refs[:5 * ngrp]
    o_ref = refs[5 * ngrp]
    pos = 5 * ngrp + 1
    o_scr, l_scr = refs[pos:pos + ngrp], refs[pos + ngrp:pos + 2 * ngrp]
    split_refs = refs[pos + 2 * ngrp:pos + 2 * ngrp + 7]
    bias_ref = refs[-1]

    qi = lax.broadcasted_iota(jnp.int32, (BLK, 2 * BLK), 0)
    ki = lax.broadcasted_iota(jnp.int32, (BLK, 2 * BLK), 1)
    lo = jnp.where(pl.program_id(0) == 0, BLK, 0)
    bias_ref[0] = jnp.where((ki >= qi) & (ki <= qi + N_BACK), 0.0, NEG_INF)
    bias_ref[1] = jnp.where((ki >= jnp.maximum(qi, lo)) & (ki <= qi + N_BACK), 0.0, NEG_INF)

    for g, (_, d) in enumerate(ATTN_GROUPS):
        q_ref, k_ref, v_ref, kp_ref, vp_ref = in_refs[5 * g:5 * g + 5]
        og, lg = o_scr[g], l_scr[g]
        sub = d * BLK
        f = _presplit(d)
        d2 = d // f
        qlen, sublen = ATTN_SPAN // f, sub // f
        if f > 1:
            q_src, kp_src, k_src, vp_src, v_src, o_dst, l_dst = split_refs
            for rf in range(f):
                for src, dst, n in ((q_ref, q_src, qlen), (k_ref, k_src, qlen), (v_ref, v_src, qlen),
                                    (kp_ref, kp_src, sublen), (vp_ref, vp_src, sublen)):
                    dst[pl.ds(rf * n, n), :] = _rows(src, rf, n, f)
        else:
            q_src, kp_src, k_src, vp_src, v_src, o_dst, l_dst = (
                q_ref, kp_ref, k_ref, vp_ref, v_ref, og, lg)

        def key_rows(prev_src, src, c, rf, r2):
            if c == 0:
                return jnp.concatenate([_rows(prev_src, rf * sublen + r2, BLK, d2),
                                        _rows(src, rf * qlen + r2, BLK, d2)], axis=0)
            return _rows(src, rf * qlen + (c - 1) * sublen + r2, 2 * BLK, d2)

        units, starts = [], []
        for idx in range(ATTN_SPAN // BLK):
            c, r = divmod(idx, d)
            rf, r2 = r % f, r // f
            qstart = rf * qlen + c * sublen + r2
            units.append((_rows(q_src, qstart, BLK, d2), key_rows(kp_src, k_src, c, rf, r2),
                          key_rows(vp_src, v_src, c, rf, r2), bias_ref[1 if c == 0 else 0]))
            starts.append(qstart)
        results = []
        for b0 in range(0, len(units), ATTN_SUB_BATCH):
            results += _attn_batch(units[b0:b0 + ATTN_SUB_BATCH])
        for qstart, (o, lse) in zip(starts, results):
            rows = pl.ds(qstart, BLK) if d2 == 1 else pl.ds(qstart, BLK, stride=d2)
            o_dst[rows, :] = o
            l_dst[rows, :] = lse
        if f > 1:
            for rf in range(f):
                og[pl.ds(rf, qlen, stride=f), :] = o_dst[pl.ds(rf * qlen, qlen), :]
                lg[pl.ds(rf, qlen, stride=f), :] = l_dst[pl.ds(rf * qlen, qlen), :]

    lses = [l[...] for l in l_scr]
    mx = functools.reduce(jnp.maximum, lses)
    ws = [jnp.exp2(l - mx) for l in lses]
    num = functools.reduce(lambda a, b: a + b, [w * o[...] for w, o in zip(ws, o_scr)])
    den = functools.reduce(lambda a, b: a + b, ws)
    o_ref[...] = (num * (1.0 / den)).astype(o_ref.dtype)


def _dilated_attention(qk, vm):
    s = qk.shape[0]
    assert s % ATTN_SPAN == 0
    hpg = HEADS_PER_GROUP
    in_specs, args = [], []
    for g, (_, d) in enumerate(ATTN_GROUPS):
        sub = d * BLK
        ratio = ATTN_SPAN // sub
        qc, kc, vc = g * hpg, ATTN_HEADS + g * hpg, g * hpg
        cur = lambda col: pl.BlockSpec((ATTN_SPAN, HEAD_DIM), lambda i, h, col=col: (i, col + h))
        prev = lambda col, ratio=ratio, sub=sub: pl.BlockSpec(
            (sub, HEAD_DIM), lambda i, h, col=col, ratio=ratio: (jnp.maximum(i * ratio - 1, 0), col + h))
        in_specs += [cur(qc), cur(kc), cur(vc), prev(kc), prev(vc)]
        args += [qk, qk, vm, qk, vm]
    ngrp = len(ATTN_GROUPS)
    return pl.pallas_call(
        _attn_kernel,
        grid=(s // ATTN_SPAN, hpg),
        in_specs=in_specs,
        out_specs=pl.BlockSpec((ATTN_SPAN, HEAD_DIM), lambda i, h: (i, h)),
        out_shape=jax.ShapeDtypeStruct((s, ATTN_OUT_W), BF16),
        scratch_shapes=([pltpu.VMEM((ATTN_SPAN, HEAD_DIM), F32)] * (2 * ngrp)
                        + [pltpu.VMEM((ATTN_SPAN, HEAD_DIM), F32)] * 7
                        + [pltpu.VMEM((2, BLK, 2 * BLK), F32)]),
        compiler_params=_params("parallel", "arbitrary"),
        name="dilated_attention",
    )(*args)


def _ffn_down_rows_kernel(a_ref, w_ref, x_ref, *rest, final_norm):
    if final_norm:
        g_ref, o_ref = rest
    else:
        (o_ref,) = rest
    width = o_ref.shape[1]
    ssq = jnp.zeros((o_ref.shape[0], 1), F32)
    for c in range(width // V7X_MXU_COLS):
        cols = slice(c * V7X_MXU_COLS, (c + 1) * V7X_MXU_COLS)
        y = x_ref[:, cols] + jnp.dot(a_ref[...], w_ref[:, cols], preferred_element_type=F32)
        o_ref[:, cols] = y
        if final_norm:
            ssq = ssq + jnp.sum(y * y, axis=-1, keepdims=True)
    if final_norm:
        scale = lax.rsqrt(ssq * (1.0 / width) + NORM_EPS)
        for c in range(width // V7X_MXU_COLS):
            cols = slice(c * V7X_MXU_COLS, (c + 1) * V7X_MXU_COLS)
            o_ref[:, cols] = o_ref[:, cols] * scale * g_ref[:, cols]


def _ffn_down_rows(a, w_bf, x, final_g=None):
    s, f = a.shape
    width = x.shape[1]
    tm = ROW_STEP
    assert w_bf.shape == (f, width) and s % tm == 0
    row = lambda cols: pl.BlockSpec((tm, cols), lambda i: (i, 0))
    in_specs = [row(f), _resident_spec(w_bf.shape), row(width)]
    args = [a, w_bf, x]
    if final_g is not None:
        in_specs.append(_gain_spec(0, width))
        args.append(final_g.reshape(1, 1, width))
    return pl.pallas_call(
        functools.partial(_ffn_down_rows_kernel, final_norm=final_g is not None),
        grid=(s // tm,),
        in_specs=in_specs,
        out_specs=row(width),
        out_shape=jax.ShapeDtypeStruct((s, width), F32),
        compiler_params=_params("parallel"),
        name="ffn_down_rows",
    )(*args)


def _ffn_gate_up_kernel(h_ref, wg_ref, wu_ref, *rest, n_jobs):
    o_ref = rest[n_jobs]
    _run_cast_jobs(rest[:n_jobs], rest[n_jobs + 1:])
    for c in range(o_ref.shape[1] // V7X_MXU_COLS):
        cols = slice(c * V7X_MXU_COLS, (c + 1) * V7X_MXU_COLS)
        gate = jnp.dot(h_ref[...], wg_ref[:, cols].astype(BF16), preferred_element_type=F32)
        up = jnp.dot(h_ref[...], wu_ref[:, cols].astype(BF16), preferred_element_type=F32)
        o_ref[:, cols] = (gate * (1.0 / (1.0 + jnp.exp(-gate))) * up).astype(o_ref.dtype)


def _ffn_gate_up(h, w_gate, w_up, layer, cast_weights=()):
    s, k = h.shape
    f = w_gate.shape[2]
    tm, tf = GATE_UP_ROWS, GATE_UP_COLS
    nj = f // tf
    wspec = pl.BlockSpec((None, k, tf), lambda i, j: (layer, 0, j))
    job_in, job_args, job_out, job_shape = _cast_jobs(cast_weights, (s // tm) * nj,
                                                      lambda i, j: i * nj + j)
    return pl.pallas_call(
        functools.partial(_ffn_gate_up_kernel, n_jobs=len(job_args)),
        grid=(s // tm, nj),
        in_specs=[pl.BlockSpec((tm, k), lambda i, j: (i, 0)), wspec, wspec] + job_in,
        out_specs=[pl.BlockSpec((tm, tf), lambda i, j: (i, j))] + job_out,
        out_shape=[jax.ShapeDtypeStruct((s, f), BF16)] + job_shape,
        compiler_params=_params("arbitrary", "arbitrary"),
        name="ffn_gate_up",
    )(h, w_gate, w_up, *job_args)


def kernel(x, mem, positions, mix_norm, mem_norm, w_mem_kv, ffn_norm, w_gate, w_up, w_down,
           attn_w_in, attn_w_out, sgu_w_in, sgu_ln_g, sgu_ln_b, sgu_w_spatial, sgu_b_spatial,
           sgu_w_out, final_norm):
    b, s, d = x.shape
    assert (b, s, d) == (1, SEQ, D_MODEL) and mem.shape == (1, MEM_LEN, D_MODEL)
    xs = x.reshape(s, d)
    mems = mem.reshape(MEM_LEN, d)

    inv_freq = ROPE_THETA ** (-jnp.arange(ROT_HALF, dtype=F32) / ROT_HALF)
    invf = jnp.tile(jnp.concatenate([inv_freq, inv_freq]), ROPE_PACK)
    pos_col, invf = positions.reshape(s, 1), invf.reshape(1, HEAD_DIM)

    def mixer_weights(layer):
        return (attn_w_in, attn_w_out) if layer % 2 == 0 else (sgu_w_in, sgu_w_out)

    kv, w_in_bf = _mem_kv(mems, mem_norm, w_mem_kv, [(attn_w_in, 0)])
    w_out_bf = None
    for i in range(DEPTH):
        j = i // 2
        if i % 2 == 0:
            casts = [(attn_w_out, j)] if w_out_bf is None else []
            qk, vm, *cast = _attn_in_proj(xs, mix_norm, i, w_in_bf, pos_col, invf, casts)
            w_out_bf = cast[0] if cast else w_out_bf
            mix = _dilated_attention(qk, vm)
            xs, hn = _out_proj(mix, vm, ATTN_W // MEM_W, kv, i, w_out_bf, xs, ffn_norm, i)
        else:
            xs, hn = _sgu_mixer(xs, mix_norm, i, w_in_bf, sgu_ln_g, sgu_ln_b, sgu_w_spatial,
                                jnp.swapaxes(sgu_b_spatial, 1, 2), j, kv, i, w_out_bf, ffn_norm, i)
        last = i + 1 == DEPTH
        casts = [(w_down, i)] + ([] if last else [(w, (i + 1) // 2) for w in mixer_weights(i + 1)])
        act, w_down_bf, *next_bf = _ffn_gate_up(hn, w_gate, w_up, i, casts)
        w_in_bf, w_out_bf = next_bf if next_bf else (None, None)
        xs = _ffn_down_rows(act, w_down_bf, xs, final_g=final_norm if last else None)
    return xs.reshape(b, s, d)
```

```python
import functools
import math

import jax
import jax.numpy as jnp
from jax import lax
from jax.experimental import pallas as pl
from jax.experimental.pallas import tpu as pltpu

D_MODEL = 2048
SEQ = 8192
DEPTH = 2
MEM_LEN = 256
HEAD_DIM = 128
MEM_HEADS = 4
MEM_W = MEM_HEADS * HEAD_DIM
ATTN_GROUPS = ((128, 1), (512, 4), (2048, 16))
ATTN_HEADS = 12
HEADS_PER_GROUP = 4
ATTN_W = ATTN_HEADS * HEAD_DIM
ATTN_OUT_W = HEADS_PER_GROUP * HEAD_DIM
BLK = 128
SGU_GROUPS = 12
SGU_W = SGU_GROUPS * HEAD_DIM
SGU_CHUNK = 128
ROT_DIM = HEAD_DIM // 4
ROT_HALF = ROT_DIM // 2
ROPE_THETA = 500000.0
NORM_EPS = 1e-6
LN_EPS = 1e-5
NEG_INF = -1e30
ATTN_SCALE_LOG2 = HEAD_DIM ** -0.5 * math.log2(math.e)

N_BACK = BLK
assert all(w // d == N_BACK for w, d in ATTN_GROUPS)
ATTN_SPAN = max(d for _, d in ATTN_GROUPS) * BLK

V7X_VMEM_LIMIT_BYTES = 56 * 1024 * 1024
V7X_MXU_COLS = 256

ROW_STEP = 512
GATE_UP_ROWS, GATE_UP_COLS = 2048, 512
MEM_KV_COLS = 512

F32 = jnp.float32
BF16 = jnp.bfloat16


def _params(*semantics):
    return pltpu.CompilerParams(dimension_semantics=semantics,
                                vmem_limit_bytes=V7X_VMEM_LIMIT_BYTES)


def _rms_rows(x, g):
    ms = jnp.mean(x * x, axis=-1, keepdims=True)
    return x * lax.rsqrt(ms + NORM_EPS) * g


def _gain_spec(layer, k):
    return pl.BlockSpec((None, 1, k), lambda *_: (layer, 0, 0))


def _gain_arg(g):
    return g.reshape(g.shape[0], 1, g.shape[1])


ROPE_PACK = HEAD_DIM // ROT_DIM


def _rope_tables(pos_ref, invf_ref, c_ref, s1_ref, s2_ref):
    n = pos_ref.shape[0] // ROPE_PACK
    lane = lax.broadcasted_iota(jnp.int32, (n, HEAD_DIM), 1)
    group = lane // ROT_DIM
    pos = jnp.zeros((n, HEAD_DIM), F32)
    for q in range(ROPE_PACK):
        pos = jnp.where(group == q, pos_ref[q * n:(q + 1) * n, :].astype(F32), pos)
    ang = pos * invf_ref[...]
    cos, sin = jnp.cos(ang), jnp.sin(ang)
    for q in range(ROPE_PACK):
        cq = cos if q == 0 else pltpu.roll(cos, HEAD_DIM - q * ROT_DIM, 1)
        sq = sin if q == 0 else pltpu.roll(sin, HEAD_DIM - q * ROT_DIM, 1)
        rows = slice(q * n, (q + 1) * n)
        c_ref[rows, :] = jnp.where(lane < ROT_DIM, cq, 1.0)
        s1_ref[rows, :] = jnp.where(lane < ROT_HALF, 0.0, jnp.where(lane < ROT_DIM, sq, 0.0))
        s2_ref[rows, :] = jnp.where(lane < ROT_HALF, -sq, 0.0)


BF16_SUBLANES = 16


def _cast_jobs(weights, n_steps, step_of):
    in_specs, args, out_specs, out_shape = [], [], [], []
    for w, layer in weights:
        _, k, n = w.shape
        slab = -(-(-(-k // n_steps)) // BF16_SUBLANES) * BF16_SUBLANES
        while k % slab:
            slab += BF16_SUBLANES
        n_slabs = k // slab
        in_specs.append(pl.BlockSpec(
            (None, slab, n), lambda *g, layer=layer, n_slabs=n_slabs:
            (layer, jnp.minimum(step_of(*g), n_slabs - 1), 0)))
        out_specs.append(pl.BlockSpec(
            (slab, n), lambda *g, n_slabs=n_slabs: (jnp.minimum(step_of(*g), n_slabs - 1), 0)))
        args.append(w)
        out_shape.append(jax.ShapeDtypeStruct((k, n), BF16))
    return in_specs, args, out_specs, out_shape


def _run_cast_jobs(in_refs, out_refs):
    for src, dst in zip(in_refs, out_refs):
        dst[...] = src[...].astype(BF16)


def _mem_kv_kernel(x_ref, g_ref, w_ref, *rest, n_jobs):
    o_ref = rest[n_jobs]
    _run_cast_jobs(rest[:n_jobs], rest[n_jobs + 1:])
    h = _rms_rows(x_ref[...], g_ref[...]).astype(BF16)
    o_ref[...] = jnp.dot(h, w_ref[...].astype(BF16), preferred_element_type=F32)


def _mem_kv(x, g, w, cast_weights=()):
    m, k = x.shape
    layers, _, n = w.shape
    tn = MEM_KV_COLS
    assert n % tn == 0
    nt = n // tn
    job_in, job_args, job_out, job_shape = _cast_jobs(cast_weights, layers * nt, lambda l, j: l * nt + j)
    return pl.pallas_call(
        functools.partial(_mem_kv_kernel, n_jobs=len(job_args)),
        grid=(layers, nt),
        in_specs=[pl.BlockSpec((m, k), lambda l, j: (0, 0)),
                  pl.BlockSpec((None, 1, k), lambda l, j: (l, 0, 0)),
                  pl.BlockSpec((None, k, tn), lambda l, j: (l, 0, j))] + job_in,
        out_specs=[pl.BlockSpec((None, m, tn), lambda l, j: (l, 0, j))] + job_out,
        out_shape=[jax.ShapeDtypeStruct((layers, m, n), F32)] + job_shape,
        compiler_params=_params("arbitrary", "arbitrary"),
        name="mem_kv",
    )(x, _gain_arg(g), w, *job_args)


def _resident_spec(shape):
    return pl.BlockSpec(shape, lambda *_: (0,) * len(shape), pipeline_mode=pl.Buffered(1))


def _gelu_tanh(x):
    return x * (0.5 * (1.0 + jnp.tanh(math.sqrt(2.0 / math.pi) * (x + 0.044715 * (x * x * x)))))


def _pass(h_ref, w_ref, col):
    return jnp.dot(h_ref[...], w_ref[:, col:col + V7X_MXU_COLS], preferred_element_type=F32)


def _heads(acc):
    return [acc[:, j * HEAD_DIM:(j + 1) * HEAD_DIM] for j in range(acc.shape[1] // HEAD_DIM)]


def _attn_in_proj_kernel(x_ref, g_ref, w_ref, pos_ref, invf_ref, *rest, n_jobs):
    qk_ref, vm_ref = rest[n_jobs:n_jobs + 2]
    h_ref, c_ref, s1_ref, s2_ref = rest[2 * n_jobs + 2:]
    _run_cast_jobs(rest[:n_jobs], rest[n_jobs + 2:2 * n_jobs + 2])
    h_ref[...] = _rms_rows(x_ref[...], g_ref[...]).astype(BF16)
    _rope_tables(pos_ref, invf_ref, c_ref, s1_ref, s2_ref)
    for c in range(2 * ATTN_W // V7X_MXU_COLS):
        for j, t in enumerate(_heads(_pass(h_ref, w_ref, c * V7X_MXU_COLS))):
            lo = c * V7X_MXU_COLS + j * HEAD_DIM
            qk_ref[:, lo:lo + HEAD_DIM] = (t * c_ref[...] + pltpu.roll(t, ROT_HALF, 1) * s1_ref[...]
                                           + pltpu.roll(t, HEAD_DIM - ROT_HALF, 1) * s2_ref[...])
    for c in range((ATTN_W + MEM_W) // V7X_MXU_COLS):
        cols = slice(c * V7X_MXU_COLS, (c + 1) * V7X_MXU_COLS)
        vm_ref[:, cols] = _pass(h_ref, w_ref, 2 * ATTN_W + c * V7X_MXU_COLS)


def _attn_in_proj(x, g, g_layer, w_bf, pos_col, invf, cast_weights=()):
    s, k = x.shape
    tm = ROW_STEP
    assert w_bf.shape == (k, 3 * ATTN_W + MEM_W)
    row = lambda cols: pl.BlockSpec((tm, cols), lambda i: (i, 0))
    job_in, job_args, job_out, job_shape = _cast_jobs(cast_weights, s // tm, lambda i: i)
    return pl.pallas_call(
        functools.partial(_attn_in_proj_kernel, n_jobs=len(job_args)),
        grid=(s // tm,),
        in_specs=[row(k), _gain_spec(g_layer, k), _resident_spec(w_bf.shape), row(1),
                  pl.BlockSpec((1, HEAD_DIM), lambda i: (0, 0))] + job_in,
        out_specs=[row(2 * ATTN_W), row(ATTN_W + MEM_W)] + job_out,
        out_shape=[jax.ShapeDtypeStruct((s, 2 * ATTN_W), F32),
                   jax.ShapeDtypeStruct((s, ATTN_W + MEM_W), F32)] + job_shape,
        scratch_shapes=[pltpu.VMEM((tm, k), BF16)] + [pltpu.VMEM((tm, HEAD_DIM), F32)] * 3,
        compiler_params=_params("arbitrary"),
        name="attn_in_proj",
    )(x, _gain_arg(g), w_bf, pos_col, invf, *job_args)


def _memory_scores(q_heads, kv_ref):
    probs = []
    for h, q in enumerate(q_heads):
        sl = slice(h * HEAD_DIM, (h + 1) * HEAD_DIM)
        s = lax.dot_general((q * ATTN_SCALE_LOG2).astype(BF16), kv_ref[:, sl].astype(BF16),
                            (((1,), (1,)), ((), ())), preferred_element_type=F32)
        probs.append(jnp.exp2(s - jnp.max(s, axis=-1, keepdims=True)).astype(BF16))
    return probs


def _memory_values(probs, kv_ref):
    ones = jnp.ones((MEM_LEN, HEAD_DIM), BF16)
    heads = []
    for h, p in enumerate(probs):
        v = kv_ref[:, MEM_W + h * HEAD_DIM:MEM_W + (h + 1) * HEAD_DIM].astype(BF16)
        ol = jnp.dot(p, jnp.concatenate([v, ones], axis=1), preferred_element_type=F32)
        heads.append((ol[:, :HEAD_DIM] * (1.0 / ol[:, HEAD_DIM:])).astype(BF16))
    return jnp.concatenate(heads, axis=1)


OUT_PROJ_COLS = 512


def _project_out(mix, probs, kv_ref, w_ref, x_ref, g_ref, o_ref, hn_ref):
    ka = mix.shape[1]
    width = o_ref.shape[1]
    chunks = [slice(c * OUT_PROJ_COLS, (c + 1) * OUT_PROJ_COLS) for c in range(width // OUT_PROJ_COLS)]
    for cols in chunks:
        o_ref[:, cols] = x_ref[:, cols] + jnp.dot(mix, w_ref[:ka, cols], preferred_element_type=F32)
    b = _memory_values(probs, kv_ref)
    ssq = jnp.zeros((o_ref.shape[0], 1), F32)
    for cols in chunks:
        y = o_ref[:, cols] + jnp.dot(b, w_ref[ka:, cols], preferred_element_type=F32)
        o_ref[:, cols] = y
        ssq = ssq + jnp.sum(y * y, axis=-1, keepdims=True)
    scale = lax.rsqrt(ssq * (1.0 / width) + NORM_EPS)
    for cols in chunks:
        hn_ref[:, cols] = (o_ref[:, cols] * scale * g_ref[:, cols]).astype(BF16)


def _sgu_mixer_kernel(x_ref, g_ref, w_ref, lng_ref, lnb_ref, ws_ref, bt_ref, kv_ref, wo_ref, gf_ref,
                      o_ref, hn_ref, h_ref, v_scr, vn_scr, mix_scr, wsb_scr):
    @pl.when(pl.program_id(0) == 0)
    def _():
        t = lax.broadcasted_iota(jnp.int32, (SGU_CHUNK, SGU_CHUNK), 0)
        s = lax.broadcasted_iota(jnp.int32, (SGU_CHUNK, SGU_CHUNK), 1)
        for g in range(SGU_GROUPS):
            wsb_scr[g] = jnp.where(t >= s, ws_ref[g], 0.0).astype(BF16)

    h_ref[...] = _rms_rows(x_ref[...], g_ref[...]).astype(BF16)
    for c in range(SGU_W // V7X_MXU_COLS):
        cols = slice(c * V7X_MXU_COLS, (c + 1) * V7X_MXU_COLS)
        v_scr[:, cols] = _gelu_tanh(_pass(h_ref, w_ref, SGU_W + c * V7X_MXU_COLS))
    q_heads = []
    for c in range(MEM_W // V7X_MXU_COLS):
        q_heads += _heads(_pass(h_ref, w_ref, 2 * SGU_W + c * V7X_MXU_COLS))
    probs = _memory_scores(q_heads, kv_ref)
    v = v_scr[...]
    mu = jnp.mean(v, axis=-1, keepdims=True)
    vc = v - mu
    var = jnp.mean(vc * vc, axis=-1, keepdims=True)
    vn_scr[...] = (vc * lax.rsqrt(var + LN_EPS) * lng_ref[...] + lnb_ref[...]).astype(BF16)

    bt = bt_ref[...]
    n_chunks = mix_scr.shape[0] // SGU_CHUNK
    for c in range(SGU_W // V7X_MXU_COLS):
        for j, u in enumerate(_heads(_gelu_tanh(_pass(h_ref, w_ref, c * V7X_MXU_COLS)))):
            g = c * (V7X_MXU_COLS // HEAD_DIM) + j
            cols = slice(g * HEAD_DIM, (g + 1) * HEAD_DIM)
            vg = jnp.concatenate([vn_scr[r * SGU_CHUNK:(r + 1) * SGU_CHUNK, cols]
                                  for r in range(n_chunks)], axis=1)
            mixed = jnp.dot(wsb_scr[g], vg, preferred_element_type=F32) + bt[:, g:g + 1]
            for r in range(n_chunks):
                rows = slice(r * SGU_CHUNK, (r + 1) * SGU_CHUNK)
                mix_scr[rows, cols] = (u[rows, :] * mixed[:, r * HEAD_DIM:(r + 1) * HEAD_DIM]
                                       ).astype(mix_scr.dtype)

    _project_out(mix_scr[...], probs, kv_ref, wo_ref, x_ref, gf_ref, o_ref, hn_ref)


def _sgu_mixer(x, g, g_layer, w_bf, ln_g, ln_b, w_spatial, b_spatial_t, layer, kv, kv_layer,
               w_out_bf, g_ffn, g_ffn_layer):
    s, k = x.shape
    tm = ROW_STEP
    assert w_bf.shape == (k, 2 * SGU_W + MEM_W) and tm % SGU_CHUNK == 0
    assert w_out_bf.shape == (SGU_W + MEM_W, k)
    row = lambda cols: pl.BlockSpec((tm, cols), lambda i: (i, 0))
    return pl.pallas_call(
        _sgu_mixer_kernel,
        grid=(s // tm,),
        in_specs=[row(k), _gain_spec(g_layer, k), _resident_spec(w_bf.shape),
                  _gain_spec(layer, SGU_W), _gain_spec(layer, SGU_W),
                  pl.BlockSpec((None, SGU_GROUPS, SGU_CHUNK, SGU_CHUNK), lambda i: (layer, 0, 0, 0)),
                  pl.BlockSpec((None, SGU_CHUNK, SGU_GROUPS), lambda i: (layer, 0, 0)),
                  pl.BlockSpec((None, MEM_LEN, 2 * MEM_W), lambda i: (kv_layer, 0, 0)),
                  _resident_spec(w_out_bf.shape), _gain_spec(g_ffn_layer, k)],
        out_specs=[row(k), row(k)],
        out_shape=[jax.ShapeDtypeStruct((s, k), F32), jax.ShapeDtypeStruct((s, k), BF16)],
        scratch_shapes=[pltpu.VMEM((tm, k), BF16), pltpu.VMEM((tm, SGU_W), F32),
                        pltpu.VMEM((tm, SGU_W), BF16), pltpu.VMEM((tm, SGU_W), BF16),
                        pltpu.VMEM((SGU_GROUPS, SGU_CHUNK, SGU_CHUNK), BF16)],
        compiler_params=_params("arbitrary"),
        name="sgu_mixer",
    )(x, _gain_arg(g), w_bf, _gain_arg(ln_g), _gain_arg(ln_b), w_spatial, b_spatial_t, kv,
      w_out_bf, _gain_arg(g_ffn))


def _out_proj_kernel(a_ref, q_ref, kv_ref, w_ref, x_ref, g_ref, o_ref, hn_ref):
    probs = _memory_scores([q_ref[:, h * HEAD_DIM:(h + 1) * HEAD_DIM] for h in range(MEM_HEADS)],
                           kv_ref)
    _project_out(a_ref[...], probs, kv_ref, w_ref, x_ref, g_ref, o_ref, hn_ref)


def _out_proj(a, q_src, q_col_block, kv, kv_layer, w_bf, x, g, g_layer):
    s, ka = a.shape
    width = x.shape[1]
    tm = ROW_STEP
    assert w_bf.shape == (ka + MEM_W, width)
    row = lambda cols: pl.BlockSpec((tm, cols), lambda i: (i, 0))
    return pl.pallas_call(
        _out_proj_kernel,
        grid=(s // tm,),
        in_specs=[row(ka), pl.BlockSpec((tm, MEM_W), lambda i: (i, q_col_block)),
                  pl.BlockSpec((None, MEM_LEN, 2 * MEM_W), lambda i: (kv_layer, 0, 0)),
                  _resident_spec((ka + MEM_W, width)), row(width), _gain_spec(g_layer, width)],
        out_specs=[row(width), row(width)],
        out_shape=[jax.ShapeDtypeStruct((s, width), F32), jax.ShapeDtypeStruct((s, width), BF16)],
        compiler_params=_params("parallel"),
        name="out_proj",
    )(a, q_src, kv, w_bf, x, _gain_arg(g))


def _rows(ref, start, size, stride):
    if stride == 1:
        return ref[pl.ds(start, size), :]
    return ref[pl.ds(start, size, stride=stride), :]


V7X_FREE_SUBLANE_STRIDE = 4


def _presplit(d):
    return d // V7X_FREE_SUBLANE_STRIDE if d > V7X_FREE_SUBLANE_STRIDE else 1


ATTN_SUB_BATCH = 8


def _carries_prev(d):
    return d * BLK == ATTN_SPAN and _presplit(d) > 1


def _attn_batch(units):
    scores = [lax.dot_general((q * ATTN_SCALE_LOG2).astype(BF16), kc.astype(BF16),
                              (((1,), (1,)), ((), ())), preferred_element_type=F32) + bias
              for q, kc, _, bias in units]
    probs = []
    for s in scores:
        m = jnp.max(s, axis=-1, keepdims=True)
        probs.append((jnp.exp2(s - m).astype(BF16), m))
    ones = jnp.ones((2 * BLK, HEAD_DIM), BF16)
    outs = []
    for (p, m), (_, _, vc, _) in zip(probs, units):
        ol = jnp.dot(p, jnp.concatenate([vc.astype(BF16), ones], axis=1),
                     preferred_element_type=F32)
        o, l = ol[:, :HEAD_DIM], ol[:, HEAD_DIM:]
        outs.append((o * (1.0 / l), m + jnp.log2(l)))
    return outs


def _attn_kernel(*refs):
    ngrp = len(ATTN_GROUPS)
    n_in = [3 if _carries_prev(d) else 5 for _, d in ATTN_GROUPS]
    in_off = [sum(n_in[:g]) for g in range(ngrp)]
    in_refs = refs[:sum(n_in)]
    o_ref = refs[sum(n_in)]
    pos = sum(n_in) + 1
    span = pl.program_id(1)
    o_scr, l_scr = refs[pos:pos + ngrp], refs[pos + ngrp:pos + 2 * ngrp]
    split_refs = refs[pos + 2 * ngrp:pos + 2 * ngrp + 7]
    bias_ref = refs[-1]

    qi = lax.broadcasted_iota(jnp.int32, (BLK, 2 * BLK), 0)
    ki = lax.broadcasted_iota(jnp.int32, (BLK, 2 * BLK), 1)
    lo = jnp.where(span == 0, BLK, 0)
    bias_ref[0] = jnp.where((ki >= qi) & (ki <= qi + N_BACK), 0.0, NEG_INF)
    bias_ref[1] = jnp.where((ki >= jnp.maximum(qi, lo)) & (ki <= qi + N_BACK), 0.0, NEG_INF)

    for g, (_, d) in enumerate(ATTN_GROUPS):
        carry = _carries_prev(d)
        q_ref, k_ref, v_ref = in_refs[in_off[g]:in_off[g] + 3]
        kp_ref, vp_ref = (None, None) if carry else in_refs[in_off[g] + 3:in_off[g] + 5]
        og, lg = o_scr[g], l_scr[g]
        sub = d * BLK
        f = _presplit(d)
        d2 = d // f
        qlen, sublen = ATTN_SPAN // f, sub // f
        if f > 1:
            q_src, kp_src, k_src, vp_src, v_src, o_dst, l_dst = split_refs
            assert carry
            @pl.when(span == 0)
            def _(kp_src=kp_src, vp_src=vp_src):
                kp_src[...] = jnp.zeros(kp_src.shape, F32)
                vp_src[...] = jnp.zeros(vp_src.shape, F32)
            for rf in range(f):
                for src, dst, n in ((q_ref, q_src, qlen), (k_ref, k_src, qlen), (v_ref, v_src, qlen)):
                    dst[pl.ds(rf * n, n), :] = _rows(src, rf, n, f)
        else:
            q_src, kp_src, k_src, vp_src, v_src, o_dst, l_dst = (
                q_ref, kp_ref, k_ref, vp_ref, v_ref, og, lg)

        def key_rows(prev_src, src, c, rf, r2):
            if c == 0:
                return jnp.concatenate([_rows(prev_src, rf * sublen + r2, BLK, d2),
                                        _rows(src, rf * qlen + r2, BLK, d2)], axis=0)
            return _rows(src, rf * qlen + (c - 1) * sublen + r2, 2 * BLK, d2)

        units, starts = [], []
        for idx in range(ATTN_SPAN // BLK):
            c, r = divmod(idx, d)
            rf, r2 = r % f, r // f
            qstart = rf * qlen + c * sublen + r2
            units.append((_rows(q_src, qstart, BLK, d2), key_rows(kp_src, k_src, c, rf, r2),
                          key_rows(vp_src, v_src, c, rf, r2), bias_ref[1 if c == 0 else 0]))
            starts.append(qstart)
        results = []
        for b0 in range(0, len(units), ATTN_SUB_BATCH):
            results += _attn_batch(units[b0:b0 + ATTN_SUB_BATCH])
        for qstart, (o, lse) in zip(starts, results):
            rows = pl.ds(qstart, BLK) if d2 == 1 else pl.ds(qstart, BLK, stride=d2)
            o_dst[rows, :] = o
            l_dst[rows, :] = lse
        if f > 1:
            for rf in range(f):
                og[pl.ds(rf, qlen, stride=f), :] = o_dst[pl.ds(rf * qlen, qlen), :]
                lg[pl.ds(rf, qlen, stride=f), :] = l_dst[pl.ds(rf * qlen, qlen), :]
            kp_src[...] = k_src[...]
            vp_src[...] = v_src[...]

    lses = [l[...] for l in l_scr]
    mx = functools.reduce(jnp.maximum, lses)
    ws = [jnp.exp2(l - mx) for l in lses]
    num = functools.reduce(lambda a, b: a + b, [w * o[...] for w, o in zip(ws, o_scr)])
    den = functools.reduce(lambda a, b: a + b, ws)
    o_ref[...] = (num * (1.0 / den)).astype(o_ref.dtype)


def _dilated_attention(qk, vm):
    s = qk.shape[0]
    assert s % ATTN_SPAN == 0
    hpg = HEADS_PER_GROUP
    in_specs, args = [], []
    for g, (_, d) in enumerate(ATTN_GROUPS):
        sub = d * BLK
        ratio = ATTN_SPAN // sub
        qc, kc, vc = g * hpg, ATTN_HEADS + g * hpg, g * hpg
        cur = lambda col: pl.BlockSpec((ATTN_SPAN, HEAD_DIM), lambda h, i, col=col: (i, col + h))
        prev = lambda col, ratio=ratio, sub=sub: pl.BlockSpec(
            (sub, HEAD_DIM), lambda h, i, col=col, ratio=ratio: (jnp.maximum(i * ratio - 1, 0), col + h))
        in_specs += [cur(qc), cur(kc), cur(vc)]
        args += [qk, qk, vm]
        if not _carries_prev(d):
            in_specs += [prev(kc), prev(vc)]
            args += [qk, vm]
    ngrp = len(ATTN_GROUPS)
    return pl.pallas_call(
        _attn_kernel,
        grid=(hpg, s // ATTN_SPAN),
        in_specs=in_specs,
        out_specs=pl.BlockSpec((ATTN_SPAN, HEAD_DIM), lambda h, i: (i, h)),
        out_shape=jax.ShapeDtypeStruct((s, ATTN_OUT_W), BF16),
        scratch_shapes=([pltpu.VMEM((ATTN_SPAN, HEAD_DIM), F32)] * (2 * ngrp)
                        + [pltpu.VMEM((ATTN_SPAN, HEAD_DIM), F32)] * 7
                        + [pltpu.VMEM((2, BLK, 2 * BLK), F32)]),
        compiler_params=_params("arbitrary", "arbitrary"),
        name="dilated_attention",
    )(*args)


def _ffn_down_rows_kernel(a_ref, w_ref, x_ref, *rest, final_norm):
    if final_norm:
        g_ref, o_ref = rest
    else:
        (o_ref,) = rest
    width = o_ref.shape[1]
    ssq = jnp.zeros((o_ref.shape[0], 1), F32)
    for c in range(width // V7X_MXU_COLS):
        cols = slice(c * V7X_MXU_COLS, (c + 1) * V7X_MXU_COLS)
        y = x_ref[:, cols] + jnp.dot(a_ref[...], w_ref[:, cols], preferred_element_type=F32)
        o_ref[:, cols] = y
        if final_norm:
            ssq = ssq + jnp.sum(y * y, axis=-1, keepdims=True)
    if final_norm:
        scale = lax.rsqrt(ssq * (1.0 / width) + NORM_EPS)
        for c in range(width // V7X_MXU_COLS):
            cols = slice(c * V7X_MXU_COLS, (c + 1) * V7X_MXU_COLS)
            o_ref[:, cols] = o_ref[:, cols] * scale * g_ref[:, cols]


def _ffn_down_rows(a, w_bf, x, final_g=None):
    s, f = a.shape
    width = x.shape[1]
    tm = ROW_STEP
    assert w_bf.shape == (f, width) and s % tm == 0
    row = lambda cols: pl.BlockSpec((tm, cols), lambda i: (i, 0))
    in_specs = [row(f), _resident_spec(w_bf.shape), row(width)]
    args = [a, w_bf, x]
    if final_g is not None:
        in_specs.append(_gain_spec(0, width))
        args.append(final_g.reshape(1, 1, width))
    return pl.pallas_call(
        functools.partial(_ffn_down_rows_kernel, final_norm=final_g is not None),
        grid=(s // tm,),
        in_specs=in_specs,
        out_specs=row(width),
        out_shape=jax.ShapeDtypeStruct((s, width), F32),
        compiler_params=_params("parallel"),
        name="ffn_down_rows",
    )(*args)


def _ffn_gate_up_kernel(h_ref, wg_ref, wu_ref, *rest, n_jobs):
    o_ref = rest[n_jobs]
    _run_cast_jobs(rest[:n_jobs], rest[n_jobs + 1:])
    for c in range(o_ref.shape[1] // V7X_MXU_COLS):
        cols = slice(c * V7X_MXU_COLS, (c + 1) * V7X_MXU_COLS)
        gate = jnp.dot(h_ref[...], wg_ref[:, cols].astype(BF16), preferred_element_type=F32)
        up = jnp.dot(h_ref[...], wu_ref[:, cols].astype(BF16), preferred_element_type=F32)
        o_ref[:, cols] = (gate * (1.0 / (1.0 + jnp.exp(-gate))) * up).astype(o_ref.dtype)


def _ffn_gate_up(h, w_gate, w_up, layer, cast_weights=()):
    s, k = h.shape
    f = w_gate.shape[2]
    tm, tf = GATE_UP_ROWS, GATE_UP_COLS
    nj = f // tf
    wspec = pl.BlockSpec((None, k, tf), lambda i, j: (layer, 0, j))
    job_in, job_args, job_out, job_shape = _cast_jobs(cast_weights, (s // tm) * nj,
                                                      lambda i, j: i * nj + j)
    return pl.pallas_call(
        functools.partial(_ffn_gate_up_kernel, n_jobs=len(job_args)),
        grid=(s // tm, nj),
        in_specs=[pl.BlockSpec((tm, k), lambda i, j: (i, 0)), wspec, wspec] + job_in,
        out_specs=[pl.BlockSpec((tm, tf), lambda i, j: (i, j))] + job_out,
        out_shape=[jax.ShapeDtypeStruct((s, f), BF16)] + job_shape,
        compiler_params=_params("arbitrary", "arbitrary"),
        name="ffn_gate_up",
    )(h, w_gate, w_up, *job_args)


def kernel(x, mem, positions, mix_norm, mem_norm, w_mem_kv, ffn_norm, w_gate, w_up, w_down,
           attn_w_in, attn_w_out, sgu_w_in, sgu_ln_g, sgu_ln_b, sgu_w_spatial, sgu_b_spatial,
           sgu_w_out, final_norm):
    b, s, d = x.shape
    assert (b, s, d) == (1, SEQ, D_MODEL) and mem.shape == (1, MEM_LEN, D_MODEL)
    xs = x.reshape(s, d)
    mems = mem.reshape(MEM_LEN, d)

    inv_freq = ROPE_THETA ** (-jnp.arange(ROT_HALF, dtype=F32) / ROT_HALF)
    invf = jnp.tile(jnp.concatenate([inv_freq, inv_freq]), ROPE_PACK)
    pos_col, invf = positions.reshape(s, 1), invf.reshape(1, HEAD_DIM)

    def mixer_weights(layer):
        return (attn_w_in, attn_w_out) if layer % 2 == 0 else (sgu_w_in, sgu_w_out)

    kv, w_in_bf = _mem_kv(mems, mem_norm, w_mem_kv, [(attn_w_in, 0)])
    w_out_bf = None
    for i in range(DEPTH):
        j = i // 2
        if i % 2 == 0:
            casts = [(attn_w_out, j)] if w_out_bf is None else []
            qk, vm, *cast = _attn_in_proj(xs, mix_norm, i, w_in_bf, pos_col, invf, casts)
            w_out_bf = cast[0] if cast else w_out_bf
            mix = _dilated_attention(qk, vm)
            xs, hn = _out_proj(mix, vm, ATTN_W // MEM_W, kv, i, w_out_bf, xs, ffn_norm, i)
        else:
            xs, hn = _sgu_mixer(xs, mix_norm, i, w_in_bf, sgu_ln_g, sgu_ln_b, sgu_w_spatial,
                                jnp.swapaxes(sgu_b_spatial, 1, 2), j, kv, i, w_out_bf, ffn_norm, i)
        last = i + 1 == DEPTH
        casts = [(w_down, i)] + ([] if last else [(w, (i + 1) // 2) for w in mixer_weights(i + 1)])
        act, w_down_bf, *next_bf = _ffn_gate_up(hn, w_gate, w_up, i, casts)
        w_in_bf, w_out_bf = next_bf if next_bf else (None, None)
        xs = _ffn_down_rows(act, w_down_bf, xs, final_g=final_norm if last else None)
    return xs.reshape(b, s, d)
```
